```python
import math
import jax
import jax.numpy as jnp
from jax import lax
import numpy as np

D_MODEL = 1024
BATCH = 8
SEQ = 2048
DEPTH = 4

GRID_W = 64
CTX_LEN = 256
N_MIXERS = 3
CONV_W = 3
DA_HEADS = 8
DA_HEAD_DIM = 64
ROPE_AXIS_DIM = DA_HEAD_DIM // 2
ROPE_THETA = 10000.0
Q_BLOCK = 128
SUBLN_EPS = 1e-5
HY_ORDER = 2
HY_EMB_DIM = 33
HY_BANDS = (HY_EMB_DIM - 1) // 2
HY_FILTER_WIDTH = 64
HY_FAST_DECAY_PCT = 0.3
HY_SLOW_DECAY_PCT = 1.5
HY_DECAY_TARGET = 1e-2
N_EXPERTS = 16
D_EXPERT = 1024
EC_CAPACITY = 2
NORM_EPS = 1e-6
N_LAYERS_A = (DEPTH + 2) // 3
N_LAYERS_B = (DEPTH + 1) // 3
N_LAYERS_C = DEPTH // 3

kernel_name = 'hybrid_dit_shortconv_diffattn_hyena_ecmoe'


def rms_norm(x, g, eps=NORM_EPS):
    xf = x.astype(jnp.float32)
    y = xf * lax.rsqrt(jnp.mean(xf * xf, axis=-1, keepdims=True) + eps)
    return y.astype(x.dtype) * g


def modulate(h, shift, scale):
    return h * (1 + scale) + shift


def short_conv3(u, w):
    up = jnp.pad(u, ((0, 0), (1, 1), (0, 0)))
    return up[:, :-2] * w[0] + up[:, 1:-1] * w[1] + up[:, 2:] * w[2]


def short_conv_mixer(h, w_in, w_conv, w_out):
    b, cg, v = jnp.split(h @ w_in, 3, axis=-1)
    return (b * short_conv3(cg * v, w_conv)) @ w_out


def axial_rope_tables(seq_len):
    rows = seq_len // GRID_W
    row = jnp.repeat(jnp.arange(rows, dtype=jnp.float32), GRID_W)
    col = jnp.tile(jnp.arange(GRID_W, dtype=jnp.float32), rows)
    inv_freq = ROPE_THETA ** (-jnp.arange(0, ROPE_AXIS_DIM, 2, dtype=jnp.float32) / ROPE_AXIS_DIM)
    ang_r = row[:, None] * inv_freq
    ang_c = col[:, None] * inv_freq
    expand = lambda t: t[None, :, None, None, :]
    return (expand(jnp.cos(ang_r)), expand(jnp.sin(ang_r)),
            expand(jnp.cos(ang_c)), expand(jnp.sin(ang_c)))


def rotate_pairs(x, cos, sin):
    x1, x2 = jnp.split(x, 2, axis=-1)
    cos = cos.astype(x.dtype)
    sin = sin.astype(x.dtype)
    return jnp.concatenate([x1 * cos - x2 * sin, x2 * cos + x1 * sin], axis=-1)


def axial_rope(x, tables):
    cos_r, sin_r, cos_c, sin_c = tables
    xr, xc = jnp.split(x, 2, axis=-1)
    return jnp.concatenate([rotate_pairs(xr, cos_r, sin_r), rotate_pairs(xc, cos_c, sin_c)], axis=-1)


def diff_attend(q, k, v, lam):
    s = jnp.einsum('bqhrd,bkhrd->bhrqk', q, k).astype(jnp.float32) * (DA_HEAD_DIM ** -0.5)
    p = jax.nn.softmax(s, axis=-1)
    a = (p[:, :, 0] - lam * p[:, :, 1]).astype(v.dtype)
    return jnp.einsum('bhqk,bkhe->bqhe', a, v)


def diff_attention(h_lat, h_ctx, w_qkv, lq1, lk1, lq2, lk2, subln_g, w_out, lam_init, ctx_out):
    B, L, _ = h_lat.shape
    Lc = h_ctx.shape[1]
    H, d = DA_HEADS, DA_HEAD_DIM
    q, k, v = jnp.split(h_lat @ w_qkv, 3, axis=-1)
    tables = axial_rope_tables(L)
    q = axial_rope(q.reshape(B, L, H, 2, d), tables)
    k = axial_rope(k.reshape(B, L, H, 2, d), tables)
    v = v.reshape(B, L, H, 2 * d)
    kc, vc = jnp.split(h_ctx @ w_qkv[:, D_MODEL:], 2, axis=-1)
    kc = kc.reshape(B, Lc, H, 2, d)
    vc = vc.reshape(B, Lc, H, 2 * d)
    lam = (jnp.exp(jnp.sum(lq1 * lk1).astype(jnp.float32))
           - jnp.exp(jnp.sum(lq2 * lk2).astype(jnp.float32)) + lam_init)
    k_all = jnp.concatenate([kc, k], axis=1)
    v_all = jnp.concatenate([vc, v], axis=1)
    n_blocks = L // Q_BLOCK
    q_blocks = jnp.moveaxis(q.reshape(B, n_blocks, Q_BLOCK, H, 2, d), 1, 0)
    o = lax.map(lambda qb: diff_attend(qb, k_all, v_all, lam), q_blocks)
    o = jnp.moveaxis(o, 0, 1).reshape(B, L, H, 2 * d)

    def finish(o_heads):
        o_n = rms_norm(o_heads, subln_g, SUBLN_EPS) * (1 - lam_init)
        return o_n.reshape(o_heads.shape[0], o_heads.shape[1], D_MODEL) @ w_out

    y_lat = finish(o)
    y_ctx = None
    if ctx_out:
        qc = (h_ctx @ w_qkv[:, :D_MODEL]).reshape(B, Lc, H, 2, d)
        y_ctx = finish(diff_attend(qc, kc, vc, lam))
    return y_lat, y_ctx


def hyena_filters(seq_len, f1_w, f1_b, f1_freq, f2_w, f2_b, f2_freq, f3_w):
    t = jnp.linspace(0.0, 1.0, seq_len, dtype=jnp.float32)[:, None]
    w = 2.0 * math.pi * jnp.arange(seq_len, dtype=jnp.float32)[:, None] / seq_len
    f = jnp.linspace(1e-4, HY_BANDS - 1, HY_BANDS, dtype=jnp.float32)[None, :]
    z = jnp.concatenate([t, jnp.cos(f * w), -jnp.sin(f * w)], axis=-1)
    hid = jnp.sin(f1_freq * (z @ f1_w + f1_b))
    hid = jnp.sin(f2_freq * (hid @ f2_w + f2_b))
    filt = (hid @ f3_w).astype(jnp.float32).reshape(seq_len, 2 * HY_ORDER, D_MODEL)
    max_decay = math.log(HY_DECAY_TARGET) / HY_FAST_DECAY_PCT
    min_decay = math.log(HY_DECAY_TARGET) / HY_SLOW_DECAY_PCT
    deltas = jnp.abs(jnp.linspace(min_decay, max_decay, D_MODEL, dtype=jnp.float32))
    return filt * jnp.exp(-t[:, :, None] * deltas)


def bidir_fftconv(u, h_fwd, h_bwd):
    B, L, C = u.shape
    k = jnp.concatenate([h_fwd, jnp.zeros((1, C), jnp.float32), h_bwd[:0:-1]], axis=0)
    U = jnp.fft.rfft(u.astype(jnp.float32), n=2 * L, axis=1)
    K = jnp.fft.rfft(k, axis=0)
    y = jnp.fft.irfft(U * K[None], n=2 * L, axis=1)[:, :L]
    return y.astype(u.dtype)


def hyena_mixer(h, w_in, w_conv, f1_w, f1_b, f1_freq, f2_w, f2_b, f2_freq, f3_w, skip, w_out):
    L = h.shape[1]
    u = short_conv3(h @ w_in, w_conv)
    x1, x2, v = jnp.split(u, 3, axis=-1)
    filt = hyena_filters(L, f1_w, f1_b, f1_freq, f2_w, f2_b, f2_freq, f3_w)
    z = x1 * (bidir_fftconv(v, filt[:, 0], filt[:, 1]) + v * skip[0])
    z = x2 * (bidir_fftconv(z, filt[:, 2], filt[:, 3]) + z * skip[1])
    return z @ w_out


def ec_moe(h, w_router, w_gate, w_up, w_down):
    B, T, D = h.shape
    cap = max(1, EC_CAPACITY * T // N_EXPERTS)
    aff = jax.nn.softmax(jnp.einsum('btd,de->bte', h, w_router).astype(jnp.float32), axis=-1)
    gate, idx = lax.top_k(jnp.swapaxes(aff, 1, 2), cap)
    x_in = jax.vmap(lambda hb, ib: hb[ib])(h, idx)
    a = jnp.einsum('becd,edf->becf', x_in, w_gate)
    b = jnp.einsum('becd,edf->becf', x_in, w_up)
    y = jnp.einsum('becf,efd->becd', jax.nn.silu(a) * b, w_down) * gate[..., None].astype(h.dtype)
    return jax.vmap(lambda yb, ib: jnp.zeros((T, D), h.dtype).at[ib.reshape(-1)].add(yb.reshape(-1, D)))(y, idx)


def setup_inputs(seed: int = 0) -> dict:
    key = jax.random.key(seed)
    keys = iter(jax.random.split(key, 40))
    nrm = lambda shape, s: jax.random.normal(next(keys), shape, jnp.float32) * s
    D, F, E, d, W = D_MODEL, D_EXPERT, N_EXPERTS, DA_HEAD_DIM, HY_FILTER_WIDTH
    return {
        'x': nrm((BATCH, SEQ, D), 1.0),
        'c': nrm((BATCH, D), 1.0),
        'ctx': nrm((BATCH, CTX_LEN, D), 1.0),
        'c_ctx': nrm((D,), 1.0),
        'w_ada': nrm((DEPTH, D, 6 * D), 0.5 * D ** -0.5),
        'b_ada': nrm((DEPTH, 6 * D), 0.02),
        'norm1_g': 1.0 + nrm((DEPTH, D), 0.02),
        'norm2_g': 1.0 + nrm((DEPTH, D), 0.02),
        'final_g': 1.0 + nrm((D,), 0.02),
        'a_w_in': nrm((N_LAYERS_A, D, 3 * D), D ** -0.5),
        'a_conv': nrm((N_LAYERS_A, CONV_W, D), CONV_W ** -0.5),
        'a_w_out': nrm((N_LAYERS_A, D, D), D ** -0.5),
        'b_w_qkv': nrm((N_LAYERS_B, D, 3 * D), D ** -0.5),
        'b_lq1': nrm((N_LAYERS_B, d), 0.1),
        'b_lk1': nrm((N_LAYERS_B, d), 0.1),
        'b_lq2': nrm((N_LAYERS_B, d), 0.1),
        'b_lk2': nrm((N_LAYERS_B, d), 0.1),
        'b_subln_g': 1.0 + nrm((N_LAYERS_B, 2 * d), 0.02),
        'b_w_out': nrm((N_LAYERS_B, D, D), D ** -0.5),
        'c_w_in': nrm((N_LAYERS_C, D, 3 * D), D ** -0.5),
        'c_conv': nrm((N_LAYERS_C, CONV_W, 3 * D), CONV_W ** -0.5),
        'c_f1_w': nrm((N_LAYERS_C, HY_EMB_DIM, W), HY_EMB_DIM ** -0.5),
        'c_f1_b': nrm((N_LAYERS_C, W), 0.1),
        'c_f1_freq': 1.0 + nrm((N_LAYERS_C, W), 0.05),
        'c_f2_w': nrm((N_LAYERS_C, W, W), W ** -0.5),
        'c_f2_b': nrm((N_LAYERS_C, W), 0.1),
        'c_f2_freq': 1.0 + nrm((N_LAYERS_C, W), 0.05),
        'c_f3_w': nrm((N_LAYERS_C, W, 2 * HY_ORDER * D), 0.05 * W ** -0.5),
        'c_skip': nrm((N_LAYERS_C, HY_ORDER, D), 0.5),
        'c_w_out': nrm((N_LAYERS_C, D, D), D ** -0.5),
        'moe_router': nrm((DEPTH, D, E), D ** -0.5),
        'moe_w_gate': nrm((DEPTH, E, D, F), D ** -0.5),
        'moe_w_up': nrm((DEPTH, E, D, F), D ** -0.5),
        'moe_w_down': nrm((DEPTH, E, F, D), F ** -0.5),
    }


def reference(x, c, ctx, c_ctx, w_ada, b_ada, norm1_g, norm2_g, final_g,
              a_w_in, a_conv, a_w_out,
              b_w_qkv, b_lq1, b_lk1, b_lq2, b_lk2, b_subln_g, b_w_out,
              c_w_in, c_conv, c_f1_w, c_f1_b, c_f1_freq, c_f2_w, c_f2_b, c_f2_freq, c_f3_w, c_skip, c_w_out,
              moe_router, moe_w_gate, moe_w_up, moe_w_down):
    attn_layers = [i for i in range(DEPTH) if i % N_MIXERS == 1]
    last_ctx_read = max(attn_layers) if attn_layers else -1
    silu_c = jax.nn.silu(c)
    silu_cc = jax.nn.silu(c_ctx)
    for i in range(DEPTH):
        kind, j = i % N_MIXERS, i // N_MIXERS
        update_ctx = i < last_ctx_read
        use_ctx = update_ctx or kind == 1
        sh1, sc1, g1, sh2, sc2, g2 = [m[:, None, :] for m in jnp.split(silu_c @ w_ada[i] + b_ada[i], 6, axis=-1)]
        h = modulate(rms_norm(x, norm1_g[i]), sh1, sc1)
        hc = None
        if use_ctx:
            csh1, csc1, cg1, csh2, csc2, cg2 = jnp.split(silu_cc @ w_ada[i] + b_ada[i], 6, axis=-1)
            hc = modulate(rms_norm(ctx, norm1_g[i]), csh1, csc1)
        yc = None
        if kind == 0:
            params = (a_w_in[j], a_conv[j], a_w_out[j])
            y = short_conv_mixer(h, *params)
            if update_ctx:
                yc = short_conv_mixer(hc, *params)
        elif kind == 1:
            lam_init = 0.8 - 0.6 * math.exp(-0.3 * i)
            y, yc = diff_attention(h, hc, b_w_qkv[j], b_lq1[j], b_lk1[j], b_lq2[j], b_lk2[j],
                                   b_subln_g[j], b_w_out[j], lam_init, update_ctx)
        else:
            params = (c_w_in[j], c_conv[j], c_f1_w[j], c_f1_b[j], c_f1_freq[j], c_f2_w[j], c_f2_b[j],
                      c_f2_freq[j], c_f3_w[j], c_skip[j], c_w_out[j])
            y = hyena_mixer(h, *params)
            if update_ctx:
                yc = hyena_mixer(hc, *params)
        x = x + g1 * y
        x = x + g2 * ec_moe(modulate(rms_norm(x, norm2_g[i]), sh2, sc2),
                            moe_router[i], moe_w_gate[i], moe_w_up[i], moe_w_down[i])
        if update_ctx:
            ctx = ctx + cg1 * yc
            ctx = ctx + cg2 * ec_moe(modulate(rms_norm(ctx, norm2_g[i]), csh2, csc2),
                                     moe_router[i], moe_w_gate[i], moe_w_up[i], moe_w_down[i])
    return rms_norm(x, final_g)
```

```python
import functools
import math

import jax
import jax.numpy as jnp
import numpy as np
from jax import lax
from jax.experimental import pallas as pl
from jax.experimental.pallas import tpu as pltpu

BF = jnp.bfloat16
F32 = jnp.float32

GRID_W = 64
DA_HEAD_DIM = 64
ROPE_AXIS_DIM = DA_HEAD_DIM // 2
ROPE_THETA = 10000.0
SUBLN_EPS = 1e-5
NORM_EPS = 1e-6
N_MIXERS = 3
EC_CAPACITY = 2
HY_EMB_DIM = 33
HY_BANDS = (HY_EMB_DIM - 1) // 2
HY_FAST_DECAY_PCT = 0.3
HY_SLOW_DECAY_PCT = 1.5
HY_DECAY_TARGET = 1e-2

V7X_LANES = 128
V7X_BF16_SUBLANES = 16
V7X_VMEM_BYTES = 64 * 2**20
VMEM_LIMIT = V7X_VMEM_BYTES - 8 * 2**20
HALO = V7X_BF16_SUBLANES
MOD_ROWS = 16
ROW_TILE = 512
PROJ_ROW_TILE = 1024
ATTN_Q_TILE = 256
CHAN_TILE = 256
ADA_COL_TILE = 768
FREQ_TILE = 512
SC_GATHER_ROWS = 64


def _params(*sem):
    return pltpu.CompilerParams(dimension_semantics=sem, vmem_limit_bytes=VMEM_LIMIT)


def _dot(a, b):
    return jnp.dot(a, b, preferred_element_type=F32)


def _dot_nt(a, b):
    return lax.dot_general(a, b, (((1,), (1,)), ((), ())), preferred_element_type=F32)


def _split(a):
    hi = a.astype(BF)
    lo = (a - hi.astype(F32)).astype(BF)
    return hi, lo


def _dot3(a, b):
    ah, al = _split(a)
    bh, bl = _split(b)
    return _dot(ah, bh) + (_dot(ah, bl) + _dot(al, bh))


def _sigmoid(a):
    return 1.0 / (1.0 + jnp.exp(-a))


def _norm_mod(x, g, shift, scale):
    ms = jnp.mean(x * x, axis=-1, keepdims=True)
    return (x * lax.rsqrt(ms + NORM_EPS) * g) * (1.0 + scale) + shift


def _tile(n, pref):
    t = min(n, pref)
    assert n % t == 0, (n, pref)
    return t


def _full(shape):
    nd = len(shape)
    return pl.BlockSpec(shape, lambda *_: (0,) * nd)


def _ada_kernel(c_ref, w_ref, b_ref, o_ref):
    c = c_ref[...]
    o_ref[0] = _dot3(c * _sigmoid(c), w_ref[0]) + b_ref[0]


def _ada(cvec, w_ada, b_ada):
    depth, d, n6 = w_ada.shape
    tn = _tile(n6, ADA_COL_TILE)
    return pl.pallas_call(
        _ada_kernel,
        grid=(depth, n6 // tn),
        in_specs=[
            _full((MOD_ROWS, d)),
            pl.BlockSpec((1, d, tn), lambda i, j: (i, 0, j)),
            pl.BlockSpec((1, 1, tn), lambda i, j: (i, 0, j)),
        ],
        out_specs=pl.BlockSpec((1, MOD_ROWS, tn), lambda i, j: (i, 0, j)),
        out_shape=jax.ShapeDtypeStruct((depth, MOD_ROWS, n6), F32),
        compiler_params=_params("parallel", "parallel"),
        name="ada",
    )(cvec, w_ada, b_ada.reshape(depth, 1, n6))


def _pack_pairs(hh):
    half = hh.shape[1] // 2
    lo = lax.shift_right_logical(pltpu.bitcast(hh[:, :half].astype(F32), jnp.uint32), jnp.uint32(16))
    hi = pltpu.bitcast(hh[:, half:].astype(F32), jnp.uint32) & jnp.uint32(0xFFFF0000)
    return pltpu.bitcast(lo | hi, jnp.int32)


def _unpack_pairs(words):
    u = pltpu.bitcast(words, jnp.uint32)
    lo = pltpu.bitcast(lax.shift_left(u, jnp.uint32(16)), F32).astype(BF)
    hi = pltpu.bitcast(u & jnp.uint32(0xFFFF0000), F32).astype(BF)
    return jnp.concatenate([lo, hi], axis=1)


def _residual_router(x, y, mod, g2_ref, wrh_ref, wrl_ref, xo_ref, hn_ref, aff_ref):
    xn = x + mod[2:3] * y
    xo_ref[...] = xn
    hn = _norm_mod(xn, g2_ref[...], mod[3:4], mod[4:5])
    hh, hl = _split(hn)
    hn_ref[...] = _pack_pairs(hh) if hn_ref.dtype == jnp.int32 else hh
    logits = _dot_nt(wrh_ref[...], hh) + (_dot_nt(wrh_ref[...], hl) + _dot_nt(wrl_ref[...], hh))
    p = jnp.exp(logits - jnp.max(logits, axis=0, keepdims=True))
    aff_ref[...] = p / jnp.sum(p, axis=0, keepdims=True)


def _router_specs(n, d, e, tm, packed):
    in_specs = [_full((1, d)), _full((e, d)), _full((e, d))]
    hn_cols, hn_dtype = (d // 2, jnp.int32) if packed else (d, BF)
    out_specs = [
        pl.BlockSpec((tm, d), lambda i: (i, 0)),
        pl.BlockSpec((tm, hn_cols), lambda i: (i, 0)),
        pl.BlockSpec((e, tm), lambda i: (0, i)),
    ]
    out_shape = [
        jax.ShapeDtypeStruct((n, d), F32),
        jax.ShapeDtypeStruct((n, hn_cols), hn_dtype),
        jax.ShapeDtypeStruct((e, n), F32),
    ]
    return in_specs, out_specs, out_shape


def _halo_specs(n, d, tm):
    per = tm // HALO
    last = n // HALO - 1
    return [
        pl.BlockSpec((HALO, d), lambda i: (jnp.maximum(i * per - 1, 0), 0)),
        pl.BlockSpec((tm, d), lambda i: (i, 0)),
        pl.BlockSpec((HALO, d), lambda i: (jnp.minimum((i + 1) * per, last), 0)),
    ]


def _mod_spec(d, tiles_per_seq, ctx_row):
    if ctx_row is None:
        return pl.BlockSpec((1, 6, d), lambda i: (i // tiles_per_seq, 0, 0))
    return pl.BlockSpec((1, 6, d), lambda i: (ctx_row, 0, 0))


def _halo_rows(xp_ref, x_ref, xn_ref, g_ref, mod, tiles_per_seq):
    t = pl.program_id(0) % tiles_per_seq
    tm = x_ref.shape[0]
    g = g_ref[...]
    hs = [_norm_mod(r[...], g, mod[0:1], mod[1:2]) for r in (xp_ref, x_ref, xn_ref)]
    h = jnp.concatenate(hs, axis=0).astype(BF)
    row = lax.broadcasted_iota(jnp.int32, (tm + 2 * HALO, 1), 0)
    outside = ((row < HALO) & (t == 0)) | ((row >= HALO + tm) & (t == tiles_per_seq - 1))
    return h, jnp.where(outside, 0.0, 1.0)


def _conv3(s, w_ref, tm):
    n = s.shape[0]
    prev = pltpu.roll(s, 1, 0)[HALO:HALO + tm]
    nxt = pltpu.roll(s, n - 1, 0)[HALO:HALO + tm]
    return prev * w_ref[0:1, :] + s[HALO:HALO + tm] * w_ref[1:2, :] + nxt * w_ref[2:3, :]


def _shortconv_kernel(xp_ref, x_ref, xn_ref, mod_ref, g1_ref, win_ref, wconv_ref, wout_ref,
                      g2_ref, wrh_ref, wrl_ref, xo_ref, hn_ref, aff_ref, *, tiles_per_seq):
    tm, d = x_ref.shape
    mod = mod_ref[0]
    h, keep = _halo_rows(xp_ref, x_ref, xn_ref, g1_ref, mod, tiles_per_seq)
    gate = _dot(h[HALO:HALO + tm], win_ref[:, 0:d])
    s = _dot(h, win_ref[:, d:2 * d]) * _dot(h, win_ref[:, 2 * d:3 * d]) * keep
    z = (gate * _conv3(s, wconv_ref, tm)).astype(BF)
    y = _dot(z, wout_ref[...])
    _residual_router(x_ref[...], y, mod, g2_ref, wrh_ref, wrl_ref, xo_ref, hn_ref, aff_ref)


def _shortconv_layer(xs, mods, g1, w_in, w_conv, w_out, g2, wrh, wrl, seq, ctx_row=None):
    n, d = xs.shape
    e = wrh.shape[0]
    tm = _tile(seq, PROJ_ROW_TILE)
    tps = seq // tm
    r_in, r_out, r_shape = _router_specs(n, d, e, tm, packed=ctx_row is None)
    return pl.pallas_call(
        functools.partial(_shortconv_kernel, tiles_per_seq=tps),
        grid=(n // tm,),
        in_specs=_halo_specs(n, d, tm) + [
            _mod_spec(d, tps, ctx_row), _full((1, d)), _full((d, 3 * d)), _full((3, d)), _full((d, d)),
        ] + r_in,
        out_specs=r_out,
        out_shape=r_shape,
        compiler_params=_params("parallel"),
        name="shortconv_layer",
    )(xs, xs, xs, mods, g1, w_in, w_conv, w_out, g2, wrh, wrl)


def _outproj_kernel(o_ref, x_ref, mod_ref, wout_ref, g2_ref, wrh_ref, wrl_ref, xo_ref, hn_ref, aff_ref):
    y = _dot(o_ref[...], wout_ref[...])
    _residual_router(x_ref[...], y, mod_ref[0], g2_ref, wrh_ref, wrl_ref, xo_ref, hn_ref, aff_ref)


def _outproj_layer(o, xs, mods, w_out, g2, wrh, wrl, seq):
    n, d = xs.shape
    e = wrh.shape[0]
    tm = _tile(seq, PROJ_ROW_TILE)
    tps = seq // tm
    r_in, r_out, r_shape = _router_specs(n, d, e, tm, packed=True)
    return pl.pallas_call(
        _outproj_kernel,
        grid=(n // tm,),
        in_specs=[pl.BlockSpec((tm, d), lambda i: (i, 0)), pl.BlockSpec((tm, d), lambda i: (i, 0)),
                  _mod_spec(d, tps, None), _full((d, d))] + r_in,
        out_specs=r_out,
        out_shape=r_shape,
        compiler_params=_params("parallel"),
        name="outproj_layer",
    )(o, xs, mods, w_out, g2, wrh, wrl)


def _excl_cumsum_lanes(m):
    rows, t = m.shape
    a = lax.broadcasted_iota(jnp.int32, (V7X_LANES, V7X_LANES), 0)
    b = lax.broadcasted_iota(jnp.int32, (V7X_LANES, V7X_LANES), 1)
    tri = jnp.where(a < b, 1.0, 0.0).astype(BF)
    carry = jnp.zeros((rows, 1), F32)
    out = []
    for c in range(t // V7X_LANES):
        blk = m[:, c * V7X_LANES:(c + 1) * V7X_LANES]
        out.append(_dot(blk.astype(BF), tri) + carry)
        carry = carry + jnp.sum(blk, axis=1, keepdims=True)
    return jnp.concatenate(out, axis=1)


def _route_kernel(aff_ref, posm_ref, post_ref, gatet_ref, *, cap, nb):
    e = aff_ref.shape[0]
    t = aff_ref.shape[1] // nb
    aff = jnp.concatenate([aff_ref[:, b * t:(b + 1) * t] for b in range(nb)], axis=0)
    bits = pltpu.bitcast(aff, jnp.int32)

    def step(i, thr):
        cand = thr | jnp.left_shift(jnp.int32(1), 30 - i)
        cnt = jnp.sum(jnp.where(bits >= cand, 1.0, 0.0), axis=1, keepdims=True)
        return jnp.where(cnt >= cap, cand, thr)

    thr = lax.fori_loop(0, 31, step, jnp.zeros((nb * e, 1), jnp.int32))
    gt = jnp.where(bits > thr, 1.0, 0.0)
    eq = jnp.where(bits == thr, 1.0, 0.0)
    need = cap - jnp.sum(gt, axis=1, keepdims=True)
    sel = gt + eq * jnp.where(_excl_cumsum_lanes(eq) < need, 1.0, 0.0)
    pos = jnp.where(sel > 0.0, _excl_cumsum_lanes(sel), -1.0)
    gate = sel * aff
    posm_ref[...] = pos.astype(jnp.int32)
    pad = jnp.zeros((V7X_LANES - e, t), F32)
    for b in range(nb):
        rows = slice(b * e, (b + 1) * e)
        post_ref[b * t:(b + 1) * t, :] = jnp.concatenate([pos[rows], pad - 1.0], axis=0).T
        gatet_ref[b * t:(b + 1) * t, :] = jnp.concatenate([gate[rows], pad], axis=0).T


def _route(aff, nb, t, cap):
    e = aff.shape[0]
    posm, post, gatet = pl.pallas_call(
        functools.partial(_route_kernel, cap=cap, nb=nb),
        grid=(1,),
        in_specs=[_full((e, nb * t))],
        out_specs=[_full((nb * e, t)), _full((nb * t, V7X_LANES)), _full((nb * t, V7X_LANES))],
        out_shape=[
            jax.ShapeDtypeStruct((nb * e, t), jnp.int32),
            jax.ShapeDtypeStruct((nb * t, V7X_LANES), F32),
            jax.ShapeDtypeStruct((nb * t, V7X_LANES), F32),
        ],
        compiler_params=_params("arbitrary"),
        name="route",
    )(aff)
    return posm, post, gatet


def _sc_gather(table, posm, e0, nb, n_experts, t, cap):
    from jax.experimental.pallas import tpu_sc as plsc

    info = plsc.get_sparse_core_info()
    cores, lanes = info.num_cores, info.num_lanes
    workers = cores * info.num_subcores
    group = workers // nb
    words = table.shape[1]
    assert workers % nb == 0 and n_experts % group == 0 and cap % SC_GATHER_ROWS == 0 and t % lanes == 0

    @functools.partial(
        pl.kernel,
        mesh=plsc.VectorSubcoreMesh(core_axis_name="core", subcore_axis_name="subcore"),
        compiler_params=pltpu.CompilerParams(needs_layout_passes=False),
        out_type=jax.ShapeDtypeStruct((group * nb * cap, words), jnp.int32),
        scratch_types=[pltpu.VMEM((t,), jnp.int32), pltpu.VMEM((cap,), jnp.int32),
                       pltpu.VMEM((SC_GATHER_ROWS, words), jnp.int32), pltpu.SemaphoreType.DMA],
    )
    def gather(table_hbm, posm_hbm, out_hbm, pos_v, idx_v, rows_v, sem):
        w = lax.axis_index("subcore") * cores + lax.axis_index("core")
        e_local = w // nb
        b = w % nb
        pltpu.sync_copy(posm_hbm.at[b * n_experts + e0 + e_local], pos_v)

        @pl.loop(0, t // lanes)
        def _(i):
            p = pos_v[pl.ds(i * lanes, lanes)]
            token = lax.iota(jnp.int32, lanes) + (i * lanes + b * t)
            plsc.store_scatter(idx_v, [p], token, mask=p >= 0)

        out_base = (e_local * nb + b) * cap

        @pl.loop(0, cap // SC_GATHER_ROWS)
        def _(j):
            pltpu.async_copy(table_hbm.at[idx_v.at[pl.ds(j * SC_GATHER_ROWS, SC_GATHER_ROWS)]], rows_v, sem).wait()
            pltpu.sync_copy(rows_v, out_hbm.at[pl.ds(out_base + j * SC_GATHER_ROWS, SC_GATHER_ROWS)])

    return gather(table, posm).reshape(group, nb * cap, words)


def _sc_group(nb):
    from jax.experimental.pallas import tpu_sc as plsc

    info = plsc.get_sparse_core_info()
    return info.num_cores * info.num_subcores // nb


def _gather_kernel(posm_ref, hn_ref, o_ref, *, cap):
    e = posm_ref.shape[0]
    t = hn_ref.shape[0]
    slot = lax.broadcasted_iota(jnp.int32, (cap, t), 0)

    def body(k, carry):
        row = posm_ref[pl.ds(k, 1), :]
        onehot = jnp.where(row == slot, 1.0, 0.0).astype(BF)
        o_ref[k] = _dot(onehot, hn_ref[...]).astype(BF)
        return carry

    lax.fori_loop(0, e, body, 0)


def _gather(posm, hn, nb, t, cap):
    e = posm.shape[0] // nb
    d = hn.shape[1]
    return pl.pallas_call(
        functools.partial(_gather_kernel, cap=cap),
        grid=(nb,),
        in_specs=[pl.BlockSpec((e, t), lambda b: (b, 0)), pl.BlockSpec((t, d), lambda b: (b, 0))],
        out_specs=pl.BlockSpec((e, cap, d), lambda b: (0, b, 0)),
        out_shape=jax.ShapeDtypeStruct((e, nb * cap, d), BF),
        compiler_params=_params("parallel"),
        name="moe_gather",
    )(posm, hn)


def _expert_kernel(*refs, n_streams):
    x_refs = refs[:n_streams]
    wg_ref, wu_ref, wd_ref = refs[n_streams:n_streams + 3]
    y_refs = refs[n_streams + 3:2 * n_streams + 3]
    w_scr = refs[-1]
    w_scr[0] = wg_ref[0, 0].astype(BF)
    w_scr[1] = wu_ref[0, 0].astype(BF)
    w_scr[2] = wd_ref[0, 0].astype(BF)

    for x_ref, y_ref in zip(x_refs, y_refs):
        tr = _tile(x_ref.shape[1], ROW_TILE)

        def body(j, carry, x_ref=x_ref, y_ref=y_ref, tr=tr):
            r0 = pl.multiple_of(j * tr, tr)
            xs = x_ref[0, pl.ds(r0, tr), :]
            if xs.dtype == jnp.int32:
                xs = _unpack_pairs(xs)
            a = _dot(xs, w_scr[0])
            b = _dot(xs, w_scr[1])
            hm = (a * _sigmoid(a) * b).astype(BF)
            y_ref[0, pl.ds(r0, tr), :] = _dot(hm, w_scr[2]).astype(BF)
            return carry

        lax.fori_loop(0, x_ref.shape[1] // tr, body, 0)


def _experts(xins, w_gate, w_up, w_down, layer, e0, group):
    d = w_gate.shape[-2]
    f = w_gate.shape[-1]
    assert f == d
    wspec = pl.BlockSpec((1, 1, d, f), lambda k: (layer, e0 + k, 0, 0))
    xspecs = [pl.BlockSpec((1,) + x.shape[1:], lambda k, first=first: (first + k, 0, 0)) for x, first in xins]
    yspecs = [pl.BlockSpec((1, x.shape[1], d), lambda k: (k, 0, 0)) for x, _ in xins]
    return pl.pallas_call(
        functools.partial(_expert_kernel, n_streams=len(xins)),
        grid=(group,),
        in_specs=xspecs + [wspec, wspec, wspec],
        out_specs=yspecs,
        out_shape=[jax.ShapeDtypeStruct((group, x.shape[1], d), BF) for x, _ in xins],
        scratch_shapes=[pltpu.VMEM((3, d, f), BF)],
        compiler_params=_params("parallel"),
        name="moe_experts",
    )(*[x for x, _ in xins], w_gate, w_up, w_down)


def _combine_kernel(post_ref, gatet_ref, *refs, cap, final):
    y_refs = refs[:-4]
    x_ref, mod_ref, fg_ref, o_ref = refs[-4:]
    group, _, d = y_refs[0].shape
    e = group * len(y_refs)
    tq = x_ref.shape[0]
    pt = post_ref[...]
    gt = gatet_ref[...]
    if cap % V7X_LANES == 0:
        slot = lax.broadcasted_iota(jnp.int32, (tq, cap), 1).astype(F32)
        pieces = [jnp.where(pt[:, k:k + 1] == slot, gt[:, k:k + 1], 0.0).astype(BF) for k in range(e)]
        scat = jnp.concatenate(pieces, axis=1)
    else:
        slot = lax.broadcasted_iota(jnp.int32, (tq, e * cap), 1).astype(F32)
        scat = jnp.zeros((tq, e * cap), F32)
        for k in range(e):
            pk = pt[:, k:k + 1]
            scat = jnp.where((pk >= 0.0) & (pk + float(k * cap) == slot), gt[:, k:k + 1], scat)
        scat = scat.astype(BF)
    width = group * cap
    out = _dot(scat[:, 0:width], y_refs[0][...].reshape(width, d))
    for g in range(1, len(y_refs)):
        out = out + _dot(scat[:, g * width:(g + 1) * width], y_refs[g][...].reshape(width, d))
    xn = x_ref[...] + mod_ref[0][5:6] * out
    if final:
        ms = jnp.mean(xn * xn, axis=-1, keepdims=True)
        xn = xn * lax.rsqrt(ms + NORM_EPS) * fg_ref[...]
    o_ref[...] = xn


def _combine(post, gatet, ys, xs, mods, fg, nb, t, cap, ctx_row=None, final=False):
    n, d = xs.shape
    group = ys[0].shape[0]
    tq = _tile(t, ROW_TILE)
    tpb = t // tq
    if ctx_row is None:
        mspec = pl.BlockSpec((1, 6, d), lambda b, i: (b, 0, 0))
    else:
        mspec = pl.BlockSpec((1, 6, d), lambda b, i: (ctx_row, 0, 0))
    return pl.pallas_call(
        functools.partial(_combine_kernel, cap=cap, final=final),
        grid=(nb, tpb),
        in_specs=[
            pl.BlockSpec((tq, V7X_LANES), lambda b, i: (b * tpb + i, 0)),
            pl.BlockSpec((tq, V7X_LANES), lambda b, i: (b * tpb + i, 0)),
        ] + [pl.BlockSpec((group, cap, d), lambda b, i: (0, b, 0))] * len(ys) + [
            pl.BlockSpec((tq, d), lambda b, i: (b * tpb + i, 0)),
            mspec,
            _full((1, d)),
        ],
        out_specs=pl.BlockSpec((tq, d), lambda b, i: (b * tpb + i, 0)),
        out_shape=jax.ShapeDtypeStruct((n, d), F32),
        compiler_params=_params("parallel", "parallel"),
        name="moe_combine",
    )(post, gatet, *ys, xs, mods, fg)


def _rope_tables(seq):
    rows = seq // GRID_W
    row = np.repeat(np.arange(rows, dtype=np.float32), GRID_W)
    col = np.tile(np.arange(GRID_W, dtype=np.float32), rows)
    inv_freq = (ROPE_THETA ** (-np.arange(0, ROPE_AXIS_DIM, 2, dtype=np.float32) / ROPE_AXIS_DIM)).astype(np.float32)
    lane = np.arange(2 * DA_HEAD_DIM)
    within = lane % DA_HEAD_DIM
    axis = within // ROPE_AXIS_DIM
    half = (within % ROPE_AXIS_DIM) // (ROPE_AXIS_DIM // 2)
    idx = within % (ROPE_AXIS_DIM // 2)
    pos = np.where(axis[None, :] == 0, row[:, None], col[:, None])
    ang = (pos * inv_freq[idx][None, :]).astype(np.float32)
    cos = np.cos(ang).astype(np.float32)
    sin = np.sin(ang).astype(np.float32)
    sin_lo = np.where(half[None, :] == 1, sin, 0.0).astype(np.float32)
    sin_hi = np.where(half[None, :] == 0, -sin, 0.0).astype(np.float32)
    return cos, sin_lo, sin_hi


def _qkv_kernel(x_ref, mod_ref, g_ref, w_ref, cos_ref, sa_ref, sb_ref, *o_refs, rope):
    d = x_ref.shape[1]
    mod = mod_ref[0]
    h = _norm_mod(x_ref[...], g_ref[...], mod[0:1], mod[1:2]).astype(BF)
    slab = 2 * DA_HEAD_DIM
    shift = ROPE_AXIS_DIM // 2
    for j, o_ref in enumerate(o_refs):
        u = _dot(h, w_ref[:, j * d:(j + 1) * d])
        if rope and j < 2:
            scale = DA_HEAD_DIM ** -0.5 * math.log2(math.e) if j == 0 else 1.0
            cos, sa, sb = cos_ref[...] * scale, sa_ref[...] * scale, sb_ref[...] * scale
            for hd in range(d // slab):
                xs = u[:, hd * slab:(hd + 1) * slab]
                r = xs * cos + pltpu.roll(xs, shift, 1) * sa + pltpu.roll(xs, slab - shift, 1) * sb
                o_ref[:, hd * slab:(hd + 1) * slab] = r.astype(BF)
        else:
            o_ref[...] = u.astype(BF)


def _qkv(xs, mods, g1, w, tables, seq, nout, rope, ctx_row=None):
    n, d = xs.shape
    tm = _tile(seq, PROJ_ROW_TILE)
    tps = seq // tm
    slab = 2 * DA_HEAD_DIM
    tspec = pl.BlockSpec((tm, slab), lambda i: (i % tps, 0))
    return pl.pallas_call(
        functools.partial(_qkv_kernel, rope=rope),
        grid=(n // tm,),
        in_specs=[pl.BlockSpec((tm, d), lambda i: (i, 0)), _mod_spec(d, tps, ctx_row), _full((1, d)),
                  _full((d, nout * d)), tspec, tspec, tspec],
        out_specs=[pl.BlockSpec((tm, d), lambda i: (i, 0))] * nout,
        out_shape=[jax.ShapeDtypeStruct((n, d), BF)] * nout,
        compiler_params=_params("parallel"),
        name="attn_qkv",
    )(xs, mods, g1, w, *tables)


def _attn_kernel(q_ref, k_ref, v_ref, kc_ref, vc_ref, lq1_ref, lk1_ref, lq2_ref, lk2_ref, sg_ref, o_ref,
                 sl_a, sc_a, sl_b, sc_b, *, lam_init, tq):
    seq = q_ref.shape[0]
    lam = (jnp.exp(jnp.sum(lq1_ref[...] * lk1_ref[...], axis=1, keepdims=True))
           - jnp.exp(jnp.sum(lq2_ref[...] * lk2_ref[...], axis=1, keepdims=True)) + lam_init)
    slots = ((sl_a, sc_a), (sl_b, sc_b))

    def scores(i, slot):
        sl_ref, sc_ref = slot
        q = q_ref[i * tq:(i + 1) * tq, :]
        lane = lax.broadcasted_iota(jnp.int32, q.shape, 1)
        zero = jnp.zeros_like(q)
        for mp, qm in enumerate((jnp.where(lane < DA_HEAD_DIM, q, zero), jnp.where(lane >= DA_HEAD_DIM, q, zero))):
            sl_ref[mp] = _dot_nt(qm, k_ref[...])
            sc_ref[mp] = _dot_nt(qm, kc_ref[...])

    def numerators(sl_ref, sc_ref, mp):
        s_l = sl_ref[mp]
        s_c = sc_ref[mp]
        m = jnp.maximum(jnp.max(s_l, axis=1, keepdims=True), jnp.max(s_c, axis=1, keepdims=True))
        p_l = jnp.exp2(s_l - m)
        p_c = jnp.exp2(s_c - m)
        return p_l, p_c, jnp.sum(p_l, axis=1, keepdims=True) + jnp.sum(p_c, axis=1, keepdims=True)

    def attend(i, slot):
        p1l, p1c, t1 = numerators(*slot, 0)
        p2l, p2c, t2 = numerators(*slot, 1)
        r1 = 1.0 / t1
        r2 = lam / t2
        a_l = (p1l * r1 - p2l * r2).astype(BF)
        a_c = (p1c * r1 - p2c * r2).astype(BF)
        o = _dot(a_l, v_ref[...]) + _dot(a_c, vc_ref[...])
        ms = jnp.mean(o * o, axis=-1, keepdims=True)
        o_ref[i * tq:(i + 1) * tq, :] = (o * lax.rsqrt(ms + SUBLN_EPS) * sg_ref[...] * (1.0 - lam_init)).astype(BF)

    n = seq // tq
    scores(0, slots[0])
    for i in range(n):
        if i + 1 < n:
            scores(i + 1, slots[(i + 1) % 2])
        attend(i, slots[i % 2])


def _attention(q, k, v, kc, vc, lq1, lk1, lq2, lk2, sg, nb, seq, ctx_len, lam_init):
    n, d = q.shape
    slab = 2 * DA_HEAD_DIM
    heads = d // slab
    tq = _tile(seq, ATTN_Q_TILE)
    small = _full((1, DA_HEAD_DIM))
    lat = pl.BlockSpec((seq, slab), lambda b, h: (b, h))
    ctx = pl.BlockSpec((ctx_len, slab), lambda b, h: (b, h))
    score_scratch = [pltpu.VMEM((2, tq, seq), F32), pltpu.VMEM((2, tq, ctx_len), F32)]
    return pl.pallas_call(
        functools.partial(_attn_kernel, lam_init=lam_init, tq=tq),
        grid=(nb, heads),
        in_specs=[lat, lat, lat, ctx, ctx, small, small, small, small, _full((1, slab))],
        out_specs=lat,
        out_shape=jax.ShapeDtypeStruct((n, d), BF),
        scratch_shapes=score_scratch + score_scratch,
        compiler_params=_params("parallel", "parallel"),
        name="diff_attention",
    )(q, k, v, kc, vc, lq1, lk1, lq2, lk2, sg)


def _hyena_in_kernel(xp_ref, x_ref, xn_ref, mod_ref, g1_ref, win_ref, wconv_ref, o_ref, *, tiles_per_seq):
    tm, d = x_ref.shape
    h, keep = _halo_rows(xp_ref, x_ref, xn_ref, g1_ref, mod_ref[0], tiles_per_seq)
    for j in range(3):
        u = _dot(h, win_ref[:, j * d:(j + 1) * d]) * keep
        o_ref[:, j * d:(j + 1) * d] = _conv3(u, wconv_ref.at[:, j * d:(j + 1) * d], tm)


def _hyena_in(xs, mods, g1, w_in, w_conv, seq):
    n, d = xs.shape
    tm = _tile(seq, PROJ_ROW_TILE)
    tps = seq // tm
    return pl.pallas_call(
        functools.partial(_hyena_in_kernel, tiles_per_seq=tps),
        grid=(n // tm,),
        in_specs=_halo_specs(n, d, tm) + [_mod_spec(d, tps, None), _full((1, d)), _full((d, 3 * d)), _full((3, 3 * d))],
        out_specs=pl.BlockSpec((tm, 3 * d), lambda i: (i, 0)),
        out_shape=jax.ShapeDtypeStruct((n, 3 * d), F32),
        compiler_params=_params("parallel"),
        name="hyena_in",
    )(xs, xs, xs, mods, g1, w_in, w_conv)


def _dft_tables(seq):
    f = np.arange(seq, dtype=np.float64) + 0.5
    ang = np.pi * np.outer(f, f) / seq
    order = np.concatenate([np.arange(seq // 2), seq - 1 - np.arange(seq // 2)])
    cs = np.concatenate([np.cos(ang)[order], np.sin(ang)[order]], axis=0).astype(np.float32)
    half = np.pi * f[order] / (2 * seq)
    rot = np.stack([np.cos(half), np.sin(half)], axis=1).astype(np.float32)
    return cs, rot


def _dft_half_tables(seq):
    h = seq // 2
    th = 2.0 * np.pi * (np.arange(h, dtype=np.float64) + 0.5) / seq
    s = np.arange(h, dtype=np.float64)
    even, odd = np.outer(th, s + 0.25), np.outer(th, s + 0.75)
    fwd = np.stack([np.cos(even), np.sin(even), np.cos(odd), np.sin(odd)]).astype(np.float32)
    inv = np.ascontiguousarray(np.transpose(fwd, (0, 2, 1)))
    return fwd, inv


def _hyena_features(seq):
    t = np.linspace(0.0, 1.0, seq, dtype=np.float32)[:, None]
    w = (2.0 * math.pi * np.arange(seq, dtype=np.float32)[:, None] / seq).astype(np.float32)
    f = np.linspace(1e-4, HY_BANDS - 1, HY_BANDS, dtype=np.float32)[None, :]
    z = np.concatenate([t, np.cos(f * w), -np.sin(f * w)], axis=-1).astype(np.float32)
    zp = np.zeros((seq, V7X_LANES), np.float32)
    zp[:, :HY_EMB_DIM] = z
    return zp


def _hyena_deltas(d):
    max_decay = math.log(HY_DECAY_TARGET) / HY_FAST_DECAY_PCT
    min_decay = math.log(HY_DECAY_TARGET) / HY_SLOW_DECAY_PCT
    return np.abs(np.linspace(min_decay, max_decay, d, dtype=np.float32))[None, :].astype(np.float32)


def _hyena_filter_kernel(z_ref, f1w_ref, f1b_ref, f1f_ref, f2w_ref, f2b_ref, f2f_ref,
                         f3a_ref, f3b_ref, f3c_ref, f3d_ref, delta_ref, cs_ref, rot_ref, k_ref, hid_scr):
    seq = z_ref.shape[0]
    tc = delta_ref.shape[1]

    @pl.when(pl.program_id(0) == 0)
    def _():
        h1 = jnp.sin(f1f_ref[...] * (_dot3(z_ref[...], f1w_ref[...]) + f1b_ref[...]))
        hid_scr[...] = jnp.sin(f2f_ref[...] * (_dot3(h1, f2w_ref[...]) + f2b_ref[...]))

    hid = hid_scr[...]
    decay = jnp.exp(-z_ref[:, 0:1] * delta_ref[...])
    row = lax.broadcasted_iota(jnp.int32, (seq, 1), 0)
    cr = rot_ref[:, 0:1]
    sr = rot_ref[:, 1:2]
    for order, (fwd_ref, bwd_ref) in enumerate(((f3a_ref, f3b_ref), (f3c_ref, f3d_ref))):
        h_fwd = _dot3(hid, fwd_ref[...]) * decay
        h_bwd = jnp.where(row == 0, 0.0, _dot3(hid, bwd_ref[...]) * decay)
        pm = jnp.concatenate([h_fwd + h_bwd, h_bwd - h_fwd], axis=1).astype(BF)
        r = _dot(cs_ref[...], pm)
        c_p, c_m = r[:seq, :tc], r[:seq, tc:]
        s_p, s_m = r[seq:, :tc], r[seq:, tc:]
        k_ref[2 * order] = cr * c_p + sr * s_p
        k_ref[2 * order + 1] = cr * s_m - sr * c_m


def _hyena_filter(z, f1w, f1b, f1f, f2w, f2b, f2f, f3w, deltas, cs, rot, d):
    seq = z.shape[0]
    w = f2w.shape[0]
    tc = _tile(d, CHAN_TILE)
    nc = d // tc
    f3spec = [pl.BlockSpec((w, tc), lambda j, o=o: (0, o * nc + j)) for o in range(4)]
    return pl.pallas_call(
        _hyena_filter_kernel,
        grid=(nc,),
        in_specs=[_full((seq, V7X_LANES)), _full((V7X_LANES, w)), _full((1, w)), _full((1, w)),
                  _full((w, w)), _full((1, w)), _full((1, w))] + f3spec + [
            pl.BlockSpec((1, tc), lambda j: (0, j)),
            pl.BlockSpec((2 * seq, seq), lambda j: (0, 0), pipeline_mode=pl.Buffered(1)),
            _full((seq, 2)),
        ],
        out_specs=pl.BlockSpec((4, seq, tc), lambda j: (0, 0, j)),
        out_shape=jax.ShapeDtypeStruct((4, seq, d), F32),
        scratch_shapes=[pltpu.VMEM((seq, w), F32)],
        compiler_params=_params("arbitrary"),
        name="hyena_filter",
    )(z, f1w, f1b, f1f, f2w, f2b, f2f, f3w, f3w, f3w, f3w, deltas, cs, rot)


def _hyena_conv_kernel(*refs, gt, pieces):
    x1_refs, x2_refs, v_refs = refs[:pieces], refs[pieces:2 * pieces], refs[2 * pieces:3 * pieces]
    fwd_ref, inv_ref, k_ref, skip_ref, o_ref = refs[3 * pieces:3 * pieces + 5]
    ue_scr, uo_scr, ze_scr, zo_scr, ye_scr, yo_scr = refs[3 * pieces + 5:3 * pieces + 11]
    out_scrs = refs[3 * pieces + 11:]
    seq = o_ref.shape[0]
    half = seq // 2
    even = pl.ds(0, half, stride=2)
    odd = pl.ds(1, half, stride=2)

    def samples(piece_refs, rows):
        return jnp.concatenate([r[rows, :] for r in piece_refs], axis=1)

    def longconv(order):
        ye_scr[...] = jnp.zeros_like(ye_scr)
        yo_scr[...] = jnp.zeros_like(yo_scr)

        def body(c, carry):
            g0 = pl.multiple_of(c * gt, gt)
            rows = pl.ds(g0, gt)
            ue = ue_scr[...]
            uo = uo_scr[...]
            a = _dot(fwd_ref[0, rows, :], ue)
            b = _dot(fwd_ref[1, rows, :], ue)
            cc = _dot(fwd_ref[2, rows, :], uo)
            d = _dot(fwd_ref[3, rows, :], uo)

            def times_filter(first, ur, ui):
                kr = k_ref[2 * order, pl.ds(first + g0, gt), :]
                ki = k_ref[2 * order + 1, pl.ds(first + g0, gt), :]
                return kr * ur + ki * ui, kr * ui - ki * ur

            yra, yia = times_filter(0, a + cc, b + d)
            yrb, yib = times_filter(half, b - d, a - cc)
            ye_scr[...] += (_dot(inv_ref[0, :, rows], (yra + yib).astype(BF))
                            + _dot(inv_ref[1, :, rows], (yia + yrb).astype(BF)))
            yo_scr[...] += (_dot(inv_ref[2, :, rows], (yra - yib).astype(BF))
                            + _dot(inv_ref[3, :, rows], (yia - yrb).astype(BF)))
            return carry

        lax.fori_loop(0, half // gt, body, 0)

    scale = 1.0 / seq
    ve = samples(v_refs, even)
    vo = samples(v_refs, odd)
    ue_scr[...] = ve.astype(BF)
    uo_scr[...] = vo.astype(BF)
    longconv(0)
    ze = samples(x1_refs, even) * (ye_scr[...] * scale + ve * skip_ref[0:1, :])
    zo = samples(x1_refs, odd) * (yo_scr[...] * scale + vo * skip_ref[0:1, :])
    ze_scr[...] = ze
    zo_scr[...] = zo
    ue_scr[...] = ze.astype(BF)
    uo_scr[...] = zo.astype(BF)
    longconv(1)
    oe = samples(x2_refs, even) * (ye_scr[...] * scale + ze_scr[...] * skip_ref[1:2, :])
    oo = samples(x2_refs, odd) * (yo_scr[...] * scale + zo_scr[...] * skip_ref[1:2, :])
    for k, out_scr in enumerate(out_scrs):
        lanes = slice(k * V7X_LANES, (k + 1) * V7X_LANES)
        out_scr[even, :] = oe[:, lanes]
        out_scr[odd, :] = oo[:, lanes]
        o_ref[:, lanes] = out_scr[...].astype(BF)


def _hyena_conv(u3, fwd, inv, kspec, skip, nb, seq, d):
    tc = _tile(d, CHAN_TILE)
    nc = d // tc
    half = seq // 2
    gt = _tile(half, FREQ_TILE)
    pieces = tc // V7X_LANES
    once = dict(pipeline_mode=pl.Buffered(1))
    piece_specs = [pl.BlockSpec((seq, V7X_LANES), lambda j, b, o=o, k=k: (b, (o * nc + j) * pieces + k))
                   for o in range(3) for k in range(pieces)]
    return pl.pallas_call(
        functools.partial(_hyena_conv_kernel, gt=gt, pieces=pieces),
        scratch_shapes=[pltpu.VMEM((half, tc), BF), pltpu.VMEM((half, tc), BF)]
        + [pltpu.VMEM((half, tc), F32)] * 4 + [pltpu.VMEM((seq, V7X_LANES), F32)] * pieces,
        grid=(nc, nb),
        in_specs=piece_specs + [
            pl.BlockSpec((4, half, half), lambda j, b: (0, 0, 0), **once),
            pl.BlockSpec((4, half, half), lambda j, b: (0, 0, 0), **once),
            pl.BlockSpec((4, seq, tc), lambda j, b: (0, 0, j), **once),
            pl.BlockSpec((2, tc), lambda j, b: (0, j)),
        ],
        out_specs=pl.BlockSpec((seq, tc), lambda j, b: (b, j)),
        out_shape=jax.ShapeDtypeStruct((nb * seq, d), BF),
        compiler_params=_params("parallel", "parallel"),
        name="hyena_conv",
    )(*([u3] * (3 * pieces)), fwd, inv, kspec, skip)


def _moe_block(streams, mods, layer, fg, w_gate, w_up, w_down, nb, final=False):
    n_experts = streams[0][2].shape[0]
    group = _sc_group(nb)
    routed = []
    for xs, hn, aff, t, ctx_row in streams:
        cap = max(1, EC_CAPACITY * t // n_experts)
        posm, post, gatet = _route(aff, nb, t, cap)
        xin = None if hn.dtype == jnp.int32 else _gather(posm, hn, nb, t, cap)
        routed.append((posm, post, gatet, cap, xin))
    ys = [[] for _ in streams]
    for e0 in range(0, n_experts, group):
        xins = []
        for (xs, hn, aff, t, ctx_row), (posm, post, gatet, cap, xin) in zip(streams, routed):
            if xin is None:
                xins.append((_sc_gather(hn, posm, e0, nb, n_experts, t, cap), 0))
            else:
                xins.append((xin, e0))
        for acc, y in zip(ys, _experts(xins, w_gate, w_up, w_down, layer, e0, group)):
            acc.append(y)
    return [_combine(post, gatet, y, xs, mods, fg, nb, t, cap, ctx_row=ctx_row, final=final and ctx_row is None)
            for (xs, _, _, t, ctx_row), (_, post, gatet, cap, _), y in zip(streams, routed, ys)]


def kernel(x, c, ctx, c_ctx, w_ada, b_ada, norm1_g, norm2_g, final_g, a_w_in, a_conv, a_w_out, b_w_qkv, b_lq1, b_lk1, b_lq2, b_lk2, b_subln_g, b_w_out, c_w_in, c_conv, c_f1_w, c_f1_b, c_f1_freq, c_f2_w, c_f2_b, c_f2_freq, c_f3_w, c_skip, c_w_out, moe_router, moe_w_gate, moe_w_up, moe_w_down):
    nb, seq, d = x.shape
    ctx_len = ctx.shape[1]
    depth = w_ada.shape[0]
    assert nb < MOD_ROWS and d % (2 * DA_HEAD_DIM) == 0
    ctx_row = nb

    cvec = jnp.concatenate([c, c_ctx[None, :], jnp.zeros((MOD_ROWS - nb - 1, d), F32)], axis=0)
    mods_all = _ada(cvec, w_ada, b_ada).reshape(depth, MOD_ROWS, 6, d)

    attn_layers = [i for i in range(depth) if i % N_MIXERS == 1]
    last_ctx_read = max(attn_layers) if attn_layers else -1

    xs = x.reshape(nb * seq, d)
    cs_tok = ctx.reshape(nb * ctx_len, d)
    fg = final_g[None, :]

    for i in range(depth):
        kind, j = i % N_MIXERS, i // N_MIXERS
        update_ctx = i < last_ctx_read
        final = i == depth - 1
        mods = mods_all[i]
        g1 = norm1_g[i][None, :]
        g2 = norm2_g[i][None, :]
        wr = moe_router[i].T
        wrh, wrl = _split(wr)
        moe_w = (moe_w_gate, moe_w_up, moe_w_down)

        ctx_stream = []
        if kind == 0:
            w_in, w_out = a_w_in[j].astype(BF), a_w_out[j].astype(BF)
            if update_ctx:
                cn, chn, caff = _shortconv_layer(cs_tok, mods, g1, w_in, a_conv[j], w_out, g2, wrh, wrl,
                                                 ctx_len, ctx_row=ctx_row)
                ctx_stream = [(cn, chn, caff, ctx_len, ctx_row)]
            xn, hn, aff = _shortconv_layer(xs, mods, g1, w_in, a_conv[j], w_out, g2, wrh, wrl, seq)
        elif kind == 1:
            assert not update_ctx
            lam_init = 0.8 - 0.6 * math.exp(-0.3 * i)
            w_qkv = b_w_qkv[j].astype(BF)
            tables = [jnp.asarray(t) for t in _rope_tables(seq)]
            q, k, v = _qkv(xs, mods, g1, w_qkv, tables, seq, 3, True)
            ctab = [t[:ctx_len] for t in tables]
            kc, vc = _qkv(cs_tok, mods, g1, w_qkv[:, d:], ctab, ctx_len, 2, False, ctx_row=ctx_row)
            o = _attention(q, k, v, kc, vc, b_lq1[j][None, :], b_lk1[j][None, :], b_lq2[j][None, :],
                           b_lk2[j][None, :], b_subln_g[j][None, :], nb, seq, ctx_len, lam_init)
            xn, hn, aff = _outproj_layer(o, xs, mods, b_w_out[j].astype(BF), g2, wrh, wrl, seq)
        else:
            assert not update_ctx
            cs_np, rot_np = _dft_tables(seq)
            cs_bf = jnp.asarray(cs_np).astype(BF)
            w = c_f2_w.shape[-1]
            f1w = jnp.zeros((V7X_LANES, w), F32).at[:HY_EMB_DIM].set(c_f1_w[j])
            kspec = _hyena_filter(jnp.asarray(_hyena_features(seq)), f1w, c_f1_b[j][None, :], c_f1_freq[j][None, :],
                                  c_f2_w[j], c_f2_b[j][None, :], c_f2_freq[j][None, :], c_f3_w[j],
                                  jnp.asarray(_hyena_deltas(d)), cs_bf, jnp.asarray(rot_np), d)
            u3 = _hyena_in(xs, mods, g1, c_w_in[j].astype(BF), c_conv[j], seq)
            fwd_np, inv_np = _dft_half_tables(seq)
            z = _hyena_conv(u3, jnp.asarray(fwd_np).astype(BF), jnp.asarray(inv_np).astype(BF), kspec, c_skip[j],
                            nb, seq, d)
            xn, hn, aff = _outproj_layer(z, xs, mods, c_w_out[j].astype(BF), g2, wrh, wrl, seq)

        outs = _moe_block([(xn, hn, aff, seq, None)] + ctx_stream, mods, i, fg, *moe_w, nb, final=final)
        xs = outs[0]
        if ctx_stream:
            cs_tok = outs[1]

    return xs.reshape(nb, seq, d)
```

```python
import functools
import math

import jax
import jax.numpy as jnp
import numpy as np
from jax import lax
from jax.experimental import pallas as pl
from jax.experimental.pallas import tpu as pltpu

BF = jnp.bfloat16
F32 = jnp.float32

GRID_W = 64
DA_HEAD_DIM = 64
ROPE_AXIS_DIM = DA_HEAD_DIM // 2
ROPE_THETA = 10000.0
SUBLN_EPS = 1e-5
NORM_EPS = 1e-6
N_MIXERS = 3
EC_CAPACITY = 2
HY_EMB_DIM = 33
HY_BANDS = (HY_EMB_DIM - 1) // 2
HY_FAST_DECAY_PCT = 0.3
HY_SLOW_DECAY_PCT = 1.5
HY_DECAY_TARGET = 1e-2

V7X_LANES = 128
V7X_BF16_SUBLANES = 16
V7X_VMEM_BYTES = 64 * 2**20
VMEM_LIMIT = V7X_VMEM_BYTES - 8 * 2**20
HALO = V7X_BF16_SUBLANES
MOD_ROWS = 16
ROW_TILE = 512
PROJ_ROW_TILE = 1024
ATTN_Q_TILE = 256
CHAN_TILE = 256
ADA_COL_TILE = 768
FREQ_TILE = 512
SC_GATHER_ROWS = 64


def _params(*sem):
    return pltpu.CompilerParams(dimension_semantics=sem, vmem_limit_bytes=VMEM_LIMIT)


def _dot(a, b):
    return jnp.dot(a, b, preferred_element_type=F32)


def _dot_nt(a, b):
    return lax.dot_general(a, b, (((1,), (1,)), ((), ())), preferred_element_type=F32)


def _split(a):
    hi = a.astype(BF)
    lo = (a - hi.astype(F32)).astype(BF)
    return hi, lo


def _dot3(a, b):
    ah, al = _split(a)
    bh, bl = _split(b)
    return _dot(ah, bh) + (_dot(ah, bl) + _dot(al, bh))


def _sigmoid(a):
    return 1.0 / (1.0 + jnp.exp(-a))


def _norm_mod(x, g, shift, scale):
    ms = jnp.mean(x * x, axis=-1, keepdims=True)
    return (x * lax.rsqrt(ms + NORM_EPS) * g) * (1.0 + scale) + shift


def _tile(n, pref):
    t = min(n, pref)
    assert n % t == 0, (n, pref)
    return t


def _full(shape):
    nd = len(shape)
    return pl.BlockSpec(shape, lambda *_: (0,) * nd)


def _ada_kernel(c_ref, w_ref, b_ref, o_ref):
    c = c_ref[...]
    o_ref[0] = _dot3(c * _sigmoid(c), w_ref[0]) + b_ref[0]


def _ada(cvec, w_ada, b_ada):
    depth, d, n6 = w_ada.shape
    tn = _tile(n6, ADA_COL_TILE)
    return pl.pallas_call(
        _ada_kernel,
        grid=(depth, n6 // tn),
        in_specs=[
            _full((MOD_ROWS, d)),
            pl.BlockSpec((1, d, tn), lambda i, j: (i, 0, j)),
            pl.BlockSpec((1, 1, tn), lambda i, j: (i, 0, j)),
        ],
        out_specs=pl.BlockSpec((1, MOD_ROWS, tn), lambda i, j: (i, 0, j)),
        out_shape=jax.ShapeDtypeStruct((depth, MOD_ROWS, n6), F32),
        compiler_params=_params("parallel", "parallel"),
        name="ada",
    )(cvec, w_ada, b_ada.reshape(depth, 1, n6))


def _pack_pairs(hh):
    half = hh.shape[1] // 2
    lo = lax.shift_right_logical(pltpu.bitcast(hh[:, :half].astype(F32), jnp.uint32), jnp.uint32(16))
    hi = pltpu.bitcast(hh[:, half:].astype(F32), jnp.uint32) & jnp.uint32(0xFFFF0000)
    return pltpu.bitcast(lo | hi, jnp.int32)


def _unpack_pairs(words):
    u = pltpu.bitcast(words, jnp.uint32)
    lo = pltpu.bitcast(lax.shift_left(u, jnp.uint32(16)), F32).astype(BF)
    hi = pltpu.bitcast(u & jnp.uint32(0xFFFF0000), F32).astype(BF)
    return jnp.concatenate([lo, hi], axis=1)


def _residual_router(x, y, mod, g2_ref, wrh_ref, wrl_ref, xo_ref, hn_ref, aff_ref):
    xn = x + mod[2:3] * y
    xo_ref[...] = xn
    hn = _norm_mod(xn, g2_ref[...], mod[3:4], mod[4:5])
    hh, hl = _split(hn)
    hn_ref[...] = _pack_pairs(hh) if hn_ref.dtype == jnp.int32 else hh
    e = wrh_ref.shape[0]
    both = _dot_nt(jnp.concatenate([wrh_ref[...], wrl_ref[...]], axis=0), hh)
    logits = both[:e] + (_dot_nt(wrh_ref[...], hl) + both[e:])
    p = jnp.exp(logits - jnp.max(logits, axis=0, keepdims=True))
    aff_ref[...] = p / jnp.sum(p, axis=0, keepdims=True)


def _router_specs(n, d, e, tm, packed):
    in_specs = [_full((1, d)), _full((e, d)), _full((e, d))]
    hn_cols, hn_dtype = (d // 2, jnp.int32) if packed else (d, BF)
    out_specs = [
        pl.BlockSpec((tm, d), lambda i: (i, 0)),
        pl.BlockSpec((tm, hn_cols), lambda i: (i, 0)),
        pl.BlockSpec((e, tm), lambda i: (0, i)),
    ]
    out_shape = [
        jax.ShapeDtypeStruct((n, d), F32),
        jax.ShapeDtypeStruct((n, hn_cols), hn_dtype),
        jax.ShapeDtypeStruct((e, n), F32),
    ]
    return in_specs, out_specs, out_shape


def _halo_specs(n, d, tm):
    per = tm // HALO
    last = n // HALO - 1
    return [
        pl.BlockSpec((HALO, d), lambda i: (jnp.maximum(i * per - 1, 0), 0)),
        pl.BlockSpec((tm, d), lambda i: (i, 0)),
        pl.BlockSpec((HALO, d), lambda i: (jnp.minimum((i + 1) * per, last), 0)),
    ]


def _mod_spec(d, tiles_per_seq, ctx_row):
    if ctx_row is None:
        return pl.BlockSpec((1, 6, d), lambda i: (i // tiles_per_seq, 0, 0))
    return pl.BlockSpec((1, 6, d), lambda i: (ctx_row, 0, 0))


def _halo_rows(xp_ref, x_ref, xn_ref, g_ref, mod, tiles_per_seq):
    t = pl.program_id(0) % tiles_per_seq
    tm = x_ref.shape[0]
    g = g_ref[...]
    hs = [_norm_mod(r[...], g, mod[0:1], mod[1:2]) for r in (xp_ref, x_ref, xn_ref)]
    h = jnp.concatenate(hs, axis=0).astype(BF)
    row = lax.broadcasted_iota(jnp.int32, (tm + 2 * HALO, 1), 0)
    outside = ((row < HALO) & (t == 0)) | ((row >= HALO + tm) & (t == tiles_per_seq - 1))
    return h, jnp.where(outside, 0.0, 1.0)


def _conv3(s, w_ref, tm):
    n = s.shape[0]
    prev = pltpu.roll(s, 1, 0)[HALO:HALO + tm]
    nxt = pltpu.roll(s, n - 1, 0)[HALO:HALO + tm]
    return prev * w_ref[0:1, :] + s[HALO:HALO + tm] * w_ref[1:2, :] + nxt * w_ref[2:3, :]


def _shortconv_kernel(xp_ref, x_ref, xn_ref, mod_ref, g1_ref, win_ref, wconv_ref, wout_ref,
                      g2_ref, wrh_ref, wrl_ref, xo_ref, hn_ref, aff_ref, *, tiles_per_seq):
    tm, d = x_ref.shape
    mod = mod_ref[0]
    h, keep = _halo_rows(xp_ref, x_ref, xn_ref, g1_ref, mod, tiles_per_seq)
    gate = _dot(h[HALO:HALO + tm], win_ref[:, 0:d])
    s = _dot(h, win_ref[:, d:2 * d]) * _dot(h, win_ref[:, 2 * d:3 * d]) * keep
    z = (gate * _conv3(s, wconv_ref, tm)).astype(BF)
    y = _dot(z, wout_ref[...])
    _residual_router(x_ref[...], y, mod, g2_ref, wrh_ref, wrl_ref, xo_ref, hn_ref, aff_ref)


def _shortconv_layer(xs, mods, g1, w_in, w_conv, w_out, g2, wrh, wrl, seq, ctx_row=None):
    n, d = xs.shape
    e = wrh.shape[0]
    tm = _tile(seq, PROJ_ROW_TILE)
    tps = seq // tm
    r_in, r_out, r_shape = _router_specs(n, d, e, tm, packed=ctx_row is None)
    return pl.pallas_call(
        functools.partial(_shortconv_kernel, tiles_per_seq=tps),
        grid=(n // tm,),
        in_specs=_halo_specs(n, d, tm) + [
            _mod_spec(d, tps, ctx_row), _full((1, d)), _full((d, 3 * d)), _full((3, d)), _full((d, d)),
        ] + r_in,
        out_specs=r_out,
        out_shape=r_shape,
        compiler_params=_params("parallel"),
        name="shortconv_layer",
    )(xs, xs, xs, mods, g1, w_in, w_conv, w_out, g2, wrh, wrl)


def _outproj_kernel(o_ref, x_ref, mod_ref, wout_ref, g2_ref, wrh_ref, wrl_ref, xo_ref, hn_ref, aff_ref):
    y = _dot(o_ref[...], wout_ref[...])
    _residual_router(x_ref[...], y, mod_ref[0], g2_ref, wrh_ref, wrl_ref, xo_ref, hn_ref, aff_ref)


def _outproj_layer(o, xs, mods, w_out, g2, wrh, wrl, seq):
    n, d = xs.shape
    e = wrh.shape[0]
    tm = _tile(seq, PROJ_ROW_TILE)
    tps = seq // tm
    r_in, r_out, r_shape = _router_specs(n, d, e, tm, packed=True)
    return pl.pallas_call(
        _outproj_kernel,
        grid=(n // tm,),
        in_specs=[pl.BlockSpec((tm, d), lambda i: (i, 0)), pl.BlockSpec((tm, d), lambda i: (i, 0)),
                  _mod_spec(d, tps, None), _full((d, d))] + r_in,
        out_specs=r_out,
        out_shape=r_shape,
        compiler_params=_params("parallel"),
        name="outproj_layer",
    )(o, xs, mods, w_out, g2, wrh, wrl)


def _excl_cumsum_lanes(m):
    rows, t = m.shape
    a = lax.broadcasted_iota(jnp.int32, (V7X_LANES, V7X_LANES), 0)
    b = lax.broadcasted_iota(jnp.int32, (V7X_LANES, V7X_LANES), 1)
    tri = jnp.where(a < b, 1.0, 0.0).astype(BF)
    carry = jnp.zeros((rows, 1), F32)
    out = []
    for c in range(t // V7X_LANES):
        blk = m[:, c * V7X_LANES:(c + 1) * V7X_LANES]
        out.append(_dot(blk.astype(BF), tri) + carry)
        carry = carry + jnp.sum(blk, axis=1, keepdims=True)
    return jnp.concatenate(out, axis=1)


def _route_kernel(aff_ref, posm_ref, post_ref, gatet_ref, *, cap, nb):
    e = aff_ref.shape[0]
    t = aff_ref.shape[1] // nb
    aff = jnp.concatenate([aff_ref[:, b * t:(b + 1) * t] for b in range(nb)], axis=0)
    bits = pltpu.bitcast(aff, jnp.int32)

    def step(i, thr):
        cand = thr | jnp.left_shift(jnp.int32(1), 30 - i)
        cnt = jnp.sum(jnp.where(bits >= cand, 1.0, 0.0), axis=1, keepdims=True)
        return jnp.where(cnt >= cap, cand, thr)

    thr = lax.fori_loop(0, 31, step, jnp.zeros((nb * e, 1), jnp.int32))
    gt = jnp.where(bits > thr, 1.0, 0.0)
    eq = jnp.where(bits == thr, 1.0, 0.0)
    need = cap - jnp.sum(gt, axis=1, keepdims=True)
    sel = gt + eq * jnp.where(_excl_cumsum_lanes(eq) < need, 1.0, 0.0)
    pos = jnp.where(sel > 0.0, _excl_cumsum_lanes(sel), -1.0)
    gate = sel * aff
    posm_ref[...] = pos.astype(jnp.int32)
    pad = jnp.zeros((V7X_LANES - e, t), F32)
    for b in range(nb):
        rows = slice(b * e, (b + 1) * e)
        post_ref[b * t:(b + 1) * t, :] = jnp.concatenate([pos[rows], pad - 1.0], axis=0).T
        gatet_ref[b * t:(b + 1) * t, :] = jnp.concatenate([gate[rows], pad], axis=0).T


def _route(aff, nb, t, cap):
    e = aff.shape[0]
    posm, post, gatet = pl.pallas_call(
        functools.partial(_route_kernel, cap=cap, nb=nb),
        grid=(1,),
        in_specs=[_full((e, nb * t))],
        out_specs=[_full((nb * e, t)), _full((nb * t, V7X_LANES)), _full((nb * t, V7X_LANES))],
        out_shape=[
            jax.ShapeDtypeStruct((nb * e, t), jnp.int32),
            jax.ShapeDtypeStruct((nb * t, V7X_LANES), F32),
            jax.ShapeDtypeStruct((nb * t, V7X_LANES), F32),
        ],
        compiler_params=_params("arbitrary"),
        name="route",
    )(aff)
    return posm, post, gatet


def _sc_gather(table, posm, e0, nb, n_experts, t, cap):
    from jax.experimental.pallas import tpu_sc as plsc

    info = plsc.get_sparse_core_info()
    cores, lanes = info.num_cores, info.num_lanes
    workers = cores * info.num_subcores
    group = workers // nb
    words = table.shape[1]
    assert workers % nb == 0 and n_experts % group == 0 and cap % SC_GATHER_ROWS == 0 and t % lanes == 0

    @functools.partial(
        pl.kernel,
        mesh=plsc.VectorSubcoreMesh(core_axis_name="core", subcore_axis_name="subcore"),
        compiler_params=pltpu.CompilerParams(needs_layout_passes=False),
        out_type=jax.ShapeDtypeStruct((group * nb * cap, words), jnp.int32),
        scratch_types=[pltpu.VMEM((t,), jnp.int32), pltpu.VMEM((cap,), jnp.int32),
                       pltpu.VMEM((SC_GATHER_ROWS, words), jnp.int32), pltpu.SemaphoreType.DMA],
    )
    def gather(table_hbm, posm_hbm, out_hbm, pos_v, idx_v, rows_v, sem):
        w = lax.axis_index("subcore") * cores + lax.axis_index("core")
        e_local = w // nb
        b = w % nb
        pltpu.sync_copy(posm_hbm.at[b * n_experts + e0 + e_local], pos_v)

        @pl.loop(0, t // lanes)
        def _(i):
            p = pos_v[pl.ds(i * lanes, lanes)]
            token = lax.iota(jnp.int32, lanes) + (i * lanes + b * t)
            plsc.store_scatter(idx_v, [p], token, mask=p >= 0)

        out_base = (e_local * nb + b) * cap

        @pl.loop(0, cap // SC_GATHER_ROWS)
        def _(j):
            pltpu.async_copy(table_hbm.at[idx_v.at[pl.ds(j * SC_GATHER_ROWS, SC_GATHER_ROWS)]], rows_v, sem).wait()
            pltpu.sync_copy(rows_v, out_hbm.at[pl.ds(out_base + j * SC_GATHER_ROWS, SC_GATHER_ROWS)])

    return gather(table, posm).reshape(group, nb * cap, words)


def _sc_group(nb):
    from jax.experimental.pallas import tpu_sc as plsc

    info = plsc.get_sparse_core_info()
    return info.num_cores * info.num_subcores // nb


def _gather_kernel(posm_ref, hn_ref, o_ref, *, cap):
    e = posm_ref.shape[0]
    t = hn_ref.shape[0]
    slot = lax.broadcasted_iota(jnp.int32, (cap, t), 0)

    def body(k, carry):
        row = posm_ref[pl.ds(k, 1), :]
        onehot = jnp.where(row == slot, 1.0, 0.0).astype(BF)
        o_ref[k] = _dot(onehot, hn_ref[...]).astype(BF)
        return carry

    lax.fori_loop(0, e, body, 0)


def _gather(posm, hn, nb, t, cap):
    e = posm.shape[0] // nb
    d = hn.shape[1]
    return pl.pallas_call(
        functools.partial(_gather_kernel, cap=cap),
        grid=(nb,),
        in_specs=[pl.BlockSpec((e, t), lambda b: (b, 0)), pl.BlockSpec((t, d), lambda b: (b, 0))],
        out_specs=pl.BlockSpec((e, cap, d), lambda b: (0, b, 0)),
        out_shape=jax.ShapeDtypeStruct((e, nb * cap, d), BF),
        compiler_params=_params("parallel"),
        name="moe_gather",
    )(posm, hn)


def _expert_kernel(*refs, n_streams):
    x_refs = refs[:n_streams]
    wg_ref, wu_ref, wd_ref = refs[n_streams:n_streams + 3]
    y_refs = refs[n_streams + 3:2 * n_streams + 3]
    w_scr = refs[-1]
    w_scr[0] = wg_ref[0, 0].astype(BF)
    w_scr[1] = wu_ref[0, 0].astype(BF)
    w_scr[2] = wd_ref[0, 0].astype(BF)

    for x_ref, y_ref in zip(x_refs, y_refs):
        tr = _tile(x_ref.shape[1], ROW_TILE)

        def body(j, carry, x_ref=x_ref, y_ref=y_ref, tr=tr):
            r0 = pl.multiple_of(j * tr, tr)
            xs = x_ref[0, pl.ds(r0, tr), :]
            if xs.dtype == jnp.int32:
                xs = _unpack_pairs(xs)
            a = _dot(xs, w_scr[0])
            b = _dot(xs, w_scr[1])
            hm = (a * _sigmoid(a) * b).astype(BF)
            y_ref[0, pl.ds(r0, tr), :] = _dot(hm, w_scr[2]).astype(BF)
            return carry

        lax.fori_loop(0, x_ref.shape[1] // tr, body, 0)


def _experts(xins, w_gate, w_up, w_down, layer, e0, group):
    d = w_gate.shape[-2]
    f = w_gate.shape[-1]
    assert f == d
    wspec = pl.BlockSpec((1, 1, d, f), lambda k: (layer, e0 + k, 0, 0))
    xspecs = [pl.BlockSpec((1,) + x.shape[1:], lambda k, first=first: (first + k, 0, 0)) for x, first in xins]
    yspecs = [pl.BlockSpec((1, x.shape[1], d), lambda k: (k, 0, 0)) for x, _ in xins]
    return pl.pallas_call(
        functools.partial(_expert_kernel, n_streams=len(xins)),
        grid=(group,),
        in_specs=xspecs + [wspec, wspec, wspec],
        out_specs=yspecs,
        out_shape=[jax.ShapeDtypeStruct((group, x.shape[1], d), BF) for x, _ in xins],
        scratch_shapes=[pltpu.VMEM((3, d, f), BF)],
        compiler_params=_params("parallel"),
        name="moe_experts",
    )(*[x for x, _ in xins], w_gate, w_up, w_down)


def _combine_kernel(post_ref, gatet_ref, *refs, cap, final):
    y_refs = refs[:-4]
    x_ref, mod_ref, fg_ref, o_ref = refs[-4:]
    group, _, d = y_refs[0].shape
    e = group * len(y_refs)
    tq = x_ref.shape[0]
    pt = post_ref[...]
    gt = gatet_ref[...]
    if cap % V7X_LANES == 0:
        slot = lax.broadcasted_iota(jnp.int32, (tq, cap), 1).astype(F32)
        pieces = [jnp.where(pt[:, k:k + 1] == slot, gt[:, k:k + 1], 0.0).astype(BF) for k in range(e)]
        scat = jnp.concatenate(pieces, axis=1)
    else:
        slot = lax.broadcasted_iota(jnp.int32, (tq, e * cap), 1).astype(F32)
        scat = jnp.zeros((tq, e * cap), F32)
        for k in range(e):
            pk = pt[:, k:k + 1]
            scat = jnp.where((pk >= 0.0) & (pk + float(k * cap) == slot), gt[:, k:k + 1], scat)
        scat = scat.astype(BF)
    width = group * cap
    out = _dot(scat[:, 0:width], y_refs[0][...].reshape(width, d))
    for g in range(1, len(y_refs)):
        out = out + _dot(scat[:, g * width:(g + 1) * width], y_refs[g][...].reshape(width, d))
    xn = x_ref[...] + mod_ref[0][5:6] * out
    if final:
        ms = jnp.mean(xn * xn, axis=-1, keepdims=True)
        xn = xn * lax.rsqrt(ms + NORM_EPS) * fg_ref[...]
    o_ref[...] = xn


def _combine(post, gatet, ys, xs, mods, fg, nb, t, cap, ctx_row=None, final=False):
    n, d = xs.shape
    group = ys[0].shape[0]
    tq = _tile(t, ROW_TILE)
    tpb = t // tq
    if ctx_row is None:
        mspec = pl.BlockSpec((1, 6, d), lambda b, i: (b, 0, 0))
    else:
        mspec = pl.BlockSpec((1, 6, d), lambda b, i: (ctx_row, 0, 0))
    return pl.pallas_call(
        functools.partial(_combine_kernel, cap=cap, final=final),
        grid=(nb, tpb),
        in_specs=[
            pl.BlockSpec((tq, V7X_LANES), lambda b, i: (b * tpb + i, 0)),
            pl.BlockSpec((tq, V7X_LANES), lambda b, i: (b * tpb + i, 0)),
        ] + [pl.BlockSpec((group, cap, d), lambda b, i: (0, b, 0))] * len(ys) + [
            pl.BlockSpec((tq, d), lambda b, i: (b * tpb + i, 0)),
            mspec,
            _full((1, d)),
        ],
        out_specs=pl.BlockSpec((tq, d), lambda b, i: (b * tpb + i, 0)),
        out_shape=jax.ShapeDtypeStruct((n, d), F32),
        compiler_params=_params("parallel", "parallel"),
        name="moe_combine",
    )(post, gatet, *ys, xs, mods, fg)


def _rope_tables(seq):
    rows = seq // GRID_W
    row = np.repeat(np.arange(rows, dtype=np.float32), GRID_W)
    col = np.tile(np.arange(GRID_W, dtype=np.float32), rows)
    inv_freq = (ROPE_THETA ** (-np.arange(0, ROPE_AXIS_DIM, 2, dtype=np.float32) / ROPE_AXIS_DIM)).astype(np.float32)
    lane = np.arange(2 * DA_HEAD_DIM)
    within = lane % DA_HEAD_DIM
    axis = within // ROPE_AXIS_DIM
    half = (within % ROPE_AXIS_DIM) // (ROPE_AXIS_DIM // 2)
    idx = within % (ROPE_AXIS_DIM // 2)
    pos = np.where(axis[None, :] == 0, row[:, None], col[:, None])
    ang = (pos * inv_freq[idx][None, :]).astype(np.float32)
    cos = np.cos(ang).astype(np.float32)
    sin = np.sin(ang).astype(np.float32)
    sin_lo = np.where(half[None, :] == 1, sin, 0.0).astype(np.float32)
    sin_hi = np.where(half[None, :] == 0, -sin, 0.0).astype(np.float32)
    return cos, sin_lo, sin_hi


def _qkv_kernel(x_ref, mod_ref, g_ref, w_ref, cos_ref, sa_ref, sb_ref, *o_refs, rope):
    d = x_ref.shape[1]
    mod = mod_ref[0]
    h = _norm_mod(x_ref[...], g_ref[...], mod[0:1], mod[1:2]).astype(BF)
    slab = 2 * DA_HEAD_DIM
    shift = ROPE_AXIS_DIM // 2
    for j, o_ref in enumerate(o_refs):
        u = _dot(h, w_ref[:, j * d:(j + 1) * d])
        if rope and j < 2:
            scale = DA_HEAD_DIM ** -0.5 * math.log2(math.e) if j == 0 else 1.0
            cos, sa, sb = cos_ref[...] * scale, sa_ref[...] * scale, sb_ref[...] * scale
            for hd in range(d // slab):
                xs = u[:, hd * slab:(hd + 1) * slab]
                r = xs * cos + pltpu.roll(xs, shift, 1) * sa + pltpu.roll(xs, slab - shift, 1) * sb
                o_ref[:, hd * slab:(hd + 1) * slab] = r.astype(BF)
        else:
            o_ref[...] = u.astype(BF)


def _qkv(xs, mods, g1, w, tables, seq, nout, rope, ctx_row=None):
    n, d = xs.shape
    tm = _tile(seq, PROJ_ROW_TILE)
    tps = seq // tm
    slab = 2 * DA_HEAD_DIM
    tspec = pl.BlockSpec((tm, slab), lambda i: (i % tps, 0))
    return pl.pallas_call(
        functools.partial(_qkv_kernel, rope=rope),
        grid=(n // tm,),
        in_specs=[pl.BlockSpec((tm, d), lambda i: (i, 0)), _mod_spec(d, tps, ctx_row), _full((1, d)),
                  _full((d, nout * d)), tspec, tspec, tspec],
        out_specs=[pl.BlockSpec((tm, d), lambda i: (i, 0))] * nout,
        out_shape=[jax.ShapeDtypeStruct((n, d), BF)] * nout,
        compiler_params=_params("parallel"),
        name="attn_qkv",
    )(xs, mods, g1, w, *tables)


def _attn_kernel(q_ref, k_ref, v_ref, kc_ref, vc_ref, lq1_ref, lk1_ref, lq2_ref, lk2_ref, sg_ref, o_ref,
                 sl_a, sc_a, sl_b, sc_b, *, lam_init, tq):
    seq = q_ref.shape[0]
    lam = (jnp.exp(jnp.sum(lq1_ref[...] * lk1_ref[...], axis=1, keepdims=True))
           - jnp.exp(jnp.sum(lq2_ref[...] * lk2_ref[...], axis=1, keepdims=True)) + lam_init)
    slots = ((sl_a, sc_a), (sl_b, sc_b))

    def scores(i, slot):
        sl_ref, sc_ref = slot
        q = q_ref[i * tq:(i + 1) * tq, :]
        lane = lax.broadcasted_iota(jnp.int32, q.shape, 1)
        zero = jnp.zeros_like(q)
        for mp, qm in enumerate((jnp.where(lane < DA_HEAD_DIM, q, zero), jnp.where(lane >= DA_HEAD_DIM, q, zero))):
            sl_ref[mp] = _dot_nt(qm, k_ref[...])
            sc_ref[mp] = _dot_nt(qm, kc_ref[...])

    def numerators(sl_ref, sc_ref, mp):
        s_l = sl_ref[mp]
        s_c = sc_ref[mp]
        m = jnp.maximum(jnp.max(s_l, axis=1, keepdims=True), jnp.max(s_c, axis=1, keepdims=True))
        p_l = jnp.exp2(s_l - m)
        p_c = jnp.exp2(s_c - m)
        return p_l, p_c, jnp.sum(p_l, axis=1, keepdims=True) + jnp.sum(p_c, axis=1, keepdims=True)

    def attend(i, slot):
        p1l, p1c, t1 = numerators(*slot, 0)
        p2l, p2c, t2 = numerators(*slot, 1)
        r1 = 1.0 / t1
        r2 = lam / t2
        a_l = (p1l * r1 - p2l * r2).astype(BF)
        a_c = (p1c * r1 - p2c * r2).astype(BF)
        o = _dot(a_l, v_ref[...]) + _dot(a_c, vc_ref[...])
        ms = jnp.mean(o * o, axis=-1, keepdims=True)
        o_ref[i * tq:(i + 1) * tq, :] = (o * lax.rsqrt(ms + SUBLN_EPS) * sg_ref[...] * (1.0 - lam_init)).astype(BF)

    n = seq // tq
    scores(0, slots[0])
    for i in range(n):
        if i + 1 < n:
            scores(i + 1, slots[(i + 1) % 2])
        attend(i, slots[i % 2])


def _attention(q, k, v, kc, vc, lq1, lk1, lq2, lk2, sg, nb, seq, ctx_len, lam_init):
    n, d = q.shape
    slab = 2 * DA_HEAD_DIM
    heads = d // slab
    tq = _tile(seq, ATTN_Q_TILE)
    small = _full((1, DA_HEAD_DIM))
    lat = pl.BlockSpec((seq, slab), lambda b, h: (b, h))
    ctx = pl.BlockSpec((ctx_len, slab), lambda b, h: (b, h))
    score_scratch = [pltpu.VMEM((2, tq, seq), F32), pltpu.VMEM((2, tq, ctx_len), F32)]
    return pl.pallas_call(
        functools.partial(_attn_kernel, lam_init=lam_init, tq=tq),
        grid=(nb, heads),
        in_specs=[lat, lat, lat, ctx, ctx, small, small, small, small, _full((1, slab))],
        out_specs=lat,
        out_shape=jax.ShapeDtypeStruct((n, d), BF),
        scratch_shapes=score_scratch + score_scratch,
        compiler_params=_params("parallel", "parallel"),
        name="diff_attention",
    )(q, k, v, kc, vc, lq1, lk1, lq2, lk2, sg)


def _hyena_in_kernel(xp_ref, x_ref, xn_ref, mod_ref, g1_ref, win_ref, wconv_ref, o_ref, *, tiles_per_seq):
    tm, d = x_ref.shape
    h, keep = _halo_rows(xp_ref, x_ref, xn_ref, g1_ref, mod_ref[0], tiles_per_seq)
    for j in range(3):
        u = _dot(h, win_ref[:, j * d:(j + 1) * d]) * keep
        o_ref[:, j * d:(j + 1) * d] = _conv3(u, wconv_ref.at[:, j * d:(j + 1) * d], tm)


def _hyena_in(xs, mods, g1, w_in, w_conv, seq):
    n, d = xs.shape
    tm = _tile(seq, PROJ_ROW_TILE)
    tps = seq // tm
    return pl.pallas_call(
        functools.partial(_hyena_in_kernel, tiles_per_seq=tps),
        grid=(n // tm,),
        in_specs=_halo_specs(n, d, tm) + [_mod_spec(d, tps, None), _full((1, d)), _full((d, 3 * d)), _full((3, 3 * d))],
        out_specs=pl.BlockSpec((tm, 3 * d), lambda i: (i, 0)),
        out_shape=jax.ShapeDtypeStruct((n, 3 * d), F32),
        compiler_params=_params("parallel"),
        name="hyena_in",
    )(xs, xs, xs, mods, g1, w_in, w_conv)


def _dft_half_shift(seq):
    order = np.concatenate([np.arange(seq // 2), seq - 1 - np.arange(seq // 2)])
    half = np.pi * (order + 0.5) / (2 * seq)
    return np.stack([np.cos(half), np.sin(half)], axis=1).astype(np.float32)


def _dft_half_tables(seq):
    h = seq // 2
    th = 2.0 * np.pi * (np.arange(h, dtype=np.float64) + 0.5) / seq
    s = np.arange(h, dtype=np.float64)
    even, odd = np.outer(th, s + 0.25), np.outer(th, s + 0.75)
    fwd = np.stack([np.cos(even), np.sin(even), np.cos(odd), np.sin(odd)]).astype(np.float32)
    inv = np.ascontiguousarray(np.transpose(fwd, (0, 2, 1)))
    return fwd, inv


def _hyena_features(seq):
    t = np.linspace(0.0, 1.0, seq, dtype=np.float32)[:, None]
    w = (2.0 * math.pi * np.arange(seq, dtype=np.float32)[:, None] / seq).astype(np.float32)
    f = np.linspace(1e-4, HY_BANDS - 1, HY_BANDS, dtype=np.float32)[None, :]
    z = np.concatenate([t, np.cos(f * w), -np.sin(f * w)], axis=-1).astype(np.float32)
    zp = np.zeros((seq, V7X_LANES), np.float32)
    zp[:, :HY_EMB_DIM] = z
    return zp


def _hyena_deltas(d):
    max_decay = math.log(HY_DECAY_TARGET) / HY_FAST_DECAY_PCT
    min_decay = math.log(HY_DECAY_TARGET) / HY_SLOW_DECAY_PCT
    return np.abs(np.linspace(min_decay, max_decay, d, dtype=np.float32))[None, :].astype(np.float32)


def _hyena_filter_kernel(z_ref, f1w_ref, f1b_ref, f1f_ref, f2w_ref, f2b_ref, f2f_ref,
                         f3a_ref, f3b_ref, f3c_ref, f3d_ref, delta_ref, fwd_ref, rot_ref, k_ref, hid_scr, *pm_scrs):
    seq = z_ref.shape[0]
    half = seq // 2
    tc = delta_ref.shape[1]
    even = pl.ds(0, half, stride=2)
    odd = pl.ds(1, half, stride=2)

    @pl.when(pl.program_id(0) == 0)
    def _():
        h1 = jnp.sin(f1f_ref[...] * (_dot3(z_ref[...], f1w_ref[...]) + f1b_ref[...]))
        hid_scr[...] = jnp.sin(f2f_ref[...] * (_dot3(h1, f2w_ref[...]) + f2b_ref[...]))

    hid = hid_scr[...]
    decay = jnp.exp(-z_ref[:, 0:1] * delta_ref[...])
    row = lax.broadcasted_iota(jnp.int32, (seq, 1), 0)
    cr = rot_ref[:, 0:1]
    sr = rot_ref[:, 1:2]
    for order, (f3_fwd_ref, f3_bwd_ref) in enumerate(((f3a_ref, f3b_ref), (f3c_ref, f3d_ref))):
        h_fwd = _dot3(hid, f3_fwd_ref[...]) * decay
        h_bwd = jnp.where(row == 0, 0.0, _dot3(hid, f3_bwd_ref[...]) * decay)
        pm = jnp.concatenate([h_fwd + h_bwd, h_bwd - h_fwd], axis=1)
        for k, scr in enumerate(pm_scrs):
            scr[...] = pm[:, k * V7X_LANES:(k + 1) * V7X_LANES]
        x_even = jnp.concatenate([scr[even, :] for scr in pm_scrs], axis=1).astype(BF)
        x_odd = jnp.concatenate([scr[odd, :] for scr in pm_scrs], axis=1).astype(BF)
        a = _dot(fwd_ref[0], x_even)
        b = _dot(fwd_ref[1], x_even)
        c = _dot(fwd_ref[2], x_odd)
        s = _dot(fwd_ref[3], x_odd)
        cos_sum = jnp.concatenate([a + c, b - s], axis=0)
        sin_sum = jnp.concatenate([b + s, a - c], axis=0)
        k_ref[2 * order] = cr * cos_sum[:, :tc] + sr * sin_sum[:, :tc]
        k_ref[2 * order + 1] = cr * sin_sum[:, tc:] - sr * cos_sum[:, tc:]


def _hyena_filter(z, f1w, f1b, f1f, f2w, f2b, f2f, f3w, deltas, fwd, rot, d):
    seq = z.shape[0]
    half = seq // 2
    w = f2w.shape[0]
    tc = _tile(d, CHAN_TILE)
    nc = d // tc
    f3spec = [pl.BlockSpec((w, tc), lambda j, o=o: (0, o * nc + j)) for o in range(4)]
    return pl.pallas_call(
        _hyena_filter_kernel,
        grid=(nc,),
        in_specs=[_full((seq, V7X_LANES)), _full((V7X_LANES, w)), _full((1, w)), _full((1, w)),
                  _full((w, w)), _full((1, w)), _full((1, w))] + f3spec + [
            pl.BlockSpec((1, tc), lambda j: (0, j)),
            pl.BlockSpec((4, half, half), lambda j: (0, 0, 0), pipeline_mode=pl.Buffered(1)),
            _full((seq, 2)),
        ],
        out_specs=pl.BlockSpec((4, seq, tc), lambda j: (0, 0, j)),
        out_shape=jax.ShapeDtypeStruct((4, seq, d), F32),
        scratch_shapes=[pltpu.VMEM((seq, w), F32)] + [pltpu.VMEM((seq, V7X_LANES), F32)] * (2 * tc // V7X_LANES),
        compiler_params=_params("arbitrary"),
        name="hyena_filter",
    )(z, f1w, f1b, f1f, f2w, f2b, f2f, f3w, f3w, f3w, f3w, deltas, fwd, rot)


def _hyena_conv_kernel(*refs, gt, pieces):
    x1_refs, x2_refs, v_refs = refs[:pieces], refs[pieces:2 * pieces], refs[2 * pieces:3 * pieces]
    fwd_ref, inv_ref, k_ref, skip_ref, o_ref = refs[3 * pieces:3 * pieces + 5]
    ue_scr, uo_scr, ze_scr, zo_scr, ye_scr, yo_scr = refs[3 * pieces + 5:3 * pieces + 11]
    out_scrs = refs[3 * pieces + 11:]
    seq = o_ref.shape[0]
    half = seq // 2
    even = pl.ds(0, half, stride=2)
    odd = pl.ds(1, half, stride=2)

    def samples(piece_refs, rows):
        return jnp.concatenate([r[rows, :] for r in piece_refs], axis=1)

    def longconv(order):
        ye_scr[...] = jnp.zeros_like(ye_scr)
        yo_scr[...] = jnp.zeros_like(yo_scr)

        def body(c, carry):
            g0 = pl.multiple_of(c * gt, gt)
            rows = pl.ds(g0, gt)
            ue = ue_scr[...]
            uo = uo_scr[...]
            a = _dot(fwd_ref[0, rows, :], ue)
            b = _dot(fwd_ref[1, rows, :], ue)
            cc = _dot(fwd_ref[2, rows, :], uo)
            d = _dot(fwd_ref[3, rows, :], uo)

            def times_filter(first, ur, ui):
                kr = k_ref[2 * order, pl.ds(first + g0, gt), :]
                ki = k_ref[2 * order + 1, pl.ds(first + g0, gt), :]
                return kr * ur + ki * ui, kr * ui - ki * ur

            yra, yia = times_filter(0, a + cc, b + d)
            yrb, yib = times_filter(half, b - d, a - cc)
            ye_scr[...] += (_dot(inv_ref[0, :, rows], (yra + yib).astype(BF))
                            + _dot(inv_ref[1, :, rows], (yia + yrb).astype(BF)))
            yo_scr[...] += (_dot(inv_ref[2, :, rows], (yra - yib).astype(BF))
                            + _dot(inv_ref[3, :, rows], (yia - yrb).astype(BF)))
            return carry

        lax.fori_loop(0, half // gt, body, 0)

    scale = 1.0 / seq
    ve = samples(v_refs, even)
    vo = samples(v_refs, odd)
    ue_scr[...] = ve.astype(BF)
    uo_scr[...] = vo.astype(BF)
    longconv(0)
    ze = samples(x1_refs, even) * (ye_scr[...] * scale + ve * skip_ref[0:1, :])
    zo = samples(x1_refs, odd) * (yo_scr[...] * scale + vo * skip_ref[0:1, :])
    ze_scr[...] = ze
    zo_scr[...] = zo
    ue_scr[...] = ze.astype(BF)
    uo_scr[...] = zo.astype(BF)
    longconv(1)
    oe = samples(x2_refs, even) * (ye_scr[...] * scale + ze_scr[...] * skip_ref[1:2, :])
    oo = samples(x2_refs, odd) * (yo_scr[...] * scale + zo_scr[...] * skip_ref[1:2, :])
    for k, out_scr in enumerate(out_scrs):
        lanes = slice(k * V7X_LANES, (k + 1) * V7X_LANES)
        out_scr[even, :] = oe[:, lanes]
        out_scr[odd, :] = oo[:, lanes]
        o_ref[:, lanes] = out_scr[...].astype(BF)


def _hyena_conv(u3, fwd, inv, kspec, skip, nb, seq, d):
    tc = _tile(d, CHAN_TILE)
    nc = d // tc
    half = seq // 2
    gt = _tile(half, FREQ_TILE)
    pieces = tc // V7X_LANES
    once = dict(pipeline_mode=pl.Buffered(1))
    piece_specs = [pl.BlockSpec((seq, V7X_LANES), lambda j, b, o=o, k=k: (b, (o * nc + j) * pieces + k))
                   for o in range(3) for k in range(pieces)]
    return pl.pallas_call(
        functools.partial(_hyena_conv_kernel, gt=gt, pieces=pieces),
        scratch_shapes=[pltpu.VMEM((half, tc), BF), pltpu.VMEM((half, tc), BF)]
        + [pltpu.VMEM((half, tc), F32)] * 4 + [pltpu.VMEM((seq, V7X_LANES), F32)] * pieces,
        grid=(nc, nb),
        in_specs=piece_specs + [
            pl.BlockSpec((4, half, half), lambda j, b: (0, 0, 0), **once),
            pl.BlockSpec((4, half, half), lambda j, b: (0, 0, 0), **once),
            pl.BlockSpec((4, seq, tc), lambda j, b: (0, 0, j), **once),
            pl.BlockSpec((2, tc), lambda j, b: (0, j)),
        ],
        out_specs=pl.BlockSpec((seq, tc), lambda j, b: (b, j)),
        out_shape=jax.ShapeDtypeStruct((nb * seq, d), BF),
        compiler_params=_params("parallel", "parallel"),
        name="hyena_conv",
    )(*([u3] * (3 * pieces)), fwd, inv, kspec, skip)


def _moe_block(streams, mods, layer, fg, w_gate, w_up, w_down, nb, final=False):
    n_experts = streams[0][2].shape[0]
    group = _sc_group(nb)
    routed = []
    for xs, hn, aff, t, ctx_row in streams:
        cap = max(1, EC_CAPACITY * t // n_experts)
        posm, post, gatet = _route(aff, nb, t, cap)
        xin = None if hn.dtype == jnp.int32 else _gather(posm, hn, nb, t, cap)
        routed.append((posm, post, gatet, cap, xin))
    ys = [[] for _ in streams]
    for e0 in range(0, n_experts, group):
        xins = []
        for (xs, hn, aff, t, ctx_row), (posm, post, gatet, cap, xin) in zip(streams, routed):
            if xin is None:
                xins.append((_sc_gather(hn, posm, e0, nb, n_experts, t, cap), 0))
            else:
                xins.append((xin, e0))
        for acc, y in zip(ys, _experts(xins, w_gate, w_up, w_down, layer, e0, group)):
            acc.append(y)
    return [_combine(post, gatet, y, xs, mods, fg, nb, t, cap, ctx_row=ctx_row, final=final and ctx_row is None)
            for (xs, _, _, t, ctx_row), (_, post, gatet, cap, _), y in zip(streams, routed, ys)]


def kernel(x, c, ctx, c_ctx, w_ada, b_ada, norm1_g, norm2_g, final_g, a_w_in, a_conv, a_w_out, b_w_qkv, b_lq1, b_lk1, b_lq2, b_lk2, b_subln_g, b_w_out, c_w_in, c_conv, c_f1_w, c_f1_b, c_f1_freq, c_f2_w, c_f2_b, c_f2_freq, c_f3_w, c_skip, c_w_out, moe_router, moe_w_gate, moe_w_up, moe_w_down):
    nb, seq, d = x.shape
    ctx_len = ctx.shape[1]
    depth = w_ada.shape[0]
    assert nb < MOD_ROWS and d % (2 * DA_HEAD_DIM) == 0
    ctx_row = nb

    cvec = jnp.concatenate([c, c_ctx[None, :], jnp.zeros((MOD_ROWS - nb - 1, d), F32)], axis=0)
    mods_all = _ada(cvec, w_ada, b_ada).reshape(depth, MOD_ROWS, 6, d)

    attn_layers = [i for i in range(depth) if i % N_MIXERS == 1]
    last_ctx_read = max(attn_layers) if attn_layers else -1

    xs = x.reshape(nb * seq, d)
    cs_tok = ctx.reshape(nb * ctx_len, d)
    fg = final_g[None, :]

    for i in range(depth):
        kind, j = i % N_MIXERS, i // N_MIXERS
        update_ctx = i < last_ctx_read
        final = i == depth - 1
        mods = mods_all[i]
        g1 = norm1_g[i][None, :]
        g2 = norm2_g[i][None, :]
        wr = moe_router[i].T
        wrh, wrl = _split(wr)
        moe_w = (moe_w_gate, moe_w_up, moe_w_down)

        ctx_stream = []
        if kind == 0:
            w_in, w_out = a_w_in[j].astype(BF), a_w_out[j].astype(BF)
            if update_ctx:
                cn, chn, caff = _shortconv_layer(cs_tok, mods, g1, w_in, a_conv[j], w_out, g2, wrh, wrl,
                                                 ctx_len, ctx_row=ctx_row)
                ctx_stream = [(cn, chn, caff, ctx_len, ctx_row)]
            xn, hn, aff = _shortconv_layer(xs, mods, g1, w_in, a_conv[j], w_out, g2, wrh, wrl, seq)
        elif kind == 1:
            assert not update_ctx
            lam_init = 0.8 - 0.6 * math.exp(-0.3 * i)
            w_qkv = b_w_qkv[j].astype(BF)
            tables = [jnp.asarray(t) for t in _rope_tables(seq)]
            q, k, v = _qkv(xs, mods, g1, w_qkv, tables, seq, 3, True)
            ctab = [t[:ctx_len] for t in tables]
            kc, vc = _qkv(cs_tok, mods, g1, w_qkv[:, d:], ctab, ctx_len, 2, False, ctx_row=ctx_row)
            o = _attention(q, k, v, kc, vc, b_lq1[j][None, :], b_lk1[j][None, :], b_lq2[j][None, :],
                           b_lk2[j][None, :], b_subln_g[j][None, :], nb, seq, ctx_len, lam_init)
            xn, hn, aff = _outproj_layer(o, xs, mods, b_w_out[j].astype(BF), g2, wrh, wrl, seq)
        else:
            assert not update_ctx
            fwd_np, inv_np = _dft_half_tables(seq)
            fwd, inv = jnp.asarray(fwd_np).astype(BF), jnp.asarray(inv_np).astype(BF)
            w = c_f2_w.shape[-1]
            f1w = jnp.zeros((V7X_LANES, w), F32).at[:HY_EMB_DIM].set(c_f1_w[j])
            kspec = _hyena_filter(jnp.asarray(_hyena_features(seq)), f1w, c_f1_b[j][None, :], c_f1_freq[j][None, :],
                                  c_f2_w[j], c_f2_b[j][None, :], c_f2_freq[j][None, :], c_f3_w[j],
                                  jnp.asarray(_hyena_deltas(d)), fwd, jnp.asarray(_dft_half_shift(seq)), d)
            u3 = _hyena_in(xs, mods, g1, c_w_in[j].astype(BF), c_conv[j], seq)
            z = _hyena_conv(u3, fwd, inv, kspec, c_skip[j], nb, seq, d)
            xn, hn, aff = _outproj_layer(z, xs, mods, c_w_out[j].astype(BF), g2, wrh, wrl, seq)

        outs = _moe_block([(xn, hn, aff, seq, None)] + ctx_stream, mods, i, fg, *moe_w, nb, final=final)
        xs = outs[0]
        if ctx_stream:
            cs_tok = outs[1]

    return xs.reshape(nb, seq, d)
```

```python
import functools
import math

import jax
import jax.numpy as jnp
import numpy as np
from jax import lax
from jax.experimental import pallas as pl
from jax.experimental.pallas import tpu as pltpu

BF = jnp.bfloat16
F32 = jnp.float32

GRID_W = 64
DA_HEAD_DIM = 64
ROPE_AXIS_DIM = DA_HEAD_DIM // 2
ROPE_THETA = 10000.0
SUBLN_EPS = 1e-5
NORM_EPS = 1e-6
N_MIXERS = 3
EC_CAPACITY = 2
HY_EMB_DIM = 33
HY_BANDS = (HY_EMB_DIM - 1) // 2
HY_FAST_DECAY_PCT = 0.3
HY_SLOW_DECAY_PCT = 1.5
HY_DECAY_TARGET = 1e-2

V7X_LANES = 128
V7X_BF16_SUBLANES = 16
V7X_VMEM_BYTES = 64 * 2**20
VMEM_LIMIT = V7X_VMEM_BYTES - 8 * 2**20
HALO = V7X_BF16_SUBLANES
MOD_ROWS = 16
ROW_TILE = 512
PROJ_ROW_TILE = 1024
ATTN_Q_TILE = 256
CHAN_TILE = 256
ADA_COL_TILE = 768
FREQ_TILE = 512
SC_GATHER_ROWS = 64


def _params(*sem):
    return pltpu.CompilerParams(dimension_semantics=sem, vmem_limit_bytes=VMEM_LIMIT)


def _dot(a, b):
    return jnp.dot(a, b, preferred_element_type=F32)


def _dot_nt(a, b):
    return lax.dot_general(a, b, (((1,), (1,)), ((), ())), preferred_element_type=F32)


def _split(a):
    hi = a.astype(BF)
    lo = (a - hi.astype(F32)).astype(BF)
    return hi, lo


def _dot3(a, b):
    ah, al = _split(a)
    bh, bl = _split(b)
    return _dot(ah, bh) + (_dot(ah, bl) + _dot(al, bh))


def _sigmoid(a):
    return 1.0 / (1.0 + jnp.exp(-a))


def _norm_mod(x, g, shift, scale):
    ms = jnp.mean(x * x, axis=-1, keepdims=True)
    return (x * lax.rsqrt(ms + NORM_EPS) * g) * (1.0 + scale) + shift


def _tile(n, pref):
    t = min(n, pref)
    assert n % t == 0, (n, pref)
    return t


def _full(shape):
    nd = len(shape)
    return pl.BlockSpec(shape, lambda *_: (0,) * nd)


def _ada_kernel(c_ref, w_ref, b_ref, o_ref):
    c = c_ref[...]
    o_ref[0] = _dot3(c * _sigmoid(c), w_ref[0]) + b_ref[0]


def _ada(cvec, w_ada, b_ada):
    depth, d, n6 = w_ada.shape
    tn = _tile(n6, ADA_COL_TILE)
    return pl.pallas_call(
        _ada_kernel,
        grid=(depth, n6 // tn),
        in_specs=[
            _full((MOD_ROWS, d)),
            pl.BlockSpec((1, d, tn), lambda i, j: (i, 0, j)),
            pl.BlockSpec((1, 1, tn), lambda i, j: (i, 0, j)),
        ],
        out_specs=pl.BlockSpec((1, MOD_ROWS, tn), lambda i, j: (i, 0, j)),
        out_shape=jax.ShapeDtypeStruct((depth, MOD_ROWS, n6), F32),
        compiler_params=_params("parallel", "parallel"),
        name="ada",
    )(cvec, w_ada, b_ada.reshape(depth, 1, n6))


def _pack_pairs(hh):
    half = hh.shape[1] // 2
    lo = lax.shift_right_logical(pltpu.bitcast(hh[:, :half].astype(F32), jnp.uint32), jnp.uint32(16))
    hi = pltpu.bitcast(hh[:, half:].astype(F32), jnp.uint32) & jnp.uint32(0xFFFF0000)
    return pltpu.bitcast(lo | hi, jnp.int32)


def _unpack_pairs(words):
    u = pltpu.bitcast(words, jnp.uint32)
    lo = pltpu.bitcast(lax.shift_left(u, jnp.uint32(16)), F32).astype(BF)
    hi = pltpu.bitcast(u & jnp.uint32(0xFFFF0000), F32).astype(BF)
    return jnp.concatenate([lo, hi], axis=1)


def _residual_router(x, y, mod, g2_ref, wrh_ref, wrl_ref, xo_ref, hn_ref, aff_ref):
    xn = x + mod[2:3] * y
    xo_ref[...] = xn
    hn = _norm_mod(xn, g2_ref[...], mod[3:4], mod[4:5])
    hh, hl = _split(hn)
    hn_ref[...] = _pack_pairs(hh) if hn_ref.dtype == jnp.int32 else hh
    e = wrh_ref.shape[0]
    both = _dot_nt(jnp.concatenate([wrh_ref[...], wrl_ref[...]], axis=0), hh)
    logits = both[:e] + (_dot_nt(wrh_ref[...], hl) + both[e:])
    p = jnp.exp(logits - jnp.max(logits, axis=0, keepdims=True))
    aff_ref[...] = p / jnp.sum(p, axis=0, keepdims=True)


def _router_specs(n, d, e, tm, packed):
    in_specs = [_full((1, d)), _full((e, d)), _full((e, d))]
    hn_cols, hn_dtype = (d // 2, jnp.int32) if packed else (d, BF)
    out_specs = [
        pl.BlockSpec((tm, d), lambda i: (i, 0)),
        pl.BlockSpec((tm, hn_cols), lambda i: (i, 0)),
        pl.BlockSpec((e, tm), lambda i: (0, i)),
    ]
    out_shape = [
        jax.ShapeDtypeStruct((n, d), F32),
        jax.ShapeDtypeStruct((n, hn_cols), hn_dtype),
        jax.ShapeDtypeStruct((e, n), F32),
    ]
    return in_specs, out_specs, out_shape


def _halo_specs(n, d, tm):
    per = tm // HALO
    last = n // HALO - 1
    return [
        pl.BlockSpec((HALO, d), lambda i: (jnp.maximum(i * per - 1, 0), 0)),
        pl.BlockSpec((tm, d), lambda i: (i, 0)),
        pl.BlockSpec((HALO, d), lambda i: (jnp.minimum((i + 1) * per, last), 0)),
    ]


def _mod_spec(d, tiles_per_seq, ctx_row):
    if ctx_row is None:
        return pl.BlockSpec((1, 6, d), lambda i: (i // tiles_per_seq, 0, 0))
    return pl.BlockSpec((1, 6, d), lambda i: (ctx_row, 0, 0))


def _halo_rows(xp_ref, x_ref, xn_ref, g_ref, mod, tiles_per_seq):
    t = pl.program_id(0) % tiles_per_seq
    tm = x_ref.shape[0]
    g = g_ref[...]
    hs = [_norm_mod(r[...], g, mod[0:1], mod[1:2]) for r in (xp_ref, x_ref, xn_ref)]
    h = jnp.concatenate(hs, axis=0).astype(BF)
    row = lax.broadcasted_iota(jnp.int32, (tm + 2 * HALO, 1), 0)
    outside = ((row < HALO) & (t == 0)) | ((row >= HALO + tm) & (t == tiles_per_seq - 1))
    return h, jnp.where(outside, 0.0, 1.0)


def _conv3(s, w_ref, tm):
    n = s.shape[0]
    prev = pltpu.roll(s, 1, 0)[HALO:HALO + tm]
    nxt = pltpu.roll(s, n - 1, 0)[HALO:HALO + tm]
    return prev * w_ref[0:1, :] + s[HALO:HALO + tm] * w_ref[1:2, :] + nxt * w_ref[2:3, :]


def _shortconv_kernel(xp_ref, x_ref, xn_ref, mod_ref, g1_ref, win_ref, wconv_ref, wout_ref,
                      g2_ref, wrh_ref, wrl_ref, xo_ref, hn_ref, aff_ref, *, tiles_per_seq):
    tm, d = x_ref.shape
    mod = mod_ref[0]
    h, keep = _halo_rows(xp_ref, x_ref, xn_ref, g1_ref, mod, tiles_per_seq)
    gate = _dot(h[HALO:HALO + tm], win_ref[:, 0:d])
    s = _dot(h, win_ref[:, d:2 * d]) * _dot(h, win_ref[:, 2 * d:3 * d]) * keep
    z = (gate * _conv3(s, wconv_ref, tm)).astype(BF)
    y = _dot(z, wout_ref[...])
    _residual_router(x_ref[...], y, mod, g2_ref, wrh_ref, wrl_ref, xo_ref, hn_ref, aff_ref)


def _shortconv_layer(xs, mods, g1, w_in, w_conv, w_out, g2, wrh, wrl, seq, ctx_row=None):
    n, d = xs.shape
    e = wrh.shape[0]
    tm = _tile(seq, PROJ_ROW_TILE)
    tps = seq // tm
    r_in, r_out, r_shape = _router_specs(n, d, e, tm, packed=ctx_row is None)
    return pl.pallas_call(
        functools.partial(_shortconv_kernel, tiles_per_seq=tps),
        grid=(n // tm,),
        in_specs=_halo_specs(n, d, tm) + [
            _mod_spec(d, tps, ctx_row), _full((1, d)), _full((d, 3 * d)), _full((3, d)), _full((d, d)),
        ] + r_in,
        out_specs=r_out,
        out_shape=r_shape,
        compiler_params=_params("parallel"),
        name="shortconv_layer",
    )(xs, xs, xs, mods, g1, w_in, w_conv, w_out, g2, wrh, wrl)


def _outproj_kernel(o_ref, x_ref, mod_ref, wout_ref, g2_ref, wrh_ref, wrl_ref, xo_ref, hn_ref, aff_ref):
    y = _dot(o_ref[...], wout_ref[...])
    _residual_router(x_ref[...], y, mod_ref[0], g2_ref, wrh_ref, wrl_ref, xo_ref, hn_ref, aff_ref)


def _outproj_layer(o, xs, mods, w_out, g2, wrh, wrl, seq):
    n, d = xs.shape
    e = wrh.shape[0]
    tm = _tile(seq, PROJ_ROW_TILE)
    tps = seq // tm
    r_in, r_out, r_shape = _router_specs(n, d, e, tm, packed=True)
    return pl.pallas_call(
        _outproj_kernel,
        grid=(n // tm,),
        in_specs=[pl.BlockSpec((tm, d), lambda i: (i, 0)), pl.BlockSpec((tm, d), lambda i: (i, 0)),
                  _mod_spec(d, tps, None), _full((d, d))] + r_in,
        out_specs=r_out,
        out_shape=r_shape,
        compiler_params=_params("parallel"),
        name="outproj_layer",
    )(o, xs, mods, w_out, g2, wrh, wrl)


def _excl_cumsum_lanes(m):
    rows, t = m.shape
    a = lax.broadcasted_iota(jnp.int32, (V7X_LANES, V7X_LANES), 0)
    b = lax.broadcasted_iota(jnp.int32, (V7X_LANES, V7X_LANES), 1)
    tri = jnp.where(a < b, 1.0, 0.0).astype(BF)
    carry = jnp.zeros((rows, 1), F32)
    out = []
    for c in range(t // V7X_LANES):
        blk = m[:, c * V7X_LANES:(c + 1) * V7X_LANES]
        out.append(_dot(blk.astype(BF), tri) + carry)
        carry = carry + jnp.sum(blk, axis=1, keepdims=True)
    return jnp.concatenate(out, axis=1)


def _route_kernel(aff_ref, posm_ref, post_ref, gatet_ref, *, cap, nb):
    e = aff_ref.shape[0]
    t = aff_ref.shape[1] // nb
    aff = jnp.concatenate([aff_ref[:, b * t:(b + 1) * t] for b in range(nb)], axis=0)
    bits = pltpu.bitcast(aff, jnp.int32)

    def step(i, thr):
        cand = thr | jnp.left_shift(jnp.int32(1), 30 - i)
        cnt = jnp.sum(jnp.where(bits >= cand, 1.0, 0.0), axis=1, keepdims=True)
        return jnp.where(cnt >= cap, cand, thr)

    thr = lax.fori_loop(0, 31, step, jnp.zeros((nb * e, 1), jnp.int32))
    gt = jnp.where(bits > thr, 1.0, 0.0)
    eq = jnp.where(bits == thr, 1.0, 0.0)
    need = cap - jnp.sum(gt, axis=1, keepdims=True)
    sel = gt + eq * jnp.where(_excl_cumsum_lanes(eq) < need, 1.0, 0.0)
    pos = jnp.where(sel > 0.0, _excl_cumsum_lanes(sel), -1.0)
    gate = sel * aff
    posm_ref[...] = pos.astype(jnp.int32)
    pad = jnp.zeros((V7X_LANES - e, t), F32)
    for b in range(nb):
        rows = slice(b * e, (b + 1) * e)
        post_ref[b * t:(b + 1) * t, :] = jnp.concatenate([pos[rows], pad - 1.0], axis=0).T
        gatet_ref[b * t:(b + 1) * t, :] = jnp.concatenate([gate[rows], pad], axis=0).T


def _route(aff, nb, t, cap):
    e = aff.shape[0]
    posm, post, gatet = pl.pallas_call(
        functools.partial(_route_kernel, cap=cap, nb=nb),
        grid=(1,),
        in_specs=[_full((e, nb * t))],
        out_specs=[_full((nb * e, t)), _full((nb * t, V7X_LANES)), _full((nb * t, V7X_LANES))],
        out_shape=[
            jax.ShapeDtypeStruct((nb * e, t), jnp.int32),
            jax.ShapeDtypeStruct((nb * t, V7X_LANES), F32),
            jax.ShapeDtypeStruct((nb * t, V7X_LANES), F32),
        ],
        compiler_params=_params("arbitrary"),
        name="route",
    )(aff)
    return posm, post, gatet


def _sc_gather(table, posm, e0, nb, n_experts, t, cap):
    from jax.experimental.pallas import tpu_sc as plsc

    info = plsc.get_sparse_core_info()
    cores, lanes = info.num_cores, info.num_lanes
    workers = cores * info.num_subcores
    group = workers // nb
    words = table.shape[1]
    assert workers % nb == 0 and n_experts % group == 0 and cap % SC_GATHER_ROWS == 0 and t % lanes == 0

    @functools.partial(
        pl.kernel,
        mesh=plsc.VectorSubcoreMesh(core_axis_name="core", subcore_axis_name="subcore"),
        compiler_params=pltpu.CompilerParams(needs_layout_passes=False),
        out_type=jax.ShapeDtypeStruct((group * nb * cap, words), jnp.int32),
        scratch_types=[pltpu.VMEM((t,), jnp.int32), pltpu.VMEM((cap,), jnp.int32),
                       pltpu.VMEM((SC_GATHER_ROWS, words), jnp.int32), pltpu.SemaphoreType.DMA],
    )
    def gather(table_hbm, posm_hbm, out_hbm, pos_v, idx_v, rows_v, sem):
        w = lax.axis_index("subcore") * cores + lax.axis_index("core")
        e_local = w // nb
        b = w % nb
        pltpu.sync_copy(posm_hbm.at[b * n_experts + e0 + e_local], pos_v)

        @pl.loop(0, t // lanes)
        def _(i):
            p = pos_v[pl.ds(i * lanes, lanes)]
            token = lax.iota(jnp.int32, lanes) + (i * lanes + b * t)
            plsc.store_scatter(idx_v, [p], token, mask=p >= 0)

        out_base = (e_local * nb + b) * cap

        @pl.loop(0, cap // SC_GATHER_ROWS)
        def _(j):
            pltpu.async_copy(table_hbm.at[idx_v.at[pl.ds(j * SC_GATHER_ROWS, SC_GATHER_ROWS)]], rows_v, sem).wait()
            pltpu.sync_copy(rows_v, out_hbm.at[pl.ds(out_base + j * SC_GATHER_ROWS, SC_GATHER_ROWS)])

    return gather(table, posm).reshape(group, nb * cap, words)


def _sc_group(nb):
    from jax.experimental.pallas import tpu_sc as plsc

    info = plsc.get_sparse_core_info()
    return info.num_cores * info.num_subcores // nb


def _gather_kernel(posm_ref, hn_ref, o_ref, *, cap):
    e = posm_ref.shape[0]
    t = hn_ref.shape[0]
    slot = lax.broadcasted_iota(jnp.int32, (cap, t), 0)

    def body(k, carry):
        row = posm_ref[pl.ds(k, 1), :]
        onehot = jnp.where(row == slot, 1.0, 0.0).astype(BF)
        o_ref[k] = _dot(onehot, hn_ref[...]).astype(BF)
        return carry

    lax.fori_loop(0, e, body, 0)


def _gather(posm, hn, nb, t, cap):
    e = posm.shape[0] // nb
    d = hn.shape[1]
    return pl.pallas_call(
        functools.partial(_gather_kernel, cap=cap),
        grid=(nb,),
        in_specs=[pl.BlockSpec((e, t), lambda b: (b, 0)), pl.BlockSpec((t, d), lambda b: (b, 0))],
        out_specs=pl.BlockSpec((e, cap, d), lambda b: (0, b, 0)),
        out_shape=jax.ShapeDtypeStruct((e, nb * cap, d), BF),
        compiler_params=_params("parallel"),
        name="moe_gather",
    )(posm, hn)


def _expert_kernel(*refs, n_streams):
    x_refs = refs[:n_streams]
    wg_ref, wu_ref, wd_ref = refs[n_streams:n_streams + 3]
    y_refs = refs[n_streams + 3:2 * n_streams + 3]
    w_scr = refs[-1]
    w_scr[0] = wg_ref[0, 0].astype(BF)
    w_scr[1] = wu_ref[0, 0].astype(BF)
    w_scr[2] = wd_ref[0, 0].astype(BF)

    for x_ref, y_ref in zip(x_refs, y_refs):
        tr = _tile(x_ref.shape[1], ROW_TILE)

        def body(j, carry, x_ref=x_ref, y_ref=y_ref, tr=tr):
            r0 = pl.multiple_of(j * tr, tr)
            xs = x_ref[0, pl.ds(r0, tr), :]
            if xs.dtype == jnp.int32:
                xs = _unpack_pairs(xs)
            a = _dot(xs, w_scr[0])
            b = _dot(xs, w_scr[1])
            hm = (a * _sigmoid(a) * b).astype(BF)
            y_ref[0, pl.ds(r0, tr), :] = _dot(hm, w_scr[2]).astype(BF)
            return carry

        lax.fori_loop(0, x_ref.shape[1] // tr, body, 0)


def _experts(xins, w_gate, w_up, w_down, layer, e0, group):
    d = w_gate.shape[-2]
    f = w_gate.shape[-1]
    assert f == d
    wspec = pl.BlockSpec((1, 1, d, f), lambda k: (layer, e0 + k, 0, 0))
    xspecs = [pl.BlockSpec((1,) + x.shape[1:], lambda k, first=first: (first + k, 0, 0)) for x, first in xins]
    yspecs = [pl.BlockSpec((1, x.shape[1], d), lambda k: (k, 0, 0)) for x, _ in xins]
    return pl.pallas_call(
        functools.partial(_expert_kernel, n_streams=len(xins)),
        grid=(group,),
        in_specs=xspecs + [wspec, wspec, wspec],
        out_specs=yspecs,
        out_shape=[jax.ShapeDtypeStruct((group, x.shape[1], d), BF) for x, _ in xins],
        scratch_shapes=[pltpu.VMEM((3, d, f), BF)],
        compiler_params=_params("parallel"),
        name="moe_experts",
    )(*[x for x, _ in xins], w_gate, w_up, w_down)


def _combine_kernel(post_ref, gatet_ref, *refs, cap, final):
    y_refs = refs[:-4]
    x_ref, mod_ref, fg_ref, o_ref = refs[-4:]
    group, _, d = y_refs[0].shape
    e = group * len(y_refs)
    tq = x_ref.shape[0]
    pt = post_ref[...]
    gt = gatet_ref[...]
    if cap % V7X_LANES == 0:
        slot = lax.broadcasted_iota(jnp.int32, (tq, cap), 1).astype(F32)
        pieces = [jnp.where(pt[:, k:k + 1] == slot, gt[:, k:k + 1], 0.0).astype(BF) for k in range(e)]
        scat = jnp.concatenate(pieces, axis=1)
    else:
        slot = lax.broadcasted_iota(jnp.int32, (tq, e * cap), 1).astype(F32)
        scat = jnp.zeros((tq, e * cap), F32)
        for k in range(e):
            pk = pt[:, k:k + 1]
            scat = jnp.where((pk >= 0.0) & (pk + float(k * cap) == slot), gt[:, k:k + 1], scat)
        scat = scat.astype(BF)
    width = group * cap
    out = _dot(scat[:, 0:width], y_refs[0][...].reshape(width, d))
    for g in range(1, len(y_refs)):
        out = out + _dot(scat[:, g * width:(g + 1) * width], y_refs[g][...].reshape(width, d))
    xn = x_ref[...] + mod_ref[0][5:6] * out
    if final:
        ms = jnp.mean(xn * xn, axis=-1, keepdims=True)
        xn = xn * lax.rsqrt(ms + NORM_EPS) * fg_ref[...]
    o_ref[...] = xn


def _combine(post, gatet, ys, xs, mods, fg, nb, t, cap, ctx_row=None, final=False):
    n, d = xs.shape
    group = ys[0].shape[0]
    tq = _tile(t, ROW_TILE)
    tpb = t // tq
    if ctx_row is None:
        mspec = pl.BlockSpec((1, 6, d), lambda b, i: (b, 0, 0))
    else:
        mspec = pl.BlockSpec((1, 6, d), lambda b, i: (ctx_row, 0, 0))
    return pl.pallas_call(
        functools.partial(_combine_kernel, cap=cap, final=final),
        grid=(nb, tpb),
        in_specs=[
            pl.BlockSpec((tq, V7X_LANES), lambda b, i: (b * tpb + i, 0)),
            pl.BlockSpec((tq, V7X_LANES), lambda b, i: (b * tpb + i, 0)),
        ] + [pl.BlockSpec((group, cap, d), lambda b, i: (0, b, 0))] * len(ys) + [
            pl.BlockSpec((tq, d), lambda b, i: (b * tpb + i, 0)),
            mspec,
            _full((1, d)),
        ],
        out_specs=pl.BlockSpec((tq, d), lambda b, i: (b * tpb + i, 0)),
        out_shape=jax.ShapeDtypeStruct((n, d), F32),
        compiler_params=_params("parallel", "parallel"),
        name="moe_combine",
    )(post, gatet, *ys, xs, mods, fg)


def _rope_tables(seq):
    rows = seq // GRID_W
    row = np.repeat(np.arange(rows, dtype=np.float32), GRID_W)
    col = np.tile(np.arange(GRID_W, dtype=np.float32), rows)
    inv_freq = (ROPE_THETA ** (-np.arange(0, ROPE_AXIS_DIM, 2, dtype=np.float32) / ROPE_AXIS_DIM)).astype(np.float32)
    lane = np.arange(2 * DA_HEAD_DIM)
    within = lane % DA_HEAD_DIM
    axis = within // ROPE_AXIS_DIM
    half = (within % ROPE_AXIS_DIM) // (ROPE_AXIS_DIM // 2)
    idx = within % (ROPE_AXIS_DIM // 2)
    pos = np.where(axis[None, :] == 0, row[:, None], col[:, None])
    ang = (pos * inv_freq[idx][None, :]).astype(np.float32)
    cos = np.cos(ang).astype(np.float32)
    sin = np.sin(ang).astype(np.float32)
    sin_lo = np.where(half[None, :] == 1, sin, 0.0).astype(np.float32)
    sin_hi = np.where(half[None, :] == 0, -sin, 0.0).astype(np.float32)
    return cos, sin_lo, sin_hi


def _qkv_kernel(x_ref, mod_ref, g_ref, w_ref, cos_ref, sa_ref, sb_ref, *o_refs, rope):
    d = x_ref.shape[1]
    mod = mod_ref[0]
    h = _norm_mod(x_ref[...], g_ref[...], mod[0:1], mod[1:2]).astype(BF)
    slab = 2 * DA_HEAD_DIM
    shift = ROPE_AXIS_DIM // 2
    for j, o_ref in enumerate(o_refs):
        u = _dot(h, w_ref[:, j * d:(j + 1) * d])
        if rope and j < 2:
            scale = DA_HEAD_DIM ** -0.5 * math.log2(math.e) if j == 0 else 1.0
            cos, sa, sb = cos_ref[...] * scale, sa_ref[...] * scale, sb_ref[...] * scale
            for hd in range(d // slab):
                xs = u[:, hd * slab:(hd + 1) * slab]
                r = xs * cos + pltpu.roll(xs, shift, 1) * sa + pltpu.roll(xs, slab - shift, 1) * sb
                o_ref[:, hd * slab:(hd + 1) * slab] = r.astype(BF)
        else:
            o_ref[...] = u.astype(BF)


def _qkv(xs, mods, g1, w, tables, seq, nout, rope, ctx_row=None):
    n, d = xs.shape
    tm = _tile(seq, PROJ_ROW_TILE)
    tps = seq // tm
    slab = 2 * DA_HEAD_DIM
    tspec = pl.BlockSpec((tm, slab), lambda i: (i % tps, 0))
    return pl.pallas_call(
        functools.partial(_qkv_kernel, rope=rope),
        grid=(n // tm,),
        in_specs=[pl.BlockSpec((tm, d), lambda i: (i, 0)), _mod_spec(d, tps, ctx_row), _full((1, d)),
                  _full((d, nout * d)), tspec, tspec, tspec],
        out_specs=[pl.BlockSpec((tm, d), lambda i: (i, 0))] * nout,
        out_shape=[jax.ShapeDtypeStruct((n, d), BF)] * nout,
        compiler_params=_params("parallel"),
        name="attn_qkv",
    )(xs, mods, g1, w, *tables)


def _attn_kernel(q_ref, k_ref, v_ref, kc_ref, vc_ref, lq1_ref, lk1_ref, lq2_ref, lk2_ref, sg_ref, o_ref,
                 sl_a, sc_a, sl_b, sc_b, va_scr, vca_scr, *, lam_init, tq):
    seq = q_ref.shape[0]
    lam = (jnp.exp(jnp.sum(lq1_ref[...] * lk1_ref[...], axis=1, keepdims=True))
           - jnp.exp(jnp.sum(lq2_ref[...] * lk2_ref[...], axis=1, keepdims=True)) + lam_init)
    slots = ((sl_a, sc_a), (sl_b, sc_b))

    def scores(i, slot):
        sl_ref, sc_ref = slot
        q = q_ref[i * tq:(i + 1) * tq, :]
        lane = lax.broadcasted_iota(jnp.int32, q.shape, 1)
        zero = jnp.zeros_like(q)
        for mp, qm in enumerate((jnp.where(lane < DA_HEAD_DIM, q, zero), jnp.where(lane >= DA_HEAD_DIM, q, zero))):
            sl_ref[mp] = _dot_nt(qm, k_ref[...])
            sc_ref[mp] = _dot_nt(qm, kc_ref[...])

    slab = v_ref.shape[1]
    va_scr[:, 0:slab] = v_ref[...]
    va_scr[:, slab:2 * slab] = jnp.ones_like(v_ref)
    vca_scr[:, 0:slab] = vc_ref[...]
    vca_scr[:, slab:2 * slab] = jnp.ones_like(vc_ref)

    def unnormalised(sl_ref, sc_ref, mp):
        s_l = sl_ref[mp]
        s_c = sc_ref[mp]
        m = jnp.maximum(jnp.max(s_l, axis=1, keepdims=True), jnp.max(s_c, axis=1, keepdims=True))
        p_l = jnp.exp2(s_l - m).astype(BF)
        p_c = jnp.exp2(s_c - m).astype(BF)
        both = _dot(p_l, va_scr[...]) + _dot(p_c, vca_scr[...])
        return both[:, 0:slab], both[:, slab:slab + 1]

    def attend(i, slot):
        o1, t1 = unnormalised(*slot, 0)
        o2, t2 = unnormalised(*slot, 1)
        o = o1 * (1.0 / t1) - o2 * (lam / t2)
        ms = jnp.mean(o * o, axis=-1, keepdims=True)
        o_ref[i * tq:(i + 1) * tq, :] = (o * lax.rsqrt(ms + SUBLN_EPS) * sg_ref[...] * (1.0 - lam_init)).astype(BF)

    n = seq // tq
    scores(0, slots[0])
    for i in range(n):
        if i + 1 < n:
            scores(i + 1, slots[(i + 1) % 2])
        attend(i, slots[i % 2])


def _attention(q, k, v, kc, vc, lq1, lk1, lq2, lk2, sg, nb, seq, ctx_len, lam_init):
    n, d = q.shape
    slab = 2 * DA_HEAD_DIM
    heads = d // slab
    tq = _tile(seq, ATTN_Q_TILE)
    small = _full((1, DA_HEAD_DIM))
    lat = pl.BlockSpec((seq, slab), lambda b, h: (b, h))
    ctx = pl.BlockSpec((ctx_len, slab), lambda b, h: (b, h))
    score_scratch = [pltpu.VMEM((2, tq, seq), F32), pltpu.VMEM((2, tq, ctx_len), F32)]
    return pl.pallas_call(
        functools.partial(_attn_kernel, lam_init=lam_init, tq=tq),
        grid=(nb, heads),
        in_specs=[lat, lat, lat, ctx, ctx, small, small, small, small, _full((1, slab))],
        out_specs=lat,
        out_shape=jax.ShapeDtypeStruct((n, d), BF),
        scratch_shapes=score_scratch + score_scratch + [pltpu.VMEM((seq, 2 * slab), BF),
                                                        pltpu.VMEM((ctx_len, 2 * slab), BF)],
        compiler_params=_params("parallel", "parallel"),
        name="diff_attention",
    )(q, k, v, kc, vc, lq1, lk1, lq2, lk2, sg)


def _hyena_in_kernel(xp_ref, x_ref, xn_ref, mod_ref, g1_ref, win_ref, wconv_ref, o_ref, *, tiles_per_seq):
    tm, d = x_ref.shape
    h, keep = _halo_rows(xp_ref, x_ref, xn_ref, g1_ref, mod_ref[0], tiles_per_seq)
    for j in range(3):
        u = _dot(h, win_ref[:, j * d:(j + 1) * d]) * keep
        o_ref[:, j * d:(j + 1) * d] = _conv3(u, wconv_ref.at[:, j * d:(j + 1) * d], tm)


def _hyena_in(xs, mods, g1, w_in, w_conv, seq):
    n, d = xs.shape
    tm = _tile(seq, PROJ_ROW_TILE)
    tps = seq // tm
    return pl.pallas_call(
        functools.partial(_hyena_in_kernel, tiles_per_seq=tps),
        grid=(n // tm,),
        in_specs=_halo_specs(n, d, tm) + [_mod_spec(d, tps, None), _full((1, d)), _full((d, 3 * d)), _full((3, 3 * d))],
        out_specs=pl.BlockSpec((tm, 3 * d), lambda i: (i, 0)),
        out_shape=jax.ShapeDtypeStruct((n, 3 * d), F32),
        compiler_params=_params("parallel"),
        name="hyena_in",
    )(xs, xs, xs, mods, g1, w_in, w_conv)


def _dft_half_shift(seq):
    order = np.concatenate([np.arange(seq // 2), seq - 1 - np.arange(seq // 2)])
    half = np.pi * (order + 0.5) / (2 * seq)
    return np.stack([np.cos(half), np.sin(half)], axis=1).astype(np.float32)


def _dft_half_tables(seq):
    h = seq // 2
    th = 2.0 * np.pi * (np.arange(h, dtype=np.float64) + 0.5) / seq
    s = np.arange(h, dtype=np.float64)
    even, odd = np.outer(th, s + 0.25), np.outer(th, s + 0.75)
    fwd = np.stack([np.cos(even), np.sin(even), np.cos(odd), np.sin(odd)]).astype(np.float32)
    inv = np.ascontiguousarray(np.transpose(fwd, (0, 2, 1)))
    return fwd, inv


def _hyena_features(seq):
    t = np.linspace(0.0, 1.0, seq, dtype=np.float32)[:, None]
    w = (2.0 * math.pi * np.arange(seq, dtype=np.float32)[:, None] / seq).astype(np.float32)
    f = np.linspace(1e-4, HY_BANDS - 1, HY_BANDS, dtype=np.float32)[None, :]
    z = np.concatenate([t, np.cos(f * w), -np.sin(f * w)], axis=-1).astype(np.float32)
    zp = np.zeros((seq, V7X_LANES), np.float32)
    zp[:, :HY_EMB_DIM] = z
    return zp


def _hyena_deltas(d):
    max_decay = math.log(HY_DECAY_TARGET) / HY_FAST_DECAY_PCT
    min_decay = math.log(HY_DECAY_TARGET) / HY_SLOW_DECAY_PCT
    return np.abs(np.linspace(min_decay, max_decay, d, dtype=np.float32))[None, :].astype(np.float32)


def _hyena_filter_kernel(z_ref, f1w_ref, f1b_ref, f1f_ref, f2w_ref, f2b_ref, f2f_ref,
                         f3a_ref, f3b_ref, f3c_ref, f3d_ref, delta_ref, fwd_ref, rot_ref, k_ref, hid_scr, *pm_scrs):
    seq = z_ref.shape[0]
    half = seq // 2
    tc = delta_ref.shape[1]
    even = pl.ds(0, half, stride=2)
    odd = pl.ds(1, half, stride=2)

    @pl.when(pl.program_id(0) == 0)
    def _():
        h1 = jnp.sin(f1f_ref[...] * (_dot3(z_ref[...], f1w_ref[...]) + f1b_ref[...]))
        hid_scr[...] = jnp.sin(f2f_ref[...] * (_dot3(h1, f2w_ref[...]) + f2b_ref[...]))

    hid = hid_scr[...]
    decay = jnp.exp(-z_ref[:, 0:1] * delta_ref[...])
    row = lax.broadcasted_iota(jnp.int32, (seq, 1), 0)
    cr = rot_ref[:, 0:1]
    sr = rot_ref[:, 1:2]
    for order, (f3_fwd_ref, f3_bwd_ref) in enumerate(((f3a_ref, f3b_ref), (f3c_ref, f3d_ref))):
        h_fwd = _dot3(hid, f3_fwd_ref[...]) * decay
        h_bwd = jnp.where(row == 0, 0.0, _dot3(hid, f3_bwd_ref[...]) * decay)
        pm = jnp.concatenate([h_fwd + h_bwd, h_bwd - h_fwd], axis=1)
        for k, scr in enumerate(pm_scrs):
            scr[...] = pm[:, k * V7X_LANES:(k + 1) * V7X_LANES]
        x_even = jnp.concatenate([scr[even, :] for scr in pm_scrs], axis=1).astype(BF)
        x_odd = jnp.concatenate([scr[odd, :] for scr in pm_scrs], axis=1).astype(BF)
        a = _dot(fwd_ref[0], x_even)
        b = _dot(fwd_ref[1], x_even)
        c = _dot(fwd_ref[2], x_odd)
        s = _dot(fwd_ref[3], x_odd)
        cos_sum = jnp.concatenate([a + c, b - s], axis=0)
        sin_sum = jnp.concatenate([b + s, a - c], axis=0)
        k_ref[2 * order] = cr * cos_sum[:, :tc] + sr * sin_sum[:, :tc]
        k_ref[2 * order + 1] = cr * sin_sum[:, tc:] - sr * cos_sum[:, tc:]


def _hyena_filter(z, f1w, f1b, f1f, f2w, f2b, f2f, f3w, deltas, fwd, rot, d):
    seq = z.shape[0]
    half = seq // 2
    w = f2w.shape[0]
    tc = _tile(d, CHAN_TILE)
    nc = d // tc
    f3spec = [pl.BlockSpec((w, tc), lambda j, o=o: (0, o * nc + j)) for o in range(4)]
    return pl.pallas_call(
        _hyena_filter_kernel,
        grid=(nc,),
        in_specs=[_full((seq, V7X_LANES)), _full((V7X_LANES, w)), _full((1, w)), _full((1, w)),
                  _full((w, w)), _full((1, w)), _full((1, w))] + f3spec + [
            pl.BlockSpec((1, tc), lambda j: (0, j)),
            pl.BlockSpec((4, half, half), lambda j: (0, 0, 0), pipeline_mode=pl.Buffered(1)),
            _full((seq, 2)),
        ],
        out_specs=pl.BlockSpec((4, seq, tc), lambda j: (0, 0, j)),
        out_shape=jax.ShapeDtypeStruct((4, seq, d), F32),
        scratch_shapes=[pltpu.VMEM((seq, w), F32)] + [pltpu.VMEM((seq, V7X_LANES), F32)] * (2 * tc // V7X_LANES),
        compiler_params=_params("arbitrary"),
        name="hyena_filter",
    )(z, f1w, f1b, f1f, f2w, f2b, f2f, f3w, f3w, f3w, f3w, deltas, fwd, rot)


def _hyena_conv_kernel(*refs, gt, pieces):
    x1_refs, x2_refs, v_refs = refs[:pieces], refs[pieces:2 * pieces], refs[2 * pieces:3 * pieces]
    fwd_ref, inv_ref, k_ref, skip_ref, o_ref = refs[3 * pieces:3 * pieces + 5]
    ue_scr, uo_scr, ze_scr, zo_scr, ye_scr, yo_scr = refs[3 * pieces + 5:3 * pieces + 11]
    out_scrs = refs[3 * pieces + 11:]
    seq = o_ref.shape[0]
    half = seq // 2
    even = pl.ds(0, half, stride=2)
    odd = pl.ds(1, half, stride=2)

    def samples(piece_refs, rows):
        return jnp.concatenate([r[rows, :] for r in piece_refs], axis=1)

    def longconv(order):
        ye_scr[...] = jnp.zeros_like(ye_scr)
        yo_scr[...] = jnp.zeros_like(yo_scr)

        def body(c, carry):
            g0 = pl.multiple_of(c * gt, gt)
            rows = pl.ds(g0, gt)
            ue = ue_scr[...]
            uo = uo_scr[...]
            a = _dot(fwd_ref[0, rows, :], ue)
            b = _dot(fwd_ref[1, rows, :], ue)
            cc = _dot(fwd_ref[2, rows, :], uo)
            d = _dot(fwd_ref[3, rows, :], uo)

            def times_filter(first, ur, ui):
                kr = k_ref[2 * order, pl.ds(first + g0, gt), :]
                ki = k_ref[2 * order + 1, pl.ds(first + g0, gt), :]
                return kr * ur + ki * ui, kr * ui - ki * ur

            yra, yia = times_filter(0, a + cc, b + d)
            yrb, yib = times_filter(half, b - d, a - cc)
            ye_scr[...] += (_dot(inv_ref[0, :, rows], (yra + yib).astype(BF))
                            + _dot(inv_ref[1, :, rows], (yia + yrb).astype(BF)))
            yo_scr[...] += (_dot(inv_ref[2, :, rows], (yra - yib).astype(BF))
                            + _dot(inv_ref[3, :, rows], (yia - yrb).astype(BF)))
            return carry

        lax.fori_loop(0, half // gt, body, 0)

    scale = 1.0 / seq
    ve = samples(v_refs, even)
    vo = samples(v_refs, odd)
    ue_scr[...] = ve.astype(BF)
    uo_scr[...] = vo.astype(BF)
    longconv(0)
    ze = samples(x1_refs, even) * (ye_scr[...] * scale + ve * skip_ref[0:1, :])
    zo = samples(x1_refs, odd) * (yo_scr[...] * scale + vo * skip_ref[0:1, :])
    ze_scr[...] = ze
    zo_scr[...] = zo
    ue_scr[...] = ze.astype(BF)
    uo_scr[...] = zo.astype(BF)
    longconv(1)
    oe = samples(x2_refs, even) * (ye_scr[...] * scale + ze_scr[...] * skip_ref[1:2, :])
    oo = samples(x2_refs, odd) * (yo_scr[...] * scale + zo_scr[...] * skip_ref[1:2, :])
    for k, out_scr in enumerate(out_scrs):
        lanes = slice(k * V7X_LANES, (k + 1) * V7X_LANES)
        out_scr[even, :] = oe[:, lanes]
        out_scr[odd, :] = oo[:, lanes]
        o_ref[:, lanes] = out_scr[...].astype(BF)


def _hyena_conv(u3, fwd, inv, kspec, skip, nb, seq, d):
    tc = _tile(d, CHAN_TILE)
    nc = d // tc
    half = seq // 2
    gt = _tile(half, FREQ_TILE)
    pieces = tc // V7X_LANES
    once = dict(pipeline_mode=pl.Buffered(1))
    piece_specs = [pl.BlockSpec((seq, V7X_LANES), lambda j, b, o=o, k=k: (b, (o * nc + j) * pieces + k))
                   for o in range(3) for k in range(pieces)]
    return pl.pallas_call(
        functools.partial(_hyena_conv_kernel, gt=gt, pieces=pieces),
        scratch_shapes=[pltpu.VMEM((half, tc), BF), pltpu.VMEM((half, tc), BF)]
        + [pltpu.VMEM((half, tc), F32)] * 4 + [pltpu.VMEM((seq, V7X_LANES), F32)] * pieces,
        grid=(nc, nb),
        in_specs=piece_specs + [
            pl.BlockSpec((4, half, half), lambda j, b: (0, 0, 0), **once),
            pl.BlockSpec((4, half, half), lambda j, b: (0, 0, 0), **once),
            pl.BlockSpec((4, seq, tc), lambda j, b: (0, 0, j), **once),
            pl.BlockSpec((2, tc), lambda j, b: (0, j)),
        ],
        out_specs=pl.BlockSpec((seq, tc), lambda j, b: (b, j)),
        out_shape=jax.ShapeDtypeStruct((nb * seq, d), BF),
        compiler_params=_params("parallel", "parallel"),
        name="hyena_conv",
    )(*([u3] * (3 * pieces)), fwd, inv, kspec, skip)


def _moe_block(streams, mods, layer, fg, w_gate, w_up, w_down, nb, final=False):
    n_experts = streams[0][2].shape[0]
    group = _sc_group(nb)
    routed = []
    for xs, hn, aff, t, ctx_row in streams:
        cap = max(1, EC_CAPACITY * t // n_experts)
        posm, post, gatet = _route(aff, nb, t, cap)
        xin = None if hn.dtype == jnp.int32 else _gather(posm, hn, nb, t, cap)
        routed.append((posm, post, gatet, cap, xin))
    ys = [[] for _ in streams]
    for e0 in range(0, n_experts, group):
        xins = []
        for (xs, hn, aff, t, ctx_row), (posm, post, gatet, cap, xin) in zip(streams, routed):
            if xin is None:
                xins.append((_sc_gather(hn, posm, e0, nb, n_experts, t, cap), 0))
            else:
                xins.append((xin, e0))
        for acc, y in zip(ys, _experts(xins, w_gate, w_up, w_down, layer, e0, group)):
            acc.append(y)
    return [_combine(post, gatet, y, xs, mods, fg, nb, t, cap, ctx_row=ctx_row, final=final and ctx_row is None)
            for (xs, _, _, t, ctx_row), (_, post, gatet, cap, _), y in zip(streams, routed, ys)]


def kernel(x, c, ctx, c_ctx, w_ada, b_ada, norm1_g, norm2_g, final_g, a_w_in, a_conv, a_w_out, b_w_qkv, b_lq1, b_lk1, b_lq2, b_lk2, b_subln_g, b_w_out, c_w_in, c_conv, c_f1_w, c_f1_b, c_f1_freq, c_f2_w, c_f2_b, c_f2_freq, c_f3_w, c_skip, c_w_out, moe_router, moe_w_gate, moe_w_up, moe_w_down):
    nb, seq, d = x.shape
    ctx_len = ctx.shape[1]
    depth = w_ada.shape[0]
    assert nb < MOD_ROWS and d % (2 * DA_HEAD_DIM) == 0
    ctx_row = nb

    cvec = jnp.concatenate([c, c_ctx[None, :], jnp.zeros((MOD_ROWS - nb - 1, d), F32)], axis=0)
    mods_all = _ada(cvec, w_ada, b_ada).reshape(depth, MOD_ROWS, 6, d)

    attn_layers = [i for i in range(depth) if i % N_MIXERS == 1]
    last_ctx_read = max(attn_layers) if attn_layers else -1

    xs = x.reshape(nb * seq, d)
    cs_tok = ctx.reshape(nb * ctx_len, d)
    fg = final_g[None, :]

    for i in range(depth):
        kind, j = i % N_MIXERS, i // N_MIXERS
        update_ctx = i < last_ctx_read
        final = i == depth - 1
        mods = mods_all[i]
        g1 = norm1_g[i][None, :]
        g2 = norm2_g[i][None, :]
        wr = moe_router[i].T
        wrh, wrl = _split(wr)
        moe_w = (moe_w_gate, moe_w_up, moe_w_down)

        ctx_stream = []
        if kind == 0:
            w_in, w_out = a_w_in[j].astype(BF), a_w_out[j].astype(BF)
            if update_ctx:
                cn, chn, caff = _shortconv_layer(cs_tok, mods, g1, w_in, a_conv[j], w_out, g2, wrh, wrl,
                                                 ctx_len, ctx_row=ctx_row)
                ctx_stream = [(cn, chn, caff, ctx_len, ctx_row)]
            xn, hn, aff = _shortconv_layer(xs, mods, g1, w_in, a_conv[j], w_out, g2, wrh, wrl, seq)
        elif kind == 1:
            assert not update_ctx
            lam_init = 0.8 - 0.6 * math.exp(-0.3 * i)
            w_qkv = b_w_qkv[j].astype(BF)
            tables = [jnp.asarray(t) for t in _rope_tables(seq)]
            q, k, v = _qkv(xs, mods, g1, w_qkv, tables, seq, 3, True)
            ctab = [t[:ctx_len] for t in tables]
            kc, vc = _qkv(cs_tok, mods, g1, w_qkv[:, d:], ctab, ctx_len, 2, False, ctx_row=ctx_row)
            o = _attention(q, k, v, kc, vc, b_lq1[j][None, :], b_lk1[j][None, :], b_lq2[j][None, :],
                           b_lk2[j][None, :], b_subln_g[j][None, :], nb, seq, ctx_len, lam_init)
            xn, hn, aff = _outproj_layer(o, xs, mods, b_w_out[j].astype(BF), g2, wrh, wrl, seq)
        else:
            assert not update_ctx
            fwd_np, inv_np = _dft_half_tables(seq)
            fwd, inv = jnp.asarray(fwd_np).astype(BF), jnp.asarray(inv_np).astype(BF)
            w = c_f2_w.shape[-1]
            f1w = jnp.zeros((V7X_LANES, w), F32).at[:HY_EMB_DIM].set(c_f1_w[j])
            kspec = _hyena_filter(jnp.asarray(_hyena_features(seq)), f1w, c_f1_b[j][None, :], c_f1_freq[j][None, :],
                                  c_f2_w[j], c_f2_b[j][None, :], c_f2_freq[j][None, :], c_f3_w[j],
                                  jnp.asarray(_hyena_deltas(d)), fwd, jnp.asarray(_dft_half_shift(seq)), d)
            u3 = _hyena_in(xs, mods, g1, c_w_in[j].astype(BF), c_conv[j], seq)
            z = _hyena_conv(u3, fwd, inv, kspec, c_skip[j], nb, seq, d)
            xn, hn, aff = _outproj_layer(z, xs, mods, c_w_out[j].astype(BF), g2, wrh, wrl, seq)

        outs = _moe_block([(xn, hn, aff, seq, None)] + ctx_stream, mods, i, fg, *moe_w, nb, final=final)
        xs = outs[0]
        if ctx_stream:
            cs_tok = outs[1]

    return xs.reshape(nb, seq, d)
```

```python
import functools
import math

import jax
import jax.numpy as jnp
import numpy as np
from jax import lax
from jax.experimental import pallas as pl
from jax.experimental.pallas import tpu as pltpu

BF = jnp.bfloat16
F32 = jnp.float32

GRID_W = 64
DA_HEAD_DIM = 64
ROPE_AXIS_DIM = DA_HEAD_DIM // 2
ROPE_THETA = 10000.0
SUBLN_EPS = 1e-5
NORM_EPS = 1e-6
N_MIXERS = 3
EC_CAPACITY = 2
HY_EMB_DIM = 33
HY_BANDS = (HY_EMB_DIM - 1) // 2
HY_FAST_DECAY_PCT = 0.3
HY_SLOW_DECAY_PCT = 1.5
HY_DECAY_TARGET = 1e-2

V7X_LANES = 128
V7X_BF16_SUBLANES = 16
V7X_VMEM_BYTES = 64 * 2**20
VMEM_LIMIT = V7X_VMEM_BYTES - 8 * 2**20
HALO = V7X_BF16_SUBLANES
MOD_ROWS = 16
ROW_TILE = 512
PROJ_ROW_TILE = 1024
ATTN_Q_TILE = 256
CHAN_TILE = 256
ADA_COL_TILE = 1536
FREQ_TILE = 1024
SC_GATHER_ROWS = 64


def _params(*sem):
    return pltpu.CompilerParams(dimension_semantics=sem, vmem_limit_bytes=VMEM_LIMIT)


def _dot(a, b):
    return jnp.dot(a, b, preferred_element_type=F32)


def _dot_nt(a, b):
    return lax.dot_general(a, b, (((1,), (1,)), ((), ())), preferred_element_type=F32)


def _split(a):
    hi = a.astype(BF)
    lo = (a - hi.astype(F32)).astype(BF)
    return hi, lo


def _dot3(a, b):
    ah, al = _split(a)
    bh, bl = _split(b)
    return _dot(ah, bh) + (_dot(ah, bl) + _dot(al, bh))


def _sigmoid(a):
    return 1.0 / (1.0 + jnp.exp(-a))


def _norm_mod(x, g, shift, scale):
    ms = jnp.mean(x * x, axis=-1, keepdims=True)
    return (x * lax.rsqrt(ms + NORM_EPS) * g) * (1.0 + scale) + shift


def _tile(n, pref):
    t = min(n, pref)
    assert n % t == 0, (n, pref)
    return t


def _full(shape):
    nd = len(shape)
    return pl.BlockSpec(shape, lambda *_: (0,) * nd)


def _ada_kernel(c_ref, w_ref, b_ref, o_ref):
    c = c_ref[...]
    o_ref[0] = _dot3(c * _sigmoid(c), w_ref[0]) + b_ref[0]


def _ada(cvec, w_ada, b_ada):
    depth, d, n6 = w_ada.shape
    tn = _tile(n6, ADA_COL_TILE)
    return pl.pallas_call(
        _ada_kernel,
        grid=(depth, n6 // tn),
        in_specs=[
            _full((MOD_ROWS, d)),
            pl.BlockSpec((1, d, tn), lambda i, j: (i, 0, j)),
            pl.BlockSpec((1, 1, tn), lambda i, j: (i, 0, j)),
        ],
        out_specs=pl.BlockSpec((1, MOD_ROWS, tn), lambda i, j: (i, 0, j)),
        out_shape=jax.ShapeDtypeStruct((depth, MOD_ROWS, n6), F32),
        compiler_params=_params("parallel", "parallel"),
        name="ada",
    )(cvec, w_ada, b_ada.reshape(depth, 1, n6))


def _pack_pairs(hh):
    half = hh.shape[1] // 2
    lo = lax.shift_right_logical(pltpu.bitcast(hh[:, :half].astype(F32), jnp.uint32), jnp.uint32(16))
    hi = pltpu.bitcast(hh[:, half:].astype(F32), jnp.uint32) & jnp.uint32(0xFFFF0000)
    return pltpu.bitcast(lo | hi, jnp.int32)


def _unpack_pairs(words):
    u = pltpu.bitcast(words, jnp.uint32)
    lo = pltpu.bitcast(lax.shift_left(u, jnp.uint32(16)), F32).astype(BF)
    hi = pltpu.bitcast(u & jnp.uint32(0xFFFF0000), F32).astype(BF)
    return jnp.concatenate([lo, hi], axis=1)


def _residual_router(x, y, mod, g2_ref, wrh_ref, wrl_ref, xo_ref, hn_ref, aff_ref):
    xn = x + mod[2:3] * y
    xo_ref[...] = xn
    hn = _norm_mod(xn, g2_ref[...], mod[3:4], mod[4:5])
    hh, hl = _split(hn)
    hn_ref[...] = _pack_pairs(hh) if hn_ref.dtype == jnp.int32 else hh
    e = wrh_ref.shape[0]
    both = _dot_nt(jnp.concatenate([wrh_ref[...], wrl_ref[...]], axis=0), hh)
    logits = both[:e] + (_dot_nt(wrh_ref[...], hl) + both[e:])
    p = jnp.exp(logits - jnp.max(logits, axis=0, keepdims=True))
    aff_ref[...] = p / jnp.sum(p, axis=0, keepdims=True)


def _router_specs(n, d, e, tm, packed):
    in_specs = [_full((1, d)), _full((e, d)), _full((e, d))]
    hn_cols, hn_dtype = (d // 2, jnp.int32) if packed else (d, BF)
    out_specs = [
        pl.BlockSpec((tm, d), lambda i: (i, 0)),
        pl.BlockSpec((tm, hn_cols), lambda i: (i, 0)),
        pl.BlockSpec((e, tm), lambda i: (0, i)),
    ]
    out_shape = [
        jax.ShapeDtypeStruct((n, d), F32),
        jax.ShapeDtypeStruct((n, hn_cols), hn_dtype),
        jax.ShapeDtypeStruct((e, n), F32),
    ]
    return in_specs, out_specs, out_shape


def _halo_specs(n, d, tm):
    per = tm // HALO
    last = n // HALO - 1
    return [
        pl.BlockSpec((HALO, d), lambda i: (jnp.maximum(i * per - 1, 0), 0)),
        pl.BlockSpec((tm, d), lambda i: (i, 0)),
        pl.BlockSpec((HALO, d), lambda i: (jnp.minimum((i + 1) * per, last), 0)),
    ]


def _mod_spec(d, tiles_per_seq, ctx_row):
    if ctx_row is None:
        return pl.BlockSpec((1, 6, d), lambda i: (i // tiles_per_seq, 0, 0))
    return pl.BlockSpec((1, 6, d), lambda i: (ctx_row, 0, 0))


def _halo_rows(xp_ref, x_ref, xn_ref, g_ref, mod, tiles_per_seq):
    t = pl.program_id(0) % tiles_per_seq
    g = g_ref[...]
    hp, hx, hn = [_norm_mod(r[...], g, mod[0:1], mod[1:2]) for r in (xp_ref, x_ref, xn_ref)]
    hp = jnp.where(t == 0, 0.0, hp)
    hn = jnp.where(t == tiles_per_seq - 1, 0.0, hn)
    return jnp.concatenate([hp, hx, hn], axis=0).astype(BF)


def _conv3(s, w_ref, tm):
    n = s.shape[0]
    prev = pltpu.roll(s, 1, 0)[HALO:HALO + tm]
    nxt = pltpu.roll(s, n - 1, 0)[HALO:HALO + tm]
    return prev * w_ref[0:1, :] + s[HALO:HALO + tm] * w_ref[1:2, :] + nxt * w_ref[2:3, :]


def _shortconv_kernel(xp_ref, x_ref, xn_ref, mod_ref, g1_ref, win_ref, wconv_ref, wout_ref,
                      g2_ref, wrh_ref, wrl_ref, xo_ref, hn_ref, aff_ref, *, tiles_per_seq):
    tm, d = x_ref.shape
    mod = mod_ref[0]
    h = _halo_rows(xp_ref, x_ref, xn_ref, g1_ref, mod, tiles_per_seq)
    gate = _dot(h[HALO:HALO + tm], win_ref[:, 0:d])
    s = _dot(h, win_ref[:, d:2 * d]) * _dot(h, win_ref[:, 2 * d:3 * d])
    z = (gate * _conv3(s, wconv_ref, tm)).astype(BF)
    y = _dot(z, wout_ref[...])
    _residual_router(x_ref[...], y, mod, g2_ref, wrh_ref, wrl_ref, xo_ref, hn_ref, aff_ref)


def _shortconv_layer(xs, mods, g1, w_in, w_conv, w_out, g2, wrh, wrl, seq, ctx_row=None):
    n, d = xs.shape
    e = wrh.shape[0]
    tm = _tile(seq, PROJ_ROW_TILE)
    tps = seq // tm
    r_in, r_out, r_shape = _router_specs(n, d, e, tm, packed=ctx_row is None)
    return pl.pallas_call(
        functools.partial(_shortconv_kernel, tiles_per_seq=tps),
        grid=(n // tm,),
        in_specs=_halo_specs(n, d, tm) + [
            _mod_spec(d, tps, ctx_row), _full((1, d)), _full((d, 3 * d)), _full((3, d)), _full((d, d)),
        ] + r_in,
        out_specs=r_out,
        out_shape=r_shape,
        compiler_params=_params("parallel"),
        name="shortconv_layer",
    )(xs, xs, xs, mods, g1, w_in, w_conv, w_out, g2, wrh, wrl)


def _outproj_kernel(o_ref, x_ref, mod_ref, wout_ref, g2_ref, wrh_ref, wrl_ref, xo_ref, hn_ref, aff_ref):
    y = _dot(o_ref[...], wout_ref[...])
    _residual_router(x_ref[...], y, mod_ref[0], g2_ref, wrh_ref, wrl_ref, xo_ref, hn_ref, aff_ref)


def _outproj_layer(o, xs, mods, w_out, g2, wrh, wrl, seq):
    n, d = xs.shape
    e = wrh.shape[0]
    tm = _tile(seq, PROJ_ROW_TILE)
    tps = seq // tm
    r_in, r_out, r_shape = _router_specs(n, d, e, tm, packed=True)
    return pl.pallas_call(
        _outproj_kernel,
        grid=(n // tm,),
        in_specs=[pl.BlockSpec((tm, d), lambda i: (i, 0)), pl.BlockSpec((tm, d), lambda i: (i, 0)),
                  _mod_spec(d, tps, None), _full((d, d))] + r_in,
        out_specs=r_out,
        out_shape=r_shape,
        compiler_params=_params("parallel"),
        name="outproj_layer",
    )(o, xs, mods, w_out, g2, wrh, wrl)


def _excl_cumsum_lanes(m):
    rows, t = m.shape
    a = lax.broadcasted_iota(jnp.int32, (V7X_LANES, V7X_LANES), 0)
    b = lax.broadcasted_iota(jnp.int32, (V7X_LANES, V7X_LANES), 1)
    tri = jnp.where(a < b, 1.0, 0.0).astype(BF)
    carry = jnp.zeros((rows, 1), F32)
    out = []
    for c in range(t // V7X_LANES):
        blk = m[:, c * V7X_LANES:(c + 1) * V7X_LANES]
        out.append(_dot(blk.astype(BF), tri) + carry)
        carry = carry + jnp.sum(blk, axis=1, keepdims=True)
    return jnp.concatenate(out, axis=1)


def _route_kernel(aff_ref, posm_ref, post_ref, gatet_ref, *, cap, nb):
    e = aff_ref.shape[0]
    t = aff_ref.shape[1] // nb
    aff = jnp.concatenate([aff_ref[:, b * t:(b + 1) * t] for b in range(nb)], axis=0)
    bits = pltpu.bitcast(aff, jnp.int32)

    def step(i, thr):
        cand = thr | jnp.left_shift(jnp.int32(1), 30 - i)
        cnt = jnp.sum(jnp.where(bits >= cand, 1.0, 0.0), axis=1, keepdims=True)
        return jnp.where(cnt >= cap, cand, thr)

    thr = lax.fori_loop(0, 31, step, jnp.zeros((nb * e, 1), jnp.int32))
    gt = jnp.where(bits > thr, 1.0, 0.0)
    eq = jnp.where(bits == thr, 1.0, 0.0)
    need = cap - jnp.sum(gt, axis=1, keepdims=True)
    sel = gt + eq * jnp.where(_excl_cumsum_lanes(eq) < need, 1.0, 0.0)
    pos = jnp.where(sel > 0.0, _excl_cumsum_lanes(sel), -1.0)
    gate = sel * aff
    posm_ref[...] = pos.astype(jnp.int32)
    pad = jnp.zeros((V7X_LANES - e, t), F32)
    for b in range(nb):
        rows = slice(b * e, (b + 1) * e)
        post_ref[b * t:(b + 1) * t, :] = jnp.concatenate([pos[rows], pad - 1.0], axis=0).T
        gatet_ref[b * t:(b + 1) * t, :] = jnp.concatenate([gate[rows], pad], axis=0).T


def _route(aff, nb, t, cap):
    e = aff.shape[0]
    posm, post, gatet = pl.pallas_call(
        functools.partial(_route_kernel, cap=cap, nb=nb),
        grid=(1,),
        in_specs=[_full((e, nb * t))],
        out_specs=[_full((nb * e, t)), _full((nb * t, V7X_LANES)), _full((nb * t, V7X_LANES))],
        out_shape=[
            jax.ShapeDtypeStruct((nb * e, t), jnp.int32),
            jax.ShapeDtypeStruct((nb * t, V7X_LANES), F32),
            jax.ShapeDtypeStruct((nb * t, V7X_LANES), F32),
        ],
        compiler_params=_params("arbitrary"),
        name="route",
    )(aff)
    return posm, post, gatet


def _sc_gather(table, posm, e0, nb, n_experts, t, cap):
    from jax.experimental.pallas import tpu_sc as plsc

    info = plsc.get_sparse_core_info()
    cores, lanes = info.num_cores, info.num_lanes
    workers = cores * info.num_subcores
    group = workers // nb
    words = table.shape[1]
    assert workers % nb == 0 and n_experts % group == 0 and cap % SC_GATHER_ROWS == 0 and t % lanes == 0

    @functools.partial(
        pl.kernel,
        mesh=plsc.VectorSubcoreMesh(core_axis_name="core", subcore_axis_name="subcore"),
        compiler_params=pltpu.CompilerParams(needs_layout_passes=False),
        out_type=jax.ShapeDtypeStruct((group * nb * cap, words), jnp.int32),
        scratch_types=[pltpu.VMEM((t,), jnp.int32), pltpu.VMEM((cap,), jnp.int32),
                       pltpu.VMEM((SC_GATHER_ROWS, words), jnp.int32), pltpu.SemaphoreType.DMA],
    )
    def gather(table_hbm, posm_hbm, out_hbm, pos_v, idx_v, rows_v, sem):
        w = lax.axis_index("subcore") * cores + lax.axis_index("core")
        e_local = w // nb
        b = w % nb
        pltpu.sync_copy(posm_hbm.at[b * n_experts + e0 + e_local], pos_v)

        @pl.loop(0, t // lanes)
        def _(i):
            p = pos_v[pl.ds(i * lanes, lanes)]
            token = lax.iota(jnp.int32, lanes) + (i * lanes + b * t)
            plsc.store_scatter(idx_v, [p], token, mask=p >= 0)

        out_base = (e_local * nb + b) * cap

        @pl.loop(0, cap // SC_GATHER_ROWS)
        def _(j):
            pltpu.async_copy(table_hbm.at[idx_v.at[pl.ds(j * SC_GATHER_ROWS, SC_GATHER_ROWS)]], rows_v, sem).wait()
            pltpu.sync_copy(rows_v, out_hbm.at[pl.ds(out_base + j * SC_GATHER_ROWS, SC_GATHER_ROWS)])

    return gather(table, posm).reshape(group, nb * cap, words)


def _sc_group(nb):
    from jax.experimental.pallas import tpu_sc as plsc

    info = plsc.get_sparse_core_info()
    return info.num_cores * info.num_subcores // nb


def _gather_kernel(posm_ref, hn_ref, o_ref, *, cap):
    e = posm_ref.shape[0]
    t = hn_ref.shape[0]
    slot = lax.broadcasted_iota(jnp.int32, (cap, t), 0)

    def body(k, carry):
        row = posm_ref[pl.ds(k, 1), :]
        onehot = jnp.where(row == slot, 1.0, 0.0).astype(BF)
        o_ref[k] = _dot(onehot, hn_ref[...]).astype(BF)
        return carry

    lax.fori_loop(0, e, body, 0)


def _gather(posm, hn, nb, t, cap):
    e = posm.shape[0] // nb
    d = hn.shape[1]
    return pl.pallas_call(
        functools.partial(_gather_kernel, cap=cap),
        grid=(nb,),
        in_specs=[pl.BlockSpec((e, t), lambda b: (b, 0)), pl.BlockSpec((t, d), lambda b: (b, 0))],
        out_specs=pl.BlockSpec((e, cap, d), lambda b: (0, b, 0)),
        out_shape=jax.ShapeDtypeStruct((e, nb * cap, d), BF),
        compiler_params=_params("parallel"),
        name="moe_gather",
    )(posm, hn)


def _expert_kernel(*refs, n_streams):
    x_refs = refs[:n_streams]
    wg_ref, wu_ref, wd_ref = refs[n_streams:n_streams + 3]
    y_refs = refs[n_streams + 3:2 * n_streams + 3]
    w_scr = refs[-1]
    w_scr[0] = wg_ref[0, 0].astype(BF)
    w_scr[1] = wu_ref[0, 0].astype(BF)
    w_scr[2] = wd_ref[0, 0].astype(BF)

    for x_ref, y_ref in zip(x_refs, y_refs):
        tr = _tile(x_ref.shape[1], ROW_TILE)

        def body(j, carry, x_ref=x_ref, y_ref=y_ref, tr=tr):
            r0 = pl.multiple_of(j * tr, tr)
            xs = x_ref[0, pl.ds(r0, tr), :]
            if xs.dtype == jnp.int32:
                xs = _unpack_pairs(xs)
            a = _dot(xs, w_scr[0])
            b = _dot(xs, w_scr[1])
            hm = (a * _sigmoid(a) * b).astype(BF)
            y_ref[0, pl.ds(r0, tr), :] = _dot(hm, w_scr[2]).astype(BF)
            return carry

        lax.fori_loop(0, x_ref.shape[1] // tr, body, 0)


def _experts(xins, w_gate, w_up, w_down, layer, e0, group):
    d = w_gate.shape[-2]
    f = w_gate.shape[-1]
    assert f == d
    wspec = pl.BlockSpec((1, 1, d, f), lambda k: (layer, e0 + k, 0, 0))
    xspecs = [pl.BlockSpec((1,) + x.shape[1:], lambda k, first=first: (first + k, 0, 0)) for x, first in xins]
    yspecs = [pl.BlockSpec((1, x.shape[1], d), lambda k: (k, 0, 0)) for x, _ in xins]
    return pl.pallas_call(
        functools.partial(_expert_kernel, n_streams=len(xins)),
        grid=(group,),
        in_specs=xspecs + [wspec, wspec, wspec],
        out_specs=yspecs,
        out_shape=[jax.ShapeDtypeStruct((group, x.shape[1], d), BF) for x, _ in xins],
        scratch_shapes=[pltpu.VMEM((3, d, f), BF)],
        compiler_params=_params("parallel"),
        name="moe_experts",
    )(*[x for x, _ in xins], w_gate, w_up, w_down)


def _combine_kernel(post_ref, gatet_ref, *refs, cap, final):
    y_refs = refs[:-4]
    x_ref, mod_ref, fg_ref, o_ref = refs[-4:]
    group, _, d = y_refs[0].shape
    e = group * len(y_refs)
    tq = x_ref.shape[0]
    pt = post_ref[...]
    gt = gatet_ref[...]
    if cap % V7X_LANES == 0:
        slot = lax.broadcasted_iota(jnp.int32, (tq, cap), 1).astype(F32)
        pieces = [jnp.where(pt[:, k:k + 1] == slot, gt[:, k:k + 1], 0.0).astype(BF) for k in range(e)]
        scat = jnp.concatenate(pieces, axis=1)
    else:
        slot = lax.broadcasted_iota(jnp.int32, (tq, e * cap), 1).astype(F32)
        scat = jnp.zeros((tq, e * cap), F32)
        for k in range(e):
            pk = pt[:, k:k + 1]
            scat = jnp.where((pk >= 0.0) & (pk + float(k * cap) == slot), gt[:, k:k + 1], scat)
        scat = scat.astype(BF)
    width = group * cap
    out = _dot(scat[:, 0:width], y_refs[0][...].reshape(width, d))
    for g in range(1, len(y_refs)):
        out = out + _dot(scat[:, g * width:(g + 1) * width], y_refs[g][...].reshape(width, d))
    xn = x_ref[...] + mod_ref[0][5:6] * out
    if final:
        ms = jnp.mean(xn * xn, axis=-1, keepdims=True)
        xn = xn * lax.rsqrt(ms + NORM_EPS) * fg_ref[...]
    o_ref[...] = xn


def _combine(post, gatet, ys, xs, mods, fg, nb, t, cap, ctx_row=None, final=False):
    n, d = xs.shape
    group = ys[0].shape[0]
    tq = _tile(t, ROW_TILE)
    tpb = t // tq
    if ctx_row is None:
        mspec = pl.BlockSpec((1, 6, d), lambda b, i: (b, 0, 0))
    else:
        mspec = pl.BlockSpec((1, 6, d), lambda b, i: (ctx_row, 0, 0))
    return pl.pallas_call(
        functools.partial(_combine_kernel, cap=cap, final=final),
        grid=(nb, tpb),
        in_specs=[
            pl.BlockSpec((tq, V7X_LANES), lambda b, i: (b * tpb + i, 0)),
            pl.BlockSpec((tq, V7X_LANES), lambda b, i: (b * tpb + i, 0)),
        ] + [pl.BlockSpec((group, cap, d), lambda b, i: (0, b, 0))] * len(ys) + [
            pl.BlockSpec((tq, d), lambda b, i: (b * tpb + i, 0)),
            mspec,
            _full((1, d)),
        ],
        out_specs=pl.BlockSpec((tq, d), lambda b, i: (b * tpb + i, 0)),
        out_shape=jax.ShapeDtypeStruct((n, d), F32),
        compiler_params=_params("parallel", "parallel"),
        name="moe_combine",
    )(post, gatet, *ys, xs, mods, fg)


def _rope_tables(seq):
    rows = seq // GRID_W
    row = np.repeat(np.arange(rows, dtype=np.float32), GRID_W)
    col = np.tile(np.arange(GRID_W, dtype=np.float32), rows)
    inv_freq = (ROPE_THETA ** (-np.arange(0, ROPE_AXIS_DIM, 2, dtype=np.float32) / ROPE_AXIS_DIM)).astype(np.float32)
    lane = np.arange(2 * DA_HEAD_DIM)
    within = lane % DA_HEAD_DIM
    axis = within // ROPE_AXIS_DIM
    half = (within % ROPE_AXIS_DIM) // (ROPE_AXIS_DIM // 2)
    idx = within % (ROPE_AXIS_DIM // 2)
    pos = np.where(axis[None, :] == 0, row[:, None], col[:, None])
    ang = (pos * inv_freq[idx][None, :]).astype(np.float32)
    cos = np.cos(ang).astype(np.float32)
    sin = np.sin(ang).astype(np.float32)
    sin_lo = np.where(half[None, :] == 1, sin, 0.0).astype(np.float32)
    sin_hi = np.where(half[None, :] == 0, -sin, 0.0).astype(np.float32)
    return cos, sin_lo, sin_hi


def _qkv_kernel(x_ref, mod_ref, g_ref, w_ref, cos_ref, sa_ref, sb_ref, *o_refs, rope):
    d = x_ref.shape[1]
    mod = mod_ref[0]
    h = _norm_mod(x_ref[...], g_ref[...], mod[0:1], mod[1:2]).astype(BF)
    slab = 2 * DA_HEAD_DIM
    shift = ROPE_AXIS_DIM // 2
    for j, o_ref in enumerate(o_refs):
        u = _dot(h, w_ref[:, j * d:(j + 1) * d])
        if rope and j < 2:
            scale = DA_HEAD_DIM ** -0.5 * math.log2(math.e) if j == 0 else 1.0
            cos, sa, sb = cos_ref[...] * scale, sa_ref[...] * scale, sb_ref[...] * scale
            for hd in range(d // slab):
                xs = u[:, hd * slab:(hd + 1) * slab]
                r = xs * cos + pltpu.roll(xs, shift, 1) * sa + pltpu.roll(xs, slab - shift, 1) * sb
                o_ref[:, hd * slab:(hd + 1) * slab] = r.astype(BF)
        else:
            o_ref[...] = u.astype(BF)


def _qkv(xs, mods, g1, w, tables, seq, nout, rope, ctx_row=None):
    n, d = xs.shape
    tm = _tile(seq, PROJ_ROW_TILE)
    tps = seq // tm
    slab = 2 * DA_HEAD_DIM
    tspec = pl.BlockSpec((tm, slab), lambda i: (i % tps, 0))
    return pl.pallas_call(
        functools.partial(_qkv_kernel, rope=rope),
        grid=(n // tm,),
        in_specs=[pl.BlockSpec((tm, d), lambda i: (i, 0)), _mod_spec(d, tps, ctx_row), _full((1, d)),
                  _full((d, nout * d)), tspec, tspec, tspec],
        out_specs=[pl.BlockSpec((tm, d), lambda i: (i, 0))] * nout,
        out_shape=[jax.ShapeDtypeStruct((n, d), BF)] * nout,
        compiler_params=_params("parallel"),
        name="attn_qkv",
    )(xs, mods, g1, w, *tables)


def _attn_kernel(q_ref, k_ref, v_ref, kc_ref, vc_ref, lq1_ref, lk1_ref, lq2_ref, lk2_ref, sg_ref, o_ref,
                 sl_a, sc_a, sl_b, sc_b, va_scr, vca_scr, *, lam_init, tq):
    seq = q_ref.shape[0]
    lam = (jnp.exp(jnp.sum(lq1_ref[...] * lk1_ref[...], axis=1, keepdims=True))
           - jnp.exp(jnp.sum(lq2_ref[...] * lk2_ref[...], axis=1, keepdims=True)) + lam_init)
    slots = ((sl_a, sc_a), (sl_b, sc_b))

    def scores(i, slot):
        sl_ref, sc_ref = slot
        q = q_ref[i * tq:(i + 1) * tq, :]
        lane = lax.broadcasted_iota(jnp.int32, q.shape, 1)
        zero = jnp.zeros_like(q)
        for mp, qm in enumerate((jnp.where(lane < DA_HEAD_DIM, q, zero), jnp.where(lane >= DA_HEAD_DIM, q, zero))):
            sl_ref[mp] = _dot_nt(qm, k_ref[...])
            sc_ref[mp] = _dot_nt(qm, kc_ref[...])

    slab = v_ref.shape[1]
    va_scr[:, 0:slab] = v_ref[...]
    va_scr[:, slab:2 * slab] = jnp.ones_like(v_ref)
    vca_scr[:, 0:slab] = vc_ref[...]
    vca_scr[:, slab:2 * slab] = jnp.ones_like(vc_ref)

    def unnormalised(sl_ref, sc_ref, mp):
        s_l = sl_ref[mp]
        s_c = sc_ref[mp]
        m = jnp.maximum(jnp.max(s_l, axis=1, keepdims=True), jnp.max(s_c, axis=1, keepdims=True))
        p_l = jnp.exp2(s_l - m).astype(BF)
        p_c = jnp.exp2(s_c - m).astype(BF)
        both = _dot(p_l, va_scr[...]) + _dot(p_c, vca_scr[...])
        return both[:, 0:slab], both[:, slab:slab + 1]

    def attend(i, slot):
        o1, t1 = unnormalised(*slot, 0)
        o2, t2 = unnormalised(*slot, 1)
        o = o1 * (1.0 / t1) - o2 * (lam / t2)
        ms = jnp.mean(o * o, axis=-1, keepdims=True)
        o_ref[i * tq:(i + 1) * tq, :] = (o * lax.rsqrt(ms + SUBLN_EPS) * sg_ref[...] * (1.0 - lam_init)).astype(BF)

    n = seq // tq
    scores(0, slots[0])
    for i in range(n):
        if i + 1 < n:
            scores(i + 1, slots[(i + 1) % 2])
        attend(i, slots[i % 2])


def _attention(q, k, v, kc, vc, lq1, lk1, lq2, lk2, sg, nb, seq, ctx_len, lam_init):
    n, d = q.shape
    slab = 2 * DA_HEAD_DIM
    heads = d // slab
    tq = _tile(seq, ATTN_Q_TILE)
    small = _full((1, DA_HEAD_DIM))
    lat = pl.BlockSpec((seq, slab), lambda b, h: (b, h))
    ctx = pl.BlockSpec((ctx_len, slab), lambda b, h: (b, h))
    score_scratch = [pltpu.VMEM((2, tq, seq), F32), pltpu.VMEM((2, tq, ctx_len), F32)]
    return pl.pallas_call(
        functools.partial(_attn_kernel, lam_init=lam_init, tq=tq),
        grid=(nb, heads),
        in_specs=[lat, lat, lat, ctx, ctx, small, small, small, small, _full((1, slab))],
        out_specs=lat,
        out_shape=jax.ShapeDtypeStruct((n, d), BF),
        scratch_shapes=score_scratch + score_scratch + [pltpu.VMEM((seq, 2 * slab), BF),
                                                        pltpu.VMEM((ctx_len, 2 * slab), BF)],
        compiler_params=_params("parallel", "parallel"),
        name="diff_attention",
    )(q, k, v, kc, vc, lq1, lk1, lq2, lk2, sg)


def _hyena_in_kernel(xp_ref, x_ref, xn_ref, mod_ref, g1_ref, win_ref, wconv_ref, o_ref, *, tiles_per_seq):
    tm, d = x_ref.shape
    h = _halo_rows(xp_ref, x_ref, xn_ref, g1_ref, mod_ref[0], tiles_per_seq)
    for j in range(3):
        u = _dot(h, win_ref[:, j * d:(j + 1) * d])
        o_ref[:, j * d:(j + 1) * d] = _conv3(u, wconv_ref.at[:, j * d:(j + 1) * d], tm)


def _hyena_in(xs, mods, g1, w_in, w_conv, seq):
    n, d = xs.shape
    tm = _tile(seq, PROJ_ROW_TILE)
    tps = seq // tm
    return pl.pallas_call(
        functools.partial(_hyena_in_kernel, tiles_per_seq=tps),
        grid=(n // tm,),
        in_specs=_halo_specs(n, d, tm) + [_mod_spec(d, tps, None), _full((1, d)), _full((d, 3 * d)), _full((3, 3 * d))],
        out_specs=pl.BlockSpec((tm, 3 * d), lambda i: (i, 0)),
        out_shape=jax.ShapeDtypeStruct((n, 3 * d), F32),
        compiler_params=_params("parallel"),
        name="hyena_in",
    )(xs, xs, xs, mods, g1, w_in, w_conv)


def _dft_half_shift(seq):
    order = np.concatenate([np.arange(seq // 2), seq - 1 - np.arange(seq // 2)])
    half = np.pi * (order + 0.5) / (2 * seq)
    return np.stack([np.cos(half), np.sin(half)], axis=1).astype(np.float32)


def _dft_half_tables(seq):
    h = seq // 2
    th = 2.0 * np.pi * (np.arange(h, dtype=np.float64) + 0.5) / seq
    s = np.arange(h, dtype=np.float64)
    even, odd = np.outer(th, s + 0.25), np.outer(th, s + 0.75)
    fwd = np.stack([np.cos(even), np.sin(even), np.cos(odd), np.sin(odd)]).astype(np.float32)
    inv = np.ascontiguousarray(np.transpose(fwd, (0, 2, 1)))
    return fwd, inv


def _hyena_features(seq):
    t = np.linspace(0.0, 1.0, seq, dtype=np.float32)[:, None]
    w = (2.0 * math.pi * np.arange(seq, dtype=np.float32)[:, None] / seq).astype(np.float32)
    f = np.linspace(1e-4, HY_BANDS - 1, HY_BANDS, dtype=np.float32)[None, :]
    z = np.concatenate([t, np.cos(f * w), -np.sin(f * w)], axis=-1).astype(np.float32)
    zp = np.zeros((seq, V7X_LANES), np.float32)
    zp[:, :HY_EMB_DIM] = z
    return zp


def _hyena_deltas(d):
    max_decay = math.log(HY_DECAY_TARGET) / HY_FAST_DECAY_PCT
    min_decay = math.log(HY_DECAY_TARGET) / HY_SLOW_DECAY_PCT
    return np.abs(np.linspace(min_decay, max_decay, d, dtype=np.float32))[None, :].astype(np.float32)


def _hyena_filter_kernel(z_ref, f1w_ref, f1b_ref, f1f_ref, f2w_ref, f2b_ref, f2f_ref,
                         f3a_ref, f3b_ref, f3c_ref, f3d_ref, delta_ref, fwd_ref, rot_ref, k_ref, hid_scr, *pm_scrs):
    seq = z_ref.shape[0]
    half = seq // 2
    tc = delta_ref.shape[1]
    even = pl.ds(0, half, stride=2)
    odd = pl.ds(1, half, stride=2)

    @pl.when(pl.program_id(0) == 0)
    def _():
        h1 = jnp.sin(f1f_ref[...] * (_dot3(z_ref[...], f1w_ref[...]) + f1b_ref[...]))
        hid_scr[...] = jnp.sin(f2f_ref[...] * (_dot3(h1, f2w_ref[...]) + f2b_ref[...]))

    hid = hid_scr[...]
    decay = jnp.exp(-z_ref[:, 0:1] * delta_ref[...])
    row = lax.broadcasted_iota(jnp.int32, (seq, 1), 0)
    cr = rot_ref[:, 0:1]
    sr = rot_ref[:, 1:2]
    for order, (f3_fwd_ref, f3_bwd_ref) in enumerate(((f3a_ref, f3b_ref), (f3c_ref, f3d_ref))):
        h_fwd = _dot3(hid, f3_fwd_ref[...]) * decay
        h_bwd = jnp.where(row == 0, 0.0, _dot3(hid, f3_bwd_ref[...]) * decay)
        pm = jnp.concatenate([h_fwd + h_bwd, h_bwd - h_fwd], axis=1)
        for k, scr in enumerate(pm_scrs):
            scr[...] = pm[:, k * V7X_LANES:(k + 1) * V7X_LANES]
        x_even = jnp.concatenate([scr[even, :] for scr in pm_scrs], axis=1).astype(BF)
        x_odd = jnp.concatenate([scr[odd, :] for scr in pm_scrs], axis=1).astype(BF)
        a = _dot(fwd_ref[0], x_even)
        b = _dot(fwd_ref[1], x_even)
        c = _dot(fwd_ref[2], x_odd)
        s = _dot(fwd_ref[3], x_odd)
        cos_sum = jnp.concatenate([a + c, b - s], axis=0)
        sin_sum = jnp.concatenate([b + s, a - c], axis=0)
        k_ref[2 * order] = cr * cos_sum[:, :tc] + sr * sin_sum[:, :tc]
        k_ref[2 * order + 1] = cr * sin_sum[:, tc:] - sr * cos_sum[:, tc:]


def _hyena_filter(z, f1w, f1b, f1f, f2w, f2b, f2f, f3w, deltas, fwd, rot, d):
    seq = z.shape[0]
    half = seq // 2
    w = f2w.shape[0]
    tc = _tile(d, CHAN_TILE)
    nc = d // tc
    f3spec = [pl.BlockSpec((w, tc), lambda j, o=o: (0, o * nc + j)) for o in range(4)]
    return pl.pallas_call(
        _hyena_filter_kernel,
        grid=(nc,),
        in_specs=[_full((seq, V7X_LANES)), _full((V7X_LANES, w)), _full((1, w)), _full((1, w)),
                  _full((w, w)), _full((1, w)), _full((1, w))] + f3spec + [
            pl.BlockSpec((1, tc), lambda j: (0, j)),
            pl.BlockSpec((4, half, half), lambda j: (0, 0, 0), pipeline_mode=pl.Buffered(1)),
            _full((seq, 2)),
        ],
        out_specs=pl.BlockSpec((4, seq, tc), lambda j: (0, 0, j)),
        out_shape=jax.ShapeDtypeStruct((4, seq, d), F32),
        scratch_shapes=[pltpu.VMEM((seq, w), F32)] + [pltpu.VMEM((seq, V7X_LANES), F32)] * (2 * tc // V7X_LANES),
        compiler_params=_params("arbitrary"),
        name="hyena_filter",
    )(z, f1w, f1b, f1f, f2w, f2b, f2f, f3w, f3w, f3w, f3w, deltas, fwd, rot)


def _hyena_conv_kernel(*refs, gt, pieces):
    x1_refs, x2_refs, v_refs = refs[:pieces], refs[pieces:2 * pieces], refs[2 * pieces:3 * pieces]
    fwd_ref, inv_ref, k_ref, skip_ref, o_ref = refs[3 * pieces:3 * pieces + 5]
    ue_scr, uo_scr, ze_scr, zo_scr, ye_scr, yo_scr = refs[3 * pieces + 5:3 * pieces + 11]
    out_scrs = refs[3 * pieces + 11:]
    seq = o_ref.shape[0]
    half = seq // 2
    even = pl.ds(0, half, stride=2)
    odd = pl.ds(1, half, stride=2)

    def samples(piece_refs, rows):
        return jnp.concatenate([r[rows, :] for r in piece_refs], axis=1)

    def longconv(order):
        ye_scr[...] = jnp.zeros_like(ye_scr)
        yo_scr[...] = jnp.zeros_like(yo_scr)

        def body(c, carry):
            g0 = pl.multiple_of(c * gt, gt)
            rows = pl.ds(g0, gt)
            ue = ue_scr[...]
            uo = uo_scr[...]
            a = _dot(fwd_ref[0, rows, :], ue)
            b = _dot(fwd_ref[1, rows, :], ue)
            cc = _dot(fwd_ref[2, rows, :], uo)
            d = _dot(fwd_ref[3, rows, :], uo)

            def times_filter(first, ur, ui):
                kr = k_ref[2 * order, pl.ds(first + g0, gt), :]
                ki = k_ref[2 * order + 1, pl.ds(first + g0, gt), :]
                return kr * ur + ki * ui, kr * ui - ki * ur

            yra, yia = times_filter(0, a + cc, b + d)
            yrb, yib = times_filter(half, b - d, a - cc)
            ye_scr[...] += (_dot(inv_ref[0, :, rows], (yra + yib).astype(BF))
                            + _dot(inv_ref[1, :, rows], (yia + yrb).astype(BF)))
            yo_scr[...] += (_dot(inv_ref[2, :, rows], (yra - yib).astype(BF))
                            + _dot(inv_ref[3, :, rows], (yia - yrb).astype(BF)))
            return carry

        lax.fori_loop(0, half // gt, body, 0)

    scale = 1.0 / seq
    ve = samples(v_refs, even)
    vo = samples(v_refs, odd)
    ue_scr[...] = ve.astype(BF)
    uo_scr[...] = vo.astype(BF)
    longconv(0)
    ze = samples(x1_refs, even) * (ye_scr[...] * scale + ve * skip_ref[0:1, :])
    zo = samples(x1_refs, odd) * (yo_scr[...] * scale + vo * skip_ref[0:1, :])
    ze_scr[...] = ze
    zo_scr[...] = zo
    ue_scr[...] = ze.astype(BF)
    uo_scr[...] = zo.astype(BF)
    longconv(1)
    oe = samples(x2_refs, even) * (ye_scr[...] * scale + ze_scr[...] * skip_ref[1:2, :])
    oo = samples(x2_refs, odd) * (yo_scr[...] * scale + zo_scr[...] * skip_ref[1:2, :])
    for k, out_scr in enumerate(out_scrs):
        lanes = slice(k * V7X_LANES, (k + 1) * V7X_LANES)
        out_scr[even, :] = oe[:, lanes]
        out_scr[odd, :] = oo[:, lanes]
        o_ref[:, lanes] = out_scr[...].astype(BF)


def _hyena_conv(u3, fwd, inv, kspec, skip, nb, seq, d):
    tc = _tile(d, CHAN_TILE)
    nc = d // tc
    half = seq // 2
    gt = _tile(half, FREQ_TILE)
    pieces = tc // V7X_LANES
    once = dict(pipeline_mode=pl.Buffered(1))
    piece_specs = [pl.BlockSpec((seq, V7X_LANES), lambda j, b, o=o, k=k: (b, (o * nc + j) * pieces + k))
                   for o in range(3) for k in range(pieces)]
    return pl.pallas_call(
        functools.partial(_hyena_conv_kernel, gt=gt, pieces=pieces),
        scratch_shapes=[pltpu.VMEM((half, tc), BF), pltpu.VMEM((half, tc), BF)]
        + [pltpu.VMEM((half, tc), F32)] * 4 + [pltpu.VMEM((seq, V7X_LANES), F32)] * pieces,
        grid=(nc, nb),
        in_specs=piece_specs + [
            pl.BlockSpec((4, half, half), lambda j, b: (0, 0, 0), **once),
            pl.BlockSpec((4, half, half), lambda j, b: (0, 0, 0), **once),
            pl.BlockSpec((4, seq, tc), lambda j, b: (0, 0, j), **once),
            pl.BlockSpec((2, tc), lambda j, b: (0, j)),
        ],
        out_specs=pl.BlockSpec((seq, tc), lambda j, b: (b, j)),
        out_shape=jax.ShapeDtypeStruct((nb * seq, d), BF),
        compiler_params=_params("parallel", "parallel"),
        name="hyena_conv",
    )(*([u3] * (3 * pieces)), fwd, inv, kspec, skip)


def _moe_block(streams, mods, layer, fg, w_gate, w_up, w_down, nb, final=False):
    n_experts = streams[0][2].shape[0]
    group = _sc_group(nb)
    routed = []
    for xs, hn, aff, t, ctx_row in streams:
        cap = max(1, EC_CAPACITY * t // n_experts)
        posm, post, gatet = _route(aff, nb, t, cap)
        xin = None if hn.dtype == jnp.int32 else _gather(posm, hn, nb, t, cap)
        routed.append((posm, post, gatet, cap, xin))
    ys = [[] for _ in streams]
    for e0 in range(0, n_experts, group):
        xins = []
        for (xs, hn, aff, t, ctx_row), (posm, post, gatet, cap, xin) in zip(streams, routed):
            if xin is None:
                xins.append((_sc_gather(hn, posm, e0, nb, n_experts, t, cap), 0))
            else:
                xins.append((xin, e0))
        for acc, y in zip(ys, _experts(xins, w_gate, w_up, w_down, layer, e0, group)):
            acc.append(y)
    return [_combine(post, gatet, y, xs, mods, fg, nb, t, cap, ctx_row=ctx_row, final=final and ctx_row is None)
            for (xs, _, _, t, ctx_row), (_, post, gatet, cap, _), y in zip(streams, routed, ys)]


def kernel(x, c, ctx, c_ctx, w_ada, b_ada, norm1_g, norm2_g, final_g, a_w_in, a_conv, a_w_out, b_w_qkv, b_lq1, b_lk1, b_lq2, b_lk2, b_subln_g, b_w_out, c_w_in, c_conv, c_f1_w, c_f1_b, c_f1_freq, c_f2_w, c_f2_b, c_f2_freq, c_f3_w, c_skip, c_w_out, moe_router, moe_w_gate, moe_w_up, moe_w_down):
    nb, seq, d = x.shape
    ctx_len = ctx.shape[1]
    depth = w_ada.shape[0]
    assert nb < MOD_ROWS and d % (2 * DA_HEAD_DIM) == 0
    ctx_row = nb

    cvec = jnp.concatenate([c, c_ctx[None, :], jnp.zeros((MOD_ROWS - nb - 1, d), F32)], axis=0)
    mods_all = _ada(cvec, w_ada, b_ada).reshape(depth, MOD_ROWS, 6, d)

    attn_layers = [i for i in range(depth) if i % N_MIXERS == 1]
    last_ctx_read = max(attn_layers) if attn_layers else -1

    xs = x.reshape(nb * seq, d)
    cs_tok = ctx.reshape(nb * ctx_len, d)
    fg = final_g[None, :]

    for i in range(depth):
        kind, j = i % N_MIXERS, i // N_MIXERS
        update_ctx = i < last_ctx_read
        final = i == depth - 1
        mods = mods_all[i]
        g1 = norm1_g[i][None, :]
        g2 = norm2_g[i][None, :]
        wr = moe_router[i].T
        wrh, wrl = _split(wr)
        moe_w = (moe_w_gate, moe_w_up, moe_w_down)

        ctx_stream = []
        if kind == 0:
            w_in, w_out = a_w_in[j].astype(BF), a_w_out[j].astype(BF)
            if update_ctx:
                cn, chn, caff = _shortconv_layer(cs_tok, mods, g1, w_in, a_conv[j], w_out, g2, wrh, wrl,
                                                 ctx_len, ctx_row=ctx_row)
                ctx_stream = [(cn, chn, caff, ctx_len, ctx_row)]
            xn, hn, aff = _shortconv_layer(xs, mods, g1, w_in, a_conv[j], w_out, g2, wrh, wrl, seq)
        elif kind == 1:
            assert not update_ctx
            lam_init = 0.8 - 0.6 * math.exp(-0.3 * i)
            w_qkv = b_w_qkv[j].astype(BF)
            tables = [jnp.asarray(t) for t in _rope_tables(seq)]
            q, k, v = _qkv(xs, mods, g1, w_qkv, tables, seq, 3, True)
            ctab = [t[:ctx_len] for t in tables]
            kc, vc = _qkv(cs_tok, mods, g1, w_qkv[:, d:], ctab, ctx_len, 2, False, ctx_row=ctx_row)
            o = _attention(q, k, v, kc, vc, b_lq1[j][None, :], b_lk1[j][None, :], b_lq2[j][None, :],
                           b_lk2[j][None, :], b_subln_g[j][None, :], nb, seq, ctx_len, lam_init)
            xn, hn, aff = _outproj_layer(o, xs, mods, b_w_out[j].astype(BF), g2, wrh, wrl, seq)
        else:
            assert not update_ctx
            fwd_np, inv_np = _dft_half_tables(seq)
            fwd, inv = jnp.asarray(fwd_np).astype(BF), jnp.asarray(inv_np).astype(BF)
            w = c_f2_w.shape[-1]
            f1w = jnp.zeros((V7X_LANES, w), F32).at[:HY_EMB_DIM].set(c_f1_w[j])
            kspec = _hyena_filter(jnp.asarray(_hyena_features(seq)), f1w, c_f1_b[j][None, :], c_f1_freq[j][None, :],
                                  c_f2_w[j], c_f2_b[j][None, :], c_f2_freq[j][None, :], c_f3_w[j],
                                  jnp.asarray(_hyena_deltas(d)), fwd, jnp.asarray(_dft_half_shift(seq)), d)
            u3 = _hyena_in(xs, mods, g1, c_w_in[j].astype(BF), c_conv[j], seq)
            z = _hyena_conv(u3, fwd, inv, kspec, c_skip[j], nb, seq, d)
            xn, hn, aff = _outproj_layer(z, xs, mods, c_w_out[j].astype(BF), g2, wrh, wrl, seq)

        outs = _moe_block([(xn, hn, aff, seq, None)] + ctx_stream, mods, i, fg, *moe_w, nb, final=final)
        xs = outs[0]
        if ctx_stream:
            cs_tok = outs[1]

    return xs.reshape(nb, seq, d)
```

```python
import functools
import math

import jax
import jax.numpy as jnp
import numpy as np
from jax import lax
from jax.experimental import pallas as pl
from jax.experimental.pallas import tpu as pltpu

BF = jnp.bfloat16
F32 = jnp.float32

GRID_W = 64
DA_HEAD_DIM = 64
ROPE_AXIS_DIM = DA_HEAD_DIM // 2
ROPE_THETA = 10000.0
SUBLN_EPS = 1e-5
NORM_EPS = 1e-6
N_MIXERS = 3
EC_CAPACITY = 2
HY_EMB_DIM = 33
HY_BANDS = (HY_EMB_DIM - 1) // 2
HY_FAST_DECAY_PCT = 0.3
HY_SLOW_DECAY_PCT = 1.5
HY_DECAY_TARGET = 1e-2

V7X_LANES = 128
V7X_BF16_SUBLANES = 16
V7X_VMEM_BYTES = 64 * 2**20
VMEM_LIMIT = V7X_VMEM_BYTES - 8 * 2**20
HALO = V7X_BF16_SUBLANES
MOD_ROWS = 16
ROW_TILE = 512
EXPERT_ROW_STEPS = 2
PROJ_ROW_TILE = 1024
ATTN_Q_TILE = 256
CHAN_TILE = 256
ADA_COL_TILE = 1536
FREQ_TILE = 1024
SC_GATHER_ROWS = 64


def _params(*sem):
    return pltpu.CompilerParams(dimension_semantics=sem, vmem_limit_bytes=VMEM_LIMIT)


def _dot(a, b):
    return jnp.dot(a, b, preferred_element_type=F32)


def _dot_nt(a, b):
    return lax.dot_general(a, b, (((1,), (1,)), ((), ())), preferred_element_type=F32)


def _split(a):
    hi = a.astype(BF)
    lo = (a - hi.astype(F32)).astype(BF)
    return hi, lo


def _dot3(a, b):
    ah, al = _split(a)
    bh, bl = _split(b)
    return _dot(ah, bh) + (_dot(ah, bl) + _dot(al, bh))


def _sigmoid(a):
    return 1.0 / (1.0 + jnp.exp(-a))


def _norm_mod(x, g, shift, scale):
    ms = jnp.mean(x * x, axis=-1, keepdims=True)
    return (x * lax.rsqrt(ms + NORM_EPS) * g) * (1.0 + scale) + shift


def _tile(n, pref):
    t = min(n, pref)
    assert n % t == 0, (n, pref)
    return t


def _full(shape):
    nd = len(shape)
    return pl.BlockSpec(shape, lambda *_: (0,) * nd)


def _ada_kernel(c_ref, w_ref, b_ref, o_ref):
    c = c_ref[...]
    o_ref[0] = _dot3(c * _sigmoid(c), w_ref[0]) + b_ref[0]


def _ada(cvec, w_ada, b_ada):
    depth, d, n6 = w_ada.shape
    tn = _tile(n6, ADA_COL_TILE)
    return pl.pallas_call(
        _ada_kernel,
        grid=(depth, n6 // tn),
        in_specs=[
            _full((MOD_ROWS, d)),
            pl.BlockSpec((1, d, tn), lambda i, j: (i, 0, j)),
            pl.BlockSpec((1, 1, tn), lambda i, j: (i, 0, j)),
        ],
        out_specs=pl.BlockSpec((1, MOD_ROWS, tn), lambda i, j: (i, 0, j)),
        out_shape=jax.ShapeDtypeStruct((depth, MOD_ROWS, n6), F32),
        compiler_params=_params("parallel", "parallel"),
        name="ada",
    )(cvec, w_ada, b_ada.reshape(depth, 1, n6))


def _pack_pairs(hh):
    half = hh.shape[1] // 2
    lo = lax.shift_right_logical(pltpu.bitcast(hh[:, :half].astype(F32), jnp.uint32), jnp.uint32(16))
    hi = pltpu.bitcast(hh[:, half:].astype(F32), jnp.uint32) & jnp.uint32(0xFFFF0000)
    return pltpu.bitcast(lo | hi, jnp.int32)


def _unpack_pairs(words):
    u = pltpu.bitcast(words, jnp.uint32)
    lo = pltpu.bitcast(lax.shift_left(u, jnp.uint32(16)), F32).astype(BF)
    hi = pltpu.bitcast(u & jnp.uint32(0xFFFF0000), F32).astype(BF)
    return jnp.concatenate([lo, hi], axis=1)


def _residual_router(x, y, mod, g2_ref, wrh_ref, wrl_ref, xo_ref, hn_ref, aff_ref):
    xn = x + mod[2:3] * y
    xo_ref[...] = xn
    hn = _norm_mod(xn, g2_ref[...], mod[3:4], mod[4:5])
    hh, hl = _split(hn)
    hn_ref[...] = _pack_pairs(hh) if hn_ref.dtype == jnp.int32 else hh
    e = wrh_ref.shape[0]
    both = _dot_nt(jnp.concatenate([wrh_ref[...], wrl_ref[...]], axis=0), hh)
    logits = both[:e] + (_dot_nt(wrh_ref[...], hl) + both[e:])
    p = jnp.exp(logits - jnp.max(logits, axis=0, keepdims=True))
    aff_ref[...] = p / jnp.sum(p, axis=0, keepdims=True)


def _router_specs(n, d, e, tm, packed):
    in_specs = [_full((1, d)), _full((e, d)), _full((e, d))]
    hn_cols, hn_dtype = (d // 2, jnp.int32) if packed else (d, BF)
    out_specs = [
        pl.BlockSpec((tm, d), lambda i: (i, 0)),
        pl.BlockSpec((tm, hn_cols), lambda i: (i, 0)),
        pl.BlockSpec((e, tm), lambda i: (0, i)),
    ]
    out_shape = [
        jax.ShapeDtypeStruct((n, d), F32),
        jax.ShapeDtypeStruct((n, hn_cols), hn_dtype),
        jax.ShapeDtypeStruct((e, n), F32),
    ]
    return in_specs, out_specs, out_shape


def _halo_specs(n, d, tm):
    per = tm // HALO
    last = n // HALO - 1
    return [
        pl.BlockSpec((HALO, d), lambda i: (jnp.maximum(i * per - 1, 0), 0)),
        pl.BlockSpec((tm, d), lambda i: (i, 0)),
        pl.BlockSpec((HALO, d), lambda i: (jnp.minimum((i + 1) * per, last), 0)),
    ]


def _mod_spec(d, tiles_per_seq, ctx_row):
    if ctx_row is None:
        return pl.BlockSpec((1, 6, d), lambda i: (i // tiles_per_seq, 0, 0))
    return pl.BlockSpec((1, 6, d), lambda i: (ctx_row, 0, 0))


def _halo_rows(xp_ref, x_ref, xn_ref, g_ref, mod, tiles_per_seq):
    t = pl.program_id(0) % tiles_per_seq
    g = g_ref[...]
    hp, hx, hn = [_norm_mod(r[...], g, mod[0:1], mod[1:2]) for r in (xp_ref, x_ref, xn_ref)]
    hp = jnp.where(t == 0, 0.0, hp)
    hn = jnp.where(t == tiles_per_seq - 1, 0.0, hn)
    return jnp.concatenate([hp, hx, hn], axis=0).astype(BF)


def _conv3(s, w_ref, tm):
    n = s.shape[0]
    prev = pltpu.roll(s, 1, 0)[HALO:HALO + tm]
    nxt = pltpu.roll(s, n - 1, 0)[HALO:HALO + tm]
    return prev * w_ref[0:1, :] + s[HALO:HALO + tm] * w_ref[1:2, :] + nxt * w_ref[2:3, :]


def _shortconv_kernel(xp_ref, x_ref, xn_ref, mod_ref, g1_ref, win_ref, wconv_ref, wout_ref,
                      g2_ref, wrh_ref, wrl_ref, xo_ref, hn_ref, aff_ref, *, tiles_per_seq):
    tm, d = x_ref.shape
    mod = mod_ref[0]
    h = _halo_rows(xp_ref, x_ref, xn_ref, g1_ref, mod, tiles_per_seq)
    gate = _dot(h[HALO:HALO + tm], win_ref[:, 0:d])
    s = _dot(h, win_ref[:, d:2 * d]) * _dot(h, win_ref[:, 2 * d:3 * d])
    z = (gate * _conv3(s, wconv_ref, tm)).astype(BF)
    y = _dot(z, wout_ref[...])
    _residual_router(x_ref[...], y, mod, g2_ref, wrh_ref, wrl_ref, xo_ref, hn_ref, aff_ref)


def _shortconv_layer(xs, mods, g1, w_in, w_conv, w_out, g2, wrh, wrl, seq, ctx_row=None):
    n, d = xs.shape
    e = wrh.shape[0]
    tm = _tile(seq, PROJ_ROW_TILE)
    tps = seq // tm
    r_in, r_out, r_shape = _router_specs(n, d, e, tm, packed=ctx_row is None)
    return pl.pallas_call(
        functools.partial(_shortconv_kernel, tiles_per_seq=tps),
        grid=(n // tm,),
        in_specs=_halo_specs(n, d, tm) + [
            _mod_spec(d, tps, ctx_row), _full((1, d)), _full((d, 3 * d)), _full((3, d)), _full((d, d)),
        ] + r_in,
        out_specs=r_out,
        out_shape=r_shape,
        compiler_params=_params("parallel"),
        name="shortconv_layer",
    )(xs, xs, xs, mods, g1, w_in, w_conv, w_out, g2, wrh, wrl)


def _outproj_kernel(o_ref, x_ref, mod_ref, wout_ref, g2_ref, wrh_ref, wrl_ref, xo_ref, hn_ref, aff_ref):
    y = _dot(o_ref[...], wout_ref[...])
    _residual_router(x_ref[...], y, mod_ref[0], g2_ref, wrh_ref, wrl_ref, xo_ref, hn_ref, aff_ref)


def _outproj_layer(o, xs, mods, w_out, g2, wrh, wrl, seq):
    n, d = xs.shape
    e = wrh.shape[0]
    tm = _tile(seq, PROJ_ROW_TILE)
    tps = seq // tm
    r_in, r_out, r_shape = _router_specs(n, d, e, tm, packed=True)
    return pl.pallas_call(
        _outproj_kernel,
        grid=(n // tm,),
        in_specs=[pl.BlockSpec((tm, d), lambda i: (i, 0)), pl.BlockSpec((tm, d), lambda i: (i, 0)),
                  _mod_spec(d, tps, None), _full((d, d))] + r_in,
        out_specs=r_out,
        out_shape=r_shape,
        compiler_params=_params("parallel"),
        name="outproj_layer",
    )(o, xs, mods, w_out, g2, wrh, wrl)


def _excl_cumsum_lanes(m):
    rows, t = m.shape
    a = lax.broadcasted_iota(jnp.int32, (V7X_LANES, V7X_LANES), 0)
    b = lax.broadcasted_iota(jnp.int32, (V7X_LANES, V7X_LANES), 1)
    tri = jnp.where(a < b, 1.0, 0.0).astype(BF)
    carry = jnp.zeros((rows, 1), F32)
    out = []
    for c in range(t // V7X_LANES):
        blk = m[:, c * V7X_LANES:(c + 1) * V7X_LANES]
        out.append(_dot(blk.astype(BF), tri) + carry)
        carry = carry + jnp.sum(blk, axis=1, keepdims=True)
    return jnp.concatenate(out, axis=1)


def _route_kernel(aff_ref, posm_ref, post_ref, gatet_ref, *, cap, nb):
    e = aff_ref.shape[0]
    t = aff_ref.shape[1] // nb
    aff = jnp.concatenate([aff_ref[:, b * t:(b + 1) * t] for b in range(nb)], axis=0)
    bits = pltpu.bitcast(aff, jnp.int32)

    def step(i, thr):
        cand = thr | jnp.left_shift(jnp.int32(1), 30 - i)
        cnt = jnp.sum(jnp.where(bits >= cand, 1.0, 0.0), axis=1, keepdims=True)
        return jnp.where(cnt >= cap, cand, thr)

    thr = lax.fori_loop(0, 31, step, jnp.zeros((nb * e, 1), jnp.int32))
    gt = jnp.where(bits > thr, 1.0, 0.0)
    eq = jnp.where(bits == thr, 1.0, 0.0)
    need = cap - jnp.sum(gt, axis=1, keepdims=True)
    sel = gt + eq * jnp.where(_excl_cumsum_lanes(eq) < need, 1.0, 0.0)
    pos = jnp.where(sel > 0.0, _excl_cumsum_lanes(sel), -1.0)
    gate = sel * aff
    posm_ref[...] = pos.astype(jnp.int32)
    pad = jnp.zeros((V7X_LANES - e, t), F32)
    for b in range(nb):
        rows = slice(b * e, (b + 1) * e)
        post_ref[b * t:(b + 1) * t, :] = jnp.concatenate([pos[rows], pad - 1.0], axis=0).T
        gatet_ref[b * t:(b + 1) * t, :] = jnp.concatenate([gate[rows], pad], axis=0).T


def _route(aff, nb, t, cap):
    e = aff.shape[0]
    posm, post, gatet = pl.pallas_call(
        functools.partial(_route_kernel, cap=cap, nb=nb),
        grid=(1,),
        in_specs=[_full((e, nb * t))],
        out_specs=[_full((nb * e, t)), _full((nb * t, V7X_LANES)), _full((nb * t, V7X_LANES))],
        out_shape=[
            jax.ShapeDtypeStruct((nb * e, t), jnp.int32),
            jax.ShapeDtypeStruct((nb * t, V7X_LANES), F32),
            jax.ShapeDtypeStruct((nb * t, V7X_LANES), F32),
        ],
        compiler_params=_params("arbitrary"),
        name="route",
    )(aff)
    return posm, post, gatet


def _sc_gather(table, posm, e0, nb, n_experts, t, cap):
    from jax.experimental.pallas import tpu_sc as plsc

    info = plsc.get_sparse_core_info()
    cores, lanes = info.num_cores, info.num_lanes
    workers = cores * info.num_subcores
    group = workers // nb
    words = table.shape[1]
    assert workers % nb == 0 and n_experts % group == 0 and cap % SC_GATHER_ROWS == 0 and t % lanes == 0

    @functools.partial(
        pl.kernel,
        mesh=plsc.VectorSubcoreMesh(core_axis_name="core", subcore_axis_name="subcore"),
        compiler_params=pltpu.CompilerParams(needs_layout_passes=False),
        out_type=jax.ShapeDtypeStruct((group * nb * cap, words), jnp.int32),
        scratch_types=[pltpu.VMEM((t,), jnp.int32), pltpu.VMEM((cap,), jnp.int32),
                       pltpu.VMEM((SC_GATHER_ROWS, words), jnp.int32), pltpu.SemaphoreType.DMA],
    )
    def gather(table_hbm, posm_hbm, out_hbm, pos_v, idx_v, rows_v, sem):
        w = lax.axis_index("subcore") * cores + lax.axis_index("core")
        e_local = w // nb
        b = w % nb
        pltpu.sync_copy(posm_hbm.at[b * n_experts + e0 + e_local], pos_v)

        @pl.loop(0, t // lanes)
        def _(i):
            p = pos_v[pl.ds(i * lanes, lanes)]
            token = lax.iota(jnp.int32, lanes) + (i * lanes + b * t)
            plsc.store_scatter(idx_v, [p], token, mask=p >= 0)

        out_base = (e_local * nb + b) * cap

        @pl.loop(0, cap // SC_GATHER_ROWS)
        def _(j):
            pltpu.async_copy(table_hbm.at[idx_v.at[pl.ds(j * SC_GATHER_ROWS, SC_GATHER_ROWS)]], rows_v, sem).wait()
            pltpu.sync_copy(rows_v, out_hbm.at[pl.ds(out_base + j * SC_GATHER_ROWS, SC_GATHER_ROWS)])

    return gather(table, posm).reshape(group, nb * cap, words)


def _sc_group(nb):
    from jax.experimental.pallas import tpu_sc as plsc

    info = plsc.get_sparse_core_info()
    return info.num_cores * info.num_subcores // nb


def _gather_kernel(posm_ref, hn_ref, o_ref, *, cap):
    e = posm_ref.shape[0]
    t = hn_ref.shape[0]
    slot = lax.broadcasted_iota(jnp.int32, (cap, t), 0)

    def body(k, carry):
        row = posm_ref[pl.ds(k, 1), :]
        onehot = jnp.where(row == slot, 1.0, 0.0).astype(BF)
        o_ref[k] = _dot(onehot, hn_ref[...]).astype(BF)
        return carry

    lax.fori_loop(0, e, body, 0)


def _gather(posm, hn, nb, t, cap):
    e = posm.shape[0] // nb
    d = hn.shape[1]
    return pl.pallas_call(
        functools.partial(_gather_kernel, cap=cap),
        grid=(nb,),
        in_specs=[pl.BlockSpec((e, t), lambda b: (b, 0)), pl.BlockSpec((t, d), lambda b: (b, 0))],
        out_specs=pl.BlockSpec((e, cap, d), lambda b: (0, b, 0)),
        out_shape=jax.ShapeDtypeStruct((e, nb * cap, d), BF),
        compiler_params=_params("parallel"),
        name="moe_gather",
    )(posm, hn)


def _expert_kernel(*refs, pieces, per):
    n_in = sum(pieces)
    wg_ref, wu_ref, wd_ref = refs[n_in:n_in + 3]
    y_refs = refs[n_in + 3:n_in + 3 + len(pieces)]
    w_scr = refs[-1]

    @pl.when(pl.program_id(1) == 0)
    def _():
        w_scr[0] = wg_ref[0, 0].astype(BF)
        w_scr[1] = wu_ref[0, 0].astype(BF)
        w_scr[2] = wd_ref[0, 0].astype(BF)

    which = pl.program_id(0) // per

    start = 0
    for n_pieces, y_ref in zip(pieces, y_refs):
        x_refs = refs[start:start + n_pieces]
        start += n_pieces
        rows = x_refs[0].shape[1]
        tr = _tile(rows, ROW_TILE)

        def body(j, carry, x_refs=x_refs, y_ref=y_ref, tr=tr):
            r0 = pl.multiple_of(j * tr, tr)
            xs = x_refs[0][0, pl.ds(r0, tr), :]
            for p in range(1, len(x_refs)):
                xs = jnp.where(which == p, x_refs[p][0, pl.ds(r0, tr), :], xs)
            if xs.dtype == jnp.int32:
                xs = _unpack_pairs(xs)
            a = _dot(xs, w_scr[0])
            b = _dot(xs, w_scr[1])
            hm = (a * _sigmoid(a) * b).astype(BF)
            y_ref[0, pl.ds(r0, tr), :] = _dot(hm, w_scr[2]).astype(BF)
            return carry

        lax.fori_loop(0, rows // tr, body, 0)


def _experts(streams, w_gate, w_up, w_down, layer, e0, count, per):
    d = w_gate.shape[-2]
    f = w_gate.shape[-1]
    assert f == d
    wspec = pl.BlockSpec((1, 1, d, f), lambda k, j: (layer, e0 + k, 0, 0))
    xspecs, xargs = [], []
    for arrays, first in streams:
        for p, x in enumerate(arrays):
            if len(arrays) == 1:
                index = lambda k, j, first=first: (first + k, j, 0)
            else:
                index = lambda k, j, p=p: (
                    jnp.clip(k - p * per, 0, per - 1),
                    jnp.where(k < p * per, 0, jnp.where(k >= (p + 1) * per, EXPERT_ROW_STEPS - 1, j)), 0)
            xspecs.append(pl.BlockSpec((1, x.shape[1] // EXPERT_ROW_STEPS, x.shape[2]), index))
            xargs.append(x)
    rows = [arrays[0].shape[1] for arrays, _ in streams]
    return pl.pallas_call(
        functools.partial(_expert_kernel, pieces=tuple(len(arrays) for arrays, _ in streams), per=per),
        grid=(count, EXPERT_ROW_STEPS),
        in_specs=xspecs + [wspec, wspec, wspec],
        out_specs=[pl.BlockSpec((1, r // EXPERT_ROW_STEPS, d), lambda k, j: (k, j, 0)) for r in rows],
        out_shape=[jax.ShapeDtypeStruct((count, r, d), BF) for r in rows],
        scratch_shapes=[pltpu.VMEM((3, d, f), BF)],
        compiler_params=_params("parallel", "arbitrary"),
        name="moe_experts",
    )(*xargs, w_gate, w_up, w_down)


def _combine_kernel(post_ref, gatet_ref, *refs, cap, final):
    y_refs = refs[:-4]
    x_ref, mod_ref, fg_ref, o_ref = refs[-4:]
    d = y_refs[0].shape[2]
    e = sum(r.shape[0] for r in y_refs)
    tq = x_ref.shape[0]
    pt = post_ref[...]
    gt = gatet_ref[...]
    if cap % V7X_LANES == 0:
        slot = lax.broadcasted_iota(jnp.int32, (tq, cap), 1).astype(F32)
        pieces = [jnp.where(pt[:, k:k + 1] == slot, gt[:, k:k + 1], 0.0).astype(BF) for k in range(e)]
        scat = jnp.concatenate(pieces, axis=1)
    else:
        slot = lax.broadcasted_iota(jnp.int32, (tq, e * cap), 1).astype(F32)
        scat = jnp.zeros((tq, e * cap), F32)
        for k in range(e):
            pk = pt[:, k:k + 1]
            scat = jnp.where((pk >= 0.0) & (pk + float(k * cap) == slot), gt[:, k:k + 1], scat)
        scat = scat.astype(BF)
    out, col = None, 0
    for y_ref in y_refs:
        width = y_ref.shape[0] * cap
        part = _dot(scat[:, col:col + width], y_ref[...].reshape(width, d))
        out = part if out is None else out + part
        col += width
    xn = x_ref[...] + mod_ref[0][5:6] * out
    if final:
        ms = jnp.mean(xn * xn, axis=-1, keepdims=True)
        xn = xn * lax.rsqrt(ms + NORM_EPS) * fg_ref[...]
    o_ref[...] = xn


def _combine(post, gatet, ys, xs, mods, fg, nb, t, cap, ctx_row=None, final=False):
    n, d = xs.shape
    tq = _tile(t, ROW_TILE)
    tpb = t // tq
    if ctx_row is None:
        mspec = pl.BlockSpec((1, 6, d), lambda b, i: (b, 0, 0))
    else:
        mspec = pl.BlockSpec((1, 6, d), lambda b, i: (ctx_row, 0, 0))
    return pl.pallas_call(
        functools.partial(_combine_kernel, cap=cap, final=final),
        grid=(nb, tpb),
        in_specs=[
            pl.BlockSpec((tq, V7X_LANES), lambda b, i: (b * tpb + i, 0)),
            pl.BlockSpec((tq, V7X_LANES), lambda b, i: (b * tpb + i, 0)),
        ] + [pl.BlockSpec((y.shape[0], cap, d), lambda b, i: (0, b, 0)) for y in ys] + [
            pl.BlockSpec((tq, d), lambda b, i: (b * tpb + i, 0)),
            mspec,
            _full((1, d)),
        ],
        out_specs=pl.BlockSpec((tq, d), lambda b, i: (b * tpb + i, 0)),
        out_shape=jax.ShapeDtypeStruct((n, d), F32),
        compiler_params=_params("parallel", "parallel"),
        name="moe_combine",
    )(post, gatet, *ys, xs, mods, fg)


def _rope_tables(seq):
    rows = seq // GRID_W
    row = np.repeat(np.arange(rows, dtype=np.float32), GRID_W)
    col = np.tile(np.arange(GRID_W, dtype=np.float32), rows)
    inv_freq = (ROPE_THETA ** (-np.arange(0, ROPE_AXIS_DIM, 2, dtype=np.float32) / ROPE_AXIS_DIM)).astype(np.float32)
    lane = np.arange(2 * DA_HEAD_DIM)
    within = lane % DA_HEAD_DIM
    axis = within // ROPE_AXIS_DIM
    half = (within % ROPE_AXIS_DIM) // (ROPE_AXIS_DIM // 2)
    idx = within % (ROPE_AXIS_DIM // 2)
    pos = np.where(axis[None, :] == 0, row[:, None], col[:, None])
    ang = (pos * inv_freq[idx][None, :]).astype(np.float32)
    cos = np.cos(ang).astype(np.float32)
    sin = np.sin(ang).astype(np.float32)
    sin_lo = np.where(half[None, :] == 1, sin, 0.0).astype(np.float32)
    sin_hi = np.where(half[None, :] == 0, -sin, 0.0).astype(np.float32)
    return cos, sin_lo, sin_hi


def _qkv_kernel(x_ref, mod_ref, g_ref, w_ref, cos_ref, sa_ref, sb_ref, *o_refs, rope):
    d = x_ref.shape[1]
    mod = mod_ref[0]
    h = _norm_mod(x_ref[...], g_ref[...], mod[0:1], mod[1:2]).astype(BF)
    slab = 2 * DA_HEAD_DIM
    shift = ROPE_AXIS_DIM // 2
    for j, o_ref in enumerate(o_refs):
        u = _dot(h, w_ref[:, j * d:(j + 1) * d])
        if rope and j < 2:
            scale = DA_HEAD_DIM ** -0.5 * math.log2(math.e) if j == 0 else 1.0
            cos, sa, sb = cos_ref[...] * scale, sa_ref[...] * scale, sb_ref[...] * scale
            for hd in range(d // slab):
                xs = u[:, hd * slab:(hd + 1) * slab]
                r = xs * cos + pltpu.roll(xs, shift, 1) * sa + pltpu.roll(xs, slab - shift, 1) * sb
                o_ref[:, hd * slab:(hd + 1) * slab] = r.astype(BF)
        else:
            o_ref[...] = u.astype(BF)


def _qkv(xs, mods, g1, w, tables, seq, nout, rope, ctx_row=None):
    n, d = xs.shape
    tm = _tile(seq, PROJ_ROW_TILE)
    tps = seq // tm
    slab = 2 * DA_HEAD_DIM
    tspec = pl.BlockSpec((tm, slab), lambda i: (i % tps, 0))
    return pl.pallas_call(
        functools.partial(_qkv_kernel, rope=rope),
        grid=(n // tm,),
        in_specs=[pl.BlockSpec((tm, d), lambda i: (i, 0)), _mod_spec(d, tps, ctx_row), _full((1, d)),
                  _full((d, nout * d)), tspec, tspec, tspec],
        out_specs=[pl.BlockSpec((tm, d), lambda i: (i, 0))] * nout,
        out_shape=[jax.ShapeDtypeStruct((n, d), BF)] * nout,
        compiler_params=_params("parallel"),
        name="attn_qkv",
    )(xs, mods, g1, w, *tables)


def _attn_kernel(q_ref, k_ref, v_ref, kc_ref, vc_ref, lq1_ref, lk1_ref, lq2_ref, lk2_ref, sg_ref, o_ref,
                 sl_a, sc_a, sl_b, sc_b, va_scr, vca_scr, *, lam_init, tq):
    seq = q_ref.shape[0]
    lam = (jnp.exp(jnp.sum(lq1_ref[...] * lk1_ref[...], axis=1, keepdims=True))
           - jnp.exp(jnp.sum(lq2_ref[...] * lk2_ref[...], axis=1, keepdims=True)) + lam_init)
    slots = ((sl_a, sc_a), (sl_b, sc_b))

    def scores(i, slot):
        sl_ref, sc_ref = slot
        q = q_ref[i * tq:(i + 1) * tq, :]
        lane = lax.broadcasted_iota(jnp.int32, q.shape, 1)
        zero = jnp.zeros_like(q)
        for mp, qm in enumerate((jnp.where(lane < DA_HEAD_DIM, q, zero), jnp.where(lane >= DA_HEAD_DIM, q, zero))):
            sl_ref[mp] = _dot_nt(qm, k_ref[...])
            sc_ref[mp] = _dot_nt(qm, kc_ref[...])

    slab = v_ref.shape[1]
    va_scr[:, 0:slab] = v_ref[...]
    va_scr[:, slab:2 * slab] = jnp.ones_like(v_ref)
    vca_scr[:, 0:slab] = vc_ref[...]
    vca_scr[:, slab:2 * slab] = jnp.ones_like(vc_ref)

    def unnormalised(sl_ref, sc_ref, mp):
        s_l = sl_ref[mp]
        s_c = sc_ref[mp]
        m = jnp.maximum(jnp.max(s_l, axis=1, keepdims=True), jnp.max(s_c, axis=1, keepdims=True))
        p_l = jnp.exp2(s_l - m).astype(BF)
        p_c = jnp.exp2(s_c - m).astype(BF)
        both = _dot(p_l, va_scr[...]) + _dot(p_c, vca_scr[...])
        return both[:, 0:slab], both[:, slab:slab + 1]

    def attend(i, slot):
        o1, t1 = unnormalised(*slot, 0)
        o2, t2 = unnormalised(*slot, 1)
        o = o1 * (1.0 / t1) - o2 * (lam / t2)
        ms = jnp.mean(o * o, axis=-1, keepdims=True)
        o_ref[i * tq:(i + 1) * tq, :] = (o * lax.rsqrt(ms + SUBLN_EPS) * sg_ref[...] * (1.0 - lam_init)).astype(BF)

    n = seq // tq
    scores(0, slots[0])
    for i in range(n):
        if i + 1 < n:
            scores(i + 1, slots[(i + 1) % 2])
        attend(i, slots[i % 2])


def _attention(q, k, v, kc, vc, lq1, lk1, lq2, lk2, sg, nb, seq, ctx_len, lam_init):
    n, d = q.shape
    slab = 2 * DA_HEAD_DIM
    heads = d // slab
    tq = _tile(seq, ATTN_Q_TILE)
    small = _full((1, DA_HEAD_DIM))
    lat = pl.BlockSpec((seq, slab), lambda b, h: (b, h))
    ctx = pl.BlockSpec((ctx_len, slab), lambda b, h: (b, h))
    score_scratch = [pltpu.VMEM((2, tq, seq), F32), pltpu.VMEM((2, tq, ctx_len), F32)]
    return pl.pallas_call(
        functools.partial(_attn_kernel, lam_init=lam_init, tq=tq),
        grid=(nb, heads),
        in_specs=[lat, lat, lat, ctx, ctx, small, small, small, small, _full((1, slab))],
        out_specs=lat,
        out_shape=jax.ShapeDtypeStruct((n, d), BF),
        scratch_shapes=score_scratch + score_scratch + [pltpu.VMEM((seq, 2 * slab), BF),
                                                        pltpu.VMEM((ctx_len, 2 * slab), BF)],
        compiler_params=_params("parallel", "parallel"),
        name="diff_attention",
    )(q, k, v, kc, vc, lq1, lk1, lq2, lk2, sg)


def _hyena_in_kernel(xp_ref, x_ref, xn_ref, mod_ref, g1_ref, win_ref, wconv_ref, o_ref, *, tiles_per_seq):
    tm, d = x_ref.shape
    h = _halo_rows(xp_ref, x_ref, xn_ref, g1_ref, mod_ref[0], tiles_per_seq)
    for j in range(3):
        u = _dot(h, win_ref[:, j * d:(j + 1) * d])
        o_ref[:, j * d:(j + 1) * d] = _conv3(u, wconv_ref.at[:, j * d:(j + 1) * d], tm)


def _hyena_in(xs, mods, g1, w_in, w_conv, seq):
    n, d = xs.shape
    tm = _tile(seq, PROJ_ROW_TILE)
    tps = seq // tm
    return pl.pallas_call(
        functools.partial(_hyena_in_kernel, tiles_per_seq=tps),
        grid=(n // tm,),
        in_specs=_halo_specs(n, d, tm) + [_mod_spec(d, tps, None), _full((1, d)), _full((d, 3 * d)), _full((3, 3 * d))],
        out_specs=pl.BlockSpec((tm, 3 * d), lambda i: (i, 0)),
        out_shape=jax.ShapeDtypeStruct((n, 3 * d), F32),
        compiler_params=_params("parallel"),
        name="hyena_in",
    )(xs, xs, xs, mods, g1, w_in, w_conv)


def _dft_half_shift(seq):
    order = np.concatenate([np.arange(seq // 2), seq - 1 - np.arange(seq // 2)])
    half = np.pi * (order + 0.5) / (2 * seq)
    return np.stack([np.cos(half), np.sin(half)], axis=1).astype(np.float32)


def _dft_half_tables(seq):
    h = seq // 2
    th = 2.0 * np.pi * (np.arange(h, dtype=np.float64) + 0.5) / seq
    s = np.arange(h, dtype=np.float64)
    even, odd = np.outer(th, s + 0.25), np.outer(th, s + 0.75)
    fwd = np.stack([np.cos(even), np.sin(even), np.cos(odd), np.sin(odd)]).astype(np.float32)
    inv = np.ascontiguousarray(np.transpose(fwd, (0, 2, 1)))
    return fwd, inv


def _hyena_features(seq):
    t = np.linspace(0.0, 1.0, seq, dtype=np.float32)[:, None]
    w = (2.0 * math.pi * np.arange(seq, dtype=np.float32)[:, None] / seq).astype(np.float32)
    f = np.linspace(1e-4, HY_BANDS - 1, HY_BANDS, dtype=np.float32)[None, :]
    z = np.concatenate([t, np.cos(f * w), -np.sin(f * w)], axis=-1).astype(np.float32)
    zp = np.zeros((seq, V7X_LANES), np.float32)
    zp[:, :HY_EMB_DIM] = z
    return zp


def _hyena_deltas(d):
    max_decay = math.log(HY_DECAY_TARGET) / HY_FAST_DECAY_PCT
    min_decay = math.log(HY_DECAY_TARGET) / HY_SLOW_DECAY_PCT
    return np.abs(np.linspace(min_decay, max_decay, d, dtype=np.float32))[None, :].astype(np.float32)


def _hyena_filter_kernel(z_ref, f1w_ref, f1b_ref, f1f_ref, f2w_ref, f2b_ref, f2f_ref,
                         f3a_ref, f3b_ref, f3c_ref, f3d_ref, delta_ref, fwd_ref, rot_ref, k_ref, hid_scr, *pm_scrs):
    seq = z_ref.shape[0]
    half = seq // 2
    tc = delta_ref.shape[1]
    even = pl.ds(0, half, stride=2)
    odd = pl.ds(1, half, stride=2)

    @pl.when(pl.program_id(0) == 0)
    def _():
        h1 = jnp.sin(f1f_ref[...] * (_dot3(z_ref[...], f1w_ref[...]) + f1b_ref[...]))
        hid_scr[...] = jnp.sin(f2f_ref[...] * (_dot3(h1, f2w_ref[...]) + f2b_ref[...]))

    hid = hid_scr[...]
    decay = jnp.exp(-z_ref[:, 0:1] * delta_ref[...])
    row = lax.broadcasted_iota(jnp.int32, (seq, 1), 0)
    cr = rot_ref[:, 0:1]
    sr = rot_ref[:, 1:2]
    for order, (f3_fwd_ref, f3_bwd_ref) in enumerate(((f3a_ref, f3b_ref), (f3c_ref, f3d_ref))):
        h_fwd = _dot3(hid, f3_fwd_ref[...]) * decay
        h_bwd = jnp.where(row == 0, 0.0, _dot3(hid, f3_bwd_ref[...]) * decay)
        pm = jnp.concatenate([h_fwd + h_bwd, h_bwd - h_fwd], axis=1)
        for k, scr in enumerate(pm_scrs):
            scr[...] = pm[:, k * V7X_LANES:(k + 1) * V7X_LANES]
        x_even = jnp.concatenate([scr[even, :] for scr in pm_scrs], axis=1).astype(BF)
        x_odd = jnp.concatenate([scr[odd, :] for scr in pm_scrs], axis=1).astype(BF)
        a = _dot(fwd_ref[0], x_even)
        b = _dot(fwd_ref[1], x_even)
        c = _dot(fwd_ref[2], x_odd)
        s = _dot(fwd_ref[3], x_odd)
        cos_sum = jnp.concatenate([a + c, b - s], axis=0)
        sin_sum = jnp.concatenate([b + s, a - c], axis=0)
        k_ref[2 * order] = cr * cos_sum[:, :tc] + sr * sin_sum[:, :tc]
        k_ref[2 * order + 1] = cr * sin_sum[:, tc:] - sr * cos_sum[:, tc:]


def _hyena_filter(z, f1w, f1b, f1f, f2w, f2b, f2f, f3w, deltas, fwd, rot, d):
    seq = z.shape[0]
    half = seq // 2
    w = f2w.shape[0]
    tc = _tile(d, CHAN_TILE)
    nc = d // tc
    f3spec = [pl.BlockSpec((w, tc), lambda j, o=o: (0, o * nc + j)) for o in range(4)]
    return pl.pallas_call(
        _hyena_filter_kernel,
        grid=(nc,),
        in_specs=[_full((seq, V7X_LANES)), _full((V7X_LANES, w)), _full((1, w)), _full((1, w)),
                  _full((w, w)), _full((1, w)), _full((1, w))] + f3spec + [
            pl.BlockSpec((1, tc), lambda j: (0, j)),
            pl.BlockSpec((4, half, half), lambda j: (0, 0, 0), pipeline_mode=pl.Buffered(1)),
            _full((seq, 2)),
        ],
        out_specs=pl.BlockSpec((4, seq, tc), lambda j: (0, 0, j)),
        out_shape=jax.ShapeDtypeStruct((4, seq, d), F32),
        scratch_shapes=[pltpu.VMEM((seq, w), F32)] + [pltpu.VMEM((seq, V7X_LANES), F32)] * (2 * tc // V7X_LANES),
        compiler_params=_params("arbitrary"),
        name="hyena_filter",
    )(z, f1w, f1b, f1f, f2w, f2b, f2f, f3w, f3w, f3w, f3w, deltas, fwd, rot)


def _hyena_conv_kernel(*refs, gt, pieces):
    x1_refs, x2_refs, v_refs = refs[:pieces], refs[pieces:2 * pieces], refs[2 * pieces:3 * pieces]
    fwd_ref, inv_ref, k_ref, skip_ref, o_ref = refs[3 * pieces:3 * pieces + 5]
    ue_scr, uo_scr, ze_scr, zo_scr, ye_scr, yo_scr = refs[3 * pieces + 5:3 * pieces + 11]
    out_scrs = refs[3 * pieces + 11:]
    seq = o_ref.shape[0]
    half = seq // 2
    even = pl.ds(0, half, stride=2)
    odd = pl.ds(1, half, stride=2)

    def samples(piece_refs, rows):
        return jnp.concatenate([r[rows, :] for r in piece_refs], axis=1)

    def longconv(order):
        ye_scr[...] = jnp.zeros_like(ye_scr)
        yo_scr[...] = jnp.zeros_like(yo_scr)

        def body(c, carry):
            g0 = pl.multiple_of(c * gt, gt)
            rows = pl.ds(g0, gt)
            ue = ue_scr[...]
            uo = uo_scr[...]
            a = _dot(fwd_ref[0, rows, :], ue)
            b = _dot(fwd_ref[1, rows, :], ue)
            cc = _dot(fwd_ref[2, rows, :], uo)
            d = _dot(fwd_ref[3, rows, :], uo)

            def times_filter(first, ur, ui):
                kr = k_ref[2 * order, pl.ds(first + g0, gt), :]
                ki = k_ref[2 * order + 1, pl.ds(first + g0, gt), :]
                return kr * ur + ki * ui, kr * ui - ki * ur

            yra, yia = times_filter(0, a + cc, b + d)
            yrb, yib = times_filter(half, b - d, a - cc)
            ye_scr[...] += (_dot(inv_ref[0, :, rows], (yra + yib).astype(BF))
                            + _dot(inv_ref[1, :, rows], (yia + yrb).astype(BF)))
            yo_scr[...] += (_dot(inv_ref[2, :, rows], (yra - yib).astype(BF))
                            + _dot(inv_ref[3, :, rows], (yia - yrb).astype(BF)))
            return carry

        lax.fori_loop(0, half // gt, body, 0)

    scale = 1.0 / seq
    ve = samples(v_refs, even)
    vo = samples(v_refs, odd)
    ue_scr[...] = ve.astype(BF)
    uo_scr[...] = vo.astype(BF)
    longconv(0)
    ze = samples(x1_refs, even) * (ye_scr[...] * scale + ve * skip_ref[0:1, :])
    zo = samples(x1_refs, odd) * (yo_scr[...] * scale + vo * skip_ref[0:1, :])
    ze_scr[...] = ze
    zo_scr[...] = zo
    ue_scr[...] = ze.astype(BF)
    uo_scr[...] = zo.astype(BF)
    longconv(1)
    oe = samples(x2_refs, even) * (ye_scr[...] * scale + ze_scr[...] * skip_ref[1:2, :])
    oo = samples(x2_refs, odd) * (yo_scr[...] * scale + zo_scr[...] * skip_ref[1:2, :])
    for k, out_scr in enumerate(out_scrs):
        lanes = slice(k * V7X_LANES, (k + 1) * V7X_LANES)
        out_scr[even, :] = oe[:, lanes]
        out_scr[odd, :] = oo[:, lanes]
        o_ref[:, lanes] = out_scr[...].astype(BF)


def _hyena_conv(u3, fwd, inv, kspec, skip, nb, seq, d):
    tc = _tile(d, CHAN_TILE)
    nc = d // tc
    half = seq // 2
    gt = _tile(half, FREQ_TILE)
    pieces = tc // V7X_LANES
    once = dict(pipeline_mode=pl.Buffered(1))
    piece_specs = [pl.BlockSpec((seq, V7X_LANES), lambda j, b, o=o, k=k: (b, (o * nc + j) * pieces + k))
                   for o in range(3) for k in range(pieces)]
    return pl.pallas_call(
        functools.partial(_hyena_conv_kernel, gt=gt, pieces=pieces),
        scratch_shapes=[pltpu.VMEM((half, tc), BF), pltpu.VMEM((half, tc), BF)]
        + [pltpu.VMEM((half, tc), F32)] * 4 + [pltpu.VMEM((seq, V7X_LANES), F32)] * pieces,
        grid=(nc, nb),
        in_specs=piece_specs + [
            pl.BlockSpec((4, half, half), lambda j, b: (0, 0, 0), **once),
            pl.BlockSpec((4, half, half), lambda j, b: (0, 0, 0), **once),
            pl.BlockSpec((4, seq, tc), lambda j, b: (0, 0, j), **once),
            pl.BlockSpec((2, tc), lambda j, b: (0, j)),
        ],
        out_specs=pl.BlockSpec((seq, tc), lambda j, b: (b, j)),
        out_shape=jax.ShapeDtypeStruct((nb * seq, d), BF),
        compiler_params=_params("parallel", "parallel"),
        name="hyena_conv",
    )(*([u3] * (3 * pieces)), fwd, inv, kspec, skip)


def _moe_block(streams, mods, layer, fg, w_gate, w_up, w_down, nb, final=False):
    n_experts = streams[0][2].shape[0]
    group = _sc_group(nb)
    routed = []
    for xs, hn, aff, t, ctx_row in streams:
        cap = max(1, EC_CAPACITY * t // n_experts)
        posm, post, gatet = _route(aff, nb, t, cap)
        xin = None if hn.dtype == jnp.int32 else _gather(posm, hn, nb, t, cap)
        routed.append((posm, post, gatet, cap, xin))
    ys = [[] for _ in streams]
    for e0, count in ((0, group), (group, n_experts - group)):
        xins = []
        for (xs, hn, aff, t, ctx_row), (posm, post, gatet, cap, xin) in zip(streams, routed):
            if xin is None:
                xins.append(([_sc_gather(hn, posm, e, nb, n_experts, t, cap) for e in range(e0, e0 + count, group)], 0))
            else:
                xins.append(([xin], e0))
        for acc, y in zip(ys, _experts(xins, w_gate, w_up, w_down, layer, e0, count, group)):
            acc.append(y)
    return [_combine(post, gatet, y, xs, mods, fg, nb, t, cap, ctx_row=ctx_row, final=final and ctx_row is None)
            for (xs, _, _, t, ctx_row), (_, post, gatet, cap, _), y in zip(streams, routed, ys)]


def kernel(x, c, ctx, c_ctx, w_ada, b_ada, norm1_g, norm2_g, final_g, a_w_in, a_conv, a_w_out, b_w_qkv, b_lq1, b_lk1, b_lq2, b_lk2, b_subln_g, b_w_out, c_w_in, c_conv, c_f1_w, c_f1_b, c_f1_freq, c_f2_w, c_f2_b, c_f2_freq, c_f3_w, c_skip, c_w_out, moe_router, moe_w_gate, moe_w_up, moe_w_down):
    nb, seq, d = x.shape
    ctx_len = ctx.shape[1]
    depth = w_ada.shape[0]
    assert nb < MOD_ROWS and d % (2 * DA_HEAD_DIM) == 0
    ctx_row = nb

    cvec = jnp.concatenate([c, c_ctx[None, :], jnp.zeros((MOD_ROWS - nb - 1, d), F32)], axis=0)
    mods_all = _ada(cvec, w_ada, b_ada).reshape(depth, MOD_ROWS, 6, d)

    attn_layers = [i for i in range(depth) if i % N_MIXERS == 1]
    last_ctx_read = max(attn_layers) if attn_layers else -1

    xs = x.reshape(nb * seq, d)
    cs_tok = ctx.reshape(nb * ctx_len, d)
    fg = final_g[None, :]

    for i in range(depth):
        kind, j = i % N_MIXERS, i // N_MIXERS
        update_ctx = i < last_ctx_read
        final = i == depth - 1
        mods = mods_all[i]
        g1 = norm1_g[i][None, :]
        g2 = norm2_g[i][None, :]
        wr = moe_router[i].T
        wrh, wrl = _split(wr)
        moe_w = (moe_w_gate, moe_w_up, moe_w_down)

        ctx_stream = []
        if kind == 0:
            w_in, w_out = a_w_in[j].astype(BF), a_w_out[j].astype(BF)
            if update_ctx:
                cn, chn, caff = _shortconv_layer(cs_tok, mods, g1, w_in, a_conv[j], w_out, g2, wrh, wrl,
                                                 ctx_len, ctx_row=ctx_row)
                ctx_stream = [(cn, chn, caff, ctx_len, ctx_row)]
            xn, hn, aff = _shortconv_layer(xs, mods, g1, w_in, a_conv[j], w_out, g2, wrh, wrl, seq)
        elif kind == 1:
            assert not update_ctx
            lam_init = 0.8 - 0.6 * math.exp(-0.3 * i)
            w_qkv = b_w_qkv[j].astype(BF)
            tables = [jnp.asarray(t) for t in _rope_tables(seq)]
            q, k, v = _qkv(xs, mods, g1, w_qkv, tables, seq, 3, True)
            ctab = [t[:ctx_len] for t in tables]
            kc, vc = _qkv(cs_tok, mods, g1, w_qkv[:, d:], ctab, ctx_len, 2, False, ctx_row=ctx_row)
            o = _attention(q, k, v, kc, vc, b_lq1[j][None, :], b_lk1[j][None, :], b_lq2[j][None, :],
                           b_lk2[j][None, :], b_subln_g[j][None, :], nb, seq, ctx_len, lam_init)
            xn, hn, aff = _outproj_layer(o, xs, mods, b_w_out[j].astype(BF), g2, wrh, wrl, seq)
        else:
            assert not update_ctx
            fwd_np, inv_np = _dft_half_tables(seq)
            fwd, inv = jnp.asarray(fwd_np).astype(BF), jnp.asarray(inv_np).astype(BF)
            w = c_f2_w.shape[-1]
            f1w = jnp.zeros((V7X_LANES, w), F32).at[:HY_EMB_DIM].set(c_f1_w[j])
            kspec = _hyena_filter(jnp.asarray(_hyena_features(seq)), f1w, c_f1_b[j][None, :], c_f1_freq[j][None, :],
                                  c_f2_w[j], c_f2_b[j][None, :], c_f2_freq[j][None, :], c_f3_w[j],
                                  jnp.asarray(_hyena_deltas(d)), fwd, jnp.asarray(_dft_half_shift(seq)), d)
            u3 = _hyena_in(xs, mods, g1, c_w_in[j].astype(BF), c_conv[j], seq)
            z = _hyena_conv(u3, fwd, inv, kspec, c_skip[j], nb, seq, d)
            xn, hn, aff = _outproj_layer(z, xs, mods, c_w_out[j].astype(BF), g2, wrh, wrl, seq)

        outs = _moe_block([(xn, hn, aff, seq, None)] + ctx_stream, mods, i, fg, *moe_w, nb, final=final)
        xs = outs[0]
        if ctx_stream:
            cs_tok = outs[1]

    return xs.reshape(nb, seq, d)
```

```python
import functools
import math

import jax
import jax.numpy as jnp
import numpy as np
from jax import lax
from jax.experimental import pallas as pl
from jax.experimental.pallas import tpu as pltpu

BF = jnp.bfloat16
F32 = jnp.float32

GRID_W = 64
DA_HEAD_DIM = 64
ROPE_AXIS_DIM = DA_HEAD_DIM // 2
ROPE_THETA = 10000.0
SUBLN_EPS = 1e-5
NORM_EPS = 1e-6
N_MIXERS = 3
EC_CAPACITY = 2
HY_EMB_DIM = 33
HY_BANDS = (HY_EMB_DIM - 1) // 2
HY_FAST_DECAY_PCT = 0.3
HY_SLOW_DECAY_PCT = 1.5
HY_DECAY_TARGET = 1e-2

V7X_LANES = 128
V7X_BF16_SUBLANES = 16
V7X_VMEM_BYTES = 64 * 2**20
VMEM_COMPILER_RESERVE = 8 * 2**20
VMEM_LIMIT = V7X_VMEM_BYTES - VMEM_COMPILER_RESERVE
F32_MAGNITUDE_BITS = 31
BF16_BITS = 16
HIGH_HALF = 0xFFFF0000
HALO = V7X_BF16_SUBLANES
MOD_ROWS = 16
ROW_TILE = 512
EXPERT_ROW_STEPS = 2
PROJ_ROW_TILE = 1024
ATTN_Q_TILE = 256
CHAN_TILE = 256
ADA_COL_TILE = 1536
FREQ_TILE = 1024
SC_GATHER_ROWS = 64


def _params(*sem):
    return pltpu.CompilerParams(dimension_semantics=sem, vmem_limit_bytes=VMEM_LIMIT)


def _dot(a, b):
    return jnp.dot(a, b, preferred_element_type=F32)


def _dot_nt(a, b):
    return lax.dot_general(a, b, (((1,), (1,)), ((), ())), preferred_element_type=F32)


def _split(a):
    hi = a.astype(BF)
    lo = (a - hi.astype(F32)).astype(BF)
    return hi, lo


def _dot3(a, b):
    ah, al = _split(a)
    bh, bl = _split(b)
    return _dot(ah, bh) + (_dot(ah, bl) + _dot(al, bh))


def _sigmoid(a):
    return 1.0 / (1.0 + jnp.exp(-a))


def _norm_mod(x, g, shift, scale):
    ms = jnp.mean(x * x, axis=-1, keepdims=True)
    return (x * lax.rsqrt(ms + NORM_EPS) * g) * (1.0 + scale) + shift


def _tile(n, pref):
    t = min(n, pref)
    assert n % t == 0, (n, pref)
    return t


def _full(shape):
    nd = len(shape)
    return pl.BlockSpec(shape, lambda *_: (0,) * nd)


def _ada_kernel(c_ref, w_ref, b_ref, o_ref):
    c = c_ref[...]
    o_ref[0] = _dot3(c * _sigmoid(c), w_ref[0]) + b_ref[0]


def _ada(cvec, w_ada, b_ada):
    depth, d, n6 = w_ada.shape
    tn = _tile(n6, ADA_COL_TILE)
    return pl.pallas_call(
        _ada_kernel,
        grid=(depth, n6 // tn),
        in_specs=[
            _full((MOD_ROWS, d)),
            pl.BlockSpec((1, d, tn), lambda i, j: (i, 0, j)),
            pl.BlockSpec((1, 1, tn), lambda i, j: (i, 0, j)),
        ],
        out_specs=pl.BlockSpec((1, MOD_ROWS, tn), lambda i, j: (i, 0, j)),
        out_shape=jax.ShapeDtypeStruct((depth, MOD_ROWS, n6), F32),
        compiler_params=_params("parallel", "parallel"),
        name="ada",
    )(cvec, w_ada, b_ada.reshape(depth, 1, n6))


def _pack_pairs(hh):
    half = hh.shape[1] // 2
    lo = lax.shift_right_logical(pltpu.bitcast(hh[:, :half].astype(F32), jnp.uint32), jnp.uint32(BF16_BITS))
    hi = pltpu.bitcast(hh[:, half:].astype(F32), jnp.uint32) & jnp.uint32(HIGH_HALF)
    return pltpu.bitcast(lo | hi, jnp.int32)


def _unpack_pairs(words):
    u = pltpu.bitcast(words, jnp.uint32)
    lo = pltpu.bitcast(lax.shift_left(u, jnp.uint32(BF16_BITS)), F32).astype(BF)
    hi = pltpu.bitcast(u & jnp.uint32(HIGH_HALF), F32).astype(BF)
    return jnp.concatenate([lo, hi], axis=1)


def _residual_router(x, y, mod, g2_ref, wrh_ref, wrl_ref, xo_ref, hn_ref, aff_ref):
    xn = x + mod[2:3] * y
    xo_ref[...] = xn
    hn = _norm_mod(xn, g2_ref[...], mod[3:4], mod[4:5])
    hh, hl = _split(hn)
    hn_ref[...] = _pack_pairs(hh) if hn_ref.dtype == jnp.int32 else hh
    e = wrh_ref.shape[0]
    both = _dot_nt(jnp.concatenate([wrh_ref[...], wrl_ref[...]], axis=0), hh)
    logits = both[:e] + (_dot_nt(wrh_ref[...], hl) + both[e:])
    p = jnp.exp(logits - jnp.max(logits, axis=0, keepdims=True))
    aff_ref[...] = p / jnp.sum(p, axis=0, keepdims=True)


def _router_specs(n, d, e, tm, packed):
    in_specs = [_full((1, d)), _full((e, d)), _full((e, d))]
    hn_cols, hn_dtype = (d // 2, jnp.int32) if packed else (d, BF)
    out_specs = [
        pl.BlockSpec((tm, d), lambda i: (i, 0)),
        pl.BlockSpec((tm, hn_cols), lambda i: (i, 0)),
        pl.BlockSpec((e, tm), lambda i: (0, i)),
    ]
    out_shape = [
        jax.ShapeDtypeStruct((n, d), F32),
        jax.ShapeDtypeStruct((n, hn_cols), hn_dtype),
        jax.ShapeDtypeStruct((e, n), F32),
    ]
    return in_specs, out_specs, out_shape


def _halo_specs(n, d, tm):
    per = tm // HALO
    last = n // HALO - 1
    return [
        pl.BlockSpec((HALO, d), lambda i: (jnp.maximum(i * per - 1, 0), 0)),
        pl.BlockSpec((tm, d), lambda i: (i, 0)),
        pl.BlockSpec((HALO, d), lambda i: (jnp.minimum((i + 1) * per, last), 0)),
    ]


def _mod_spec(d, tiles_per_seq, ctx_row):
    if ctx_row is None:
        return pl.BlockSpec((1, 6, d), lambda i: (i // tiles_per_seq, 0, 0))
    return pl.BlockSpec((1, 6, d), lambda i: (ctx_row, 0, 0))


def _halo_rows(xp_ref, x_ref, xn_ref, g_ref, mod, tiles_per_seq):
    t = pl.program_id(0) % tiles_per_seq
    g = g_ref[...]
    hp, hx, hn = [_norm_mod(r[...], g, mod[0:1], mod[1:2]) for r in (xp_ref, x_ref, xn_ref)]
    hp = jnp.where(t == 0, 0.0, hp)
    hn = jnp.where(t == tiles_per_seq - 1, 0.0, hn)
    return jnp.concatenate([hp, hx, hn], axis=0).astype(BF)


def _conv3(s, w_ref, tm):
    n = s.shape[0]
    prev = pltpu.roll(s, 1, 0)[HALO:HALO + tm]
    nxt = pltpu.roll(s, n - 1, 0)[HALO:HALO + tm]
    return prev * w_ref[0:1, :] + s[HALO:HALO + tm] * w_ref[1:2, :] + nxt * w_ref[2:3, :]


def _shortconv_kernel(xp_ref, x_ref, xn_ref, mod_ref, g1_ref, win_ref, wconv_ref, wout_ref,
                      g2_ref, wrh_ref, wrl_ref, xo_ref, hn_ref, aff_ref, *, tiles_per_seq):
    tm, d = x_ref.shape
    mod = mod_ref[0]
    h = _halo_rows(xp_ref, x_ref, xn_ref, g1_ref, mod, tiles_per_seq)
    gate = _dot(h[HALO:HALO + tm], win_ref[:, 0:d])
    s = _dot(h, win_ref[:, d:2 * d]) * _dot(h, win_ref[:, 2 * d:3 * d])
    z = (gate * _conv3(s, wconv_ref, tm)).astype(BF)
    y = _dot(z, wout_ref[...])
    _residual_router(x_ref[...], y, mod, g2_ref, wrh_ref, wrl_ref, xo_ref, hn_ref, aff_ref)


def _shortconv_layer(xs, mods, g1, w_in, w_conv, w_out, g2, wrh, wrl, seq, ctx_row=None):
    n, d = xs.shape
    e = wrh.shape[0]
    tm = _tile(seq, PROJ_ROW_TILE)
    tps = seq // tm
    r_in, r_out, r_shape = _router_specs(n, d, e, tm, packed=ctx_row is None)
    return pl.pallas_call(
        functools.partial(_shortconv_kernel, tiles_per_seq=tps),
        grid=(n // tm,),
        in_specs=_halo_specs(n, d, tm) + [
            _mod_spec(d, tps, ctx_row), _full((1, d)), _full((d, 3 * d)), _full((3, d)), _full((d, d)),
        ] + r_in,
        out_specs=r_out,
        out_shape=r_shape,
        compiler_params=_params("parallel"),
        name="shortconv_layer",
    )(xs, xs, xs, mods, g1, w_in, w_conv, w_out, g2, wrh, wrl)


def _outproj_kernel(o_ref, x_ref, mod_ref, wout_ref, g2_ref, wrh_ref, wrl_ref, xo_ref, hn_ref, aff_ref):
    y = _dot(o_ref[...], wout_ref[...])
    _residual_router(x_ref[...], y, mod_ref[0], g2_ref, wrh_ref, wrl_ref, xo_ref, hn_ref, aff_ref)


def _outproj_layer(o, xs, mods, w_out, g2, wrh, wrl, seq):
    n, d = xs.shape
    e = wrh.shape[0]
    tm = _tile(seq, PROJ_ROW_TILE)
    tps = seq // tm
    r_in, r_out, r_shape = _router_specs(n, d, e, tm, packed=True)
    return pl.pallas_call(
        _outproj_kernel,
        grid=(n // tm,),
        in_specs=[pl.BlockSpec((tm, d), lambda i: (i, 0)), pl.BlockSpec((tm, d), lambda i: (i, 0)),
                  _mod_spec(d, tps, None), _full((d, d))] + r_in,
        out_specs=r_out,
        out_shape=r_shape,
        compiler_params=_params("parallel"),
        name="outproj_layer",
    )(o, xs, mods, w_out, g2, wrh, wrl)


def _excl_cumsum_lanes(m):
    rows, t = m.shape
    a = lax.broadcasted_iota(jnp.int32, (V7X_LANES, V7X_LANES), 0)
    b = lax.broadcasted_iota(jnp.int32, (V7X_LANES, V7X_LANES), 1)
    tri = jnp.where(a < b, 1.0, 0.0).astype(BF)
    carry = jnp.zeros((rows, 1), F32)
    out = []
    for c in range(t // V7X_LANES):
        blk = m[:, c * V7X_LANES:(c + 1) * V7X_LANES]
        out.append(_dot(blk.astype(BF), tri) + carry)
        carry = carry + jnp.sum(blk, axis=1, keepdims=True)
    return jnp.concatenate(out, axis=1)


def _route_kernel(aff_ref, posm_ref, gate_ref, *, cap, nb):
    e = aff_ref.shape[0]
    t = aff_ref.shape[1] // nb
    aff = jnp.concatenate([aff_ref[:, b * t:(b + 1) * t] for b in range(nb)], axis=0)
    bits = pltpu.bitcast(aff, jnp.int32)

    def step(i, thr):
        cand = thr | jnp.left_shift(jnp.int32(1), F32_MAGNITUDE_BITS - 1 - i)
        cnt = jnp.sum(jnp.where(bits >= cand, 1.0, 0.0), axis=1, keepdims=True)
        return jnp.where(cnt >= cap, cand, thr)

    thr = lax.fori_loop(0, F32_MAGNITUDE_BITS, step, jnp.zeros((nb * e, 1), jnp.int32))
    gt = jnp.where(bits > thr, 1.0, 0.0)
    eq = jnp.where(bits == thr, 1.0, 0.0)
    need = cap - jnp.sum(gt, axis=1, keepdims=True)
    sel = gt + eq * jnp.where(_excl_cumsum_lanes(eq) < need, 1.0, 0.0)
    pos = jnp.where(sel > 0.0, _excl_cumsum_lanes(sel), -1.0)
    posm_ref[...] = pos.astype(jnp.int32)
    gate_ref[...] = sel * aff


def _route_by_token_kernel(posm_ref, gate_ref, post_ref, gatet_ref, *, nb):
    e = posm_ref.shape[0] // nb
    t = posm_ref.shape[1]
    pad = jnp.zeros((V7X_LANES - e, t), F32)
    for b in range(nb):
        rows = slice(b * e, (b + 1) * e)
        post_ref[b * t:(b + 1) * t, :] = jnp.concatenate([posm_ref[rows, :].astype(F32), pad - 1.0], axis=0).T
        gatet_ref[b * t:(b + 1) * t, :] = jnp.concatenate([gate_ref[rows, :], pad], axis=0).T


def _route(aff, nb, t, cap):
    e = aff.shape[0]
    return pl.pallas_call(
        functools.partial(_route_kernel, cap=cap, nb=nb),
        grid=(1,),
        in_specs=[_full((e, nb * t))],
        out_specs=[_full((nb * e, t)), _full((nb * e, t))],
        out_shape=[jax.ShapeDtypeStruct((nb * e, t), jnp.int32), jax.ShapeDtypeStruct((nb * e, t), F32)],
        compiler_params=_params("arbitrary"),
        name="route",
    )(aff)


def _route_by_token(posm, gate, nb):
    rows, t = posm.shape
    by_token = jax.ShapeDtypeStruct((nb * t, V7X_LANES), F32)
    return pl.pallas_call(
        functools.partial(_route_by_token_kernel, nb=nb),
        grid=(1,),
        in_specs=[_full((rows, t)), _full((rows, t))],
        out_specs=[_full((nb * t, V7X_LANES)), _full((nb * t, V7X_LANES))],
        out_shape=[by_token, by_token],
        compiler_params=_params("arbitrary"),
        name="route_by_token",
    )(posm, gate)


def _sc_gather(table, posm, e0, nb, n_experts, t, cap):
    from jax.experimental.pallas import tpu_sc as plsc

    info = plsc.get_sparse_core_info()
    cores, lanes = info.num_cores, info.num_lanes
    workers = cores * info.num_subcores
    group = workers // nb
    words = table.shape[1]
    assert workers % nb == 0 and n_experts % group == 0 and cap % SC_GATHER_ROWS == 0 and t % lanes == 0

    @functools.partial(
        pl.kernel,
        mesh=plsc.VectorSubcoreMesh(core_axis_name="core", subcore_axis_name="subcore"),
        compiler_params=pltpu.CompilerParams(needs_layout_passes=False),
        out_type=jax.ShapeDtypeStruct((group * nb * cap, words), jnp.int32),
        scratch_types=[pltpu.VMEM((t,), jnp.int32), pltpu.VMEM((cap,), jnp.int32),
                       pltpu.VMEM((SC_GATHER_ROWS, words), jnp.int32), pltpu.SemaphoreType.DMA],
    )
    def gather(table_hbm, posm_hbm, out_hbm, pos_v, idx_v, rows_v, sem):
        w = lax.axis_index("subcore") * cores + lax.axis_index("core")
        e_local = w // nb
        b = w % nb
        pltpu.sync_copy(posm_hbm.at[b * n_experts + e0 + e_local], pos_v)

        @pl.loop(0, t // lanes)
        def _(i):
            p = pos_v[pl.ds(i * lanes, lanes)]
            token = lax.iota(jnp.int32, lanes) + (i * lanes + b * t)
            plsc.store_scatter(idx_v, [p], token, mask=p >= 0)

        out_base = (e_local * nb + b) * cap

        @pl.loop(0, cap // SC_GATHER_ROWS)
        def _(j):
            pltpu.async_copy(table_hbm.at[idx_v.at[pl.ds(j * SC_GATHER_ROWS, SC_GATHER_ROWS)]], rows_v, sem).wait()
            pltpu.sync_copy(rows_v, out_hbm.at[pl.ds(out_base + j * SC_GATHER_ROWS, SC_GATHER_ROWS)])

    return gather(table, posm).reshape(group, nb * cap, words)


def _sc_group(nb):
    from jax.experimental.pallas import tpu_sc as plsc

    info = plsc.get_sparse_core_info()
    return info.num_cores * info.num_subcores // nb


def _gather_kernel(posm_ref, hn_ref, o_ref, *, cap):
    e = posm_ref.shape[0]
    t = hn_ref.shape[0]
    slot = lax.broadcasted_iota(jnp.int32, (cap, t), 0)

    def body(k, carry):
        row = posm_ref[pl.ds(k, 1), :]
        onehot = jnp.where(row == slot, 1.0, 0.0).astype(BF)
        o_ref[k] = _dot(onehot, hn_ref[...]).astype(BF)
        return carry

    lax.fori_loop(0, e, body, 0)


def _gather(posm, hn, nb, t, cap):
    e = posm.shape[0] // nb
    d = hn.shape[1]
    return pl.pallas_call(
        functools.partial(_gather_kernel, cap=cap),
        grid=(nb,),
        in_specs=[pl.BlockSpec((e, t), lambda b: (b, 0)), pl.BlockSpec((t, d), lambda b: (b, 0))],
        out_specs=pl.BlockSpec((e, cap, d), lambda b: (0, b, 0)),
        out_shape=jax.ShapeDtypeStruct((e, nb * cap, d), BF),
        compiler_params=_params("parallel"),
        name="moe_gather",
    )(posm, hn)


def _expert_kernel(*refs, pieces, per):
    n_in = sum(pieces)
    wg_ref, wu_ref, wd_ref = refs[n_in:n_in + 3]
    y_refs = refs[n_in + 3:n_in + 3 + len(pieces)]
    w_scr = refs[-1]

    @pl.when(pl.program_id(1) == 0)
    def _():
        w_scr[0] = wg_ref[0, 0].astype(BF)
        w_scr[1] = wu_ref[0, 0].astype(BF)
        w_scr[2] = wd_ref[0, 0].astype(BF)

    which = pl.program_id(0) // per

    start = 0
    for n_pieces, y_ref in zip(pieces, y_refs):
        x_refs = refs[start:start + n_pieces]
        start += n_pieces
        rows = x_refs[0].shape[1]
        tr = _tile(rows, ROW_TILE)

        def body(j, carry, x_refs=x_refs, y_ref=y_ref, tr=tr):
            r0 = pl.multiple_of(j * tr, tr)
            xs = x_refs[0][0, pl.ds(r0, tr), :]
            for p in range(1, len(x_refs)):
                xs = jnp.where(which == p, x_refs[p][0, pl.ds(r0, tr), :], xs)
            if xs.dtype == jnp.int32:
                xs = _unpack_pairs(xs)
            a = _dot(xs, w_scr[0])
            b = _dot(xs, w_scr[1])
            hm = (a * _sigmoid(a) * b).astype(BF)
            y_ref[0, pl.ds(r0, tr), :] = _dot(hm, w_scr[2]).astype(BF)
            return carry

        lax.fori_loop(0, rows // tr, body, 0)


def _experts(streams, w_gate, w_up, w_down, layer, e0, count, per):
    d = w_gate.shape[-2]
    f = w_gate.shape[-1]
    assert f == d
    wspec = pl.BlockSpec((1, 1, d, f), lambda k, j: (layer, e0 + k, 0, 0))
    xspecs, xargs = [], []
    for arrays, first in streams:
        for p, x in enumerate(arrays):
            if len(arrays) == 1:
                index = lambda k, j, first=first: (first + k, j, 0)
            else:
                index = lambda k, j, p=p: (
                    jnp.clip(k - p * per, 0, per - 1),
                    jnp.where(k < p * per, 0, jnp.where(k >= (p + 1) * per, EXPERT_ROW_STEPS - 1, j)), 0)
            xspecs.append(pl.BlockSpec((1, x.shape[1] // EXPERT_ROW_STEPS, x.shape[2]), index))
            xargs.append(x)
    rows = [arrays[0].shape[1] for arrays, _ in streams]
    return pl.pallas_call(
        functools.partial(_expert_kernel, pieces=tuple(len(arrays) for arrays, _ in streams), per=per),
        grid=(count, EXPERT_ROW_STEPS),
        in_specs=xspecs + [wspec, wspec, wspec],
        out_specs=[pl.BlockSpec((1, r // EXPERT_ROW_STEPS, d), lambda k, j: (k, j, 0)) for r in rows],
        out_shape=[jax.ShapeDtypeStruct((count, r, d), BF) for r in rows],
        scratch_shapes=[pltpu.VMEM((3, d, f), BF)],
        compiler_params=_params("parallel", "arbitrary"),
        name="moe_experts",
    )(*xargs, w_gate, w_up, w_down)


def _combine_kernel(post_ref, gatet_ref, *refs, cap, final):
    y_refs = refs[:-4]
    x_ref, mod_ref, fg_ref, o_ref = refs[-4:]
    d = y_refs[0].shape[2]
    e = sum(r.shape[0] for r in y_refs)
    tq = x_ref.shape[0]
    pt = post_ref[...]
    gt = gatet_ref[...]
    if cap % V7X_LANES == 0:
        slot = lax.broadcasted_iota(jnp.int32, (tq, cap), 1).astype(F32)
        pieces = [jnp.where(pt[:, k:k + 1] == slot, gt[:, k:k + 1], 0.0).astype(BF) for k in range(e)]
        scat = jnp.concatenate(pieces, axis=1)
    else:
        slot = lax.broadcasted_iota(jnp.int32, (tq, e * cap), 1).astype(F32)
        scat = jnp.zeros((tq, e * cap), F32)
        for k in range(e):
            pk = pt[:, k:k + 1]
            scat = jnp.where((pk >= 0.0) & (pk + float(k * cap) == slot), gt[:, k:k + 1], scat)
        scat = scat.astype(BF)
    out, col = None, 0
    for y_ref in y_refs:
        width = y_ref.shape[0] * cap
        part = _dot(scat[:, col:col + width], y_ref[...].reshape(width, d))
        out = part if out is None else out + part
        col += width
    xn = x_ref[...] + mod_ref[0][5:6] * out
    if final:
        ms = jnp.mean(xn * xn, axis=-1, keepdims=True)
        xn = xn * lax.rsqrt(ms + NORM_EPS) * fg_ref[...]
    o_ref[...] = xn


def _combine(post, gatet, ys, xs, mods, fg, nb, t, cap, ctx_row=None, final=False):
    n, d = xs.shape
    tq = _tile(t, ROW_TILE)
    tpb = t // tq
    if ctx_row is None:
        mspec = pl.BlockSpec((1, 6, d), lambda b, i: (b, 0, 0))
    else:
        mspec = pl.BlockSpec((1, 6, d), lambda b, i: (ctx_row, 0, 0))
    return pl.pallas_call(
        functools.partial(_combine_kernel, cap=cap, final=final),
        grid=(nb, tpb),
        in_specs=[
            pl.BlockSpec((tq, V7X_LANES), lambda b, i: (b * tpb + i, 0)),
            pl.BlockSpec((tq, V7X_LANES), lambda b, i: (b * tpb + i, 0)),
        ] + [pl.BlockSpec((y.shape[0], cap, d), lambda b, i: (0, b, 0)) for y in ys] + [
            pl.BlockSpec((tq, d), lambda b, i: (b * tpb + i, 0)),
            mspec,
            _full((1, d)),
        ],
        out_specs=pl.BlockSpec((tq, d), lambda b, i: (b * tpb + i, 0)),
        out_shape=jax.ShapeDtypeStruct((n, d), F32),
        compiler_params=_params("parallel", "parallel"),
        name="moe_combine",
    )(post, gatet, *ys, xs, mods, fg)


def _rope_tables(seq):
    rows = seq // GRID_W
    row = np.repeat(np.arange(rows, dtype=np.float32), GRID_W)
    col = np.tile(np.arange(GRID_W, dtype=np.float32), rows)
    inv_freq = (ROPE_THETA ** (-np.arange(0, ROPE_AXIS_DIM, 2, dtype=np.float32) / ROPE_AXIS_DIM)).astype(np.float32)
    lane = np.arange(2 * DA_HEAD_DIM)
    within = lane % DA_HEAD_DIM
    axis = within // ROPE_AXIS_DIM
    half = (within % ROPE_AXIS_DIM) // (ROPE_AXIS_DIM // 2)
    idx = within % (ROPE_AXIS_DIM // 2)
    pos = np.where(axis[None, :] == 0, row[:, None], col[:, None])
    ang = (pos * inv_freq[idx][None, :]).astype(np.float32)
    cos = np.cos(ang).astype(np.float32)
    sin = np.sin(ang).astype(np.float32)
    sin_lo = np.where(half[None, :] == 1, sin, 0.0).astype(np.float32)
    sin_hi = np.where(half[None, :] == 0, -sin, 0.0).astype(np.float32)
    return cos, sin_lo, sin_hi


def _qkv_kernel(x_ref, mod_ref, g_ref, w_ref, cos_ref, sa_ref, sb_ref, *o_refs, rope):
    d = x_ref.shape[1]
    mod = mod_ref[0]
    h = _norm_mod(x_ref[...], g_ref[...], mod[0:1], mod[1:2]).astype(BF)
    slab = 2 * DA_HEAD_DIM
    shift = ROPE_AXIS_DIM // 2
    for j, o_ref in enumerate(o_refs):
        u = _dot(h, w_ref[:, j * d:(j + 1) * d])
        if rope and j < 2:
            scale = DA_HEAD_DIM ** -0.5 * math.log2(math.e) if j == 0 else 1.0
            cos, sa, sb = cos_ref[...] * scale, sa_ref[...] * scale, sb_ref[...] * scale
            for hd in range(d // slab):
                xs = u[:, hd * slab:(hd + 1) * slab]
                r = xs * cos + pltpu.roll(xs, shift, 1) * sa + pltpu.roll(xs, slab - shift, 1) * sb
                o_ref[:, hd * slab:(hd + 1) * slab] = r.astype(BF)
        else:
            o_ref[...] = u.astype(BF)


def _qkv(xs, mods, g1, w, tables, seq, nout, rope, ctx_row=None):
    n, d = xs.shape
    tm = _tile(seq, PROJ_ROW_TILE)
    tps = seq // tm
    slab = 2 * DA_HEAD_DIM
    tspec = pl.BlockSpec((tm, slab), lambda i: (i % tps, 0))
    return pl.pallas_call(
        functools.partial(_qkv_kernel, rope=rope),
        grid=(n // tm,),
        in_specs=[pl.BlockSpec((tm, d), lambda i: (i, 0)), _mod_spec(d, tps, ctx_row), _full((1, d)),
                  _full((d, nout * d)), tspec, tspec, tspec],
        out_specs=[pl.BlockSpec((tm, d), lambda i: (i, 0))] * nout,
        out_shape=[jax.ShapeDtypeStruct((n, d), BF)] * nout,
        compiler_params=_params("parallel"),
        name="attn_qkv",
    )(xs, mods, g1, w, *tables)


def _attn_kernel(q_ref, k_ref, v_ref, kc_ref, vc_ref, lq1_ref, lk1_ref, lq2_ref, lk2_ref, sg_ref, o_ref,
                 sl_a, sc_a, sl_b, sc_b, va_scr, vca_scr, *, lam_init, tq):
    seq = q_ref.shape[0]
    lam = (jnp.exp(jnp.sum(lq1_ref[...] * lk1_ref[...], axis=1, keepdims=True))
           - jnp.exp(jnp.sum(lq2_ref[...] * lk2_ref[...], axis=1, keepdims=True)) + lam_init)
    slots = ((sl_a, sc_a), (sl_b, sc_b))

    def scores(i, slot):
        sl_ref, sc_ref = slot
        q = q_ref[i * tq:(i + 1) * tq, :]
        lane = lax.broadcasted_iota(jnp.int32, q.shape, 1)
        zero = jnp.zeros_like(q)
        for mp, qm in enumerate((jnp.where(lane < DA_HEAD_DIM, q, zero), jnp.where(lane >= DA_HEAD_DIM, q, zero))):
            sl_ref[mp] = _dot_nt(qm, k_ref[...])
            sc_ref[mp] = _dot_nt(qm, kc_ref[...])

    slab = v_ref.shape[1]
    va_scr[:, 0:slab] = v_ref[...]
    va_scr[:, slab:2 * slab] = jnp.ones_like(v_ref)
    vca_scr[:, 0:slab] = vc_ref[...]
    vca_scr[:, slab:2 * slab] = jnp.ones_like(vc_ref)

    def unnormalised(sl_ref, sc_ref, mp):
        s_l = sl_ref[mp]
        s_c = sc_ref[mp]
        m = jnp.maximum(jnp.max(s_l, axis=1, keepdims=True), jnp.max(s_c, axis=1, keepdims=True))
        p_l = jnp.exp2(s_l - m).astype(BF)
        p_c = jnp.exp2(s_c - m).astype(BF)
        both = _dot(p_l, va_scr[...]) + _dot(p_c, vca_scr[...])
        return both[:, 0:slab], both[:, slab:slab + 1]

    def attend(i, slot):
        o1, t1 = unnormalised(*slot, 0)
        o2, t2 = unnormalised(*slot, 1)
        o = o1 * (1.0 / t1) - o2 * (lam / t2)
        ms = jnp.mean(o * o, axis=-1, keepdims=True)
        o_ref[i * tq:(i + 1) * tq, :] = (o * lax.rsqrt(ms + SUBLN_EPS) * sg_ref[...] * (1.0 - lam_init)).astype(BF)

    n = seq // tq
    scores(0, slots[0])
    for i in range(n):
        if i + 1 < n:
            scores(i + 1, slots[(i + 1) % 2])
        attend(i, slots[i % 2])


def _attention(q, k, v, kc, vc, lq1, lk1, lq2, lk2, sg, nb, seq, ctx_len, lam_init):
    n, d = q.shape
    slab = 2 * DA_HEAD_DIM
    heads = d // slab
    tq = _tile(seq, ATTN_Q_TILE)
    small = _full((1, DA_HEAD_DIM))
    lat = pl.BlockSpec((seq, slab), lambda b, h: (b, h))
    ctx = pl.BlockSpec((ctx_len, slab), lambda b, h: (b, h))
    score_scratch = [pltpu.VMEM((2, tq, seq), F32), pltpu.VMEM((2, tq, ctx_len), F32)]
    return pl.pallas_call(
        functools.partial(_attn_kernel, lam_init=lam_init, tq=tq),
        grid=(nb, heads),
        in_specs=[lat, lat, lat, ctx, ctx, small, small, small, small, _full((1, slab))],
        out_specs=lat,
        out_shape=jax.ShapeDtypeStruct((n, d), BF),
        scratch_shapes=score_scratch + score_scratch + [pltpu.VMEM((seq, 2 * slab), BF),
                                                        pltpu.VMEM((ctx_len, 2 * slab), BF)],
        compiler_params=_params("parallel", "parallel"),
        name="diff_attention",
    )(q, k, v, kc, vc, lq1, lk1, lq2, lk2, sg)


def _hyena_in_kernel(xp_ref, x_ref, xn_ref, mod_ref, g1_ref, win_ref, wconv_ref, o_ref, *, tiles_per_seq):
    tm, d = x_ref.shape
    h = _halo_rows(xp_ref, x_ref, xn_ref, g1_ref, mod_ref[0], tiles_per_seq)
    for j in range(3):
        u = _dot(h, win_ref[:, j * d:(j + 1) * d])
        o_ref[:, j * d:(j + 1) * d] = _conv3(u, wconv_ref.at[:, j * d:(j + 1) * d], tm)


def _hyena_in(xs, mods, g1, w_in, w_conv, seq):
    n, d = xs.shape
    tm = _tile(seq, PROJ_ROW_TILE)
    tps = seq // tm
    return pl.pallas_call(
        functools.partial(_hyena_in_kernel, tiles_per_seq=tps),
        grid=(n // tm,),
        in_specs=_halo_specs(n, d, tm) + [_mod_spec(d, tps, None), _full((1, d)), _full((d, 3 * d)), _full((3, 3 * d))],
        out_specs=pl.BlockSpec((tm, 3 * d), lambda i: (i, 0)),
        out_shape=jax.ShapeDtypeStruct((n, 3 * d), F32),
        compiler_params=_params("parallel"),
        name="hyena_in",
    )(xs, xs, xs, mods, g1, w_in, w_conv)


def _dft_half_shift(seq):
    order = np.concatenate([np.arange(seq // 2), seq - 1 - np.arange(seq // 2)])
    half = np.pi * (order + 0.5) / (2 * seq)
    return np.stack([np.cos(half), np.sin(half)], axis=1).astype(np.float32)


def _dft_half_tables(seq):
    h = seq // 2
    th = 2.0 * np.pi * (np.arange(h, dtype=np.float64) + 0.5) / seq
    s = np.arange(h, dtype=np.float64)
    even, odd = np.outer(th, s + 0.25), np.outer(th, s + 0.75)
    fwd = np.stack([np.cos(even), np.sin(even), np.cos(odd), np.sin(odd)]).astype(np.float32)
    inv = np.ascontiguousarray(np.transpose(fwd, (0, 2, 1)))
    return fwd, inv


def _hyena_features(seq):
    t = np.linspace(0.0, 1.0, seq, dtype=np.float32)[:, None]
    w = (2.0 * math.pi * np.arange(seq, dtype=np.float32)[:, None] / seq).astype(np.float32)
    f = np.linspace(1e-4, HY_BANDS - 1, HY_BANDS, dtype=np.float32)[None, :]
    z = np.concatenate([t, np.cos(f * w), -np.sin(f * w)], axis=-1).astype(np.float32)
    zp = np.zeros((seq, V7X_LANES), np.float32)
    zp[:, :HY_EMB_DIM] = z
    return zp


def _hyena_deltas(d):
    max_decay = math.log(HY_DECAY_TARGET) / HY_FAST_DECAY_PCT
    min_decay = math.log(HY_DECAY_TARGET) / HY_SLOW_DECAY_PCT
    return np.abs(np.linspace(min_decay, max_decay, d, dtype=np.float32))[None, :].astype(np.float32)


def _hyena_filter_kernel(z_ref, f1w_ref, f1b_ref, f1f_ref, f2w_ref, f2b_ref, f2f_ref,
                         f3a_ref, f3b_ref, f3c_ref, f3d_ref, delta_ref, fwd_ref, rot_ref, k_ref, hid_scr, *pm_scrs):
    seq = z_ref.shape[0]
    half = seq // 2
    tc = delta_ref.shape[1]
    even = pl.ds(0, half, stride=2)
    odd = pl.ds(1, half, stride=2)

    @pl.when(pl.program_id(0) == 0)
    def _():
        h1 = jnp.sin(f1f_ref[...] * (_dot3(z_ref[...], f1w_ref[...]) + f1b_ref[...]))
        hid_scr[...] = jnp.sin(f2f_ref[...] * (_dot3(h1, f2w_ref[...]) + f2b_ref[...]))

    hid = hid_scr[...]
    decay = jnp.exp(-z_ref[:, 0:1] * delta_ref[...])
    row = lax.broadcasted_iota(jnp.int32, (seq, 1), 0)
    cr = rot_ref[:, 0:1]
    sr = rot_ref[:, 1:2]
    for order, (f3_fwd_ref, f3_bwd_ref) in enumerate(((f3a_ref, f3b_ref), (f3c_ref, f3d_ref))):
        h_fwd = _dot3(hid, f3_fwd_ref[...]) * decay
        h_bwd = jnp.where(row == 0, 0.0, _dot3(hid, f3_bwd_ref[...]) * decay)
        pm = jnp.concatenate([h_fwd + h_bwd, h_bwd - h_fwd], axis=1)
        for k, scr in enumerate(pm_scrs):
            scr[...] = pm[:, k * V7X_LANES:(k + 1) * V7X_LANES]
        x_even = jnp.concatenate([scr[even, :] for scr in pm_scrs], axis=1).astype(BF)
        x_odd = jnp.concatenate([scr[odd, :] for scr in pm_scrs], axis=1).astype(BF)
        a = _dot(fwd_ref[0], x_even)
        b = _dot(fwd_ref[1], x_even)
        c = _dot(fwd_ref[2], x_odd)
        s = _dot(fwd_ref[3], x_odd)
        cos_sum = jnp.concatenate([a + c, b - s], axis=0)
        sin_sum = jnp.concatenate([b + s, a - c], axis=0)
        k_ref[2 * order] = cr * cos_sum[:, :tc] + sr * sin_sum[:, :tc]
        k_ref[2 * order + 1] = cr * sin_sum[:, tc:] - sr * cos_sum[:, tc:]


def _hyena_filter(z, f1w, f1b, f1f, f2w, f2b, f2f, f3w, deltas, fwd, rot, d):
    seq = z.shape[0]
    half = seq // 2
    w = f2w.shape[0]
    tc = _tile(d, CHAN_TILE)
    nc = d // tc
    f3spec = [pl.BlockSpec((w, tc), lambda j, o=o: (0, o * nc + j)) for o in range(4)]
    return pl.pallas_call(
        _hyena_filter_kernel,
        grid=(nc,),
        in_specs=[_full((seq, V7X_LANES)), _full((V7X_LANES, w)), _full((1, w)), _full((1, w)),
                  _full((w, w)), _full((1, w)), _full((1, w))] + f3spec + [
            pl.BlockSpec((1, tc), lambda j: (0, j)),
            pl.BlockSpec((4, half, half), lambda j: (0, 0, 0), pipeline_mode=pl.Buffered(1)),
            _full((seq, 2)),
        ],
        out_specs=pl.BlockSpec((4, seq, tc), lambda j: (0, 0, j)),
        out_shape=jax.ShapeDtypeStruct((4, seq, d), F32),
        scratch_shapes=[pltpu.VMEM((seq, w), F32)] + [pltpu.VMEM((seq, V7X_LANES), F32)] * (2 * tc // V7X_LANES),
        compiler_params=_params("arbitrary"),
        name="hyena_filter",
    )(z, f1w, f1b, f1f, f2w, f2b, f2f, f3w, f3w, f3w, f3w, deltas, fwd, rot)


def _hyena_conv_kernel(*refs, gt, pieces):
    x1_refs, x2_refs, v_refs = refs[:pieces], refs[pieces:2 * pieces], refs[2 * pieces:3 * pieces]
    fwd_ref, inv_ref, k_ref, skip_ref, o_ref = refs[3 * pieces:3 * pieces + 5]
    ue_scr, uo_scr, ze_scr, zo_scr, ye_scr, yo_scr = refs[3 * pieces + 5:3 * pieces + 11]
    out_scrs = refs[3 * pieces + 11:]
    seq = o_ref.shape[0]
    half = seq // 2
    even = pl.ds(0, half, stride=2)
    odd = pl.ds(1, half, stride=2)

    def samples(piece_refs, rows):
        return jnp.concatenate([r[rows, :] for r in piece_refs], axis=1)

    def longconv(order):
        ye_scr[...] = jnp.zeros_like(ye_scr)
        yo_scr[...] = jnp.zeros_like(yo_scr)

        def body(c, carry):
            g0 = pl.multiple_of(c * gt, gt)
            rows = pl.ds(g0, gt)
            ue = ue_scr[...]
            uo = uo_scr[...]
            a = _dot(fwd_ref[0, rows, :], ue)
            b = _dot(fwd_ref[1, rows, :], ue)
            cc = _dot(fwd_ref[2, rows, :], uo)
            d = _dot(fwd_ref[3, rows, :], uo)

            def times_filter(first, ur, ui):
                kr = k_ref[2 * order, pl.ds(first + g0, gt), :]
                ki = k_ref[2 * order + 1, pl.ds(first + g0, gt), :]
                return kr * ur + ki * ui, kr * ui - ki * ur

            yra, yia = times_filter(0, a + cc, b + d)
            yrb, yib = times_filter(half, b - d, a - cc)
            ye_scr[...] += (_dot(inv_ref[0, :, rows], (yra + yib).astype(BF))
                            + _dot(inv_ref[1, :, rows], (yia + yrb).astype(BF)))
            yo_scr[...] += (_dot(inv_ref[2, :, rows], (yra - yib).astype(BF))
                            + _dot(inv_ref[3, :, rows], (yia - yrb).astype(BF)))
            return carry

        lax.fori_loop(0, half // gt, body, 0)

    scale = 1.0 / seq
    ve = samples(v_refs, even)
    vo = samples(v_refs, odd)
    ue_scr[...] = ve.astype(BF)
    uo_scr[...] = vo.astype(BF)
    longconv(0)
    ze = samples(x1_refs, even) * (ye_scr[...] * scale + ve * skip_ref[0:1, :])
    zo = samples(x1_refs, odd) * (yo_scr[...] * scale + vo * skip_ref[0:1, :])
    ze_scr[...] = ze
    zo_scr[...] = zo
    ue_scr[...] = ze.astype(BF)
    uo_scr[...] = zo.astype(BF)
    longconv(1)
    oe = samples(x2_refs, even) * (ye_scr[...] * scale + ze_scr[...] * skip_ref[1:2, :])
    oo = samples(x2_refs, odd) * (yo_scr[...] * scale + zo_scr[...] * skip_ref[1:2, :])
    for k, out_scr in enumerate(out_scrs):
        lanes = slice(k * V7X_LANES, (k + 1) * V7X_LANES)
        out_scr[even, :] = oe[:, lanes]
        out_scr[odd, :] = oo[:, lanes]
        o_ref[:, lanes] = out_scr[...].astype(BF)


def _hyena_conv(u3, fwd, inv, kspec, skip, nb, seq, d):
    tc = _tile(d, CHAN_TILE)
    nc = d // tc
    half = seq // 2
    gt = _tile(half, FREQ_TILE)
    pieces = tc // V7X_LANES
    once = dict(pipeline_mode=pl.Buffered(1))
    piece_specs = [pl.BlockSpec((seq, V7X_LANES), lambda j, b, o=o, k=k: (b, (o * nc + j) * pieces + k))
                   for o in range(3) for k in range(pieces)]
    return pl.pallas_call(
        functools.partial(_hyena_conv_kernel, gt=gt, pieces=pieces),
        scratch_shapes=[pltpu.VMEM((half, tc), BF), pltpu.VMEM((half, tc), BF)]
        + [pltpu.VMEM((half, tc), F32)] * 4 + [pltpu.VMEM((seq, V7X_LANES), F32)] * pieces,
        grid=(nc, nb),
        in_specs=piece_specs + [
            pl.BlockSpec((4, half, half), lambda j, b: (0, 0, 0), **once),
            pl.BlockSpec((4, half, half), lambda j, b: (0, 0, 0), **once),
            pl.BlockSpec((4, seq, tc), lambda j, b: (0, 0, j), **once),
            pl.BlockSpec((2, tc), lambda j, b: (0, j)),
        ],
        out_specs=pl.BlockSpec((seq, tc), lambda j, b: (b, j)),
        out_shape=jax.ShapeDtypeStruct((nb * seq, d), BF),
        compiler_params=_params("parallel", "parallel"),
        name="hyena_conv",
    )(*([u3] * (3 * pieces)), fwd, inv, kspec, skip)


def _moe_block(streams, mods, layer, fg, w_gate, w_up, w_down, nb, final=False):
    n_experts = streams[0][2].shape[0]
    group = _sc_group(nb)
    routed = []
    for xs, hn, aff, t, ctx_row in streams:
        cap = max(1, EC_CAPACITY * t // n_experts)
        posm, gate = _route(aff, nb, t, cap)
        post, gatet = _route_by_token(posm, gate, nb)
        xin = None if hn.dtype == jnp.int32 else _gather(posm, hn, nb, t, cap)
        routed.append((posm, post, gatet, cap, xin))
    ys = [[] for _ in streams]
    for e0, count in ((0, group), (group, n_experts - group)):
        xins = []
        for (xs, hn, aff, t, ctx_row), (posm, post, gatet, cap, xin) in zip(streams, routed):
            if xin is None:
                xins.append(([_sc_gather(hn, posm, e, nb, n_experts, t, cap) for e in range(e0, e0 + count, group)], 0))
            else:
                xins.append(([xin], e0))
        for acc, y in zip(ys, _experts(xins, w_gate, w_up, w_down, layer, e0, count, group)):
            acc.append(y)
    return [_combine(post, gatet, y, xs, mods, fg, nb, t, cap, ctx_row=ctx_row, final=final and ctx_row is None)
            for (xs, _, _, t, ctx_row), (_, post, gatet, cap, _), y in zip(streams, routed, ys)]


def kernel(x, c, ctx, c_ctx, w_ada, b_ada, norm1_g, norm2_g, final_g, a_w_in, a_conv, a_w_out, b_w_qkv, b_lq1, b_lk1, b_lq2, b_lk2, b_subln_g, b_w_out, c_w_in, c_conv, c_f1_w, c_f1_b, c_f1_freq, c_f2_w, c_f2_b, c_f2_freq, c_f3_w, c_skip, c_w_out, moe_router, moe_w_gate, moe_w_up, moe_w_down):
    nb, seq, d = x.shape
    ctx_len = ctx.shape[1]
    depth = w_ada.shape[0]
    assert nb < MOD_ROWS and d % (2 * DA_HEAD_DIM) == 0
    ctx_row = nb

    cvec = jnp.concatenate([c, c_ctx[None, :], jnp.zeros((MOD_ROWS - nb - 1, d), F32)], axis=0)
    mods_all = _ada(cvec, w_ada, b_ada).reshape(depth, MOD_ROWS, 6, d)

    attn_layers = [i for i in range(depth) if i % N_MIXERS == 1]
    last_ctx_read = max(attn_layers) if attn_layers else -1

    xs = x.reshape(nb * seq, d)
    cs_tok = ctx.reshape(nb * ctx_len, d)
    fg = final_g[None, :]

    for i in range(depth):
        kind, j = i % N_MIXERS, i // N_MIXERS
        update_ctx = i < last_ctx_read
        final = i == depth - 1
        mods = mods_all[i]
        g1 = norm1_g[i][None, :]
        g2 = norm2_g[i][None, :]
        wr = moe_router[i].T
        wrh, wrl = _split(wr)
        moe_w = (moe_w_gate, moe_w_up, moe_w_down)

        ctx_stream = []
        if kind == 0:
            w_in, w_out = a_w_in[j].astype(BF), a_w_out[j].astype(BF)
            if update_ctx:
                cn, chn, caff = _shortconv_layer(cs_tok, mods, g1, w_in, a_conv[j], w_out, g2, wrh, wrl,
                                                 ctx_len, ctx_row=ctx_row)
                ctx_stream = [(cn, chn, caff, ctx_len, ctx_row)]
            xn, hn, aff = _shortconv_layer(xs, mods, g1, w_in, a_conv[j], w_out, g2, wrh, wrl, seq)
        elif kind == 1:
            assert not update_ctx
            lam_init = 0.8 - 0.6 * math.exp(-0.3 * i)
            w_qkv = b_w_qkv[j].astype(BF)
            tables = [jnp.asarray(t) for t in _rope_tables(seq)]
            q, k, v = _qkv(xs, mods, g1, w_qkv, tables, seq, 3, True)
            ctab = [t[:ctx_len] for t in tables]
            kc, vc = _qkv(cs_tok, mods, g1, w_qkv[:, d:], ctab, ctx_len, 2, False, ctx_row=ctx_row)
            o = _attention(q, k, v, kc, vc, b_lq1[j][None, :], b_lk1[j][None, :], b_lq2[j][None, :],
                           b_lk2[j][None, :], b_subln_g[j][None, :], nb, seq, ctx_len, lam_init)
            xn, hn, aff = _outproj_layer(o, xs, mods, b_w_out[j].astype(BF), g2, wrh, wrl, seq)
        else:
            assert not update_ctx
            fwd_np, inv_np = _dft_half_tables(seq)
            fwd, inv = jnp.asarray(fwd_np).astype(BF), jnp.asarray(inv_np).astype(BF)
            w = c_f2_w.shape[-1]
            f1w = jnp.zeros((V7X_LANES, w), F32).at[:HY_EMB_DIM].set(c_f1_w[j])
            kspec = _hyena_filter(jnp.asarray(_hyena_features(seq)), f1w, c_f1_b[j][None, :], c_f1_freq[j][None, :],
                                  c_f2_w[j], c_f2_b[j][None, :], c_f2_freq[j][None, :], c_f3_w[j],
                                  jnp.asarray(_hyena_deltas(d)), fwd, jnp.asarray(_dft_half_shift(seq)), d)
            u3 = _hyena_in(xs, mods, g1, c_w_in[j].astype(BF), c_conv[j], seq)
            z = _hyena_conv(u3, fwd, inv, kspec, c_skip[j], nb, seq, d)
            xn, hn, aff = _outproj_layer(z, xs, mods, c_w_out[j].astype(BF), g2, wrh, wrl, seq)

        outs = _moe_block([(xn, hn, aff, seq, None)] + ctx_stream, mods, i, fg, *moe_w, nb, final=final)
        xs = outs[0]
        if ctx_stream:
            cs_tok = outs[1]

    return xs.reshape(nb, seq, d)
```

```python
import functools
import math

import jax
import jax.numpy as jnp
import numpy as np
from jax import lax
from jax.experimental import pallas as pl
from jax.experimental.pallas import tpu as pltpu

BF = jnp.bfloat16
F32 = jnp.float32

GRID_W = 64
DA_HEAD_DIM = 64
ROPE_AXIS_DIM = DA_HEAD_DIM // 2
ROPE_THETA = 10000.0
SUBLN_EPS = 1e-5
NORM_EPS = 1e-6
N_MIXERS = 3
EC_CAPACITY = 2
HY_EMB_DIM = 33
HY_BANDS = (HY_EMB_DIM - 1) // 2
HY_FAST_DECAY_PCT = 0.3
HY_SLOW_DECAY_PCT = 1.5
HY_DECAY_TARGET = 1e-2

V7X_LANES = 128
V7X_BF16_SUBLANES = 16
V7X_VMEM_BYTES = 64 * 2**20
VMEM_COMPILER_RESERVE = 8 * 2**20
VMEM_LIMIT = V7X_VMEM_BYTES - VMEM_COMPILER_RESERVE
F32_MAGNITUDE_BITS = 31
BF16_BITS = 16
HIGH_HALF = 0xFFFF0000
HALO = V7X_BF16_SUBLANES
MOD_ROWS = 16
ROW_TILE = 512
EXPERT_ROW_STEPS = 2
PROJ_ROW_TILE = 1024
ATTN_Q_TILE = 256
CHAN_TILE = 256
ADA_COL_TILE = 1536
FREQ_TILE = 1024
SC_GATHER_ROWS = 64


def _params(*sem):
    return pltpu.CompilerParams(dimension_semantics=sem, vmem_limit_bytes=VMEM_LIMIT)


def _dot(a, b):
    return jnp.dot(a, b, preferred_element_type=F32)


def _dot_nt(a, b):
    return lax.dot_general(a, b, (((1,), (1,)), ((), ())), preferred_element_type=F32)


def _split(a):
    hi = a.astype(BF)
    lo = (a - hi.astype(F32)).astype(BF)
    return hi, lo


def _dot3(a, b):
    ah, al = _split(a)
    bh, bl = _split(b)
    return _dot(ah, bh) + (_dot(ah, bl) + _dot(al, bh))


def _sigmoid(a):
    return 1.0 / (1.0 + jnp.exp(-a))


def _norm_mod(x, g, shift, scale):
    ms = jnp.mean(x * x, axis=-1, keepdims=True)
    return (x * lax.rsqrt(ms + NORM_EPS) * g) * (1.0 + scale) + shift


def _tile(n, pref):
    t = min(n, pref)
    assert n % t == 0, (n, pref)
    return t


def _full(shape):
    nd = len(shape)
    return pl.BlockSpec(shape, lambda *_: (0,) * nd)


def _ada_kernel(c_ref, w_ref, b_ref, o_ref):
    c = c_ref[...]
    o_ref[0] = _dot3(c * _sigmoid(c), w_ref[0]) + b_ref[0]


def _ada(cvec, w_ada, b_ada):
    depth, d, n6 = w_ada.shape
    tn = _tile(n6, ADA_COL_TILE)
    return pl.pallas_call(
        _ada_kernel,
        grid=(depth, n6 // tn),
        in_specs=[
            _full((MOD_ROWS, d)),
            pl.BlockSpec((1, d, tn), lambda i, j: (i, 0, j)),
            pl.BlockSpec((1, 1, tn), lambda i, j: (i, 0, j)),
        ],
        out_specs=pl.BlockSpec((1, MOD_ROWS, tn), lambda i, j: (i, 0, j)),
        out_shape=jax.ShapeDtypeStruct((depth, MOD_ROWS, n6), F32),
        compiler_params=_params("parallel", "parallel"),
        name="ada",
    )(cvec, w_ada, b_ada.reshape(depth, 1, n6))


def _pack_pairs(hh):
    half = hh.shape[1] // 2
    lo = lax.shift_right_logical(pltpu.bitcast(hh[:, :half].astype(F32), jnp.uint32), jnp.uint32(BF16_BITS))
    hi = pltpu.bitcast(hh[:, half:].astype(F32), jnp.uint32) & jnp.uint32(HIGH_HALF)
    return pltpu.bitcast(lo | hi, jnp.int32)


def _unpack_pairs(words):
    u = pltpu.bitcast(words, jnp.uint32)
    lo = pltpu.bitcast(lax.shift_left(u, jnp.uint32(BF16_BITS)), F32).astype(BF)
    hi = pltpu.bitcast(u & jnp.uint32(HIGH_HALF), F32).astype(BF)
    return jnp.concatenate([lo, hi], axis=1)


def _residual_router(x, y, mod, g2_ref, wrh_ref, wrl_ref, xo_ref, hn_ref, aff_ref):
    xn = x + mod[2:3] * y
    xo_ref[...] = xn
    hn = _norm_mod(xn, g2_ref[...], mod[3:4], mod[4:5])
    hh, hl = _split(hn)
    hn_ref[...] = _pack_pairs(hh) if hn_ref.dtype == jnp.int32 else hh
    e = wrh_ref.shape[0]
    both = _dot_nt(jnp.concatenate([wrh_ref[...], wrl_ref[...]], axis=0), hh)
    logits = both[:e] + (_dot_nt(wrh_ref[...], hl) + both[e:])
    p = jnp.exp(logits - jnp.max(logits, axis=0, keepdims=True))
    aff_ref[...] = p / jnp.sum(p, axis=0, keepdims=True)


def _router_specs(n, d, e, tm, packed):
    in_specs = [_full((1, d)), _full((e, d)), _full((e, d))]
    hn_cols, hn_dtype = (d // 2, jnp.int32) if packed else (d, BF)
    out_specs = [
        pl.BlockSpec((tm, d), lambda i: (i, 0)),
        pl.BlockSpec((tm, hn_cols), lambda i: (i, 0)),
        pl.BlockSpec((e, tm), lambda i: (0, i)),
    ]
    out_shape = [
        jax.ShapeDtypeStruct((n, d), F32),
        jax.ShapeDtypeStruct((n, hn_cols), hn_dtype),
        jax.ShapeDtypeStruct((e, n), F32),
    ]
    return in_specs, out_specs, out_shape


def _halo_specs(n, d, tm):
    per = tm // HALO
    last = n // HALO - 1
    return [
        pl.BlockSpec((HALO, d), lambda i: (jnp.maximum(i * per - 1, 0), 0)),
        pl.BlockSpec((tm, d), lambda i: (i, 0)),
        pl.BlockSpec((HALO, d), lambda i: (jnp.minimum((i + 1) * per, last), 0)),
    ]


def _mod_spec(d, tiles_per_seq, ctx_row):
    if ctx_row is None:
        return pl.BlockSpec((1, 6, d), lambda i: (i // tiles_per_seq, 0, 0))
    return pl.BlockSpec((1, 6, d), lambda i: (ctx_row, 0, 0))


def _halo_rows(xp_ref, x_ref, xn_ref, g_ref, mod, tiles_per_seq):
    t = pl.program_id(0) % tiles_per_seq
    g = g_ref[...]
    hp, hx, hn = [_norm_mod(r[...], g, mod[0:1], mod[1:2]) for r in (xp_ref, x_ref, xn_ref)]
    hp = jnp.where(t == 0, 0.0, hp)
    hn = jnp.where(t == tiles_per_seq - 1, 0.0, hn)
    return jnp.concatenate([hp, hx, hn], axis=0).astype(BF)


def _conv3(s, w_ref, tm):
    n = s.shape[0]
    prev = pltpu.roll(s, 1, 0)[HALO:HALO + tm]
    nxt = pltpu.roll(s, n - 1, 0)[HALO:HALO + tm]
    return prev * w_ref[0:1, :] + s[HALO:HALO + tm] * w_ref[1:2, :] + nxt * w_ref[2:3, :]


def _shortconv_kernel(xp_ref, x_ref, xn_ref, mod_ref, g1_ref, win_ref, wconv_ref, wout_ref,
                      g2_ref, wrh_ref, wrl_ref, xo_ref, hn_ref, aff_ref, *, tiles_per_seq):
    tm, d = x_ref.shape
    mod = mod_ref[0]
    h = _halo_rows(xp_ref, x_ref, xn_ref, g1_ref, mod, tiles_per_seq)
    gate = _dot(h[HALO:HALO + tm], win_ref[:, 0:d])
    s = _dot(h, win_ref[:, d:2 * d]) * _dot(h, win_ref[:, 2 * d:3 * d])
    z = (gate * _conv3(s, wconv_ref, tm)).astype(BF)
    y = _dot(z, wout_ref[...])
    _residual_router(x_ref[...], y, mod, g2_ref, wrh_ref, wrl_ref, xo_ref, hn_ref, aff_ref)


def _shortconv_layer(xs, mods, g1, w_in, w_conv, w_out, g2, wrh, wrl, seq, ctx_row=None):
    n, d = xs.shape
    e = wrh.shape[0]
    tm = _tile(seq, PROJ_ROW_TILE)
    tps = seq // tm
    r_in, r_out, r_shape = _router_specs(n, d, e, tm, packed=ctx_row is None)
    return pl.pallas_call(
        functools.partial(_shortconv_kernel, tiles_per_seq=tps),
        grid=(n // tm,),
        in_specs=_halo_specs(n, d, tm) + [
            _mod_spec(d, tps, ctx_row), _full((1, d)), _full((d, 3 * d)), _full((3, d)), _full((d, d)),
        ] + r_in,
        out_specs=r_out,
        out_shape=r_shape,
        compiler_params=_params("parallel"),
        name="shortconv_layer",
    )(xs, xs, xs, mods, g1, w_in, w_conv, w_out, g2, wrh, wrl)


def _outproj_kernel(o_ref, x_ref, mod_ref, wout_ref, g2_ref, wrh_ref, wrl_ref, xo_ref, hn_ref, aff_ref):
    y = _dot(o_ref[...], wout_ref[...])
    _residual_router(x_ref[...], y, mod_ref[0], g2_ref, wrh_ref, wrl_ref, xo_ref, hn_ref, aff_ref)


def _outproj_layer(o, xs, mods, w_out, g2, wrh, wrl, seq):
    n, d = xs.shape
    e = wrh.shape[0]
    tm = _tile(seq, PROJ_ROW_TILE)
    tps = seq // tm
    r_in, r_out, r_shape = _router_specs(n, d, e, tm, packed=True)
    return pl.pallas_call(
        _outproj_kernel,
        grid=(n // tm,),
        in_specs=[pl.BlockSpec((tm, d), lambda i: (i, 0)), pl.BlockSpec((tm, d), lambda i: (i, 0)),
                  _mod_spec(d, tps, None), _full((d, d))] + r_in,
        out_specs=r_out,
        out_shape=r_shape,
        compiler_params=_params("parallel"),
        name="outproj_layer",
    )(o, xs, mods, w_out, g2, wrh, wrl)


def _excl_cumsum_lanes(m):
    rows, t = m.shape
    a = lax.broadcasted_iota(jnp.int32, (V7X_LANES, V7X_LANES), 0)
    b = lax.broadcasted_iota(jnp.int32, (V7X_LANES, V7X_LANES), 1)
    tri = jnp.where(a < b, 1.0, 0.0).astype(BF)
    carry = jnp.zeros((rows, 1), F32)
    out = []
    for c in range(t // V7X_LANES):
        blk = m[:, c * V7X_LANES:(c + 1) * V7X_LANES]
        out.append(_dot(blk.astype(BF), tri) + carry)
        carry = carry + jnp.sum(blk, axis=1, keepdims=True)
    return jnp.concatenate(out, axis=1)


def _route_kernel(aff_ref, posm_ref, gate_ref, *, cap, nb):
    e = aff_ref.shape[0]
    t = aff_ref.shape[1] // nb
    aff = jnp.concatenate([aff_ref[:, b * t:(b + 1) * t] for b in range(nb)], axis=0)
    bits = pltpu.bitcast(aff, jnp.int32)

    def step(i, thr):
        cand = thr | jnp.left_shift(jnp.int32(1), F32_MAGNITUDE_BITS - 1 - i)
        cnt = jnp.sum(jnp.where(bits >= cand, 1.0, 0.0), axis=1, keepdims=True)
        return jnp.where(cnt >= cap, cand, thr)

    thr = lax.fori_loop(0, F32_MAGNITUDE_BITS, step, jnp.zeros((nb * e, 1), jnp.int32))
    gt = jnp.where(bits > thr, 1.0, 0.0)
    eq = jnp.where(bits == thr, 1.0, 0.0)
    need = cap - jnp.sum(gt, axis=1, keepdims=True)
    sel = gt + eq * jnp.where(_excl_cumsum_lanes(eq) < need, 1.0, 0.0)
    pos = jnp.where(sel > 0.0, _excl_cumsum_lanes(sel), -1.0)
    posm_ref[...] = pos.astype(jnp.int32)
    gate_ref[...] = sel * aff


def _route_by_token_kernel(posm_ref, gate_ref, bt_ref, *, nb):
    e = posm_ref.shape[0] // nb
    t = posm_ref.shape[1]
    pad = jnp.zeros((V7X_LANES - 2 * e, t), F32)
    for b in range(nb):
        rows = slice(b * e, (b + 1) * e)
        bt_ref[b * t:(b + 1) * t, :] = jnp.concatenate(
            [posm_ref[rows, :].astype(F32), gate_ref[rows, :], pad], axis=0).T


def _route(aff, nb, t, cap):
    e = aff.shape[0]
    return pl.pallas_call(
        functools.partial(_route_kernel, cap=cap, nb=nb),
        grid=(1,),
        in_specs=[_full((e, nb * t))],
        out_specs=[_full((nb * e, t)), _full((nb * e, t))],
        out_shape=[jax.ShapeDtypeStruct((nb * e, t), jnp.int32), jax.ShapeDtypeStruct((nb * e, t), F32)],
        compiler_params=_params("arbitrary"),
        name="route",
    )(aff)


def _route_by_token(posm, gate, nb):
    rows, t = posm.shape
    assert 2 * rows // nb <= V7X_LANES
    return pl.pallas_call(
        functools.partial(_route_by_token_kernel, nb=nb),
        grid=(1,),
        in_specs=[_full((rows, t)), _full((rows, t))],
        out_specs=_full((nb * t, V7X_LANES)),
        out_shape=jax.ShapeDtypeStruct((nb * t, V7X_LANES), F32),
        compiler_params=_params("arbitrary"),
        name="route_by_token",
    )(posm, gate)


def _sc_gather(table, posm, e0, nb, n_experts, t, cap):
    from jax.experimental.pallas import tpu_sc as plsc

    info = plsc.get_sparse_core_info()
    cores, lanes = info.num_cores, info.num_lanes
    workers = cores * info.num_subcores
    group = workers // nb
    words = table.shape[1]
    assert workers % nb == 0 and n_experts % group == 0 and cap % SC_GATHER_ROWS == 0 and t % lanes == 0

    @functools.partial(
        pl.kernel,
        mesh=plsc.VectorSubcoreMesh(core_axis_name="core", subcore_axis_name="subcore"),
        compiler_params=pltpu.CompilerParams(needs_layout_passes=False),
        out_type=jax.ShapeDtypeStruct((group * nb * cap, words), jnp.int32),
        scratch_types=[pltpu.VMEM((t,), jnp.int32), pltpu.VMEM((cap,), jnp.int32),
                       pltpu.VMEM((SC_GATHER_ROWS, words), jnp.int32), pltpu.SemaphoreType.DMA],
    )
    def gather(table_hbm, posm_hbm, out_hbm, pos_v, idx_v, rows_v, sem):
        w = lax.axis_index("subcore") * cores + lax.axis_index("core")
        e_local = w // nb
        b = w % nb
        pltpu.sync_copy(posm_hbm.at[b * n_experts + e0 + e_local], pos_v)

        @pl.loop(0, t // lanes)
        def _(i):
            p = pos_v[pl.ds(i * lanes, lanes)]
            token = lax.iota(jnp.int32, lanes) + (i * lanes + b * t)
            plsc.store_scatter(idx_v, [p], token, mask=p >= 0)

        out_base = (e_local * nb + b) * cap

        @pl.loop(0, cap // SC_GATHER_ROWS)
        def _(j):
            pltpu.async_copy(table_hbm.at[idx_v.at[pl.ds(j * SC_GATHER_ROWS, SC_GATHER_ROWS)]], rows_v, sem).wait()
            pltpu.sync_copy(rows_v, out_hbm.at[pl.ds(out_base + j * SC_GATHER_ROWS, SC_GATHER_ROWS)])

    return gather(table, posm).reshape(group, nb * cap, words)


def _sc_group(nb):
    from jax.experimental.pallas import tpu_sc as plsc

    info = plsc.get_sparse_core_info()
    return info.num_cores * info.num_subcores // nb


def _gather_kernel(posm_ref, hn_ref, o_ref, *, cap):
    e = posm_ref.shape[0]
    t, d = hn_ref.shape
    slot = lax.broadcasted_iota(jnp.int32, (cap, t), 0)
    onehot = jnp.concatenate(
        [jnp.where(posm_ref[k:k + 1, :] == slot, 1.0, 0.0).astype(BF) for k in range(e)], axis=0)
    o_ref[...] = _dot(onehot, hn_ref[...]).astype(BF).reshape(e, cap, d)


def _gather(posm, hn, nb, t, cap):
    e = posm.shape[0] // nb
    d = hn.shape[1]
    return pl.pallas_call(
        functools.partial(_gather_kernel, cap=cap),
        grid=(nb,),
        in_specs=[pl.BlockSpec((e, t), lambda b: (b, 0)), pl.BlockSpec((t, d), lambda b: (b, 0))],
        out_specs=pl.BlockSpec((e, cap, d), lambda b: (0, b, 0)),
        out_shape=jax.ShapeDtypeStruct((e, nb * cap, d), BF),
        compiler_params=_params("parallel"),
        name="moe_gather",
    )(posm, hn)


def _expert_kernel(*refs, pieces, per):
    n_in = sum(pieces)
    wg_ref, wu_ref, wd_ref = refs[n_in:n_in + 3]
    y_refs = refs[n_in + 3:n_in + 3 + len(pieces)]
    w_scr = refs[-1]

    @pl.when(pl.program_id(1) == 0)
    def _():
        w_scr[0] = wg_ref[0, 0].astype(BF)
        w_scr[1] = wu_ref[0, 0].astype(BF)
        w_scr[2] = wd_ref[0, 0].astype(BF)

    which = pl.program_id(0) // per

    start = 0
    for n_pieces, y_ref in zip(pieces, y_refs):
        x_refs = refs[start:start + n_pieces]
        start += n_pieces
        rows = x_refs[0].shape[1]
        tr = _tile(rows, ROW_TILE)

        def body(j, carry, x_refs=x_refs, y_ref=y_ref, tr=tr):
            r0 = pl.multiple_of(j * tr, tr)
            xs = x_refs[0][0, pl.ds(r0, tr), :]
            for p in range(1, len(x_refs)):
                xs = jnp.where(which == p, x_refs[p][0, pl.ds(r0, tr), :], xs)
            if xs.dtype == jnp.int32:
                xs = _unpack_pairs(xs)
            a = _dot(xs, w_scr[0])
            b = _dot(xs, w_scr[1])
            hm = (a * _sigmoid(a) * b).astype(BF)
            y_ref[0, pl.ds(r0, tr), :] = _dot(hm, w_scr[2]).astype(BF)
            return carry

        lax.fori_loop(0, rows // tr, body, 0)


def _experts(streams, w_gate, w_up, w_down, layer, e0, count, per):
    d = w_gate.shape[-2]
    f = w_gate.shape[-1]
    assert f == d
    wspec = pl.BlockSpec((1, 1, d, f), lambda k, j: (layer, e0 + k, 0, 0))
    xspecs, xargs = [], []
    for arrays, first in streams:
        for p, x in enumerate(arrays):
            if len(arrays) == 1:
                index = lambda k, j, first=first: (first + k, j, 0)
            else:
                index = lambda k, j, p=p: (
                    jnp.clip(k - p * per, 0, per - 1),
                    jnp.where(k < p * per, 0, jnp.where(k >= (p + 1) * per, EXPERT_ROW_STEPS - 1, j)), 0)
            xspecs.append(pl.BlockSpec((1, x.shape[1] // EXPERT_ROW_STEPS, x.shape[2]), index))
            xargs.append(x)
    rows = [arrays[0].shape[1] for arrays, _ in streams]
    return pl.pallas_call(
        functools.partial(_expert_kernel, pieces=tuple(len(arrays) for arrays, _ in streams), per=per),
        grid=(count, EXPERT_ROW_STEPS),
        in_specs=xspecs + [wspec, wspec, wspec],
        out_specs=[pl.BlockSpec((1, r // EXPERT_ROW_STEPS, d), lambda k, j: (k, j, 0)) for r in rows],
        out_shape=[jax.ShapeDtypeStruct((count, r, d), BF) for r in rows],
        scratch_shapes=[pltpu.VMEM((3, d, f), BF)],
        compiler_params=_params("parallel", "arbitrary"),
        name="moe_experts",
    )(*xargs, w_gate, w_up, w_down)


def _combine_kernel(bt_ref, *refs, cap, final):
    y_refs = refs[:-4]
    x_ref, mod_ref, fg_ref, o_ref = refs[-4:]
    d = y_refs[0].shape[2]
    e = sum(r.shape[0] for r in y_refs)
    tq = x_ref.shape[0]
    pt = bt_ref[:, 0:e]
    gt = bt_ref[:, e:2 * e]
    if cap % V7X_LANES == 0:
        slot = lax.broadcasted_iota(jnp.int32, (tq, cap), 1).astype(F32)
        pieces = [jnp.where(pt[:, k:k + 1] == slot, gt[:, k:k + 1], 0.0).astype(BF) for k in range(e)]
        scat = jnp.concatenate(pieces, axis=1)
    else:
        slot = lax.broadcasted_iota(jnp.int32, (tq, e * cap), 1).astype(F32)
        scat = jnp.zeros((tq, e * cap), F32)
        for k in range(e):
            pk = pt[:, k:k + 1]
            scat = jnp.where((pk >= 0.0) & (pk + float(k * cap) == slot), gt[:, k:k + 1], scat)
        scat = scat.astype(BF)
    out, col = None, 0
    for y_ref in y_refs:
        width = y_ref.shape[0] * cap
        part = _dot(scat[:, col:col + width], y_ref[...].reshape(width, d))
        out = part if out is None else out + part
        col += width
    xn = x_ref[...] + mod_ref[0][5:6] * out
    if final:
        ms = jnp.mean(xn * xn, axis=-1, keepdims=True)
        xn = xn * lax.rsqrt(ms + NORM_EPS) * fg_ref[...]
    o_ref[...] = xn


def _combine(by_token, ys, xs, mods, fg, nb, t, cap, ctx_row=None, final=False):
    n, d = xs.shape
    tq = _tile(t, ROW_TILE)
    tpb = t // tq
    if ctx_row is None:
        mspec = pl.BlockSpec((1, 6, d), lambda b, i: (b, 0, 0))
    else:
        mspec = pl.BlockSpec((1, 6, d), lambda b, i: (ctx_row, 0, 0))
    return pl.pallas_call(
        functools.partial(_combine_kernel, cap=cap, final=final),
        grid=(nb, tpb),
        in_specs=[
            pl.BlockSpec((tq, V7X_LANES), lambda b, i: (b * tpb + i, 0)),
        ] + [pl.BlockSpec((y.shape[0], cap, d), lambda b, i: (0, b, 0)) for y in ys] + [
            pl.BlockSpec((tq, d), lambda b, i: (b * tpb + i, 0)),
            mspec,
            _full((1, d)),
        ],
        out_specs=pl.BlockSpec((tq, d), lambda b, i: (b * tpb + i, 0)),
        out_shape=jax.ShapeDtypeStruct((n, d), F32),
        compiler_params=_params("parallel", "parallel"),
        name="moe_combine",
    )(by_token, *ys, xs, mods, fg)


def _rope_tables(seq):
    rows = seq // GRID_W
    row = np.repeat(np.arange(rows, dtype=np.float32), GRID_W)
    col = np.tile(np.arange(GRID_W, dtype=np.float32), rows)
    inv_freq = (ROPE_THETA ** (-np.arange(0, ROPE_AXIS_DIM, 2, dtype=np.float32) / ROPE_AXIS_DIM)).astype(np.float32)
    lane = np.arange(2 * DA_HEAD_DIM)
    within = lane % DA_HEAD_DIM
    axis = within // ROPE_AXIS_DIM
    half = (within % ROPE_AXIS_DIM) // (ROPE_AXIS_DIM // 2)
    idx = within % (ROPE_AXIS_DIM // 2)
    pos = np.where(axis[None, :] == 0, row[:, None], col[:, None])
    ang = (pos * inv_freq[idx][None, :]).astype(np.float32)
    cos = np.cos(ang).astype(np.float32)
    sin = np.sin(ang).astype(np.float32)
    sin_lo = np.where(half[None, :] == 1, sin, 0.0).astype(np.float32)
    sin_hi = np.where(half[None, :] == 0, -sin, 0.0).astype(np.float32)
    return cos, sin_lo, sin_hi


def _qkv_kernel(x_ref, mod_ref, g_ref, w_ref, cos_ref, sa_ref, sb_ref, *o_refs, rope):
    d = x_ref.shape[1]
    mod = mod_ref[0]
    h = _norm_mod(x_ref[...], g_ref[...], mod[0:1], mod[1:2]).astype(BF)
    slab = 2 * DA_HEAD_DIM
    shift = ROPE_AXIS_DIM // 2
    for j, o_ref in enumerate(o_refs):
        u = _dot(h, w_ref[:, j * d:(j + 1) * d])
        if rope and j < 2:
            scale = DA_HEAD_DIM ** -0.5 * math.log2(math.e) if j == 0 else 1.0
            cos, sa, sb = cos_ref[...] * scale, sa_ref[...] * scale, sb_ref[...] * scale
            for hd in range(d // slab):
                xs = u[:, hd * slab:(hd + 1) * slab]
                r = xs * cos + pltpu.roll(xs, shift, 1) * sa + pltpu.roll(xs, slab - shift, 1) * sb
                o_ref[:, hd * slab:(hd + 1) * slab] = r.astype(BF)
        else:
            o_ref[...] = u.astype(BF)


def _qkv(xs, mods, g1, w, tables, seq, nout, rope, ctx_row=None):
    n, d = xs.shape
    tm = _tile(seq, PROJ_ROW_TILE)
    tps = seq // tm
    slab = 2 * DA_HEAD_DIM
    tspec = pl.BlockSpec((tm, slab), lambda i: (i % tps, 0))
    return pl.pallas_call(
        functools.partial(_qkv_kernel, rope=rope),
        grid=(n // tm,),
        in_specs=[pl.BlockSpec((tm, d), lambda i: (i, 0)), _mod_spec(d, tps, ctx_row), _full((1, d)),
                  _full((d, nout * d)), tspec, tspec, tspec],
        out_specs=[pl.BlockSpec((tm, d), lambda i: (i, 0))] * nout,
        out_shape=[jax.ShapeDtypeStruct((n, d), BF)] * nout,
        compiler_params=_params("parallel"),
        name="attn_qkv",
    )(xs, mods, g1, w, *tables)


def _attn_kernel(q_ref, k_ref, v_ref, kc_ref, vc_ref, lq1_ref, lk1_ref, lq2_ref, lk2_ref, sg_ref, o_ref,
                 sl_a, sc_a, sl_b, sc_b, va_scr, vca_scr, *, lam_init, tq):
    seq = q_ref.shape[0]
    lam = (jnp.exp(jnp.sum(lq1_ref[...] * lk1_ref[...], axis=1, keepdims=True))
           - jnp.exp(jnp.sum(lq2_ref[...] * lk2_ref[...], axis=1, keepdims=True)) + lam_init)
    slots = ((sl_a, sc_a), (sl_b, sc_b))

    def scores(i, slot):
        sl_ref, sc_ref = slot
        q = q_ref[i * tq:(i + 1) * tq, :]
        lane = lax.broadcasted_iota(jnp.int32, q.shape, 1)
        zero = jnp.zeros_like(q)
        for mp, qm in enumerate((jnp.where(lane < DA_HEAD_DIM, q, zero), jnp.where(lane >= DA_HEAD_DIM, q, zero))):
            sl_ref[mp] = _dot_nt(qm, k_ref[...])
            sc_ref[mp] = _dot_nt(qm, kc_ref[...])

    slab = v_ref.shape[1]
    va_scr[:, 0:slab] = v_ref[...]
    va_scr[:, slab:2 * slab] = jnp.ones_like(v_ref)
    vca_scr[:, 0:slab] = vc_ref[...]
    vca_scr[:, slab:2 * slab] = jnp.ones_like(vc_ref)

    def unnormalised(sl_ref, sc_ref, mp):
        s_l = sl_ref[mp]
        s_c = sc_ref[mp]
        m = jnp.maximum(jnp.max(s_l, axis=1, keepdims=True), jnp.max(s_c, axis=1, keepdims=True))
        p_l = jnp.exp2(s_l - m).astype(BF)
        p_c = jnp.exp2(s_c - m).astype(BF)
        both = _dot(p_l, va_scr[...]) + _dot(p_c, vca_scr[...])
        return both[:, 0:slab], both[:, slab:slab + 1]

    def attend(i, slot):
        o1, t1 = unnormalised(*slot, 0)
        o2, t2 = unnormalised(*slot, 1)
        o = o1 * (1.0 / t1) - o2 * (lam / t2)
        ms = jnp.mean(o * o, axis=-1, keepdims=True)
        o_ref[i * tq:(i + 1) * tq, :] = (o * lax.rsqrt(ms + SUBLN_EPS) * sg_ref[...] * (1.0 - lam_init)).astype(BF)

    n = seq // tq
    scores(0, slots[0])
    for i in range(n):
        if i + 1 < n:
            scores(i + 1, slots[(i + 1) % 2])
        attend(i, slots[i % 2])


def _attention(q, k, v, kc, vc, lq1, lk1, lq2, lk2, sg, nb, seq, ctx_len, lam_init):
    n, d = q.shape
    slab = 2 * DA_HEAD_DIM
    heads = d // slab
    tq = _tile(seq, ATTN_Q_TILE)
    small = _full((1, DA_HEAD_DIM))
    lat = pl.BlockSpec((seq, slab), lambda b, h: (b, h))
    ctx = pl.BlockSpec((ctx_len, slab), lambda b, h: (b, h))
    score_scratch = [pltpu.VMEM((2, tq, seq), F32), pltpu.VMEM((2, tq, ctx_len), F32)]
    return pl.pallas_call(
        functools.partial(_attn_kernel, lam_init=lam_init, tq=tq),
        grid=(nb, heads),
        in_specs=[lat, lat, lat, ctx, ctx, small, small, small, small, _full((1, slab))],
        out_specs=lat,
        out_shape=jax.ShapeDtypeStruct((n, d), BF),
        scratch_shapes=score_scratch + score_scratch + [pltpu.VMEM((seq, 2 * slab), BF),
                                                        pltpu.VMEM((ctx_len, 2 * slab), BF)],
        compiler_params=_params("parallel", "parallel"),
        name="diff_attention",
    )(q, k, v, kc, vc, lq1, lk1, lq2, lk2, sg)


def _hyena_in_kernel(xp_ref, x_ref, xn_ref, mod_ref, g1_ref, win_ref, wconv_ref, o_ref, *, tiles_per_seq):
    tm, d = x_ref.shape
    h = _halo_rows(xp_ref, x_ref, xn_ref, g1_ref, mod_ref[0], tiles_per_seq)
    for j in range(3):
        u = _dot(h, win_ref[:, j * d:(j + 1) * d])
        o_ref[:, j * d:(j + 1) * d] = _conv3(u, wconv_ref.at[:, j * d:(j + 1) * d], tm)


def _hyena_in(xs, mods, g1, w_in, w_conv, seq):
    n, d = xs.shape
    tm = _tile(seq, PROJ_ROW_TILE)
    tps = seq // tm
    return pl.pallas_call(
        functools.partial(_hyena_in_kernel, tiles_per_seq=tps),
        grid=(n // tm,),
        in_specs=_halo_specs(n, d, tm) + [_mod_spec(d, tps, None), _full((1, d)), _full((d, 3 * d)), _full((3, 3 * d))],
        out_specs=pl.BlockSpec((tm, 3 * d), lambda i: (i, 0)),
        out_shape=jax.ShapeDtypeStruct((n, 3 * d), F32),
        compiler_params=_params("parallel"),
        name="hyena_in",
    )(xs, xs, xs, mods, g1, w_in, w_conv)


def _dft_half_shift(seq):
    order = np.concatenate([np.arange(seq // 2), seq - 1 - np.arange(seq // 2)])
    half = np.pi * (order + 0.5) / (2 * seq)
    return np.stack([np.cos(half), np.sin(half)], axis=1).astype(np.float32)


def _dft_half_tables(seq):
    h = seq // 2
    th = 2.0 * np.pi * (np.arange(h, dtype=np.float64) + 0.5) / seq
    s = np.arange(h, dtype=np.float64)
    even, odd = np.outer(th, s + 0.25), np.outer(th, s + 0.75)
    fwd = np.stack([np.cos(even), np.sin(even), np.cos(odd), np.sin(odd)]).astype(np.float32)
    inv = np.ascontiguousarray(np.transpose(fwd, (0, 2, 1)))
    return fwd, inv


def _hyena_features(seq):
    t = np.linspace(0.0, 1.0, seq, dtype=np.float32)[:, None]
    w = (2.0 * math.pi * np.arange(seq, dtype=np.float32)[:, None] / seq).astype(np.float32)
    f = np.linspace(1e-4, HY_BANDS - 1, HY_BANDS, dtype=np.float32)[None, :]
    z = np.concatenate([t, np.cos(f * w), -np.sin(f * w)], axis=-1).astype(np.float32)
    zp = np.zeros((seq, V7X_LANES), np.float32)
    zp[:, :HY_EMB_DIM] = z
    return zp


def _hyena_deltas(d):
    max_decay = math.log(HY_DECAY_TARGET) / HY_FAST_DECAY_PCT
    min_decay = math.log(HY_DECAY_TARGET) / HY_SLOW_DECAY_PCT
    return np.abs(np.linspace(min_decay, max_decay, d, dtype=np.float32))[None, :].astype(np.float32)


def _hyena_filter_kernel(z_ref, f1w_ref, f1b_ref, f1f_ref, f2w_ref, f2b_ref, f2f_ref,
                         f3a_ref, f3b_ref, f3c_ref, f3d_ref, delta_ref, fwd_ref, rot_ref, k_ref, hid_scr, *pm_scrs):
    seq = z_ref.shape[0]
    half = seq // 2
    tc = delta_ref.shape[1]
    even = pl.ds(0, half, stride=2)
    odd = pl.ds(1, half, stride=2)

    @pl.when(pl.program_id(0) == 0)
    def _():
        h1 = jnp.sin(f1f_ref[...] * (_dot3(z_ref[...], f1w_ref[...]) + f1b_ref[...]))
        hid_scr[...] = jnp.sin(f2f_ref[...] * (_dot3(h1, f2w_ref[...]) + f2b_ref[...]))

    hid = hid_scr[...]
    decay = jnp.exp(-z_ref[:, 0:1] * delta_ref[...])
    row = lax.broadcasted_iota(jnp.int32, (seq, 1), 0)
    cr = rot_ref[:, 0:1]
    sr = rot_ref[:, 1:2]
    for order, (f3_fwd_ref, f3_bwd_ref) in enumerate(((f3a_ref, f3b_ref), (f3c_ref, f3d_ref))):
        h_fwd = _dot3(hid, f3_fwd_ref[...]) * decay
        h_bwd = jnp.where(row == 0, 0.0, _dot3(hid, f3_bwd_ref[...]) * decay)
        pm = jnp.concatenate([h_fwd + h_bwd, h_bwd - h_fwd], axis=1)
        for k, scr in enumerate(pm_scrs):
            scr[...] = pm[:, k * V7X_LANES:(k + 1) * V7X_LANES]
        x_even = jnp.concatenate([scr[even, :] for scr in pm_scrs], axis=1).astype(BF)
        x_odd = jnp.concatenate([scr[odd, :] for scr in pm_scrs], axis=1).astype(BF)
        a = _dot(fwd_ref[0], x_even)
        b = _dot(fwd_ref[1], x_even)
        c = _dot(fwd_ref[2], x_odd)
        s = _dot(fwd_ref[3], x_odd)
        cos_sum = jnp.concatenate([a + c, b - s], axis=0)
        sin_sum = jnp.concatenate([b + s, a - c], axis=0)
        k_ref[2 * order] = cr * cos_sum[:, :tc] + sr * sin_sum[:, :tc]
        k_ref[2 * order + 1] = cr * sin_sum[:, tc:] - sr * cos_sum[:, tc:]


def _hyena_filter(z, f1w, f1b, f1f, f2w, f2b, f2f, f3w, deltas, fwd, rot, d):
    seq = z.shape[0]
    half = seq // 2
    w = f2w.shape[0]
    tc = _tile(d, CHAN_TILE)
    nc = d // tc
    f3spec = [pl.BlockSpec((w, tc), lambda j, o=o: (0, o * nc + j)) for o in range(4)]
    return pl.pallas_call(
        _hyena_filter_kernel,
        grid=(nc,),
        in_specs=[_full((seq, V7X_LANES)), _full((V7X_LANES, w)), _full((1, w)), _full((1, w)),
                  _full((w, w)), _full((1, w)), _full((1, w))] + f3spec + [
            pl.BlockSpec((1, tc), lambda j: (0, j)),
            pl.BlockSpec((4, half, half), lambda j: (0, 0, 0), pipeline_mode=pl.Buffered(1)),
            _full((seq, 2)),
        ],
        out_specs=pl.BlockSpec((4, seq, tc), lambda j: (0, 0, j)),
        out_shape=jax.ShapeDtypeStruct((4, seq, d), F32),
        scratch_shapes=[pltpu.VMEM((seq, w), F32)] + [pltpu.VMEM((seq, V7X_LANES), F32)] * (2 * tc // V7X_LANES),
        compiler_params=_params("arbitrary"),
        name="hyena_filter",
    )(z, f1w, f1b, f1f, f2w, f2b, f2f, f3w, f3w, f3w, f3w, deltas, fwd, rot)


def _hyena_conv_kernel(*refs, gt, pieces):
    x1_refs, x2_refs, v_refs = refs[:pieces], refs[pieces:2 * pieces], refs[2 * pieces:3 * pieces]
    fwd_ref, inv_ref, k_ref, skip_ref, o_ref = refs[3 * pieces:3 * pieces + 5]
    ue_scr, uo_scr, ze_scr, zo_scr, ye_scr, yo_scr = refs[3 * pieces + 5:3 * pieces + 11]
    out_scrs = refs[3 * pieces + 11:]
    seq = o_ref.shape[0]
    half = seq // 2
    even = pl.ds(0, half, stride=2)
    odd = pl.ds(1, half, stride=2)

    def samples(piece_refs, rows):
        return jnp.concatenate([r[rows, :] for r in piece_refs], axis=1)

    def longconv(order):
        ye_scr[...] = jnp.zeros_like(ye_scr)
        yo_scr[...] = jnp.zeros_like(yo_scr)

        def body(c, carry):
            g0 = pl.multiple_of(c * gt, gt)
            rows = pl.ds(g0, gt)
            ue = ue_scr[...]
            uo = uo_scr[...]
            a = _dot(fwd_ref[0, rows, :], ue)
            b = _dot(fwd_ref[1, rows, :], ue)
            cc = _dot(fwd_ref[2, rows, :], uo)
            d = _dot(fwd_ref[3, rows, :], uo)

            def times_filter(first, ur, ui):
                kr = k_ref[2 * order, pl.ds(first + g0, gt), :]
                ki = k_ref[2 * order + 1, pl.ds(first + g0, gt), :]
                return kr * ur + ki * ui, kr * ui - ki * ur

            yra, yia = times_filter(0, a + cc, b + d)
            yrb, yib = times_filter(half, b - d, a - cc)
            ye_scr[...] += (_dot(inv_ref[0, :, rows], (yra + yib).astype(BF))
                            + _dot(inv_ref[1, :, rows], (yia + yrb).astype(BF)))
            yo_scr[...] += (_dot(inv_ref[2, :, rows], (yra - yib).astype(BF))
                            + _dot(inv_ref[3, :, rows], (yia - yrb).astype(BF)))
            return carry

        lax.fori_loop(0, half // gt, body, 0)

    scale = 1.0 / seq
    ve = samples(v_refs, even)
    vo = samples(v_refs, odd)
    ue_scr[...] = ve.astype(BF)
    uo_scr[...] = vo.astype(BF)
    longconv(0)
    ze = samples(x1_refs, even) * (ye_scr[...] * scale + ve * skip_ref[0:1, :])
    zo = samples(x1_refs, odd) * (yo_scr[...] * scale + vo * skip_ref[0:1, :])
    ze_scr[...] = ze
    zo_scr[...] = zo
    ue_scr[...] = ze.astype(BF)
    uo_scr[...] = zo.astype(BF)
    longconv(1)
    oe = samples(x2_refs, even) * (ye_scr[...] * scale + ze_scr[...] * skip_ref[1:2, :])
    oo = samples(x2_refs, odd) * (yo_scr[...] * scale + zo_scr[...] * skip_ref[1:2, :])
    for k, out_scr in enumerate(out_scrs):
        lanes = slice(k * V7X_LANES, (k + 1) * V7X_LANES)
        out_scr[even, :] = oe[:, lanes]
        out_scr[odd, :] = oo[:, lanes]
        o_ref[:, lanes] = out_scr[...].astype(BF)


def _hyena_conv(u3, fwd, inv, kspec, skip, nb, seq, d):
    tc = _tile(d, CHAN_TILE)
    nc = d // tc
    half = seq // 2
    gt = _tile(half, FREQ_TILE)
    pieces = tc // V7X_LANES
    once = dict(pipeline_mode=pl.Buffered(1))
    piece_specs = [pl.BlockSpec((seq, V7X_LANES), lambda j, b, o=o, k=k: (b, (o * nc + j) * pieces + k))
                   for o in range(3) for k in range(pieces)]
    return pl.pallas_call(
        functools.partial(_hyena_conv_kernel, gt=gt, pieces=pieces),
        scratch_shapes=[pltpu.VMEM((half, tc), BF), pltpu.VMEM((half, tc), BF)]
        + [pltpu.VMEM((half, tc), F32)] * 4 + [pltpu.VMEM((seq, V7X_LANES), F32)] * pieces,
        grid=(nc, nb),
        in_specs=piece_specs + [
            pl.BlockSpec((4, half, half), lambda j, b: (0, 0, 0), **once),
            pl.BlockSpec((4, half, half), lambda j, b: (0, 0, 0), **once),
            pl.BlockSpec((4, seq, tc), lambda j, b: (0, 0, j), **once),
            pl.BlockSpec((2, tc), lambda j, b: (0, j)),
        ],
        out_specs=pl.BlockSpec((seq, tc), lambda j, b: (b, j)),
        out_shape=jax.ShapeDtypeStruct((nb * seq, d), BF),
        compiler_params=_params("parallel", "parallel"),
        name="hyena_conv",
    )(*([u3] * (3 * pieces)), fwd, inv, kspec, skip)


def _moe_block(streams, mods, layer, fg, w_gate, w_up, w_down, nb, final=False):
    n_experts = streams[0][2].shape[0]
    group = _sc_group(nb)
    routed = []
    for xs, hn, aff, t, ctx_row in streams:
        cap = max(1, EC_CAPACITY * t // n_experts)
        posm, gate = _route(aff, nb, t, cap)
        by_token = _route_by_token(posm, gate, nb)
        xin = None if hn.dtype == jnp.int32 else _gather(posm, hn, nb, t, cap)
        routed.append((posm, by_token, cap, xin))
    ys = [[] for _ in streams]
    for e0, count in ((0, group), (group, n_experts - group)):
        xins = []
        for (xs, hn, aff, t, ctx_row), (posm, by_token, cap, xin) in zip(streams, routed):
            if xin is None:
                xins.append(([_sc_gather(hn, posm, e, nb, n_experts, t, cap) for e in range(e0, e0 + count, group)], 0))
            else:
                xins.append(([xin], e0))
        for acc, y in zip(ys, _experts(xins, w_gate, w_up, w_down, layer, e0, count, group)):
            acc.append(y)
    return [_combine(by_token, y, xs, mods, fg, nb, t, cap, ctx_row=ctx_row, final=final and ctx_row is None)
            for (xs, _, _, t, ctx_row), (_, by_token, cap, _), y in zip(streams, routed, ys)]


def kernel(x, c, ctx, c_ctx, w_ada, b_ada, norm1_g, norm2_g, final_g, a_w_in, a_conv, a_w_out, b_w_qkv, b_lq1, b_lk1, b_lq2, b_lk2, b_subln_g, b_w_out, c_w_in, c_conv, c_f1_w, c_f1_b, c_f1_freq, c_f2_w, c_f2_b, c_f2_freq, c_f3_w, c_skip, c_w_out, moe_router, moe_w_gate, moe_w_up, moe_w_down):
    nb, seq, d = x.shape
    ctx_len = ctx.shape[1]
    depth = w_ada.shape[0]
    assert nb < MOD_ROWS and d % (2 * DA_HEAD_DIM) == 0
    ctx_row = nb

    cvec = jnp.concatenate([c, c_ctx[None, :], jnp.zeros((MOD_ROWS - nb - 1, d), F32)], axis=0)
    mods_all = _ada(cvec, w_ada, b_ada).reshape(depth, MOD_ROWS, 6, d)

    attn_layers = [i for i in range(depth) if i % N_MIXERS == 1]
    last_ctx_read = max(attn_layers) if attn_layers else -1

    xs = x.reshape(nb * seq, d)
    cs_tok = ctx.reshape(nb * ctx_len, d)
    fg = final_g[None, :]

    for i in range(depth):
        kind, j = i % N_MIXERS, i // N_MIXERS
        update_ctx = i < last_ctx_read
        final = i == depth - 1
        mods = mods_all[i]
        g1 = norm1_g[i][None, :]
        g2 = norm2_g[i][None, :]
        wr = moe_router[i].T
        wrh, wrl = _split(wr)
        moe_w = (moe_w_gate, moe_w_up, moe_w_down)

        ctx_stream = []
        if kind == 0:
            w_in, w_out = a_w_in[j].astype(BF), a_w_out[j].astype(BF)
            if update_ctx:
                cn, chn, caff = _shortconv_layer(cs_tok, mods, g1, w_in, a_conv[j], w_out, g2, wrh, wrl,
                                                 ctx_len, ctx_row=ctx_row)
                ctx_stream = [(cn, chn, caff, ctx_len, ctx_row)]
            xn, hn, aff = _shortconv_layer(xs, mods, g1, w_in, a_conv[j], w_out, g2, wrh, wrl, seq)
        elif kind == 1:
            assert not update_ctx
            lam_init = 0.8 - 0.6 * math.exp(-0.3 * i)
            w_qkv = b_w_qkv[j].astype(BF)
            tables = [jnp.asarray(t) for t in _rope_tables(seq)]
            q, k, v = _qkv(xs, mods, g1, w_qkv, tables, seq, 3, True)
            ctab = [t[:ctx_len] for t in tables]
            kc, vc = _qkv(cs_tok, mods, g1, w_qkv[:, d:], ctab, ctx_len, 2, False, ctx_row=ctx_row)
            o = _attention(q, k, v, kc, vc, b_lq1[j][None, :], b_lk1[j][None, :], b_lq2[j][None, :],
                           b_lk2[j][None, :], b_subln_g[j][None, :], nb, seq, ctx_len, lam_init)
            xn, hn, aff = _outproj_layer(o, xs, mods, b_w_out[j].astype(BF), g2, wrh, wrl, seq)
        else:
            assert not update_ctx
            fwd_np, inv_np = _dft_half_tables(seq)
            fwd, inv = jnp.asarray(fwd_np).astype(BF), jnp.asarray(inv_np).astype(BF)
            w = c_f2_w.shape[-1]
            f1w = jnp.zeros((V7X_LANES, w), F32).at[:HY_EMB_DIM].set(c_f1_w[j])
            kspec = _hyena_filter(jnp.asarray(_hyena_features(seq)), f1w, c_f1_b[j][None, :], c_f1_freq[j][None, :],
                                  c_f2_w[j], c_f2_b[j][None, :], c_f2_freq[j][None, :], c_f3_w[j],
                                  jnp.asarray(_hyena_deltas(d)), fwd, jnp.asarray(_dft_half_shift(seq)), d)
            u3 = _hyena_in(xs, mods, g1, c_w_in[j].astype(BF), c_conv[j], seq)
            z = _hyena_conv(u3, fwd, inv, kspec, c_skip[j], nb, seq, d)
            xn, hn, aff = _outproj_layer(z, xs, mods, c_w_out[j].astype(BF), g2, wrh, wrl, seq)

        outs = _moe_block([(xn, hn, aff, seq, None)] + ctx_stream, mods, i, fg, *moe_w, nb, final=final)
        xs = outs[0]
        if ctx_stream:
            cs_tok = outs[1]

    return xs.reshape(nb, seq, d)
```

```python
import functools
import math

import jax
import jax.numpy as jnp
import numpy as np
from jax import lax
from jax.experimental import pallas as pl
from jax.experimental.pallas import tpu as pltpu

BF = jnp.bfloat16
F32 = jnp.float32

GRID_W = 64
DA_HEAD_DIM = 64
ROPE_AXIS_DIM = DA_HEAD_DIM // 2
ROPE_THETA = 10000.0
SUBLN_EPS = 1e-5
NORM_EPS = 1e-6
N_MIXERS = 3
EC_CAPACITY = 2
HY_EMB_DIM = 33
HY_BANDS = (HY_EMB_DIM - 1) // 2
HY_FAST_DECAY_PCT = 0.3
HY_SLOW_DECAY_PCT = 1.5
HY_DECAY_TARGET = 1e-2

V7X_LANES = 128
V7X_BF16_SUBLANES = 16
V7X_VMEM_BYTES = 64 * 2**20
VMEM_COMPILER_RESERVE = 8 * 2**20
VMEM_LIMIT = V7X_VMEM_BYTES - VMEM_COMPILER_RESERVE
F32_MAGNITUDE_BITS = 31
BF16_BITS = 16
HIGH_HALF = 0xFFFF0000
HALO = V7X_BF16_SUBLANES
MOD_ROWS = 16
ROW_TILE = 512
EXPERT_ROW_STEPS = 2
PROJ_ROW_TILE = 1024
ATTN_Q_TILE = 256
CHAN_TILE = 256
ADA_COL_TILE = 1536
FREQ_TILE = 1024
SC_GATHER_ROWS = 64


def _params(*sem):
    return pltpu.CompilerParams(dimension_semantics=sem, vmem_limit_bytes=VMEM_LIMIT)


def _dot(a, b):
    return jnp.dot(a, b, preferred_element_type=F32)


def _dot_nt(a, b):
    return lax.dot_general(a, b, (((1,), (1,)), ((), ())), preferred_element_type=F32)


def _split(a):
    hi = a.astype(BF)
    lo = (a - hi.astype(F32)).astype(BF)
    return hi, lo


def _dot3(a, b):
    ah, al = _split(a)
    bh, bl = _split(b)
    return _dot(ah, bh) + (_dot(ah, bl) + _dot(al, bh))


def _sigmoid(a):
    return 1.0 / (1.0 + jnp.exp(-a))


def _norm_mod(x, g, shift, scale):
    ms = jnp.mean(x * x, axis=-1, keepdims=True)
    return (x * lax.rsqrt(ms + NORM_EPS) * g) * (1.0 + scale) + shift


def _tile(n, pref):
    t = min(n, pref)
    assert n % t == 0, (n, pref)
    return t


def _full(shape):
    nd = len(shape)
    return pl.BlockSpec(shape, lambda *_: (0,) * nd)


def _ada_kernel(c_ref, w_ref, b_ref, o_ref):
    c = c_ref[...]
    o_ref[0] = _dot3(c * _sigmoid(c), w_ref[0]) + b_ref[0]


def _ada(cvec, w_ada, b_ada):
    depth, d, n6 = w_ada.shape
    tn = _tile(n6, ADA_COL_TILE)
    return pl.pallas_call(
        _ada_kernel,
        grid=(depth, n6 // tn),
        in_specs=[
            _full((MOD_ROWS, d)),
            pl.BlockSpec((1, d, tn), lambda i, j: (i, 0, j)),
            pl.BlockSpec((1, 1, tn), lambda i, j: (i, 0, j)),
        ],
        out_specs=pl.BlockSpec((1, MOD_ROWS, tn), lambda i, j: (i, 0, j)),
        out_shape=jax.ShapeDtypeStruct((depth, MOD_ROWS, n6), F32),
        compiler_params=_params("parallel", "parallel"),
        name="ada",
    )(cvec, w_ada, b_ada.reshape(depth, 1, n6))


def _pack_pairs(hh):
    half = hh.shape[1] // 2
    lo = lax.shift_right_logical(pltpu.bitcast(hh[:, :half].astype(F32), jnp.uint32), jnp.uint32(BF16_BITS))
    hi = pltpu.bitcast(hh[:, half:].astype(F32), jnp.uint32) & jnp.uint32(HIGH_HALF)
    return pltpu.bitcast(lo | hi, jnp.int32)


def _unpack_pairs(words):
    u = pltpu.bitcast(words, jnp.uint32)
    lo = pltpu.bitcast(lax.shift_left(u, jnp.uint32(BF16_BITS)), F32).astype(BF)
    hi = pltpu.bitcast(u & jnp.uint32(HIGH_HALF), F32).astype(BF)
    return jnp.concatenate([lo, hi], axis=1)


def _residual_router(x, y, mod, g2_ref, wrh_ref, wrl_ref, xo_ref, hn_ref, aff_ref):
    xn = x + mod[2:3] * y
    xo_ref[...] = xn
    hn = _norm_mod(xn, g2_ref[...], mod[3:4], mod[4:5])
    hh, hl = _split(hn)
    hn_ref[...] = _pack_pairs(hh) if hn_ref.dtype == jnp.int32 else hh
    e = wrh_ref.shape[0]
    both = _dot_nt(jnp.concatenate([wrh_ref[...], wrl_ref[...]], axis=0), hh)
    logits = both[:e] + (_dot_nt(wrh_ref[...], hl) + both[e:])
    p = jnp.exp(logits - jnp.max(logits, axis=0, keepdims=True))
    aff_ref[...] = p / jnp.sum(p, axis=0, keepdims=True)


def _router_specs(n, d, e, tm, packed):
    in_specs = [_full((1, d)), _full((e, d)), _full((e, d))]
    hn_cols, hn_dtype = (d // 2, jnp.int32) if packed else (d, BF)
    out_specs = [
        pl.BlockSpec((tm, d), lambda i: (i, 0)),
        pl.BlockSpec((tm, hn_cols), lambda i: (i, 0)),
        pl.BlockSpec((e, tm), lambda i: (0, i)),
    ]
    out_shape = [
        jax.ShapeDtypeStruct((n, d), F32),
        jax.ShapeDtypeStruct((n, hn_cols), hn_dtype),
        jax.ShapeDtypeStruct((e, n), F32),
    ]
    return in_specs, out_specs, out_shape


def _halo_specs(n, d, tm):
    per = tm // HALO
    last = n // HALO - 1
    return [
        pl.BlockSpec((HALO, d), lambda i: (jnp.maximum(i * per - 1, 0), 0)),
        pl.BlockSpec((tm, d), lambda i: (i, 0)),
        pl.BlockSpec((HALO, d), lambda i: (jnp.minimum((i + 1) * per, last), 0)),
    ]


def _mod_spec(d, tiles_per_seq, ctx_row):
    if ctx_row is None:
        return pl.BlockSpec((1, 6, d), lambda i: (i // tiles_per_seq, 0, 0))
    return pl.BlockSpec((1, 6, d), lambda i: (ctx_row, 0, 0))


def _halo_rows(xp_ref, x_ref, xn_ref, g_ref, mod, tiles_per_seq):
    t = pl.program_id(0) % tiles_per_seq
    g = g_ref[...]
    hp, hx, hn = [_norm_mod(r[...], g, mod[0:1], mod[1:2]) for r in (xp_ref, x_ref, xn_ref)]
    hp = jnp.where(t == 0, 0.0, hp)
    hn = jnp.where(t == tiles_per_seq - 1, 0.0, hn)
    return jnp.concatenate([hp, hx, hn], axis=0).astype(BF)


def _conv3(s, w_ref, tm, seq_rows=None):
    n = s.shape[0]
    prev = pltpu.roll(s, 1, 0)[HALO:HALO + tm]
    nxt = pltpu.roll(s, n - 1, 0)[HALO:HALO + tm]
    if seq_rows is not None and seq_rows < tm:
        within = lax.broadcasted_iota(jnp.int32, (tm, 1), 0) % seq_rows
        prev = jnp.where(within == 0, 0.0, prev)
        nxt = jnp.where(within == seq_rows - 1, 0.0, nxt)
    return prev * w_ref[0:1, :] + s[HALO:HALO + tm] * w_ref[1:2, :] + nxt * w_ref[2:3, :]


def _shortconv_kernel(xp_ref, x_ref, xn_ref, mod_ref, g1_ref, win_ref, wconv_ref, wout_ref,
                      g2_ref, wrh_ref, wrl_ref, xo_ref, hn_ref, aff_ref, *, tiles_per_seq, seq_rows):
    tm, d = x_ref.shape
    mod = mod_ref[0]
    h = _halo_rows(xp_ref, x_ref, xn_ref, g1_ref, mod, tiles_per_seq)
    gate = _dot(h[HALO:HALO + tm], win_ref[:, 0:d])
    s = _dot(h, win_ref[:, d:2 * d]) * _dot(h, win_ref[:, 2 * d:3 * d])
    z = (gate * _conv3(s, wconv_ref, tm, seq_rows)).astype(BF)
    y = _dot(z, wout_ref[...])
    _residual_router(x_ref[...], y, mod, g2_ref, wrh_ref, wrl_ref, xo_ref, hn_ref, aff_ref)


def _shortconv_layer(xs, mods, g1, w_in, w_conv, w_out, g2, wrh, wrl, seq, ctx_row=None):
    n, d = xs.shape
    e = wrh.shape[0]
    if ctx_row is not None and seq < PROJ_ROW_TILE:
        tm = _tile(n, PROJ_ROW_TILE // seq * seq)
        tps = 1
    else:
        tm = _tile(seq, PROJ_ROW_TILE)
        tps = seq // tm
    r_in, r_out, r_shape = _router_specs(n, d, e, tm, packed=ctx_row is None)
    return pl.pallas_call(
        functools.partial(_shortconv_kernel, tiles_per_seq=tps, seq_rows=seq),
        grid=(n // tm,),
        in_specs=_halo_specs(n, d, tm) + [
            _mod_spec(d, tps, ctx_row), _full((1, d)), _full((d, 3 * d)), _full((3, d)), _full((d, d)),
        ] + r_in,
        out_specs=r_out,
        out_shape=r_shape,
        compiler_params=_params("parallel"),
        name="shortconv_layer",
    )(xs, xs, xs, mods, g1, w_in, w_conv, w_out, g2, wrh, wrl)


def _outproj_kernel(o_ref, x_ref, mod_ref, wout_ref, g2_ref, wrh_ref, wrl_ref, xo_ref, hn_ref, aff_ref):
    y = _dot(o_ref[...], wout_ref[...])
    _residual_router(x_ref[...], y, mod_ref[0], g2_ref, wrh_ref, wrl_ref, xo_ref, hn_ref, aff_ref)


def _outproj_layer(o, xs, mods, w_out, g2, wrh, wrl, seq):
    n, d = xs.shape
    e = wrh.shape[0]
    tm = _tile(seq, PROJ_ROW_TILE)
    tps = seq // tm
    r_in, r_out, r_shape = _router_specs(n, d, e, tm, packed=True)
    return pl.pallas_call(
        _outproj_kernel,
        grid=(n // tm,),
        in_specs=[pl.BlockSpec((tm, d), lambda i: (i, 0)), pl.BlockSpec((tm, d), lambda i: (i, 0)),
                  _mod_spec(d, tps, None), _full((d, d))] + r_in,
        out_specs=r_out,
        out_shape=r_shape,
        compiler_params=_params("parallel"),
        name="outproj_layer",
    )(o, xs, mods, w_out, g2, wrh, wrl)


def _excl_cumsum_lanes(m):
    rows, t = m.shape
    a = lax.broadcasted_iota(jnp.int32, (V7X_LANES, V7X_LANES), 0)
    b = lax.broadcasted_iota(jnp.int32, (V7X_LANES, V7X_LANES), 1)
    tri = jnp.where(a < b, 1.0, 0.0).astype(BF)
    carry = jnp.zeros((rows, 1), F32)
    out = []
    for c in range(t // V7X_LANES):
        blk = m[:, c * V7X_LANES:(c + 1) * V7X_LANES]
        out.append(_dot(blk.astype(BF), tri) + carry)
        carry = carry + jnp.sum(blk, axis=1, keepdims=True)
    return jnp.concatenate(out, axis=1)


def _route_kernel(aff_ref, posm_ref, gate_ref, *, cap, nb):
    e = aff_ref.shape[0]
    t = aff_ref.shape[1] // nb
    aff = jnp.concatenate([aff_ref[:, b * t:(b + 1) * t] for b in range(nb)], axis=0)
    bits = pltpu.bitcast(aff, jnp.int32)

    def step(i, thr):
        cand = thr | jnp.left_shift(jnp.int32(1), F32_MAGNITUDE_BITS - 1 - i)
        cnt = jnp.sum(jnp.where(bits >= cand, 1.0, 0.0), axis=1, keepdims=True)
        return jnp.where(cnt >= cap, cand, thr)

    thr = lax.fori_loop(0, F32_MAGNITUDE_BITS, step, jnp.zeros((nb * e, 1), jnp.int32))
    gt = jnp.where(bits > thr, 1.0, 0.0)
    eq = jnp.where(bits == thr, 1.0, 0.0)
    need = cap - jnp.sum(gt, axis=1, keepdims=True)
    sel = gt + eq * jnp.where(_excl_cumsum_lanes(eq) < need, 1.0, 0.0)
    pos = jnp.where(sel > 0.0, _excl_cumsum_lanes(sel), -1.0)
    posm_ref[...] = pos.astype(jnp.int32)
    gate_ref[...] = sel * aff


def _route_by_token_kernel(posm_ref, gate_ref, bt_ref, *, nb):
    e = posm_ref.shape[0] // nb
    t = posm_ref.shape[1]
    pad = jnp.zeros((V7X_LANES - 2 * e, t), F32)
    for b in range(nb):
        rows = slice(b * e, (b + 1) * e)
        bt_ref[b * t:(b + 1) * t, :] = jnp.concatenate(
            [posm_ref[rows, :].astype(F32), gate_ref[rows, :], pad], axis=0).T


def _route(aff, nb, t, cap):
    e = aff.shape[0]
    return pl.pallas_call(
        functools.partial(_route_kernel, cap=cap, nb=nb),
        grid=(1,),
        in_specs=[_full((e, nb * t))],
        out_specs=[_full((nb * e, t)), _full((nb * e, t))],
        out_shape=[jax.ShapeDtypeStruct((nb * e, t), jnp.int32), jax.ShapeDtypeStruct((nb * e, t), F32)],
        compiler_params=_params("arbitrary"),
        name="route",
    )(aff)


def _route_by_token(posm, gate, nb):
    rows, t = posm.shape
    assert 2 * rows // nb <= V7X_LANES
    return pl.pallas_call(
        functools.partial(_route_by_token_kernel, nb=nb),
        grid=(1,),
        in_specs=[_full((rows, t)), _full((rows, t))],
        out_specs=_full((nb * t, V7X_LANES)),
        out_shape=jax.ShapeDtypeStruct((nb * t, V7X_LANES), F32),
        compiler_params=_params("arbitrary"),
        name="route_by_token",
    )(posm, gate)


def _sc_gather(table, posm, e0, nb, n_experts, t, cap):
    from jax.experimental.pallas import tpu_sc as plsc

    info = plsc.get_sparse_core_info()
    cores, lanes = info.num_cores, info.num_lanes
    workers = cores * info.num_subcores
    group = workers // nb
    words = table.shape[1]
    assert workers % nb == 0 and n_experts % group == 0 and cap % SC_GATHER_ROWS == 0 and t % lanes == 0

    @functools.partial(
        pl.kernel,
        mesh=plsc.VectorSubcoreMesh(core_axis_name="core", subcore_axis_name="subcore"),
        compiler_params=pltpu.CompilerParams(needs_layout_passes=False),
        out_type=jax.ShapeDtypeStruct((group * nb * cap, words), jnp.int32),
        scratch_types=[pltpu.VMEM((t,), jnp.int32), pltpu.VMEM((cap,), jnp.int32),
                       pltpu.VMEM((SC_GATHER_ROWS, words), jnp.int32), pltpu.SemaphoreType.DMA],
    )
    def gather(table_hbm, posm_hbm, out_hbm, pos_v, idx_v, rows_v, sem):
        w = lax.axis_index("subcore") * cores + lax.axis_index("core")
        e_local = w // nb
        b = w % nb
        pltpu.sync_copy(posm_hbm.at[b * n_experts + e0 + e_local], pos_v)

        @pl.loop(0, t // lanes)
        def _(i):
            p = pos_v[pl.ds(i * lanes, lanes)]
            token = lax.iota(jnp.int32, lanes) + (i * lanes + b * t)
            plsc.store_scatter(idx_v, [p], token, mask=p >= 0)

        out_base = (e_local * nb + b) * cap

        @pl.loop(0, cap // SC_GATHER_ROWS)
        def _(j):
            pltpu.async_copy(table_hbm.at[idx_v.at[pl.ds(j * SC_GATHER_ROWS, SC_GATHER_ROWS)]], rows_v, sem).wait()
            pltpu.sync_copy(rows_v, out_hbm.at[pl.ds(out_base + j * SC_GATHER_ROWS, SC_GATHER_ROWS)])

    return gather(table, posm).reshape(group, nb * cap, words)


def _sc_group(nb):
    from jax.experimental.pallas import tpu_sc as plsc

    info = plsc.get_sparse_core_info()
    return info.num_cores * info.num_subcores // nb


def _gather_kernel(posm_ref, hn_ref, o_ref, *, cap):
    e = posm_ref.shape[0]
    t, d = hn_ref.shape
    slot = lax.broadcasted_iota(jnp.int32, (cap, t), 0)
    onehot = jnp.concatenate(
        [jnp.where(posm_ref[k:k + 1, :] == slot, 1.0, 0.0).astype(BF) for k in range(e)], axis=0)
    o_ref[...] = _dot(onehot, hn_ref[...]).astype(BF).reshape(e, cap, d)


def _gather(posm, hn, nb, t, cap):
    e = posm.shape[0] // nb
    d = hn.shape[1]
    return pl.pallas_call(
        functools.partial(_gather_kernel, cap=cap),
        grid=(nb,),
        in_specs=[pl.BlockSpec((e, t), lambda b: (b, 0)), pl.BlockSpec((t, d), lambda b: (b, 0))],
        out_specs=pl.BlockSpec((e, cap, d), lambda b: (0, b, 0)),
        out_shape=jax.ShapeDtypeStruct((e, nb * cap, d), BF),
        compiler_params=_params("parallel"),
        name="moe_gather",
    )(posm, hn)


def _expert_kernel(*refs, pieces, per):
    n_in = sum(pieces)
    wg_ref, wu_ref, wd_ref = refs[n_in:n_in + 3]
    y_refs = refs[n_in + 3:n_in + 3 + len(pieces)]
    w_scr = refs[-1]

    @pl.when(pl.program_id(1) == 0)
    def _():
        w_scr[0] = wg_ref[0, 0].astype(BF)
        w_scr[1] = wu_ref[0, 0].astype(BF)
        w_scr[2] = wd_ref[0, 0].astype(BF)

    which = pl.program_id(0) // per

    start = 0
    for n_pieces, y_ref in zip(pieces, y_refs):
        x_refs = refs[start:start + n_pieces]
        start += n_pieces
        rows = x_refs[0].shape[1]
        tr = _tile(rows, ROW_TILE)

        def body(j, carry, x_refs=x_refs, y_ref=y_ref, tr=tr):
            r0 = pl.multiple_of(j * tr, tr)
            xs = x_refs[0][0, pl.ds(r0, tr), :]
            for p in range(1, len(x_refs)):
                xs = jnp.where(which == p, x_refs[p][0, pl.ds(r0, tr), :], xs)
            if xs.dtype == jnp.int32:
                xs = _unpack_pairs(xs)
            a = _dot(xs, w_scr[0])
            b = _dot(xs, w_scr[1])
            hm = (a * _sigmoid(a) * b).astype(BF)
            y_ref[0, pl.ds(r0, tr), :] = _dot(hm, w_scr[2]).astype(BF)
            return carry

        lax.fori_loop(0, rows // tr, body, 0)


def _experts(streams, w_gate, w_up, w_down, layer, e0, count, per):
    d = w_gate.shape[-2]
    f = w_gate.shape[-1]
    assert f == d
    wspec = pl.BlockSpec((1, 1, d, f), lambda k, j: (layer, e0 + k, 0, 0))
    xspecs, xargs = [], []
    for arrays, first in streams:
        for p, x in enumerate(arrays):
            if len(arrays) == 1:
                index = lambda k, j, first=first: (first + k, j, 0)
            else:
                index = lambda k, j, p=p: (
                    jnp.clip(k - p * per, 0, per - 1),
                    jnp.where(k < p * per, 0, jnp.where(k >= (p + 1) * per, EXPERT_ROW_STEPS - 1, j)), 0)
            xspecs.append(pl.BlockSpec((1, x.shape[1] // EXPERT_ROW_STEPS, x.shape[2]), index))
            xargs.append(x)
    rows = [arrays[0].shape[1] for arrays, _ in streams]
    return pl.pallas_call(
        functools.partial(_expert_kernel, pieces=tuple(len(arrays) for arrays, _ in streams), per=per),
        grid=(count, EXPERT_ROW_STEPS),
        in_specs=xspecs + [wspec, wspec, wspec],
        out_specs=[pl.BlockSpec((1, r // EXPERT_ROW_STEPS, d), lambda k, j: (k, j, 0)) for r in rows],
        out_shape=[jax.ShapeDtypeStruct((count, r, d), BF) for r in rows],
        scratch_shapes=[pltpu.VMEM((3, d, f), BF)],
        compiler_params=_params("parallel", "arbitrary"),
        name="moe_experts",
    )(*xargs, w_gate, w_up, w_down)


def _combine_kernel(bt_ref, *refs, cap, final):
    y_refs = refs[:-4]
    x_ref, mod_ref, fg_ref, o_ref = refs[-4:]
    d = y_refs[0].shape[2]
    e = sum(r.shape[0] for r in y_refs)
    tq = x_ref.shape[0]
    pt = bt_ref[:, 0:e]
    gt = bt_ref[:, e:2 * e]
    if cap % V7X_LANES == 0:
        slot = lax.broadcasted_iota(jnp.int32, (tq, cap), 1).astype(F32)
        pieces = [jnp.where(pt[:, k:k + 1] == slot, gt[:, k:k + 1], 0.0).astype(BF) for k in range(e)]
        scat = jnp.concatenate(pieces, axis=1)
    else:
        slot = lax.broadcasted_iota(jnp.int32, (tq, e * cap), 1).astype(F32)
        scat = jnp.zeros((tq, e * cap), F32)
        for k in range(e):
            pk = pt[:, k:k + 1]
            scat = jnp.where((pk >= 0.0) & (pk + float(k * cap) == slot), gt[:, k:k + 1], scat)
        scat = scat.astype(BF)
    out, col = None, 0
    for y_ref in y_refs:
        width = y_ref.shape[0] * cap
        part = _dot(scat[:, col:col + width], y_ref[...].reshape(width, d))
        out = part if out is None else out + part
        col += width
    xn = x_ref[...] + mod_ref[0][5:6] * out
    if final:
        ms = jnp.mean(xn * xn, axis=-1, keepdims=True)
        xn = xn * lax.rsqrt(ms + NORM_EPS) * fg_ref[...]
    o_ref[...] = xn


def _combine(by_token, ys, xs, mods, fg, nb, t, cap, ctx_row=None, final=False):
    n, d = xs.shape
    tq = _tile(t, ROW_TILE)
    tpb = t // tq
    if ctx_row is None:
        mspec = pl.BlockSpec((1, 6, d), lambda b, i: (b, 0, 0))
    else:
        mspec = pl.BlockSpec((1, 6, d), lambda b, i: (ctx_row, 0, 0))
    return pl.pallas_call(
        functools.partial(_combine_kernel, cap=cap, final=final),
        grid=(nb, tpb),
        in_specs=[
            pl.BlockSpec((tq, V7X_LANES), lambda b, i: (b * tpb + i, 0)),
        ] + [pl.BlockSpec((y.shape[0], cap, d), lambda b, i: (0, b, 0)) for y in ys] + [
            pl.BlockSpec((tq, d), lambda b, i: (b * tpb + i, 0)),
            mspec,
            _full((1, d)),
        ],
        out_specs=pl.BlockSpec((tq, d), lambda b, i: (b * tpb + i, 0)),
        out_shape=jax.ShapeDtypeStruct((n, d), F32),
        compiler_params=_params("parallel", "parallel"),
        name="moe_combine",
    )(by_token, *ys, xs, mods, fg)


def _rope_tables(seq):
    rows = seq // GRID_W
    row = np.repeat(np.arange(rows, dtype=np.float32), GRID_W)
    col = np.tile(np.arange(GRID_W, dtype=np.float32), rows)
    inv_freq = (ROPE_THETA ** (-np.arange(0, ROPE_AXIS_DIM, 2, dtype=np.float32) / ROPE_AXIS_DIM)).astype(np.float32)
    lane = np.arange(2 * DA_HEAD_DIM)
    within = lane % DA_HEAD_DIM
    axis = within // ROPE_AXIS_DIM
    half = (within % ROPE_AXIS_DIM) // (ROPE_AXIS_DIM // 2)
    idx = within % (ROPE_AXIS_DIM // 2)
    pos = np.where(axis[None, :] == 0, row[:, None], col[:, None])
    ang = (pos * inv_freq[idx][None, :]).astype(np.float32)
    cos = np.cos(ang).astype(np.float32)
    sin = np.sin(ang).astype(np.float32)
    sin_lo = np.where(half[None, :] == 1, sin, 0.0).astype(np.float32)
    sin_hi = np.where(half[None, :] == 0, -sin, 0.0).astype(np.float32)
    return cos, sin_lo, sin_hi


def _qkv_kernel(x_ref, mod_ref, g_ref, w_ref, cos_ref, sa_ref, sb_ref, *o_refs, rope):
    d = x_ref.shape[1]
    mod = mod_ref[0]
    h = _norm_mod(x_ref[...], g_ref[...], mod[0:1], mod[1:2]).astype(BF)
    slab = 2 * DA_HEAD_DIM
    shift = ROPE_AXIS_DIM // 2
    for j, o_ref in enumerate(o_refs):
        u = _dot(h, w_ref[:, j * d:(j + 1) * d])
        if rope and j < 2:
            scale = DA_HEAD_DIM ** -0.5 * math.log2(math.e) if j == 0 else 1.0
            cos, sa, sb = cos_ref[...] * scale, sa_ref[...] * scale, sb_ref[...] * scale
            for hd in range(d // slab):
                xs = u[:, hd * slab:(hd + 1) * slab]
                r = xs * cos + pltpu.roll(xs, shift, 1) * sa + pltpu.roll(xs, slab - shift, 1) * sb
                o_ref[:, hd * slab:(hd + 1) * slab] = r.astype(BF)
        else:
            o_ref[...] = u.astype(BF)


def _qkv(xs, mods, g1, w, tables, seq, nout, rope, ctx_row=None):
    n, d = xs.shape
    tm = _tile(seq, PROJ_ROW_TILE)
    tps = seq // tm
    slab = 2 * DA_HEAD_DIM
    tspec = pl.BlockSpec((tm, slab), lambda i: (i % tps, 0))
    return pl.pallas_call(
        functools.partial(_qkv_kernel, rope=rope),
        grid=(n // tm,),
        in_specs=[pl.BlockSpec((tm, d), lambda i: (i, 0)), _mod_spec(d, tps, ctx_row), _full((1, d)),
                  _full((d, nout * d)), tspec, tspec, tspec],
        out_specs=[pl.BlockSpec((tm, d), lambda i: (i, 0))] * nout,
        out_shape=[jax.ShapeDtypeStruct((n, d), BF)] * nout,
        compiler_params=_params("parallel"),
        name="attn_qkv",
    )(xs, mods, g1, w, *tables)


def _attn_kernel(q_ref, k_ref, v_ref, kc_ref, vc_ref, lq1_ref, lk1_ref, lq2_ref, lk2_ref, sg_ref, o_ref,
                 sl_a, sc_a, sl_b, sc_b, va_scr, vca_scr, *, lam_init, tq):
    seq = q_ref.shape[0]
    lam = (jnp.exp(jnp.sum(lq1_ref[...] * lk1_ref[...], axis=1, keepdims=True))
           - jnp.exp(jnp.sum(lq2_ref[...] * lk2_ref[...], axis=1, keepdims=True)) + lam_init)
    slots = ((sl_a, sc_a), (sl_b, sc_b))

    def scores(i, slot):
        sl_ref, sc_ref = slot
        q = q_ref[i * tq:(i + 1) * tq, :]
        lane = lax.broadcasted_iota(jnp.int32, q.shape, 1)
        zero = jnp.zeros_like(q)
        for mp, qm in enumerate((jnp.where(lane < DA_HEAD_DIM, q, zero), jnp.where(lane >= DA_HEAD_DIM, q, zero))):
            sl_ref[mp] = _dot_nt(qm, k_ref[...])
            sc_ref[mp] = _dot_nt(qm, kc_ref[...])

    slab = v_ref.shape[1]
    va_scr[:, 0:slab] = v_ref[...]
    va_scr[:, slab:2 * slab] = jnp.ones_like(v_ref)
    vca_scr[:, 0:slab] = vc_ref[...]
    vca_scr[:, slab:2 * slab] = jnp.ones_like(vc_ref)

    def unnormalised(sl_ref, sc_ref, mp):
        s_l = sl_ref[mp]
        s_c = sc_ref[mp]
        m = jnp.maximum(jnp.max(s_l, axis=1, keepdims=True), jnp.max(s_c, axis=1, keepdims=True))
        p_l = jnp.exp2(s_l - m).astype(BF)
        p_c = jnp.exp2(s_c - m).astype(BF)
        both = _dot(p_l, va_scr[...]) + _dot(p_c, vca_scr[...])
        return both[:, 0:slab], both[:, slab:slab + 1]

    def attend(i, slot):
        o1, t1 = unnormalised(*slot, 0)
        o2, t2 = unnormalised(*slot, 1)
        o = o1 * (1.0 / t1) - o2 * (lam / t2)
        ms = jnp.mean(o * o, axis=-1, keepdims=True)
        o_ref[i * tq:(i + 1) * tq, :] = (o * lax.rsqrt(ms + SUBLN_EPS) * sg_ref[...] * (1.0 - lam_init)).astype(BF)

    n = seq // tq
    scores(0, slots[0])
    for i in range(n):
        if i + 1 < n:
            scores(i + 1, slots[(i + 1) % 2])
        attend(i, slots[i % 2])


def _attention(q, k, v, kc, vc, lq1, lk1, lq2, lk2, sg, nb, seq, ctx_len, lam_init):
    n, d = q.shape
    slab = 2 * DA_HEAD_DIM
    heads = d // slab
    tq = _tile(seq, ATTN_Q_TILE)
    small = _full((1, DA_HEAD_DIM))
    lat = pl.BlockSpec((seq, slab), lambda b, h: (b, h))
    ctx = pl.BlockSpec((ctx_len, slab), lambda b, h: (b, h))
    score_scratch = [pltpu.VMEM((2, tq, seq), F32), pltpu.VMEM((2, tq, ctx_len), F32)]
    return pl.pallas_call(
        functools.partial(_attn_kernel, lam_init=lam_init, tq=tq),
        grid=(nb, heads),
        in_specs=[lat, lat, lat, ctx, ctx, small, small, small, small, _full((1, slab))],
        out_specs=lat,
        out_shape=jax.ShapeDtypeStruct((n, d), BF),
        scratch_shapes=score_scratch + score_scratch + [pltpu.VMEM((seq, 2 * slab), BF),
                                                        pltpu.VMEM((ctx_len, 2 * slab), BF)],
        compiler_params=_params("parallel", "parallel"),
        name="diff_attention",
    )(q, k, v, kc, vc, lq1, lk1, lq2, lk2, sg)


def _hyena_in_kernel(xp_ref, x_ref, xn_ref, mod_ref, g1_ref, win_ref, wconv_ref, o_ref, *, tiles_per_seq):
    tm, d = x_ref.shape
    h = _halo_rows(xp_ref, x_ref, xn_ref, g1_ref, mod_ref[0], tiles_per_seq)
    for j in range(3):
        u = _dot(h, win_ref[:, j * d:(j + 1) * d])
        o_ref[:, j * d:(j + 1) * d] = _conv3(u, wconv_ref.at[:, j * d:(j + 1) * d], tm)


def _hyena_in(xs, mods, g1, w_in, w_conv, seq):
    n, d = xs.shape
    tm = _tile(seq, PROJ_ROW_TILE)
    tps = seq // tm
    return pl.pallas_call(
        functools.partial(_hyena_in_kernel, tiles_per_seq=tps),
        grid=(n // tm,),
        in_specs=_halo_specs(n, d, tm) + [_mod_spec(d, tps, None), _full((1, d)), _full((d, 3 * d)), _full((3, 3 * d))],
        out_specs=pl.BlockSpec((tm, 3 * d), lambda i: (i, 0)),
        out_shape=jax.ShapeDtypeStruct((n, 3 * d), F32),
        compiler_params=_params("parallel"),
        name="hyena_in",
    )(xs, xs, xs, mods, g1, w_in, w_conv)


def _dft_half_shift(seq):
    order = np.concatenate([np.arange(seq // 2), seq - 1 - np.arange(seq // 2)])
    half = np.pi * (order + 0.5) / (2 * seq)
    return np.stack([np.cos(half), np.sin(half)], axis=1).astype(np.float32)


def _dft_half_tables(seq):
    h = seq // 2
    th = 2.0 * np.pi * (np.arange(h, dtype=np.float64) + 0.5) / seq
    s = np.arange(h, dtype=np.float64)
    even, odd = np.outer(th, s + 0.25), np.outer(th, s + 0.75)
    fwd = np.stack([np.cos(even), np.sin(even), np.cos(odd), np.sin(odd)]).astype(np.float32)
    inv = np.ascontiguousarray(np.transpose(fwd, (0, 2, 1)))
    return fwd, inv


def _hyena_features(seq):
    t = np.linspace(0.0, 1.0, seq, dtype=np.float32)[:, None]
    w = (2.0 * math.pi * np.arange(seq, dtype=np.float32)[:, None] / seq).astype(np.float32)
    f = np.linspace(1e-4, HY_BANDS - 1, HY_BANDS, dtype=np.float32)[None, :]
    z = np.concatenate([t, np.cos(f * w), -np.sin(f * w)], axis=-1).astype(np.float32)
    zp = np.zeros((seq, V7X_LANES), np.float32)
    zp[:, :HY_EMB_DIM] = z
    return zp


def _hyena_deltas(d):
    max_decay = math.log(HY_DECAY_TARGET) / HY_FAST_DECAY_PCT
    min_decay = math.log(HY_DECAY_TARGET) / HY_SLOW_DECAY_PCT
    return np.abs(np.linspace(min_decay, max_decay, d, dtype=np.float32))[None, :].astype(np.float32)


def _hyena_filter_kernel(z_ref, f1w_ref, f1b_ref, f1f_ref, f2w_ref, f2b_ref, f2f_ref,
                         f3a_ref, f3b_ref, f3c_ref, f3d_ref, delta_ref, fwd_ref, rot_ref, k_ref, hid_scr, *pm_scrs):
    seq = z_ref.shape[0]
    half = seq // 2
    tc = delta_ref.shape[1]
    even = pl.ds(0, half, stride=2)
    odd = pl.ds(1, half, stride=2)

    @pl.when(pl.program_id(0) == 0)
    def _():
        h1 = jnp.sin(f1f_ref[...] * (_dot3(z_ref[...], f1w_ref[...]) + f1b_ref[...]))
        hid_scr[...] = jnp.sin(f2f_ref[...] * (_dot3(h1, f2w_ref[...]) + f2b_ref[...]))

    hid = hid_scr[...]
    decay = jnp.exp(-z_ref[:, 0:1] * delta_ref[...])
    row = lax.broadcasted_iota(jnp.int32, (seq, 1), 0)
    cr = rot_ref[:, 0:1]
    sr = rot_ref[:, 1:2]
    for order, (f3_fwd_ref, f3_bwd_ref) in enumerate(((f3a_ref, f3b_ref), (f3c_ref, f3d_ref))):
        h_fwd = _dot3(hid, f3_fwd_ref[...]) * decay
        h_bwd = jnp.where(row == 0, 0.0, _dot3(hid, f3_bwd_ref[...]) * decay)
        pm = jnp.concatenate([h_fwd + h_bwd, h_bwd - h_fwd], axis=1)
        for k, scr in enumerate(pm_scrs):
            scr[...] = pm[:, k * V7X_LANES:(k + 1) * V7X_LANES]
        x_even = jnp.concatenate([scr[even, :] for scr in pm_scrs], axis=1).astype(BF)
        x_odd = jnp.concatenate([scr[odd, :] for scr in pm_scrs], axis=1).astype(BF)
        a = _dot(fwd_ref[0], x_even)
        b = _dot(fwd_ref[1], x_even)
        c = _dot(fwd_ref[2], x_odd)
        s = _dot(fwd_ref[3], x_odd)
        cos_sum = jnp.concatenate([a + c, b - s], axis=0)
        sin_sum = jnp.concatenate([b + s, a - c], axis=0)
        k_ref[2 * order] = cr * cos_sum[:, :tc] + sr * sin_sum[:, :tc]
        k_ref[2 * order + 1] = cr * sin_sum[:, tc:] - sr * cos_sum[:, tc:]


def _hyena_filter(z, f1w, f1b, f1f, f2w, f2b, f2f, f3w, deltas, fwd, rot, d):
    seq = z.shape[0]
    half = seq // 2
    w = f2w.shape[0]
    tc = _tile(d, CHAN_TILE)
    nc = d // tc
    f3spec = [pl.BlockSpec((w, tc), lambda j, o=o: (0, o * nc + j)) for o in range(4)]
    return pl.pallas_call(
        _hyena_filter_kernel,
        grid=(nc,),
        in_specs=[_full((seq, V7X_LANES)), _full((V7X_LANES, w)), _full((1, w)), _full((1, w)),
                  _full((w, w)), _full((1, w)), _full((1, w))] + f3spec + [
            pl.BlockSpec((1, tc), lambda j: (0, j)),
            pl.BlockSpec((4, half, half), lambda j: (0, 0, 0), pipeline_mode=pl.Buffered(1)),
            _full((seq, 2)),
        ],
        out_specs=pl.BlockSpec((4, seq, tc), lambda j: (0, 0, j)),
        out_shape=jax.ShapeDtypeStruct((4, seq, d), F32),
        scratch_shapes=[pltpu.VMEM((seq, w), F32)] + [pltpu.VMEM((seq, V7X_LANES), F32)] * (2 * tc // V7X_LANES),
        compiler_params=_params("arbitrary"),
        name="hyena_filter",
    )(z, f1w, f1b, f1f, f2w, f2b, f2f, f3w, f3w, f3w, f3w, deltas, fwd, rot)


def _hyena_conv_kernel(*refs, gt, pieces):
    x1_refs, x2_refs, v_refs = refs[:pieces], refs[pieces:2 * pieces], refs[2 * pieces:3 * pieces]
    fwd_ref, inv_ref, k_ref, skip_ref, o_ref = refs[3 * pieces:3 * pieces + 5]
    ue_scr, uo_scr, ze_scr, zo_scr, ye_scr, yo_scr = refs[3 * pieces + 5:3 * pieces + 11]
    out_scrs = refs[3 * pieces + 11:]
    seq = o_ref.shape[0]
    half = seq // 2
    even = pl.ds(0, half, stride=2)
    odd = pl.ds(1, half, stride=2)

    def samples(piece_refs, rows):
        return jnp.concatenate([r[rows, :] for r in piece_refs], axis=1)

    def longconv(order):
        ye_scr[...] = jnp.zeros_like(ye_scr)
        yo_scr[...] = jnp.zeros_like(yo_scr)

        def body(c, carry):
            g0 = pl.multiple_of(c * gt, gt)
            rows = pl.ds(g0, gt)
            ue = ue_scr[...]
            uo = uo_scr[...]
            a = _dot(fwd_ref[0, rows, :], ue)
            b = _dot(fwd_ref[1, rows, :], ue)
            cc = _dot(fwd_ref[2, rows, :], uo)
            d = _dot(fwd_ref[3, rows, :], uo)

            def times_filter(first, ur, ui):
                kr = k_ref[2 * order, pl.ds(first + g0, gt), :]
                ki = k_ref[2 * order + 1, pl.ds(first + g0, gt), :]
                return kr * ur + ki * ui, kr * ui - ki * ur

            yra, yia = times_filter(0, a + cc, b + d)
            yrb, yib = times_filter(half, b - d, a - cc)
            ye_scr[...] += (_dot(inv_ref[0, :, rows], (yra + yib).astype(BF))
                            + _dot(inv_ref[1, :, rows], (yia + yrb).astype(BF)))
            yo_scr[...] += (_dot(inv_ref[2, :, rows], (yra - yib).astype(BF))
                            + _dot(inv_ref[3, :, rows], (yia - yrb).astype(BF)))
            return carry

        lax.fori_loop(0, half // gt, body, 0)

    scale = 1.0 / seq
    ve = samples(v_refs, even)
    vo = samples(v_refs, odd)
    ue_scr[...] = ve.astype(BF)
    uo_scr[...] = vo.astype(BF)
    longconv(0)
    ze = samples(x1_refs, even) * (ye_scr[...] * scale + ve * skip_ref[0:1, :])
    zo = samples(x1_refs, odd) * (yo_scr[...] * scale + vo * skip_ref[0:1, :])
    ze_scr[...] = ze
    zo_scr[...] = zo
    ue_scr[...] = ze.astype(BF)
    uo_scr[...] = zo.astype(BF)
    longconv(1)
    oe = samples(x2_refs, even) * (ye_scr[...] * scale + ze_scr[...] * skip_ref[1:2, :])
    oo = samples(x2_refs, odd) * (yo_scr[...] * scale + zo_scr[...] * skip_ref[1:2, :])
    for k, out_scr in enumerate(out_scrs):
        lanes = slice(k * V7X_LANES, (k + 1) * V7X_LANES)
        out_scr[even, :] = oe[:, lanes]
        out_scr[odd, :] = oo[:, lanes]
        o_ref[:, lanes] = out_scr[...].astype(BF)


def _hyena_conv(u3, fwd, inv, kspec, skip, nb, seq, d):
    tc = _tile(d, CHAN_TILE)
    nc = d // tc
    half = seq // 2
    gt = _tile(half, FREQ_TILE)
    pieces = tc // V7X_LANES
    once = dict(pipeline_mode=pl.Buffered(1))
    piece_specs = [pl.BlockSpec((seq, V7X_LANES), lambda j, b, o=o, k=k: (b, (o * nc + j) * pieces + k))
                   for o in range(3) for k in range(pieces)]
    return pl.pallas_call(
        functools.partial(_hyena_conv_kernel, gt=gt, pieces=pieces),
        scratch_shapes=[pltpu.VMEM((half, tc), BF), pltpu.VMEM((half, tc), BF)]
        + [pltpu.VMEM((half, tc), F32)] * 4 + [pltpu.VMEM((seq, V7X_LANES), F32)] * pieces,
        grid=(nc, nb),
        in_specs=piece_specs + [
            pl.BlockSpec((4, half, half), lambda j, b: (0, 0, 0), **once),
            pl.BlockSpec((4, half, half), lambda j, b: (0, 0, 0), **once),
            pl.BlockSpec((4, seq, tc), lambda j, b: (0, 0, j), **once),
            pl.BlockSpec((2, tc), lambda j, b: (0, j)),
        ],
        out_specs=pl.BlockSpec((seq, tc), lambda j, b: (b, j)),
        out_shape=jax.ShapeDtypeStruct((nb * seq, d), BF),
        compiler_params=_params("parallel", "parallel"),
        name="hyena_conv",
    )(*([u3] * (3 * pieces)), fwd, inv, kspec, skip)


def _moe_block(streams, mods, layer, fg, w_gate, w_up, w_down, nb, final=False):
    n_experts = streams[0][2].shape[0]
    group = _sc_group(nb)
    routed = []
    for xs, hn, aff, t, ctx_row in streams:
        cap = max(1, EC_CAPACITY * t // n_experts)
        posm, gate = _route(aff, nb, t, cap)
        by_token = _route_by_token(posm, gate, nb)
        xin = None if hn.dtype == jnp.int32 else _gather(posm, hn, nb, t, cap)
        routed.append((posm, by_token, cap, xin))
    ys = [[] for _ in streams]
    for e0, count in ((0, group), (group, n_experts - group)):
        xins = []
        for (xs, hn, aff, t, ctx_row), (posm, by_token, cap, xin) in zip(streams, routed):
            if xin is None:
                xins.append(([_sc_gather(hn, posm, e, nb, n_experts, t, cap) for e in range(e0, e0 + count, group)], 0))
            else:
                xins.append(([xin], e0))
        for acc, y in zip(ys, _experts(xins, w_gate, w_up, w_down, layer, e0, count, group)):
            acc.append(y)
    return [_combine(by_token, y, xs, mods, fg, nb, t, cap, ctx_row=ctx_row, final=final and ctx_row is None)
            for (xs, _, _, t, ctx_row), (_, by_token, cap, _), y in zip(streams, routed, ys)]


def kernel(x, c, ctx, c_ctx, w_ada, b_ada, norm1_g, norm2_g, final_g, a_w_in, a_conv, a_w_out, b_w_qkv, b_lq1, b_lk1, b_lq2, b_lk2, b_subln_g, b_w_out, c_w_in, c_conv, c_f1_w, c_f1_b, c_f1_freq, c_f2_w, c_f2_b, c_f2_freq, c_f3_w, c_skip, c_w_out, moe_router, moe_w_gate, moe_w_up, moe_w_down):
    nb, seq, d = x.shape
    ctx_len = ctx.shape[1]
    depth = w_ada.shape[0]
    assert nb < MOD_ROWS and d % (2 * DA_HEAD_DIM) == 0
    ctx_row = nb

    cvec = jnp.concatenate([c, c_ctx[None, :], jnp.zeros((MOD_ROWS - nb - 1, d), F32)], axis=0)
    mods_all = _ada(cvec, w_ada, b_ada).reshape(depth, MOD_ROWS, 6, d)

    attn_layers = [i for i in range(depth) if i % N_MIXERS == 1]
    last_ctx_read = max(attn_layers) if attn_layers else -1

    xs = x.reshape(nb * seq, d)
    cs_tok = ctx.reshape(nb * ctx_len, d)
    fg = final_g[None, :]

    for i in range(depth):
        kind, j = i % N_MIXERS, i // N_MIXERS
        update_ctx = i < last_ctx_read
        final = i == depth - 1
        mods = mods_all[i]
        g1 = norm1_g[i][None, :]
        g2 = norm2_g[i][None, :]
        wr = moe_router[i].T
        wrh, wrl = _split(wr)
        moe_w = (moe_w_gate, moe_w_up, moe_w_down)

        ctx_stream = []
        if kind == 0:
            w_in, w_out = a_w_in[j].astype(BF), a_w_out[j].astype(BF)
            if update_ctx:
                cn, chn, caff = _shortconv_layer(cs_tok, mods, g1, w_in, a_conv[j], w_out, g2, wrh, wrl,
                                                 ctx_len, ctx_row=ctx_row)
                ctx_stream = [(cn, chn, caff, ctx_len, ctx_row)]
            xn, hn, aff = _shortconv_layer(xs, mods, g1, w_in, a_conv[j], w_out, g2, wrh, wrl, seq)
        elif kind == 1:
            assert not update_ctx
            lam_init = 0.8 - 0.6 * math.exp(-0.3 * i)
            w_qkv = b_w_qkv[j].astype(BF)
            tables = [jnp.asarray(t) for t in _rope_tables(seq)]
            q, k, v = _qkv(xs, mods, g1, w_qkv, tables, seq, 3, True)
            kc, vc = _qkv(cs_tok, mods, g1, w_qkv[:, d:], tables, min(seq, nb * ctx_len), 2, False, ctx_row=ctx_row)
            o = _attention(q, k, v, kc, vc, b_lq1[j][None, :], b_lk1[j][None, :], b_lq2[j][None, :],
                           b_lk2[j][None, :], b_subln_g[j][None, :], nb, seq, ctx_len, lam_init)
            xn, hn, aff = _outproj_layer(o, xs, mods, b_w_out[j].astype(BF), g2, wrh, wrl, seq)
        else:
            assert not update_ctx
            fwd_np, inv_np = _dft_half_tables(seq)
            fwd, inv = jnp.asarray(fwd_np).astype(BF), jnp.asarray(inv_np).astype(BF)
            w = c_f2_w.shape[-1]
            f1w = jnp.zeros((V7X_LANES, w), F32).at[:HY_EMB_DIM].set(c_f1_w[j])
            kspec = _hyena_filter(jnp.asarray(_hyena_features(seq)), f1w, c_f1_b[j][None, :], c_f1_freq[j][None, :],
                                  c_f2_w[j], c_f2_b[j][None, :], c_f2_freq[j][None, :], c_f3_w[j],
                                  jnp.asarray(_hyena_deltas(d)), fwd, jnp.asarray(_dft_half_shift(seq)), d)
            u3 = _hyena_in(xs, mods, g1, c_w_in[j].astype(BF), c_conv[j], seq)
            z = _hyena_conv(u3, fwd, inv, kspec, c_skip[j], nb, seq, d)
            xn, hn, aff = _outproj_layer(z, xs, mods, c_w_out[j].astype(BF), g2, wrh, wrl, seq)

        outs = _moe_block([(xn, hn, aff, seq, None)] + ctx_stream, mods, i, fg, *moe_w, nb, final=final)
        xs = outs[0]
        if ctx_stream:
            cs_tok = outs[1]

    return xs.reshape(nb, seq, d)
```

```python
import functools
import math

import jax
import jax.numpy as jnp
import numpy as np
from jax import lax
from jax.experimental import pallas as pl
from jax.experimental.pallas import tpu as pltpu

BF = jnp.bfloat16
F32 = jnp.float32

GRID_W = 64
DA_HEAD_DIM = 64
ROPE_AXIS_DIM = DA_HEAD_DIM // 2
ROPE_THETA = 10000.0
SUBLN_EPS = 1e-5
NORM_EPS = 1e-6
N_MIXERS = 3
EC_CAPACITY = 2
HY_EMB_DIM = 33
HY_BANDS = (HY_EMB_DIM - 1) // 2
HY_FAST_DECAY_PCT = 0.3
HY_SLOW_DECAY_PCT = 1.5
HY_DECAY_TARGET = 1e-2

V7X_LANES = 128
V7X_BF16_SUBLANES = 16
V7X_VMEM_BYTES = 64 * 2**20
VMEM_COMPILER_RESERVE = 8 * 2**20
VMEM_LIMIT = V7X_VMEM_BYTES - VMEM_COMPILER_RESERVE
F32_MAGNITUDE_BITS = 31
BF16_BITS = 16
HIGH_HALF = 0xFFFF0000
HALO = V7X_BF16_SUBLANES
MOD_ROWS = 16
ROW_TILE = 512
COMBINE_ROW_TILE = 1024
EXPERT_ROW_STEPS = 2
PROJ_ROW_TILE = 1024
ATTN_Q_TILE = 256
CHAN_TILE = 256
ADA_COL_TILE = 1536
FREQ_TILE = 1024
SC_GATHER_ROWS = 64


def _params(*sem):
    return pltpu.CompilerParams(dimension_semantics=sem, vmem_limit_bytes=VMEM_LIMIT)


def _dot(a, b):
    return jnp.dot(a, b, preferred_element_type=F32)


def _dot_nt(a, b):
    return lax.dot_general(a, b, (((1,), (1,)), ((), ())), preferred_element_type=F32)


def _split(a):
    hi = a.astype(BF)
    lo = (a - hi.astype(F32)).astype(BF)
    return hi, lo


def _dot3(a, b):
    ah, al = _split(a)
    bh, bl = _split(b)
    return _dot(ah, bh) + (_dot(ah, bl) + _dot(al, bh))


def _sigmoid(a):
    return 1.0 / (1.0 + jnp.exp(-a))


def _norm_mod(x, g, shift, scale):
    ms = jnp.mean(x * x, axis=-1, keepdims=True)
    return (x * lax.rsqrt(ms + NORM_EPS) * g) * (1.0 + scale) + shift


def _tile(n, pref):
    t = min(n, pref)
    assert n % t == 0, (n, pref)
    return t


def _full(shape):
    nd = len(shape)
    return pl.BlockSpec(shape, lambda *_: (0,) * nd)


def _ada_kernel(c_ref, w_ref, b_ref, o_ref):
    c = c_ref[...]
    o_ref[0] = _dot3(c * _sigmoid(c), w_ref[0]) + b_ref[0]


def _ada(cvec, w_ada, b_ada):
    depth, d, n6 = w_ada.shape
    tn = _tile(n6, ADA_COL_TILE)
    return pl.pallas_call(
        _ada_kernel,
        grid=(depth, n6 // tn),
        in_specs=[
            _full((MOD_ROWS, d)),
            pl.BlockSpec((1, d, tn), lambda i, j: (i, 0, j)),
            pl.BlockSpec((1, 1, tn), lambda i, j: (i, 0, j)),
        ],
        out_specs=pl.BlockSpec((1, MOD_ROWS, tn), lambda i, j: (i, 0, j)),
        out_shape=jax.ShapeDtypeStruct((depth, MOD_ROWS, n6), F32),
        compiler_params=_params("parallel", "parallel"),
        name="ada",
    )(cvec, w_ada, b_ada.reshape(depth, 1, n6))


def _pack_pairs(hh):
    half = hh.shape[1] // 2
    lo = lax.shift_right_logical(pltpu.bitcast(hh[:, :half].astype(F32), jnp.uint32), jnp.uint32(BF16_BITS))
    hi = pltpu.bitcast(hh[:, half:].astype(F32), jnp.uint32) & jnp.uint32(HIGH_HALF)
    return pltpu.bitcast(lo | hi, jnp.int32)


def _unpack_pairs(words):
    u = pltpu.bitcast(words, jnp.uint32)
    lo = pltpu.bitcast(lax.shift_left(u, jnp.uint32(BF16_BITS)), F32).astype(BF)
    hi = pltpu.bitcast(u & jnp.uint32(HIGH_HALF), F32).astype(BF)
    return jnp.concatenate([lo, hi], axis=1)


def _residual_router(x, y, mod, g2_ref, wrh_ref, wrl_ref, xo_ref, hn_ref, aff_ref):
    xn = x + mod[2:3] * y
    xo_ref[...] = xn
    hn = _norm_mod(xn, g2_ref[...], mod[3:4], mod[4:5])
    hh, hl = _split(hn)
    hn_ref[...] = _pack_pairs(hh) if hn_ref.dtype == jnp.int32 else hh
    e = wrh_ref.shape[0]
    both = _dot_nt(jnp.concatenate([wrh_ref[...], wrl_ref[...]], axis=0), hh)
    logits = both[:e] + (_dot_nt(wrh_ref[...], hl) + both[e:])
    p = jnp.exp(logits - jnp.max(logits, axis=0, keepdims=True))
    aff_ref[...] = p / jnp.sum(p, axis=0, keepdims=True)


def _router_specs(n, d, e, tm, packed):
    in_specs = [_full((1, d)), _full((e, d)), _full((e, d))]
    hn_cols, hn_dtype = (d // 2, jnp.int32) if packed else (d, BF)
    out_specs = [
        pl.BlockSpec((tm, d), lambda i: (i, 0)),
        pl.BlockSpec((tm, hn_cols), lambda i: (i, 0)),
        pl.BlockSpec((e, tm), lambda i: (0, i)),
    ]
    out_shape = [
        jax.ShapeDtypeStruct((n, d), F32),
        jax.ShapeDtypeStruct((n, hn_cols), hn_dtype),
        jax.ShapeDtypeStruct((e, n), F32),
    ]
    return in_specs, out_specs, out_shape


def _halo_specs(n, d, tm):
    per = tm // HALO
    last = n // HALO - 1
    return [
        pl.BlockSpec((HALO, d), lambda i: (jnp.maximum(i * per - 1, 0), 0)),
        pl.BlockSpec((tm, d), lambda i: (i, 0)),
        pl.BlockSpec((HALO, d), lambda i: (jnp.minimum((i + 1) * per, last), 0)),
    ]


def _mod_spec(d, tiles_per_seq, ctx_row):
    if ctx_row is None:
        return pl.BlockSpec((1, 6, d), lambda i: (i // tiles_per_seq, 0, 0))
    return pl.BlockSpec((1, 6, d), lambda i: (ctx_row, 0, 0))


def _halo_rows(xp_ref, x_ref, xn_ref, g_ref, mod, tiles_per_seq):
    t = pl.program_id(0) % tiles_per_seq
    g = g_ref[...]
    hp, hx, hn = [_norm_mod(r[...], g, mod[0:1], mod[1:2]) for r in (xp_ref, x_ref, xn_ref)]
    hp = jnp.where(t == 0, 0.0, hp)
    hn = jnp.where(t == tiles_per_seq - 1, 0.0, hn)
    return jnp.concatenate([hp, hx, hn], axis=0).astype(BF)


def _conv3(s, w_ref, tm, seq_rows=None):
    n = s.shape[0]
    prev = pltpu.roll(s, 1, 0)[HALO:HALO + tm]
    nxt = pltpu.roll(s, n - 1, 0)[HALO:HALO + tm]
    if seq_rows is not None and seq_rows < tm:
        within = lax.broadcasted_iota(jnp.int32, (tm, 1), 0) % seq_rows
        prev = jnp.where(within == 0, 0.0, prev)
        nxt = jnp.where(within == seq_rows - 1, 0.0, nxt)
    return prev * w_ref[0:1, :] + s[HALO:HALO + tm] * w_ref[1:2, :] + nxt * w_ref[2:3, :]


def _shortconv_kernel(xp_ref, x_ref, xn_ref, mod_ref, g1_ref, win_ref, wconv_ref, wout_ref,
                      g2_ref, wrh_ref, wrl_ref, xo_ref, hn_ref, aff_ref, *, tiles_per_seq, seq_rows):
    tm, d = x_ref.shape
    mod = mod_ref[0]
    h = _halo_rows(xp_ref, x_ref, xn_ref, g1_ref, mod, tiles_per_seq)
    gate = _dot(h[HALO:HALO + tm], win_ref[:, 0:d])
    s = _dot(h, win_ref[:, d:2 * d]) * _dot(h, win_ref[:, 2 * d:3 * d])
    z = (gate * _conv3(s, wconv_ref, tm, seq_rows)).astype(BF)
    y = _dot(z, wout_ref[...])
    _residual_router(x_ref[...], y, mod, g2_ref, wrh_ref, wrl_ref, xo_ref, hn_ref, aff_ref)


def _shortconv_layer(xs, mods, g1, w_in, w_conv, w_out, g2, wrh, wrl, seq, ctx_row=None):
    n, d = xs.shape
    e = wrh.shape[0]
    if ctx_row is not None and seq < PROJ_ROW_TILE:
        tm = _tile(n, PROJ_ROW_TILE // seq * seq)
        tps = 1
    else:
        tm = _tile(seq, PROJ_ROW_TILE)
        tps = seq // tm
    r_in, r_out, r_shape = _router_specs(n, d, e, tm, packed=ctx_row is None)
    return pl.pallas_call(
        functools.partial(_shortconv_kernel, tiles_per_seq=tps, seq_rows=seq),
        grid=(n // tm,),
        in_specs=_halo_specs(n, d, tm) + [
            _mod_spec(d, tps, ctx_row), _full((1, d)), _full((d, 3 * d)), _full((3, d)), _full((d, d)),
        ] + r_in,
        out_specs=r_out,
        out_shape=r_shape,
        compiler_params=_params("parallel"),
        name="shortconv_layer",
    )(xs, xs, xs, mods, g1, w_in, w_conv, w_out, g2, wrh, wrl)


def _outproj_kernel(o_ref, x_ref, mod_ref, wout_ref, g2_ref, wrh_ref, wrl_ref, xo_ref, hn_ref, aff_ref):
    y = _dot(o_ref[...], wout_ref[...])
    _residual_router(x_ref[...], y, mod_ref[0], g2_ref, wrh_ref, wrl_ref, xo_ref, hn_ref, aff_ref)


def _outproj_layer(o, xs, mods, w_out, g2, wrh, wrl, seq):
    n, d = xs.shape
    e = wrh.shape[0]
    tm = _tile(seq, PROJ_ROW_TILE)
    tps = seq // tm
    r_in, r_out, r_shape = _router_specs(n, d, e, tm, packed=True)
    return pl.pallas_call(
        _outproj_kernel,
        grid=(n // tm,),
        in_specs=[pl.BlockSpec((tm, d), lambda i: (i, 0)), pl.BlockSpec((tm, d), lambda i: (i, 0)),
                  _mod_spec(d, tps, None), _full((d, d))] + r_in,
        out_specs=r_out,
        out_shape=r_shape,
        compiler_params=_params("parallel"),
        name="outproj_layer",
    )(o, xs, mods, w_out, g2, wrh, wrl)


def _excl_cumsum_lanes(m):
    rows, t = m.shape
    a = lax.broadcasted_iota(jnp.int32, (V7X_LANES, V7X_LANES), 0)
    b = lax.broadcasted_iota(jnp.int32, (V7X_LANES, V7X_LANES), 1)
    tri = jnp.where(a < b, 1.0, 0.0).astype(BF)
    carry = jnp.zeros((rows, 1), F32)
    out = []
    for c in range(t // V7X_LANES):
        blk = m[:, c * V7X_LANES:(c + 1) * V7X_LANES]
        out.append(_dot(blk.astype(BF), tri) + carry)
        carry = carry + jnp.sum(blk, axis=1, keepdims=True)
    return jnp.concatenate(out, axis=1)


def _route_kernel(aff_ref, posm_ref, gate_ref, *, cap, nb):
    e = aff_ref.shape[0]
    t = aff_ref.shape[1] // nb
    aff = jnp.concatenate([aff_ref[:, b * t:(b + 1) * t] for b in range(nb)], axis=0)
    bits = pltpu.bitcast(aff, jnp.int32)

    def step(i, thr):
        cand = thr | jnp.left_shift(jnp.int32(1), F32_MAGNITUDE_BITS - 1 - i)
        cnt = jnp.sum(jnp.where(bits >= cand, 1.0, 0.0), axis=1, keepdims=True)
        return jnp.where(cnt >= cap, cand, thr)

    thr = lax.fori_loop(0, F32_MAGNITUDE_BITS, step, jnp.zeros((nb * e, 1), jnp.int32))
    gt = jnp.where(bits > thr, 1.0, 0.0)
    eq = jnp.where(bits == thr, 1.0, 0.0)
    need = cap - jnp.sum(gt, axis=1, keepdims=True)
    sel = gt + eq * jnp.where(_excl_cumsum_lanes(eq) < need, 1.0, 0.0)
    pos = jnp.where(sel > 0.0, _excl_cumsum_lanes(sel), -1.0)
    posm_ref[...] = pos.astype(jnp.int32)
    gate_ref[...] = sel * aff


def _route_by_token_kernel(posm_ref, gate_ref, bt_ref, *, nb):
    e = posm_ref.shape[0] // nb
    t = posm_ref.shape[1]
    pad = jnp.zeros((V7X_LANES - 2 * e, t), F32)
    for b in range(nb):
        rows = slice(b * e, (b + 1) * e)
        bt_ref[b * t:(b + 1) * t, :] = jnp.concatenate(
            [posm_ref[rows, :].astype(F32), gate_ref[rows, :], pad], axis=0).T


def _route(aff, nb, t, cap):
    e = aff.shape[0]
    return pl.pallas_call(
        functools.partial(_route_kernel, cap=cap, nb=nb),
        grid=(1,),
        in_specs=[_full((e, nb * t))],
        out_specs=[_full((nb * e, t)), _full((nb * e, t))],
        out_shape=[jax.ShapeDtypeStruct((nb * e, t), jnp.int32), jax.ShapeDtypeStruct((nb * e, t), F32)],
        compiler_params=_params("arbitrary"),
        name="route",
    )(aff)


def _route_by_token(posm, gate, nb):
    rows, t = posm.shape
    assert 2 * rows // nb <= V7X_LANES
    return pl.pallas_call(
        functools.partial(_route_by_token_kernel, nb=nb),
        grid=(1,),
        in_specs=[_full((rows, t)), _full((rows, t))],
        out_specs=_full((nb * t, V7X_LANES)),
        out_shape=jax.ShapeDtypeStruct((nb * t, V7X_LANES), F32),
        compiler_params=_params("arbitrary"),
        name="route_by_token",
    )(posm, gate)


def _sc_gather(table, posm, e0, nb, n_experts, t, cap):
    from jax.experimental.pallas import tpu_sc as plsc

    info = plsc.get_sparse_core_info()
    cores, lanes = info.num_cores, info.num_lanes
    workers = cores * info.num_subcores
    group = workers // nb
    words = table.shape[1]
    assert workers % nb == 0 and n_experts % group == 0 and cap % SC_GATHER_ROWS == 0 and t % lanes == 0

    @functools.partial(
        pl.kernel,
        mesh=plsc.VectorSubcoreMesh(core_axis_name="core", subcore_axis_name="subcore"),
        compiler_params=pltpu.CompilerParams(needs_layout_passes=False),
        out_type=jax.ShapeDtypeStruct((group * nb * cap, words), jnp.int32),
        scratch_types=[pltpu.VMEM((t,), jnp.int32), pltpu.VMEM((cap,), jnp.int32),
                       pltpu.VMEM((SC_GATHER_ROWS, words), jnp.int32), pltpu.SemaphoreType.DMA],
    )
    def gather(table_hbm, posm_hbm, out_hbm, pos_v, idx_v, rows_v, sem):
        w = lax.axis_index("subcore") * cores + lax.axis_index("core")
        e_local = w // nb
        b = w % nb
        pltpu.sync_copy(posm_hbm.at[b * n_experts + e0 + e_local], pos_v)

        @pl.loop(0, t // lanes)
        def _(i):
            p = pos_v[pl.ds(i * lanes, lanes)]
            token = lax.iota(jnp.int32, lanes) + (i * lanes + b * t)
            plsc.store_scatter(idx_v, [p], token, mask=p >= 0)

        out_base = (e_local * nb + b) * cap

        @pl.loop(0, cap // SC_GATHER_ROWS)
        def _(j):
            pltpu.async_copy(table_hbm.at[idx_v.at[pl.ds(j * SC_GATHER_ROWS, SC_GATHER_ROWS)]], rows_v, sem).wait()
            pltpu.sync_copy(rows_v, out_hbm.at[pl.ds(out_base + j * SC_GATHER_ROWS, SC_GATHER_ROWS)])

    return gather(table, posm).reshape(group, nb * cap, words)


def _sc_group(nb):
    from jax.experimental.pallas import tpu_sc as plsc

    info = plsc.get_sparse_core_info()
    return info.num_cores * info.num_subcores // nb


def _gather_kernel(posm_ref, hn_ref, o_ref, *, cap):
    e = posm_ref.shape[0]
    t, d = hn_ref.shape
    slot = lax.broadcasted_iota(jnp.int32, (cap, t), 0)
    onehot = jnp.concatenate(
        [jnp.where(posm_ref[k:k + 1, :] == slot, 1.0, 0.0).astype(BF) for k in range(e)], axis=0)
    o_ref[...] = _dot(onehot, hn_ref[...]).astype(BF).reshape(e, cap, d)


def _gather(posm, hn, nb, t, cap):
    e = posm.shape[0] // nb
    d = hn.shape[1]
    return pl.pallas_call(
        functools.partial(_gather_kernel, cap=cap),
        grid=(nb,),
        in_specs=[pl.BlockSpec((e, t), lambda b: (b, 0)), pl.BlockSpec((t, d), lambda b: (b, 0))],
        out_specs=pl.BlockSpec((e, cap, d), lambda b: (0, b, 0)),
        out_shape=jax.ShapeDtypeStruct((e, nb * cap, d), BF),
        compiler_params=_params("parallel"),
        name="moe_gather",
    )(posm, hn)


def _expert_kernel(*refs, pieces, per):
    n_in = sum(pieces)
    wg_ref, wu_ref, wd_ref = refs[n_in:n_in + 3]
    y_refs = refs[n_in + 3:n_in + 3 + len(pieces)]
    w_scr = refs[-1]

    @pl.when(pl.program_id(1) == 0)
    def _():
        w_scr[0] = wg_ref[0, 0].astype(BF)
        w_scr[1] = wu_ref[0, 0].astype(BF)
        w_scr[2] = wd_ref[0, 0].astype(BF)

    which = pl.program_id(0) // per

    start = 0
    for n_pieces, y_ref in zip(pieces, y_refs):
        x_refs = refs[start:start + n_pieces]
        start += n_pieces
        rows = x_refs[0].shape[1]
        tr = _tile(rows, ROW_TILE)

        def body(j, carry, x_refs=x_refs, y_ref=y_ref, tr=tr):
            r0 = pl.multiple_of(j * tr, tr)
            xs = x_refs[0][0, pl.ds(r0, tr), :]
            for p in range(1, len(x_refs)):
                xs = jnp.where(which == p, x_refs[p][0, pl.ds(r0, tr), :], xs)
            if xs.dtype == jnp.int32:
                xs = _unpack_pairs(xs)
            a = _dot(xs, w_scr[0])
            b = _dot(xs, w_scr[1])
            hm = (a * _sigmoid(a) * b).astype(BF)
            y_ref[0, pl.ds(r0, tr), :] = _dot(hm, w_scr[2]).astype(BF)
            return carry

        lax.fori_loop(0, rows // tr, body, 0)


def _experts(streams, w_gate, w_up, w_down, layer, e0, count, per):
    d = w_gate.shape[-2]
    f = w_gate.shape[-1]
    assert f == d
    wspec = pl.BlockSpec((1, 1, d, f), lambda k, j: (layer, e0 + k, 0, 0))
    xspecs, xargs = [], []
    for arrays, first in streams:
        for p, x in enumerate(arrays):
            if len(arrays) == 1:
                index = lambda k, j, first=first: (first + k, j, 0)
            else:
                index = lambda k, j, p=p: (
                    jnp.clip(k - p * per, 0, per - 1),
                    jnp.where(k < p * per, 0, jnp.where(k >= (p + 1) * per, EXPERT_ROW_STEPS - 1, j)), 0)
            xspecs.append(pl.BlockSpec((1, x.shape[1] // EXPERT_ROW_STEPS, x.shape[2]), index))
            xargs.append(x)
    rows = [arrays[0].shape[1] for arrays, _ in streams]
    return pl.pallas_call(
        functools.partial(_expert_kernel, pieces=tuple(len(arrays) for arrays, _ in streams), per=per),
        grid=(count, EXPERT_ROW_STEPS),
        in_specs=xspecs + [wspec, wspec, wspec],
        out_specs=[pl.BlockSpec((1, r // EXPERT_ROW_STEPS, d), lambda k, j: (k, j, 0)) for r in rows],
        out_shape=[jax.ShapeDtypeStruct((count, r, d), BF) for r in rows],
        scratch_shapes=[pltpu.VMEM((3, d, f), BF)],
        compiler_params=_params("parallel", "arbitrary"),
        name="moe_experts",
    )(*xargs, w_gate, w_up, w_down)


def _combine_kernel(bt_ref, *refs, cap, final):
    y_refs = refs[:-4]
    x_ref, mod_ref, fg_ref, o_ref = refs[-4:]
    d = y_refs[0].shape[2]
    e = sum(r.shape[0] for r in y_refs)
    tq = x_ref.shape[0]
    pt = bt_ref[:, 0:e]
    gt = bt_ref[:, e:2 * e]
    if cap % V7X_LANES == 0:
        slot = lax.broadcasted_iota(jnp.int32, (tq, cap), 1).astype(F32)
        pieces = [jnp.where(pt[:, k:k + 1] == slot, gt[:, k:k + 1], 0.0).astype(BF) for k in range(e)]
        scat = jnp.concatenate(pieces, axis=1)
    else:
        slot = lax.broadcasted_iota(jnp.int32, (tq, e * cap), 1).astype(F32)
        scat = jnp.zeros((tq, e * cap), F32)
        for k in range(e):
            pk = pt[:, k:k + 1]
            scat = jnp.where((pk >= 0.0) & (pk + float(k * cap) == slot), gt[:, k:k + 1], scat)
        scat = scat.astype(BF)
    out, col = None, 0
    for y_ref in y_refs:
        width = y_ref.shape[0] * cap
        part = _dot(scat[:, col:col + width], y_ref[...].reshape(width, d))
        out = part if out is None else out + part
        col += width
    xn = x_ref[...] + mod_ref[0][5:6] * out
    if final:
        ms = jnp.mean(xn * xn, axis=-1, keepdims=True)
        xn = xn * lax.rsqrt(ms + NORM_EPS) * fg_ref[...]
    o_ref[...] = xn


def _combine(by_token, ys, xs, mods, fg, nb, t, cap, ctx_row=None, final=False):
    n, d = xs.shape
    tq = _tile(t, COMBINE_ROW_TILE)
    tpb = t // tq
    if ctx_row is None:
        mspec = pl.BlockSpec((1, 6, d), lambda b, i: (b, 0, 0))
    else:
        mspec = pl.BlockSpec((1, 6, d), lambda b, i: (ctx_row, 0, 0))
    return pl.pallas_call(
        functools.partial(_combine_kernel, cap=cap, final=final),
        grid=(nb, tpb),
        in_specs=[
            pl.BlockSpec((tq, V7X_LANES), lambda b, i: (b * tpb + i, 0)),
        ] + [pl.BlockSpec((y.shape[0], cap, d), lambda b, i: (0, b, 0)) for y in ys] + [
            pl.BlockSpec((tq, d), lambda b, i: (b * tpb + i, 0)),
            mspec,
            _full((1, d)),
        ],
        out_specs=pl.BlockSpec((tq, d), lambda b, i: (b * tpb + i, 0)),
        out_shape=jax.ShapeDtypeStruct((n, d), F32),
        compiler_params=_params("parallel", "parallel"),
        name="moe_combine",
    )(by_token, *ys, xs, mods, fg)


def _rope_tables(seq):
    rows = seq // GRID_W
    row = np.repeat(np.arange(rows, dtype=np.float32), GRID_W)
    col = np.tile(np.arange(GRID_W, dtype=np.float32), rows)
    inv_freq = (ROPE_THETA ** (-np.arange(0, ROPE_AXIS_DIM, 2, dtype=np.float32) / ROPE_AXIS_DIM)).astype(np.float32)
    lane = np.arange(2 * DA_HEAD_DIM)
    within = lane % DA_HEAD_DIM
    axis = within // ROPE_AXIS_DIM
    half = (within % ROPE_AXIS_DIM) // (ROPE_AXIS_DIM // 2)
    idx = within % (ROPE_AXIS_DIM // 2)
    pos = np.where(axis[None, :] == 0, row[:, None], col[:, None])
    ang = (pos * inv_freq[idx][None, :]).astype(np.float32)
    cos = np.cos(ang).astype(np.float32)
    sin = np.sin(ang).astype(np.float32)
    sin_lo = np.where(half[None, :] == 1, sin, 0.0).astype(np.float32)
    sin_hi = np.where(half[None, :] == 0, -sin, 0.0).astype(np.float32)
    return cos, sin_lo, sin_hi


def _qkv_kernel(x_ref, mod_ref, g_ref, w_ref, cos_ref, sa_ref, sb_ref, *o_refs, rope, first):
    d = x_ref.shape[1]
    mod = mod_ref[0]
    h = _norm_mod(x_ref[...], g_ref[...], mod[0:1], mod[1:2]).astype(BF)
    slab = 2 * DA_HEAD_DIM
    shift = ROPE_AXIS_DIM // 2
    for j, o_ref in enumerate(o_refs):
        u = _dot(h, w_ref[:, (first + j) * d:(first + j + 1) * d])
        if rope and j < 2:
            scale = DA_HEAD_DIM ** -0.5 * math.log2(math.e) if j == 0 else 1.0
            cos, sa, sb = cos_ref[...] * scale, sa_ref[...] * scale, sb_ref[...] * scale
            for hd in range(d // slab):
                xs = u[:, hd * slab:(hd + 1) * slab]
                r = xs * cos + pltpu.roll(xs, shift, 1) * sa + pltpu.roll(xs, slab - shift, 1) * sb
                o_ref[:, hd * slab:(hd + 1) * slab] = r.astype(BF)
        else:
            o_ref[...] = u.astype(BF)


def _qkv(xs, mods, g1, w, tables, seq, nout, rope, ctx_row=None, first=0):
    n, d = xs.shape
    tm = _tile(seq, PROJ_ROW_TILE)
    tps = seq // tm
    slab = 2 * DA_HEAD_DIM
    tspec = pl.BlockSpec((tm, slab), lambda i: (i % tps, 0))
    return pl.pallas_call(
        functools.partial(_qkv_kernel, rope=rope, first=first),
        grid=(n // tm,),
        in_specs=[pl.BlockSpec((tm, d), lambda i: (i, 0)), _mod_spec(d, tps, ctx_row), _full((1, d)),
                  _full(w.shape), tspec, tspec, tspec],
        out_specs=[pl.BlockSpec((tm, d), lambda i: (i, 0))] * nout,
        out_shape=[jax.ShapeDtypeStruct((n, d), BF)] * nout,
        compiler_params=_params("parallel"),
        name="attn_qkv",
    )(xs, mods, g1, w, *tables)


def _attn_kernel(q_ref, k_ref, v_ref, kc_ref, vc_ref, lq1_ref, lk1_ref, lq2_ref, lk2_ref, sg_ref, o_ref,
                 sl_a, sc_a, sl_b, sc_b, va_scr, vca_scr, *, lam_init, tq):
    seq = q_ref.shape[0]
    lam = (jnp.exp(jnp.sum(lq1_ref[...] * lk1_ref[...], axis=1, keepdims=True))
           - jnp.exp(jnp.sum(lq2_ref[...] * lk2_ref[...], axis=1, keepdims=True)) + lam_init)
    slots = ((sl_a, sc_a), (sl_b, sc_b))

    def scores(i, slot):
        sl_ref, sc_ref = slot
        q = q_ref[i * tq:(i + 1) * tq, :]
        lane = lax.broadcasted_iota(jnp.int32, q.shape, 1)
        zero = jnp.zeros_like(q)
        for mp, qm in enumerate((jnp.where(lane < DA_HEAD_DIM, q, zero), jnp.where(lane >= DA_HEAD_DIM, q, zero))):
            sl_ref[mp] = _dot_nt(qm, k_ref[...])
            sc_ref[mp] = _dot_nt(qm, kc_ref[...])

    slab = v_ref.shape[1]
    va_scr[:, 0:slab] = v_ref[...]
    va_scr[:, slab:2 * slab] = jnp.ones_like(v_ref)
    vca_scr[:, 0:slab] = vc_ref[...]
    vca_scr[:, slab:2 * slab] = jnp.ones_like(vc_ref)

    def unnormalised(sl_ref, sc_ref, mp):
        s_l = sl_ref[mp]
        s_c = sc_ref[mp]
        m = jnp.maximum(jnp.max(s_l, axis=1, keepdims=True), jnp.max(s_c, axis=1, keepdims=True))
        p_l = jnp.exp2(s_l - m).astype(BF)
        p_c = jnp.exp2(s_c - m).astype(BF)
        both = _dot(p_l, va_scr[...]) + _dot(p_c, vca_scr[...])
        return both[:, 0:slab], both[:, slab:slab + 1]

    def attend(i, slot):
        o1, t1 = unnormalised(*slot, 0)
        o2, t2 = unnormalised(*slot, 1)
        o = o1 * (1.0 / t1) - o2 * (lam / t2)
        ms = jnp.mean(o * o, axis=-1, keepdims=True)
        o_ref[i * tq:(i + 1) * tq, :] = (o * lax.rsqrt(ms + SUBLN_EPS) * sg_ref[...] * (1.0 - lam_init)).astype(BF)

    n = seq // tq
    scores(0, slots[0])
    for i in range(n):
        if i + 1 < n:
            scores(i + 1, slots[(i + 1) % 2])
        attend(i, slots[i % 2])


def _attention(q, k, v, kc, vc, lq1, lk1, lq2, lk2, sg, nb, seq, ctx_len, lam_init):
    n, d = q.shape
    slab = 2 * DA_HEAD_DIM
    heads = d // slab
    tq = _tile(seq, ATTN_Q_TILE)
    small = _full((1, DA_HEAD_DIM))
    lat = pl.BlockSpec((seq, slab), lambda b, h: (b, h))
    ctx = pl.BlockSpec((ctx_len, slab), lambda b, h: (b, h))
    score_scratch = [pltpu.VMEM((2, tq, seq), F32), pltpu.VMEM((2, tq, ctx_len), F32)]
    return pl.pallas_call(
        functools.partial(_attn_kernel, lam_init=lam_init, tq=tq),
        grid=(nb, heads),
        in_specs=[lat, lat, lat, ctx, ctx, small, small, small, small, _full((1, slab))],
        out_specs=lat,
        out_shape=jax.ShapeDtypeStruct((n, d), BF),
        scratch_shapes=score_scratch + score_scratch + [pltpu.VMEM((seq, 2 * slab), BF),
                                                        pltpu.VMEM((ctx_len, 2 * slab), BF)],
        compiler_params=_params("parallel", "parallel"),
        name="diff_attention",
    )(q, k, v, kc, vc, lq1, lk1, lq2, lk2, sg)


def _hyena_in_kernel(xp_ref, x_ref, xn_ref, mod_ref, g1_ref, win_ref, wconv_ref, o_ref, *, tiles_per_seq):
    tm, d = x_ref.shape
    h = _halo_rows(xp_ref, x_ref, xn_ref, g1_ref, mod_ref[0], tiles_per_seq)
    for j in range(3):
        u = _dot(h, win_ref[:, j * d:(j + 1) * d])
        o_ref[:, j * d:(j + 1) * d] = _conv3(u, wconv_ref.at[:, j * d:(j + 1) * d], tm)


def _hyena_in(xs, mods, g1, w_in, w_conv, seq):
    n, d = xs.shape
    tm = _tile(seq, PROJ_ROW_TILE)
    tps = seq // tm
    return pl.pallas_call(
        functools.partial(_hyena_in_kernel, tiles_per_seq=tps),
        grid=(n // tm,),
        in_specs=_halo_specs(n, d, tm) + [_mod_spec(d, tps, None), _full((1, d)), _full((d, 3 * d)), _full((3, 3 * d))],
        out_specs=pl.BlockSpec((tm, 3 * d), lambda i: (i, 0)),
        out_shape=jax.ShapeDtypeStruct((n, 3 * d), F32),
        compiler_params=_params("parallel"),
        name="hyena_in",
    )(xs, xs, xs, mods, g1, w_in, w_conv)


def _dft_half_shift(seq):
    order = np.concatenate([np.arange(seq // 2), seq - 1 - np.arange(seq // 2)])
    half = np.pi * (order + 0.5) / (2 * seq)
    return np.stack([np.cos(half), np.sin(half)], axis=1).astype(np.float32)


def _dft_half_tables(seq):
    h = seq // 2
    th = 2.0 * np.pi * (np.arange(h, dtype=np.float64) + 0.5) / seq
    s = np.arange(h, dtype=np.float64)
    even, odd = np.outer(th, s + 0.25), np.outer(th, s + 0.75)
    fwd = np.stack([np.cos(even), np.sin(even), np.cos(odd), np.sin(odd)]).astype(np.float32)
    inv = np.ascontiguousarray(np.transpose(fwd, (0, 2, 1)))
    return fwd, inv


def _hyena_features(seq):
    t = np.linspace(0.0, 1.0, seq, dtype=np.float32)[:, None]
    w = (2.0 * math.pi * np.arange(seq, dtype=np.float32)[:, None] / seq).astype(np.float32)
    f = np.linspace(1e-4, HY_BANDS - 1, HY_BANDS, dtype=np.float32)[None, :]
    z = np.concatenate([t, np.cos(f * w), -np.sin(f * w)], axis=-1).astype(np.float32)
    zp = np.zeros((seq, V7X_LANES), np.float32)
    zp[:, :HY_EMB_DIM] = z
    return zp


def _hyena_deltas(d):
    max_decay = math.log(HY_DECAY_TARGET) / HY_FAST_DECAY_PCT
    min_decay = math.log(HY_DECAY_TARGET) / HY_SLOW_DECAY_PCT
    return np.abs(np.linspace(min_decay, max_decay, d, dtype=np.float32))[None, :].astype(np.float32)


def _hyena_filter_kernel(z_ref, f1w_ref, f1b_ref, f1f_ref, f2w_ref, f2b_ref, f2f_ref,
                         f3a_ref, f3b_ref, f3c_ref, f3d_ref, delta_ref, fwd_ref, rot_ref, k_ref, hid_scr, *pm_scrs):
    seq = z_ref.shape[0]
    half = seq // 2
    tc = delta_ref.shape[1]
    even = pl.ds(0, half, stride=2)
    odd = pl.ds(1, half, stride=2)

    @pl.when(pl.program_id(0) == 0)
    def _():
        h1 = jnp.sin(f1f_ref[...] * (_dot3(z_ref[...], f1w_ref[...]) + f1b_ref[...]))
        hid_scr[...] = jnp.sin(f2f_ref[...] * (_dot3(h1, f2w_ref[...]) + f2b_ref[...]))

    hid = hid_scr[...]
    decay = jnp.exp(-z_ref[:, 0:1] * delta_ref[...])
    row = lax.broadcasted_iota(jnp.int32, (seq, 1), 0)
    cr = rot_ref[:, 0:1]
    sr = rot_ref[:, 1:2]
    for order, (f3_fwd_ref, f3_bwd_ref) in enumerate(((f3a_ref, f3b_ref), (f3c_ref, f3d_ref))):
        h_fwd = _dot3(hid, f3_fwd_ref[...]) * decay
        h_bwd = jnp.where(row == 0, 0.0, _dot3(hid, f3_bwd_ref[...]) * decay)
        pm = jnp.concatenate([h_fwd + h_bwd, h_bwd - h_fwd], axis=1)
        for k, scr in enumerate(pm_scrs):
            scr[...] = pm[:, k * V7X_LANES:(k + 1) * V7X_LANES]
        x_even = jnp.concatenate([scr[even, :] for scr in pm_scrs], axis=1).astype(BF)
        x_odd = jnp.concatenate([scr[odd, :] for scr in pm_scrs], axis=1).astype(BF)
        a = _dot(fwd_ref[0], x_even)
        b = _dot(fwd_ref[1], x_even)
        c = _dot(fwd_ref[2], x_odd)
        s = _dot(fwd_ref[3], x_odd)
        cos_sum = jnp.concatenate([a + c, b - s], axis=0)
        sin_sum = jnp.concatenate([b + s, a - c], axis=0)
        k_ref[2 * order] = cr * cos_sum[:, :tc] + sr * sin_sum[:, :tc]
        k_ref[2 * order + 1] = cr * sin_sum[:, tc:] - sr * cos_sum[:, tc:]


def _hyena_filter(z, f1w, f1b, f1f, f2w, f2b, f2f, f3w, deltas, fwd, rot, d):
    seq = z.shape[0]
    half = seq // 2
    w = f2w.shape[0]
    tc = _tile(d, CHAN_TILE)
    nc = d // tc
    f3spec = [pl.BlockSpec((w, tc), lambda j, o=o: (0, o * nc + j)) for o in range(4)]
    return pl.pallas_call(
        _hyena_filter_kernel,
        grid=(nc,),
        in_specs=[_full((seq, V7X_LANES)), _full((V7X_LANES, w)), _full((1, w)), _full((1, w)),
                  _full((w, w)), _full((1, w)), _full((1, w))] + f3spec + [
            pl.BlockSpec((1, tc), lambda j: (0, j)),
            pl.BlockSpec((4, half, half), lambda j: (0, 0, 0), pipeline_mode=pl.Buffered(1)),
            _full((seq, 2)),
        ],
        out_specs=pl.BlockSpec((4, seq, tc), lambda j: (0, 0, j)),
        out_shape=jax.ShapeDtypeStruct((4, seq, d), F32),
        scratch_shapes=[pltpu.VMEM((seq, w), F32)] + [pltpu.VMEM((seq, V7X_LANES), F32)] * (2 * tc // V7X_LANES),
        compiler_params=_params("arbitrary"),
        name="hyena_filter",
    )(z, f1w, f1b, f1f, f2w, f2b, f2f, f3w, f3w, f3w, f3w, deltas, fwd, rot)


def _hyena_conv_kernel(*refs, gt, pieces):
    x1_refs, x2_refs, v_refs = refs[:pieces], refs[pieces:2 * pieces], refs[2 * pieces:3 * pieces]
    fwd_ref, inv_ref, k_ref, skip_ref, o_ref = refs[3 * pieces:3 * pieces + 5]
    ue_scr, uo_scr, ze_scr, zo_scr, ye_scr, yo_scr = refs[3 * pieces + 5:3 * pieces + 11]
    out_scrs = refs[3 * pieces + 11:]
    seq = o_ref.shape[0]
    half = seq // 2
    even = pl.ds(0, half, stride=2)
    odd = pl.ds(1, half, stride=2)

    def samples(piece_refs, rows):
        return jnp.concatenate([r[rows, :] for r in piece_refs], axis=1)

    def longconv(order):
        ye_scr[...] = jnp.zeros_like(ye_scr)
        yo_scr[...] = jnp.zeros_like(yo_scr)

        def body(c, carry):
            g0 = pl.multiple_of(c * gt, gt)
            rows = pl.ds(g0, gt)
            ue = ue_scr[...]
            uo = uo_scr[...]
            a = _dot(fwd_ref[0, rows, :], ue)
            b = _dot(fwd_ref[1, rows, :], ue)
            cc = _dot(fwd_ref[2, rows, :], uo)
            d = _dot(fwd_ref[3, rows, :], uo)

            def times_filter(first, ur, ui):
                kr = k_ref[2 * order, pl.ds(first + g0, gt), :]
                ki = k_ref[2 * order + 1, pl.ds(first + g0, gt), :]
                return kr * ur + ki * ui, kr * ui - ki * ur

            yra, yia = times_filter(0, a + cc, b + d)
            yrb, yib = times_filter(half, b - d, a - cc)
            ye_scr[...] += (_dot(inv_ref[0, :, rows], (yra + yib).astype(BF))
                            + _dot(inv_ref[1, :, rows], (yia + yrb).astype(BF)))
            yo_scr[...] += (_dot(inv_ref[2, :, rows], (yra - yib).astype(BF))
                            + _dot(inv_ref[3, :, rows], (yia - yrb).astype(BF)))
            return carry

        lax.fori_loop(0, half // gt, body, 0)

    scale = 1.0 / seq
    ve = samples(v_refs, even)
    vo = samples(v_refs, odd)
    ue_scr[...] = ve.astype(BF)
    uo_scr[...] = vo.astype(BF)
    longconv(0)
    ze = samples(x1_refs, even) * (ye_scr[...] * scale + ve * skip_ref[0:1, :])
    zo = samples(x1_refs, odd) * (yo_scr[...] * scale + vo * skip_ref[0:1, :])
    ze_scr[...] = ze
    zo_scr[...] = zo
    ue_scr[...] = ze.astype(BF)
    uo_scr[...] = zo.astype(BF)
    longconv(1)
    oe = samples(x2_refs, even) * (ye_scr[...] * scale + ze_scr[...] * skip_ref[1:2, :])
    oo = samples(x2_refs, odd) * (yo_scr[...] * scale + zo_scr[...] * skip_ref[1:2, :])
    for k, out_scr in enumerate(out_scrs):
        lanes = slice(k * V7X_LANES, (k + 1) * V7X_LANES)
        out_scr[even, :] = oe[:, lanes]
        out_scr[odd, :] = oo[:, lanes]
        o_ref[:, lanes] = out_scr[...].astype(BF)


def _hyena_conv(u3, fwd, inv, kspec, skip, nb, seq, d):
    tc = _tile(d, CHAN_TILE)
    nc = d // tc
    half = seq // 2
    gt = _tile(half, FREQ_TILE)
    pieces = tc // V7X_LANES
    once = dict(pipeline_mode=pl.Buffered(1))
    piece_specs = [pl.BlockSpec((seq, V7X_LANES), lambda j, b, o=o, k=k: (b, (o * nc + j) * pieces + k))
                   for o in range(3) for k in range(pieces)]
    return pl.pallas_call(
        functools.partial(_hyena_conv_kernel, gt=gt, pieces=pieces),
        scratch_shapes=[pltpu.VMEM((half, tc), BF), pltpu.VMEM((half, tc), BF)]
        + [pltpu.VMEM((half, tc), F32)] * 4 + [pltpu.VMEM((seq, V7X_LANES), F32)] * pieces,
        grid=(nc, nb),
        in_specs=piece_specs + [
            pl.BlockSpec((4, half, half), lambda j, b: (0, 0, 0), **once),
            pl.BlockSpec((4, half, half), lambda j, b: (0, 0, 0), **once),
            pl.BlockSpec((4, seq, tc), lambda j, b: (0, 0, j), **once),
            pl.BlockSpec((2, tc), lambda j, b: (0, j)),
        ],
        out_specs=pl.BlockSpec((seq, tc), lambda j, b: (b, j)),
        out_shape=jax.ShapeDtypeStruct((nb * seq, d), BF),
        compiler_params=_params("parallel", "parallel"),
        name="hyena_conv",
    )(*([u3] * (3 * pieces)), fwd, inv, kspec, skip)


def _moe_block(streams, mods, layer, fg, w_gate, w_up, w_down, nb, final=False):
    n_experts = streams[0][2].shape[0]
    group = _sc_group(nb)
    routed = []
    for xs, hn, aff, t, ctx_row in streams:
        cap = max(1, EC_CAPACITY * t // n_experts)
        posm, gate = _route(aff, nb, t, cap)
        by_token = _route_by_token(posm, gate, nb)
        xin = None if hn.dtype == jnp.int32 else _gather(posm, hn, nb, t, cap)
        routed.append((posm, by_token, cap, xin))
    ys = [[] for _ in streams]
    for e0, count in ((0, group), (group, n_experts - group)):
        xins = []
        for (xs, hn, aff, t, ctx_row), (posm, by_token, cap, xin) in zip(streams, routed):
            if xin is None:
                xins.append(([_sc_gather(hn, posm, e, nb, n_experts, t, cap) for e in range(e0, e0 + count, group)], 0))
            else:
                xins.append(([xin], e0))
        for acc, y in zip(ys, _experts(xins, w_gate, w_up, w_down, layer, e0, count, group)):
            acc.append(y)
    return [_combine(by_token, y, xs, mods, fg, nb, t, cap, ctx_row=ctx_row, final=final and ctx_row is None)
            for (xs, _, _, t, ctx_row), (_, by_token, cap, _), y in zip(streams, routed, ys)]


def kernel(x, c, ctx, c_ctx, w_ada, b_ada, norm1_g, norm2_g, final_g, a_w_in, a_conv, a_w_out, b_w_qkv, b_lq1, b_lk1, b_lq2, b_lk2, b_subln_g, b_w_out, c_w_in, c_conv, c_f1_w, c_f1_b, c_f1_freq, c_f2_w, c_f2_b, c_f2_freq, c_f3_w, c_skip, c_w_out, moe_router, moe_w_gate, moe_w_up, moe_w_down):
    nb, seq, d = x.shape
    ctx_len = ctx.shape[1]
    depth = w_ada.shape[0]
    assert nb < MOD_ROWS and d % (2 * DA_HEAD_DIM) == 0
    ctx_row = nb

    cvec = jnp.concatenate([c, c_ctx[None, :], jnp.zeros((MOD_ROWS - nb - 1, d), F32)], axis=0)
    mods_all = _ada(cvec, w_ada, b_ada).reshape(depth, MOD_ROWS, 6, d)

    attn_layers = [i for i in range(depth) if i % N_MIXERS == 1]
    last_ctx_read = max(attn_layers) if attn_layers else -1

    xs = x.reshape(nb * seq, d)
    cs_tok = ctx.reshape(nb * ctx_len, d)
    fg = final_g[None, :]
    wrh_all, wrl_all = _split(jnp.swapaxes(moe_router, 1, 2))

    for i in range(depth):
        kind, j = i % N_MIXERS, i // N_MIXERS
        update_ctx = i < last_ctx_read
        final = i == depth - 1
        mods = mods_all[i]
        g1 = norm1_g[i][None, :]
        g2 = norm2_g[i][None, :]
        wrh, wrl = wrh_all[i], wrl_all[i]
        moe_w = (moe_w_gate, moe_w_up, moe_w_down)

        ctx_stream = []
        if kind == 0:
            w_in, w_out = a_w_in[j].astype(BF), a_w_out[j].astype(BF)
            if update_ctx:
                cn, chn, caff = _shortconv_layer(cs_tok, mods, g1, w_in, a_conv[j], w_out, g2, wrh, wrl,
                                                 ctx_len, ctx_row=ctx_row)
                ctx_stream = [(cn, chn, caff, ctx_len, ctx_row)]
            xn, hn, aff = _shortconv_layer(xs, mods, g1, w_in, a_conv[j], w_out, g2, wrh, wrl, seq)
        elif kind == 1:
            assert not update_ctx
            lam_init = 0.8 - 0.6 * math.exp(-0.3 * i)
            w_qkv = b_w_qkv[j].astype(BF)
            tables = [jnp.asarray(t) for t in _rope_tables(seq)]
            q, k, v = _qkv(xs, mods, g1, w_qkv, tables, seq, 3, True)
            kc, vc = _qkv(cs_tok, mods, g1, w_qkv, tables, min(seq, nb * ctx_len), 2, False, ctx_row=ctx_row, first=1)
            o = _attention(q, k, v, kc, vc, b_lq1[j][None, :], b_lk1[j][None, :], b_lq2[j][None, :],
                           b_lk2[j][None, :], b_subln_g[j][None, :], nb, seq, ctx_len, lam_init)
            xn, hn, aff = _outproj_layer(o, xs, mods, b_w_out[j].astype(BF), g2, wrh, wrl, seq)
        else:
            assert not update_ctx
            fwd_np, inv_np = _dft_half_tables(seq)
            fwd, inv = jnp.asarray(fwd_np).astype(BF), jnp.asarray(inv_np).astype(BF)
            w = c_f2_w.shape[-1]
            f1w = jnp.zeros((V7X_LANES, w), F32).at[:HY_EMB_DIM].set(c_f1_w[j])
            kspec = _hyena_filter(jnp.asarray(_hyena_features(seq)), f1w, c_f1_b[j][None, :], c_f1_freq[j][None, :],
                                  c_f2_w[j], c_f2_b[j][None, :], c_f2_freq[j][None, :], c_f3_w[j],
                                  jnp.asarray(_hyena_deltas(d)), fwd, jnp.asarray(_dft_half_shift(seq)), d)
            u3 = _hyena_in(xs, mods, g1, c_w_in[j].astype(BF), c_conv[j], seq)
            z = _hyena_conv(u3, fwd, inv, kspec, c_skip[j], nb, seq, d)
            xn, hn, aff = _outproj_layer(z, xs, mods, c_w_out[j].astype(BF), g2, wrh, wrl, seq)

        outs = _moe_block([(xn, hn, aff, seq, None)] + ctx_stream, mods, i, fg, *moe_w, nb, final=final)
        xs = outs[0]
        if ctx_stream:
            cs_tok = outs[1]

    return xs.reshape(nb, seq, d)
```

```python
import functools
import math

import jax
import jax.numpy as jnp
import numpy as np
from jax import lax
from jax.experimental import pallas as pl
from jax.experimental.pallas import tpu as pltpu

BF = jnp.bfloat16
F32 = jnp.float32

GRID_W = 64
DA_HEAD_DIM = 64
ROPE_AXIS_DIM = DA_HEAD_DIM // 2
ROPE_THETA = 10000.0
SUBLN_EPS = 1e-5
NORM_EPS = 1e-6
N_MIXERS = 3
EC_CAPACITY = 2
HY_EMB_DIM = 33
HY_BANDS = (HY_EMB_DIM - 1) // 2
HY_FAST_DECAY_PCT = 0.3
HY_SLOW_DECAY_PCT = 1.5
HY_DECAY_TARGET = 1e-2

V7X_LANES = 128
V7X_BF16_SUBLANES = 16
V7X_VMEM_BYTES = 64 * 2**20
VMEM_COMPILER_RESERVE = 8 * 2**20
VMEM_LIMIT = V7X_VMEM_BYTES - VMEM_COMPILER_RESERVE
F32_MAGNITUDE_BITS = 31
BF16_BITS = 16
HIGH_HALF = 0xFFFF0000
HALO = V7X_BF16_SUBLANES
MOD_ROWS = 16
ROW_TILE = 512
COMBINE_ROW_TILE = 1024
EXPERT_ROW_STEPS = 2
PROJ_ROW_TILE = 1024
ATTN_Q_TILE = 256
CHAN_TILE = 256
ADA_COL_TILE = 1536
FREQ_TILE = 1024
SC_GATHER_ROWS = 64


def _params(*sem):
    return pltpu.CompilerParams(dimension_semantics=sem, vmem_limit_bytes=VMEM_LIMIT)


def _dot(a, b):
    return jnp.dot(a, b, preferred_element_type=F32)


def _dot_nt(a, b):
    return lax.dot_general(a, b, (((1,), (1,)), ((), ())), preferred_element_type=F32)


def _split(a):
    hi = a.astype(BF)
    lo = (a - hi.astype(F32)).astype(BF)
    return hi, lo


def _dot3(a, b):
    ah, al = _split(a)
    bh, bl = _split(b)
    return _dot(ah, bh) + (_dot(ah, bl) + _dot(al, bh))


def _sigmoid(a):
    return 1.0 / (1.0 + jnp.exp(-a))


def _norm_mod(x, g, shift, scale):
    ms = jnp.mean(x * x, axis=-1, keepdims=True)
    return (x * lax.rsqrt(ms + NORM_EPS) * g) * (1.0 + scale) + shift


def _tile(n, pref):
    t = min(n, pref)
    assert n % t == 0, (n, pref)
    return t


def _full(shape):
    nd = len(shape)
    return pl.BlockSpec(shape, lambda *_: (0,) * nd)


def _ada_kernel(c_ref, w_ref, b_ref, o_ref):
    c = c_ref[...]
    o_ref[0] = _dot3(c * _sigmoid(c), w_ref[0]) + b_ref[0]


def _ada(cvec, w_ada, b_ada):
    depth, d, n6 = w_ada.shape
    tn = _tile(n6, ADA_COL_TILE)
    return pl.pallas_call(
        _ada_kernel,
        grid=(depth, n6 // tn),
        in_specs=[
            _full((MOD_ROWS, d)),
            pl.BlockSpec((1, d, tn), lambda i, j: (i, 0, j)),
            pl.BlockSpec((1, 1, tn), lambda i, j: (i, 0, j)),
        ],
        out_specs=pl.BlockSpec((1, MOD_ROWS, tn), lambda i, j: (i, 0, j)),
        out_shape=jax.ShapeDtypeStruct((depth, MOD_ROWS, n6), F32),
        compiler_params=_params("parallel", "parallel"),
        name="ada",
    )(cvec, w_ada, b_ada.reshape(depth, 1, n6))


def _pack_pairs(hh):
    half = hh.shape[1] // 2
    lo = lax.shift_right_logical(pltpu.bitcast(hh[:, :half].astype(F32), jnp.uint32), jnp.uint32(BF16_BITS))
    hi = pltpu.bitcast(hh[:, half:].astype(F32), jnp.uint32) & jnp.uint32(HIGH_HALF)
    return pltpu.bitcast(lo | hi, jnp.int32)


def _unpack_pairs(words):
    u = pltpu.bitcast(words, jnp.uint32)
    lo = pltpu.bitcast(lax.shift_left(u, jnp.uint32(BF16_BITS)), F32).astype(BF)
    hi = pltpu.bitcast(u & jnp.uint32(HIGH_HALF), F32).astype(BF)
    return jnp.concatenate([lo, hi], axis=1)


def _residual_router(x, y, mod, g2_ref, wrh_ref, wrl_ref, xo_ref, hn_ref, aff_ref):
    xn = x + mod[2:3] * y
    xo_ref[...] = xn
    hn = _norm_mod(xn, g2_ref[...], mod[3:4], mod[4:5])
    hh, hl = _split(hn)
    hn_ref[...] = _pack_pairs(hh) if hn_ref.dtype == jnp.int32 else hh
    e = wrh_ref.shape[0]
    both = _dot_nt(jnp.concatenate([wrh_ref[...], wrl_ref[...]], axis=0), hh)
    logits = both[:e] + (_dot_nt(wrh_ref[...], hl) + both[e:])
    p = jnp.exp(logits - jnp.max(logits, axis=0, keepdims=True))
    aff_ref[...] = p / jnp.sum(p, axis=0, keepdims=True)


def _router_specs(n, d, e, tm, packed):
    in_specs = [_full((1, d)), _full((e, d)), _full((e, d))]
    hn_cols, hn_dtype = (d // 2, jnp.int32) if packed else (d, BF)
    out_specs = [
        pl.BlockSpec((tm, d), lambda i: (i, 0)),
        pl.BlockSpec((tm, hn_cols), lambda i: (i, 0)),
        pl.BlockSpec((e, tm), lambda i: (0, i)),
    ]
    out_shape = [
        jax.ShapeDtypeStruct((n, d), F32),
        jax.ShapeDtypeStruct((n, hn_cols), hn_dtype),
        jax.ShapeDtypeStruct((e, n), F32),
    ]
    return in_specs, out_specs, out_shape


def _halo_specs(n, d, tm):
    per = tm // HALO
    last = n // HALO - 1
    return [
        pl.BlockSpec((HALO, d), lambda i: (jnp.maximum(i * per - 1, 0), 0)),
        pl.BlockSpec((tm, d), lambda i: (i, 0)),
        pl.BlockSpec((HALO, d), lambda i: (jnp.minimum((i + 1) * per, last), 0)),
    ]


def _mod_spec(d, tiles_per_seq, ctx_row):
    if ctx_row is None:
        return pl.BlockSpec((1, 6, d), lambda i: (i // tiles_per_seq, 0, 0))
    return pl.BlockSpec((1, 6, d), lambda i: (ctx_row, 0, 0))


def _halo_rows(xp_ref, x_ref, xn_ref, g_ref, mod, tiles_per_seq):
    t = pl.program_id(0) % tiles_per_seq
    g = g_ref[...]
    hp, hx, hn = [_norm_mod(r[...], g, mod[0:1], mod[1:2]) for r in (xp_ref, x_ref, xn_ref)]
    hp = jnp.where(t == 0, 0.0, hp)
    hn = jnp.where(t == tiles_per_seq - 1, 0.0, hn)
    return jnp.concatenate([hp, hx, hn], axis=0).astype(BF)


def _conv3(s, w_ref, tm, seq_rows=None):
    n = s.shape[0]
    prev = pltpu.roll(s, 1, 0)[HALO:HALO + tm]
    nxt = pltpu.roll(s, n - 1, 0)[HALO:HALO + tm]
    if seq_rows is not None and seq_rows < tm:
        within = lax.broadcasted_iota(jnp.int32, (tm, 1), 0) % seq_rows
        prev = jnp.where(within == 0, 0.0, prev)
        nxt = jnp.where(within == seq_rows - 1, 0.0, nxt)
    return prev * w_ref[0:1, :] + s[HALO:HALO + tm] * w_ref[1:2, :] + nxt * w_ref[2:3, :]


def _stacked_weight_spec(w_all, j):
    return pl.BlockSpec((1,) + w_all.shape[1:], lambda i: (j, 0, 0), pipeline_mode=pl.Buffered(1))


def _cast_weight_once(w_ref, w_scr):
    @pl.when(pl.program_id(0) == 0)
    def _():
        w_scr[...] = w_ref[0].astype(BF)


def _shortconv_kernel(xp_ref, x_ref, xn_ref, mod_ref, g1_ref, win32_ref, wconv_ref, wout_ref,
                      g2_ref, wrh_ref, wrl_ref, xo_ref, hn_ref, aff_ref, win_ref, *, tiles_per_seq, seq_rows):
    tm, d = x_ref.shape
    _cast_weight_once(win32_ref, win_ref)
    mod = mod_ref[0]
    h = _halo_rows(xp_ref, x_ref, xn_ref, g1_ref, mod, tiles_per_seq)
    gate = _dot(h[HALO:HALO + tm], win_ref[:, 0:d])
    s = _dot(h, win_ref[:, d:2 * d]) * _dot(h, win_ref[:, 2 * d:3 * d])
    z = (gate * _conv3(s, wconv_ref, tm, seq_rows)).astype(BF)
    y = _dot(z, wout_ref[...])
    _residual_router(x_ref[...], y, mod, g2_ref, wrh_ref, wrl_ref, xo_ref, hn_ref, aff_ref)


def _shortconv_layer(xs, mods, g1, w_in_all, j, w_conv, w_out, g2, wrh, wrl, seq, ctx_row=None):
    n, d = xs.shape
    e = wrh.shape[0]
    if ctx_row is not None and seq < PROJ_ROW_TILE:
        tm = _tile(n, PROJ_ROW_TILE // seq * seq)
        tps = 1
    else:
        tm = _tile(seq, PROJ_ROW_TILE)
        tps = seq // tm
    r_in, r_out, r_shape = _router_specs(n, d, e, tm, packed=ctx_row is None)
    return pl.pallas_call(
        functools.partial(_shortconv_kernel, tiles_per_seq=tps, seq_rows=seq),
        grid=(n // tm,),
        in_specs=_halo_specs(n, d, tm) + [
            _mod_spec(d, tps, ctx_row), _full((1, d)), _stacked_weight_spec(w_in_all, j), _full((3, d)), _full((d, d)),
        ] + r_in,
        out_specs=r_out,
        out_shape=r_shape,
        scratch_shapes=[pltpu.VMEM((d, 3 * d), BF)],
        compiler_params=_params("arbitrary"),
        name="shortconv_layer",
    )(xs, xs, xs, mods, g1, w_in_all, w_conv, w_out, g2, wrh, wrl)


def _outproj_kernel(o_ref, x_ref, mod_ref, wout_ref, g2_ref, wrh_ref, wrl_ref, xo_ref, hn_ref, aff_ref):
    y = _dot(o_ref[...], wout_ref[...])
    _residual_router(x_ref[...], y, mod_ref[0], g2_ref, wrh_ref, wrl_ref, xo_ref, hn_ref, aff_ref)


def _outproj_layer(o, xs, mods, w_out, g2, wrh, wrl, seq):
    n, d = xs.shape
    e = wrh.shape[0]
    tm = _tile(seq, PROJ_ROW_TILE)
    tps = seq // tm
    r_in, r_out, r_shape = _router_specs(n, d, e, tm, packed=True)
    return pl.pallas_call(
        _outproj_kernel,
        grid=(n // tm,),
        in_specs=[pl.BlockSpec((tm, d), lambda i: (i, 0)), pl.BlockSpec((tm, d), lambda i: (i, 0)),
                  _mod_spec(d, tps, None), _full((d, d))] + r_in,
        out_specs=r_out,
        out_shape=r_shape,
        compiler_params=_params("parallel"),
        name="outproj_layer",
    )(o, xs, mods, w_out, g2, wrh, wrl)


def _excl_cumsum_lanes(m):
    rows, t = m.shape
    a = lax.broadcasted_iota(jnp.int32, (V7X_LANES, V7X_LANES), 0)
    b = lax.broadcasted_iota(jnp.int32, (V7X_LANES, V7X_LANES), 1)
    tri = jnp.where(a < b, 1.0, 0.0).astype(BF)
    carry = jnp.zeros((rows, 1), F32)
    out = []
    for c in range(t // V7X_LANES):
        blk = m[:, c * V7X_LANES:(c + 1) * V7X_LANES]
        out.append(_dot(blk.astype(BF), tri) + carry)
        carry = carry + jnp.sum(blk, axis=1, keepdims=True)
    return jnp.concatenate(out, axis=1)


def _route_kernel(aff_ref, posm_ref, gate_ref, *, cap, nb):
    e = aff_ref.shape[0]
    t = aff_ref.shape[1] // nb
    aff = jnp.concatenate([aff_ref[:, b * t:(b + 1) * t] for b in range(nb)], axis=0)
    bits = pltpu.bitcast(aff, jnp.int32)

    def step(i, thr):
        cand = thr | jnp.left_shift(jnp.int32(1), F32_MAGNITUDE_BITS - 1 - i)
        cnt = jnp.sum(jnp.where(bits >= cand, 1.0, 0.0), axis=1, keepdims=True)
        return jnp.where(cnt >= cap, cand, thr)

    thr = lax.fori_loop(0, F32_MAGNITUDE_BITS, step, jnp.zeros((nb * e, 1), jnp.int32))
    gt = jnp.where(bits > thr, 1.0, 0.0)
    eq = jnp.where(bits == thr, 1.0, 0.0)
    need = cap - jnp.sum(gt, axis=1, keepdims=True)
    sel = gt + eq * jnp.where(_excl_cumsum_lanes(eq) < need, 1.0, 0.0)
    pos = jnp.where(sel > 0.0, _excl_cumsum_lanes(sel), -1.0)
    posm_ref[...] = pos.astype(jnp.int32)
    gate_ref[...] = sel * aff


def _route_by_token_kernel(posm_ref, gate_ref, bt_ref, *, nb):
    e = posm_ref.shape[0] // nb
    t = posm_ref.shape[1]
    pad = jnp.zeros((V7X_LANES - 2 * e, t), F32)
    for b in range(nb):
        rows = slice(b * e, (b + 1) * e)
        bt_ref[b * t:(b + 1) * t, :] = jnp.concatenate(
            [posm_ref[rows, :].astype(F32), gate_ref[rows, :], pad], axis=0).T


def _route(aff, nb, t, cap):
    e = aff.shape[0]
    return pl.pallas_call(
        functools.partial(_route_kernel, cap=cap, nb=nb),
        grid=(1,),
        in_specs=[_full((e, nb * t))],
        out_specs=[_full((nb * e, t)), _full((nb * e, t))],
        out_shape=[jax.ShapeDtypeStruct((nb * e, t), jnp.int32), jax.ShapeDtypeStruct((nb * e, t), F32)],
        compiler_params=_params("arbitrary"),
        name="route",
    )(aff)


def _route_by_token(posm, gate, nb):
    rows, t = posm.shape
    assert 2 * rows // nb <= V7X_LANES
    return pl.pallas_call(
        functools.partial(_route_by_token_kernel, nb=nb),
        grid=(1,),
        in_specs=[_full((rows, t)), _full((rows, t))],
        out_specs=_full((nb * t, V7X_LANES)),
        out_shape=jax.ShapeDtypeStruct((nb * t, V7X_LANES), F32),
        compiler_params=_params("arbitrary"),
        name="route_by_token",
    )(posm, gate)


def _sc_gather(table, posm, e0, nb, n_experts, t, cap):
    from jax.experimental.pallas import tpu_sc as plsc

    info = plsc.get_sparse_core_info()
    cores, lanes = info.num_cores, info.num_lanes
    workers = cores * info.num_subcores
    group = workers // nb
    words = table.shape[1]
    assert workers % nb == 0 and n_experts % group == 0 and cap % SC_GATHER_ROWS == 0 and t % lanes == 0

    @functools.partial(
        pl.kernel,
        mesh=plsc.VectorSubcoreMesh(core_axis_name="core", subcore_axis_name="subcore"),
        compiler_params=pltpu.CompilerParams(needs_layout_passes=False),
        out_type=jax.ShapeDtypeStruct((group * nb * cap, words), jnp.int32),
        scratch_types=[pltpu.VMEM((t,), jnp.int32), pltpu.VMEM((cap,), jnp.int32),
                       pltpu.VMEM((SC_GATHER_ROWS, words), jnp.int32), pltpu.SemaphoreType.DMA],
    )
    def gather(table_hbm, posm_hbm, out_hbm, pos_v, idx_v, rows_v, sem):
        w = lax.axis_index("subcore") * cores + lax.axis_index("core")
        e_local = w // nb
        b = w % nb
        pltpu.sync_copy(posm_hbm.at[b * n_experts + e0 + e_local], pos_v)

        @pl.loop(0, t // lanes)
        def _(i):
            p = pos_v[pl.ds(i * lanes, lanes)]
            token = lax.iota(jnp.int32, lanes) + (i * lanes + b * t)
            plsc.store_scatter(idx_v, [p], token, mask=p >= 0)

        out_base = (e_local * nb + b) * cap

        @pl.loop(0, cap // SC_GATHER_ROWS)
        def _(j):
            pltpu.async_copy(table_hbm.at[idx_v.at[pl.ds(j * SC_GATHER_ROWS, SC_GATHER_ROWS)]], rows_v, sem).wait()
            pltpu.sync_copy(rows_v, out_hbm.at[pl.ds(out_base + j * SC_GATHER_ROWS, SC_GATHER_ROWS)])

    return gather(table, posm).reshape(group, nb * cap, words)


def _sc_group(nb):
    from jax.experimental.pallas import tpu_sc as plsc

    info = plsc.get_sparse_core_info()
    return info.num_cores * info.num_subcores // nb


def _gather_kernel(posm_ref, hn_ref, o_ref, *, cap):
    e = posm_ref.shape[0]
    t, d = hn_ref.shape
    slot = lax.broadcasted_iota(jnp.int32, (cap, t), 0)
    onehot = jnp.concatenate(
        [jnp.where(posm_ref[k:k + 1, :] == slot, 1.0, 0.0).astype(BF) for k in range(e)], axis=0)
    o_ref[...] = _dot(onehot, hn_ref[...]).astype(BF).reshape(e, cap, d)


def _gather(posm, hn, nb, t, cap):
    e = posm.shape[0] // nb
    d = hn.shape[1]
    return pl.pallas_call(
        functools.partial(_gather_kernel, cap=cap),
        grid=(nb,),
        in_specs=[pl.BlockSpec((e, t), lambda b: (b, 0)), pl.BlockSpec((t, d), lambda b: (b, 0))],
        out_specs=pl.BlockSpec((e, cap, d), lambda b: (0, b, 0)),
        out_shape=jax.ShapeDtypeStruct((e, nb * cap, d), BF),
        compiler_params=_params("parallel"),
        name="moe_gather",
    )(posm, hn)


def _expert_kernel(*refs, pieces, per):
    n_in = sum(pieces)
    wg_ref, wu_ref, wd_ref = refs[n_in:n_in + 3]
    y_refs = refs[n_in + 3:n_in + 3 + len(pieces)]
    w_scr = refs[-1]

    @pl.when(pl.program_id(1) == 0)
    def _():
        w_scr[0] = wg_ref[0, 0].astype(BF)
        w_scr[1] = wu_ref[0, 0].astype(BF)
        w_scr[2] = wd_ref[0, 0].astype(BF)

    which = pl.program_id(0) // per

    start = 0
    for n_pieces, y_ref in zip(pieces, y_refs):
        x_refs = refs[start:start + n_pieces]
        start += n_pieces
        rows = x_refs[0].shape[1]
        tr = _tile(rows, ROW_TILE)

        def body(j, carry, x_refs=x_refs, y_ref=y_ref, tr=tr):
            r0 = pl.multiple_of(j * tr, tr)
            xs = x_refs[0][0, pl.ds(r0, tr), :]
            for p in range(1, len(x_refs)):
                xs = jnp.where(which == p, x_refs[p][0, pl.ds(r0, tr), :], xs)
            if xs.dtype == jnp.int32:
                xs = _unpack_pairs(xs)
            a = _dot(xs, w_scr[0])
            b = _dot(xs, w_scr[1])
            hm = (a * _sigmoid(a) * b).astype(BF)
            y_ref[0, pl.ds(r0, tr), :] = _dot(hm, w_scr[2]).astype(BF)
            return carry

        lax.fori_loop(0, rows // tr, body, 0)


def _experts(streams, w_gate, w_up, w_down, layer, e0, count, per):
    d = w_gate.shape[-2]
    f = w_gate.shape[-1]
    assert f == d
    wspec = pl.BlockSpec((1, 1, d, f), lambda k, j: (layer, e0 + k, 0, 0))
    xspecs, xargs = [], []
    for arrays, first in streams:
        for p, x in enumerate(arrays):
            if len(arrays) == 1:
                index = lambda k, j, first=first: (first + k, j, 0)
            else:
                index = lambda k, j, p=p: (
                    jnp.clip(k - p * per, 0, per - 1),
                    jnp.where(k < p * per, 0, jnp.where(k >= (p + 1) * per, EXPERT_ROW_STEPS - 1, j)), 0)
            xspecs.append(pl.BlockSpec((1, x.shape[1] // EXPERT_ROW_STEPS, x.shape[2]), index))
            xargs.append(x)
    rows = [arrays[0].shape[1] for arrays, _ in streams]
    return pl.pallas_call(
        functools.partial(_expert_kernel, pieces=tuple(len(arrays) for arrays, _ in streams), per=per),
        grid=(count, EXPERT_ROW_STEPS),
        in_specs=xspecs + [wspec, wspec, wspec],
        out_specs=[pl.BlockSpec((1, r // EXPERT_ROW_STEPS, d), lambda k, j: (k, j, 0)) for r in rows],
        out_shape=[jax.ShapeDtypeStruct((count, r, d), BF) for r in rows],
        scratch_shapes=[pltpu.VMEM((3, d, f), BF)],
        compiler_params=_params("parallel", "arbitrary"),
        name="moe_experts",
    )(*xargs, w_gate, w_up, w_down)


def _combine_kernel(bt_ref, *refs, cap, final):
    y_refs = refs[:-4]
    x_ref, mod_ref, fg_ref, o_ref = refs[-4:]
    d = y_refs[0].shape[2]
    e = sum(r.shape[0] for r in y_refs)
    tq = x_ref.shape[0]
    pt = bt_ref[:, 0:e]
    gt = bt_ref[:, e:2 * e]
    if cap % V7X_LANES == 0:
        slot = lax.broadcasted_iota(jnp.int32, (tq, cap), 1).astype(F32)
        pieces = [jnp.where(pt[:, k:k + 1] == slot, gt[:, k:k + 1], 0.0).astype(BF) for k in range(e)]
        scat = jnp.concatenate(pieces, axis=1)
    else:
        slot = lax.broadcasted_iota(jnp.int32, (tq, e * cap), 1).astype(F32)
        scat = jnp.zeros((tq, e * cap), F32)
        for k in range(e):
            pk = pt[:, k:k + 1]
            scat = jnp.where((pk >= 0.0) & (pk + float(k * cap) == slot), gt[:, k:k + 1], scat)
        scat = scat.astype(BF)
    out, col = None, 0
    for y_ref in y_refs:
        width = y_ref.shape[0] * cap
        part = _dot(scat[:, col:col + width], y_ref[...].reshape(width, d))
        out = part if out is None else out + part
        col += width
    xn = x_ref[...] + mod_ref[0][5:6] * out
    if final:
        ms = jnp.mean(xn * xn, axis=-1, keepdims=True)
        xn = xn * lax.rsqrt(ms + NORM_EPS) * fg_ref[...]
    o_ref[...] = xn


def _combine(by_token, ys, xs, mods, fg, nb, t, cap, ctx_row=None, final=False):
    n, d = xs.shape
    tq = _tile(t, COMBINE_ROW_TILE)
    tpb = t // tq
    if ctx_row is None:
        mspec = pl.BlockSpec((1, 6, d), lambda b, i: (b, 0, 0))
    else:
        mspec = pl.BlockSpec((1, 6, d), lambda b, i: (ctx_row, 0, 0))
    return pl.pallas_call(
        functools.partial(_combine_kernel, cap=cap, final=final),
        grid=(nb, tpb),
        in_specs=[
            pl.BlockSpec((tq, V7X_LANES), lambda b, i: (b * tpb + i, 0)),
        ] + [pl.BlockSpec((y.shape[0], cap, d), lambda b, i: (0, b, 0)) for y in ys] + [
            pl.BlockSpec((tq, d), lambda b, i: (b * tpb + i, 0)),
            mspec,
            _full((1, d)),
        ],
        out_specs=pl.BlockSpec((tq, d), lambda b, i: (b * tpb + i, 0)),
        out_shape=jax.ShapeDtypeStruct((n, d), F32),
        compiler_params=_params("parallel", "parallel"),
        name="moe_combine",
    )(by_token, *ys, xs, mods, fg)


def _rope_tables(seq):
    rows = seq // GRID_W
    row = np.repeat(np.arange(rows, dtype=np.float32), GRID_W)
    col = np.tile(np.arange(GRID_W, dtype=np.float32), rows)
    inv_freq = (ROPE_THETA ** (-np.arange(0, ROPE_AXIS_DIM, 2, dtype=np.float32) / ROPE_AXIS_DIM)).astype(np.float32)
    lane = np.arange(2 * DA_HEAD_DIM)
    within = lane % DA_HEAD_DIM
    axis = within // ROPE_AXIS_DIM
    half = (within % ROPE_AXIS_DIM) // (ROPE_AXIS_DIM // 2)
    idx = within % (ROPE_AXIS_DIM // 2)
    pos = np.where(axis[None, :] == 0, row[:, None], col[:, None])
    ang = (pos * inv_freq[idx][None, :]).astype(np.float32)
    cos = np.cos(ang).astype(np.float32)
    sin = np.sin(ang).astype(np.float32)
    sin_lo = np.where(half[None, :] == 1, sin, 0.0).astype(np.float32)
    sin_hi = np.where(half[None, :] == 0, -sin, 0.0).astype(np.float32)
    return cos, sin_lo, sin_hi


def _qkv_kernel(x_ref, mod_ref, g_ref, w32_ref, cos_ref, sa_ref, sb_ref, *refs, rope, first):
    o_refs, w_ref = refs[:-1], refs[-1]
    d = x_ref.shape[1]
    _cast_weight_once(w32_ref, w_ref)
    mod = mod_ref[0]
    h = _norm_mod(x_ref[...], g_ref[...], mod[0:1], mod[1:2]).astype(BF)
    slab = 2 * DA_HEAD_DIM
    shift = ROPE_AXIS_DIM // 2
    for j, o_ref in enumerate(o_refs):
        u = _dot(h, w_ref[:, (first + j) * d:(first + j + 1) * d])
        if rope and j < 2:
            scale = DA_HEAD_DIM ** -0.5 * math.log2(math.e) if j == 0 else 1.0
            cos, sa, sb = cos_ref[...] * scale, sa_ref[...] * scale, sb_ref[...] * scale
            for hd in range(d // slab):
                xs = u[:, hd * slab:(hd + 1) * slab]
                r = xs * cos + pltpu.roll(xs, shift, 1) * sa + pltpu.roll(xs, slab - shift, 1) * sb
                o_ref[:, hd * slab:(hd + 1) * slab] = r.astype(BF)
        else:
            o_ref[...] = u.astype(BF)


def _qkv(xs, mods, g1, w_all, j, tables, seq, nout, rope, ctx_row=None, first=0):
    n, d = xs.shape
    tm = _tile(seq, PROJ_ROW_TILE)
    tps = seq // tm
    slab = 2 * DA_HEAD_DIM
    tspec = pl.BlockSpec((tm, slab), lambda i: (i % tps, 0))
    return pl.pallas_call(
        functools.partial(_qkv_kernel, rope=rope, first=first),
        grid=(n // tm,),
        in_specs=[pl.BlockSpec((tm, d), lambda i: (i, 0)), _mod_spec(d, tps, ctx_row), _full((1, d)),
                  _stacked_weight_spec(w_all, j), tspec, tspec, tspec],
        out_specs=[pl.BlockSpec((tm, d), lambda i: (i, 0))] * nout,
        out_shape=[jax.ShapeDtypeStruct((n, d), BF)] * nout,
        scratch_shapes=[pltpu.VMEM(w_all.shape[1:], BF)],
        compiler_params=_params("arbitrary"),
        name="attn_qkv",
    )(xs, mods, g1, w_all, *tables)


def _attn_kernel(q_ref, k_ref, v_ref, kc_ref, vc_ref, lq1_ref, lk1_ref, lq2_ref, lk2_ref, sg_ref, o_ref,
                 sl_a, sc_a, sl_b, sc_b, va_scr, vca_scr, *, lam_init, tq):
    seq = q_ref.shape[0]
    lam = (jnp.exp(jnp.sum(lq1_ref[...] * lk1_ref[...], axis=1, keepdims=True))
           - jnp.exp(jnp.sum(lq2_ref[...] * lk2_ref[...], axis=1, keepdims=True)) + lam_init)
    slots = ((sl_a, sc_a), (sl_b, sc_b))

    def scores(i, slot):
        sl_ref, sc_ref = slot
        q = q_ref[i * tq:(i + 1) * tq, :]
        lane = lax.broadcasted_iota(jnp.int32, q.shape, 1)
        zero = jnp.zeros_like(q)
        for mp, qm in enumerate((jnp.where(lane < DA_HEAD_DIM, q, zero), jnp.where(lane >= DA_HEAD_DIM, q, zero))):
            sl_ref[mp] = _dot_nt(qm, k_ref[...])
            sc_ref[mp] = _dot_nt(qm, kc_ref[...])

    slab = v_ref.shape[1]
    va_scr[:, 0:slab] = v_ref[...]
    va_scr[:, slab:2 * slab] = jnp.ones_like(v_ref)
    vca_scr[:, 0:slab] = vc_ref[...]
    vca_scr[:, slab:2 * slab] = jnp.ones_like(vc_ref)

    def unnormalised(sl_ref, sc_ref, mp):
        s_l = sl_ref[mp]
        s_c = sc_ref[mp]
        m = jnp.maximum(jnp.max(s_l, axis=1, keepdims=True), jnp.max(s_c, axis=1, keepdims=True))
        p_l = jnp.exp2(s_l - m).astype(BF)
        p_c = jnp.exp2(s_c - m).astype(BF)
        both = _dot(p_l, va_scr[...]) + _dot(p_c, vca_scr[...])
        return both[:, 0:slab], both[:, slab:slab + 1]

    def attend(i, slot):
        o1, t1 = unnormalised(*slot, 0)
        o2, t2 = unnormalised(*slot, 1)
        o = o1 * (1.0 / t1) - o2 * (lam / t2)
        ms = jnp.mean(o * o, axis=-1, keepdims=True)
        o_ref[i * tq:(i + 1) * tq, :] = (o * lax.rsqrt(ms + SUBLN_EPS) * sg_ref[...] * (1.0 - lam_init)).astype(BF)

    n = seq // tq
    scores(0, slots[0])
    for i in range(n):
        if i + 1 < n:
            scores(i + 1, slots[(i + 1) % 2])
        attend(i, slots[i % 2])


def _attention(q, k, v, kc, vc, lq1, lk1, lq2, lk2, sg, nb, seq, ctx_len, lam_init):
    n, d = q.shape
    slab = 2 * DA_HEAD_DIM
    heads = d // slab
    tq = _tile(seq, ATTN_Q_TILE)
    small = _full((1, DA_HEAD_DIM))
    lat = pl.BlockSpec((seq, slab), lambda b, h: (b, h))
    ctx = pl.BlockSpec((ctx_len, slab), lambda b, h: (b, h))
    score_scratch = [pltpu.VMEM((2, tq, seq), F32), pltpu.VMEM((2, tq, ctx_len), F32)]
    return pl.pallas_call(
        functools.partial(_attn_kernel, lam_init=lam_init, tq=tq),
        grid=(nb, heads),
        in_specs=[lat, lat, lat, ctx, ctx, small, small, small, small, _full((1, slab))],
        out_specs=lat,
        out_shape=jax.ShapeDtypeStruct((n, d), BF),
        scratch_shapes=score_scratch + score_scratch + [pltpu.VMEM((seq, 2 * slab), BF),
                                                        pltpu.VMEM((ctx_len, 2 * slab), BF)],
        compiler_params=_params("parallel", "parallel"),
        name="diff_attention",
    )(q, k, v, kc, vc, lq1, lk1, lq2, lk2, sg)


def _hyena_in_kernel(xp_ref, x_ref, xn_ref, mod_ref, g1_ref, win32_ref, wconv_ref, o_ref, win_ref, *, tiles_per_seq):
    tm, d = x_ref.shape
    _cast_weight_once(win32_ref, win_ref)
    h = _halo_rows(xp_ref, x_ref, xn_ref, g1_ref, mod_ref[0], tiles_per_seq)
    for j in range(3):
        u = _dot(h, win_ref[:, j * d:(j + 1) * d])
        o_ref[:, j * d:(j + 1) * d] = _conv3(u, wconv_ref.at[:, j * d:(j + 1) * d], tm)


def _hyena_in(xs, mods, g1, w_in_all, j, w_conv, seq):
    n, d = xs.shape
    tm = _tile(seq, PROJ_ROW_TILE)
    tps = seq // tm
    return pl.pallas_call(
        functools.partial(_hyena_in_kernel, tiles_per_seq=tps),
        grid=(n // tm,),
        in_specs=_halo_specs(n, d, tm) + [_mod_spec(d, tps, None), _full((1, d)), _stacked_weight_spec(w_in_all, j),
                                          _full((3, 3 * d))],
        out_specs=pl.BlockSpec((tm, 3 * d), lambda i: (i, 0)),
        out_shape=jax.ShapeDtypeStruct((n, 3 * d), F32),
        scratch_shapes=[pltpu.VMEM((d, 3 * d), BF)],
        compiler_params=_params("arbitrary"),
        name="hyena_in",
    )(xs, xs, xs, mods, g1, w_in_all, w_conv)


def _dft_half_shift(seq):
    order = np.concatenate([np.arange(seq // 2), seq - 1 - np.arange(seq // 2)])
    half = np.pi * (order + 0.5) / (2 * seq)
    return np.stack([np.cos(half), np.sin(half)], axis=1).astype(np.float32)


def _dft_half_tables(seq):
    h = seq // 2
    th = 2.0 * np.pi * (np.arange(h, dtype=np.float64) + 0.5) / seq
    s = np.arange(h, dtype=np.float64)
    even, odd = np.outer(th, s + 0.25), np.outer(th, s + 0.75)
    fwd = np.stack([np.cos(even), np.sin(even), np.cos(odd), np.sin(odd)]).astype(np.float32)
    inv = np.ascontiguousarray(np.transpose(fwd, (0, 2, 1)))
    return fwd, inv


def _hyena_features(seq):
    t = np.linspace(0.0, 1.0, seq, dtype=np.float32)[:, None]
    w = (2.0 * math.pi * np.arange(seq, dtype=np.float32)[:, None] / seq).astype(np.float32)
    f = np.linspace(1e-4, HY_BANDS - 1, HY_BANDS, dtype=np.float32)[None, :]
    z = np.concatenate([t, np.cos(f * w), -np.sin(f * w)], axis=-1).astype(np.float32)
    zp = np.zeros((seq, V7X_LANES), np.float32)
    zp[:, :HY_EMB_DIM] = z
    return zp


def _hyena_deltas(d):
    max_decay = math.log(HY_DECAY_TARGET) / HY_FAST_DECAY_PCT
    min_decay = math.log(HY_DECAY_TARGET) / HY_SLOW_DECAY_PCT
    return np.abs(np.linspace(min_decay, max_decay, d, dtype=np.float32))[None, :].astype(np.float32)


def _hyena_filter_kernel(z_ref, f1w_ref, f1b_ref, f1f_ref, f2w_ref, f2b_ref, f2f_ref,
                         f3a_ref, f3b_ref, f3c_ref, f3d_ref, delta_ref, fwd_ref, rot_ref, k_ref, hid_scr, *pm_scrs):
    seq = z_ref.shape[0]
    half = seq // 2
    tc = delta_ref.shape[1]
    even = pl.ds(0, half, stride=2)
    odd = pl.ds(1, half, stride=2)

    @pl.when(pl.program_id(0) == 0)
    def _():
        h1 = jnp.sin(f1f_ref[...] * (_dot3(z_ref[...], f1w_ref[...]) + f1b_ref[...]))
        hid_scr[...] = jnp.sin(f2f_ref[...] * (_dot3(h1, f2w_ref[...]) + f2b_ref[...]))

    hid = hid_scr[...]
    decay = jnp.exp(-z_ref[:, 0:1] * delta_ref[...])
    row = lax.broadcasted_iota(jnp.int32, (seq, 1), 0)
    cr = rot_ref[:, 0:1]
    sr = rot_ref[:, 1:2]
    for order, (f3_fwd_ref, f3_bwd_ref) in enumerate(((f3a_ref, f3b_ref), (f3c_ref, f3d_ref))):
        h_fwd = _dot3(hid, f3_fwd_ref[...]) * decay
        h_bwd = jnp.where(row == 0, 0.0, _dot3(hid, f3_bwd_ref[...]) * decay)
        pm = jnp.concatenate([h_fwd + h_bwd, h_bwd - h_fwd], axis=1)
        for k, scr in enumerate(pm_scrs):
            scr[...] = pm[:, k * V7X_LANES:(k + 1) * V7X_LANES]
        x_even = jnp.concatenate([scr[even, :] for scr in pm_scrs], axis=1).astype(BF)
        x_odd = jnp.concatenate([scr[odd, :] for scr in pm_scrs], axis=1).astype(BF)
        a = _dot(fwd_ref[0], x_even)
        b = _dot(fwd_ref[1], x_even)
        c = _dot(fwd_ref[2], x_odd)
        s = _dot(fwd_ref[3], x_odd)
        cos_sum = jnp.concatenate([a + c, b - s], axis=0)
        sin_sum = jnp.concatenate([b + s, a - c], axis=0)
        k_ref[2 * order] = cr * cos_sum[:, :tc] + sr * sin_sum[:, :tc]
        k_ref[2 * order + 1] = cr * sin_sum[:, tc:] - sr * cos_sum[:, tc:]


def _hyena_filter(z, f1w, f1b, f1f, f2w, f2b, f2f, f3w, deltas, fwd, rot, d):
    seq = z.shape[0]
    half = seq // 2
    w = f2w.shape[0]
    tc = _tile(d, CHAN_TILE)
    nc = d // tc
    f3spec = [pl.BlockSpec((w, tc), lambda j, o=o: (0, o * nc + j)) for o in range(4)]
    return pl.pallas_call(
        _hyena_filter_kernel,
        grid=(nc,),
        in_specs=[_full((seq, V7X_LANES)), _full((V7X_LANES, w)), _full((1, w)), _full((1, w)),
                  _full((w, w)), _full((1, w)), _full((1, w))] + f3spec + [
            pl.BlockSpec((1, tc), lambda j: (0, j)),
            pl.BlockSpec((4, half, half), lambda j: (0, 0, 0), pipeline_mode=pl.Buffered(1)),
            _full((seq, 2)),
        ],
        out_specs=pl.BlockSpec((4, seq, tc), lambda j: (0, 0, j)),
        out_shape=jax.ShapeDtypeStruct((4, seq, d), F32),
        scratch_shapes=[pltpu.VMEM((seq, w), F32)] + [pltpu.VMEM((seq, V7X_LANES), F32)] * (2 * tc // V7X_LANES),
        compiler_params=_params("arbitrary"),
        name="hyena_filter",
    )(z, f1w, f1b, f1f, f2w, f2b, f2f, f3w, f3w, f3w, f3w, deltas, fwd, rot)


def _hyena_conv_kernel(*refs, gt, pieces):
    x1_refs, x2_refs, v_refs = refs[:pieces], refs[pieces:2 * pieces], refs[2 * pieces:3 * pieces]
    fwd_ref, inv_ref, k_ref, skip_ref, o_ref = refs[3 * pieces:3 * pieces + 5]
    ue_scr, uo_scr, ze_scr, zo_scr, ye_scr, yo_scr = refs[3 * pieces + 5:3 * pieces + 11]
    out_scrs = refs[3 * pieces + 11:]
    seq = o_ref.shape[0]
    half = seq // 2
    even = pl.ds(0, half, stride=2)
    odd = pl.ds(1, half, stride=2)

    def samples(piece_refs, rows):
        return jnp.concatenate([r[rows, :] for r in piece_refs], axis=1)

    def longconv(order):
        ye_scr[...] = jnp.zeros_like(ye_scr)
        yo_scr[...] = jnp.zeros_like(yo_scr)

        def body(c, carry):
            g0 = pl.multiple_of(c * gt, gt)
            rows = pl.ds(g0, gt)
            ue = ue_scr[...]
            uo = uo_scr[...]
            a = _dot(fwd_ref[0, rows, :], ue)
            b = _dot(fwd_ref[1, rows, :], ue)
            cc = _dot(fwd_ref[2, rows, :], uo)
            d = _dot(fwd_ref[3, rows, :], uo)

            def times_filter(first, ur, ui):
                kr = k_ref[2 * order, pl.ds(first + g0, gt), :]
                ki = k_ref[2 * order + 1, pl.ds(first + g0, gt), :]
                return kr * ur + ki * ui, kr * ui - ki * ur

            yra, yia = times_filter(0, a + cc, b + d)
            yrb, yib = times_filter(half, b - d, a - cc)
            ye_scr[...] += (_dot(inv_ref[0, :, rows], (yra + yib).astype(BF))
                            + _dot(inv_ref[1, :, rows], (yia + yrb).astype(BF)))
            yo_scr[...] += (_dot(inv_ref[2, :, rows], (yra - yib).astype(BF))
                            + _dot(inv_ref[3, :, rows], (yia - yrb).astype(BF)))
            return carry

        lax.fori_loop(0, half // gt, body, 0)

    scale = 1.0 / seq
    ve = samples(v_refs, even)
    vo = samples(v_refs, odd)
    ue_scr[...] = ve.astype(BF)
    uo_scr[...] = vo.astype(BF)
    longconv(0)
    ze = samples(x1_refs, even) * (ye_scr[...] * scale + ve * skip_ref[0:1, :])
    zo = samples(x1_refs, odd) * (yo_scr[...] * scale + vo * skip_ref[0:1, :])
    ze_scr[...] = ze
    zo_scr[...] = zo
    ue_scr[...] = ze.astype(BF)
    uo_scr[...] = zo.astype(BF)
    longconv(1)
    oe = samples(x2_refs, even) * (ye_scr[...] * scale + ze_scr[...] * skip_ref[1:2, :])
    oo = samples(x2_refs, odd) * (yo_scr[...] * scale + zo_scr[...] * skip_ref[1:2, :])
    for k, out_scr in enumerate(out_scrs):
        lanes = slice(k * V7X_LANES, (k + 1) * V7X_LANES)
        out_scr[even, :] = oe[:, lanes]
        out_scr[odd, :] = oo[:, lanes]
        o_ref[:, lanes] = out_scr[...].astype(BF)


def _hyena_conv(u3, fwd, inv, kspec, skip, nb, seq, d):
    tc = _tile(d, CHAN_TILE)
    nc = d // tc
    half = seq // 2
    gt = _tile(half, FREQ_TILE)
    pieces = tc // V7X_LANES
    once = dict(pipeline_mode=pl.Buffered(1))
    piece_specs = [pl.BlockSpec((seq, V7X_LANES), lambda j, b, o=o, k=k: (b, (o * nc + j) * pieces + k))
                   for o in range(3) for k in range(pieces)]
    return pl.pallas_call(
        functools.partial(_hyena_conv_kernel, gt=gt, pieces=pieces),
        scratch_shapes=[pltpu.VMEM((half, tc), BF), pltpu.VMEM((half, tc), BF)]
        + [pltpu.VMEM((half, tc), F32)] * 4 + [pltpu.VMEM((seq, V7X_LANES), F32)] * pieces,
        grid=(nc, nb),
        in_specs=piece_specs + [
            pl.BlockSpec((4, half, half), lambda j, b: (0, 0, 0), **once),
            pl.BlockSpec((4, half, half), lambda j, b: (0, 0, 0), **once),
            pl.BlockSpec((4, seq, tc), lambda j, b: (0, 0, j), **once),
            pl.BlockSpec((2, tc), lambda j, b: (0, j)),
        ],
        out_specs=pl.BlockSpec((seq, tc), lambda j, b: (b, j)),
        out_shape=jax.ShapeDtypeStruct((nb * seq, d), BF),
        compiler_params=_params("parallel", "parallel"),
        name="hyena_conv",
    )(*([u3] * (3 * pieces)), fwd, inv, kspec, skip)


def _moe_block(streams, mods, layer, fg, w_gate, w_up, w_down, nb, final=False):
    n_experts = streams[0][2].shape[0]
    group = _sc_group(nb)
    routed = []
    for xs, hn, aff, t, ctx_row in streams:
        cap = max(1, EC_CAPACITY * t // n_experts)
        posm, gate = _route(aff, nb, t, cap)
        by_token = _route_by_token(posm, gate, nb)
        xin = None if hn.dtype == jnp.int32 else _gather(posm, hn, nb, t, cap)
        routed.append((posm, by_token, cap, xin))
    ys = [[] for _ in streams]
    for e0, count in ((0, group), (group, n_experts - group)):
        xins = []
        for (xs, hn, aff, t, ctx_row), (posm, by_token, cap, xin) in zip(streams, routed):
            if xin is None:
                xins.append(([_sc_gather(hn, posm, e, nb, n_experts, t, cap) for e in range(e0, e0 + count, group)], 0))
            else:
                xins.append(([xin], e0))
        for acc, y in zip(ys, _experts(xins, w_gate, w_up, w_down, layer, e0, count, group)):
            acc.append(y)
    return [_combine(by_token, y, xs, mods, fg, nb, t, cap, ctx_row=ctx_row, final=final and ctx_row is None)
            for (xs, _, _, t, ctx_row), (_, by_token, cap, _), y in zip(streams, routed, ys)]


def kernel(x, c, ctx, c_ctx, w_ada, b_ada, norm1_g, norm2_g, final_g, a_w_in, a_conv, a_w_out, b_w_qkv, b_lq1, b_lk1, b_lq2, b_lk2, b_subln_g, b_w_out, c_w_in, c_conv, c_f1_w, c_f1_b, c_f1_freq, c_f2_w, c_f2_b, c_f2_freq, c_f3_w, c_skip, c_w_out, moe_router, moe_w_gate, moe_w_up, moe_w_down):
    nb, seq, d = x.shape
    ctx_len = ctx.shape[1]
    depth = w_ada.shape[0]
    assert nb < MOD_ROWS and d % (2 * DA_HEAD_DIM) == 0
    ctx_row = nb

    cvec = jnp.concatenate([c, c_ctx[None, :], jnp.zeros((MOD_ROWS - nb - 1, d), F32)], axis=0)
    mods_all = _ada(cvec, w_ada, b_ada).reshape(depth, MOD_ROWS, 6, d)

    attn_layers = [i for i in range(depth) if i % N_MIXERS == 1]
    last_ctx_read = max(attn_layers) if attn_layers else -1

    xs = x.reshape(nb * seq, d)
    cs_tok = ctx.reshape(nb * ctx_len, d)
    fg = final_g[None, :]
    wrh_all, wrl_all = _split(jnp.swapaxes(moe_router, 1, 2))

    for i in range(depth):
        kind, j = i % N_MIXERS, i // N_MIXERS
        update_ctx = i < last_ctx_read
        final = i == depth - 1
        mods = mods_all[i]
        g1 = norm1_g[i][None, :]
        g2 = norm2_g[i][None, :]
        wrh, wrl = wrh_all[i], wrl_all[i]
        moe_w = (moe_w_gate, moe_w_up, moe_w_down)

        ctx_stream = []
        if kind == 0:
            w_out = a_w_out[j].astype(BF)
            if update_ctx:
                cn, chn, caff = _shortconv_layer(cs_tok, mods, g1, a_w_in, j, a_conv[j], w_out, g2, wrh, wrl,
                                                 ctx_len, ctx_row=ctx_row)
                ctx_stream = [(cn, chn, caff, ctx_len, ctx_row)]
            xn, hn, aff = _shortconv_layer(xs, mods, g1, a_w_in, j, a_conv[j], w_out, g2, wrh, wrl, seq)
        elif kind == 1:
            assert not update_ctx
            lam_init = 0.8 - 0.6 * math.exp(-0.3 * i)
            tables = [jnp.asarray(t) for t in _rope_tables(seq)]
            q, k, v = _qkv(xs, mods, g1, b_w_qkv, j, tables, seq, 3, True)
            kc, vc = _qkv(cs_tok, mods, g1, b_w_qkv, j, tables, min(seq, nb * ctx_len), 2, False, ctx_row=ctx_row,
                          first=1)
            o = _attention(q, k, v, kc, vc, b_lq1[j][None, :], b_lk1[j][None, :], b_lq2[j][None, :],
                           b_lk2[j][None, :], b_subln_g[j][None, :], nb, seq, ctx_len, lam_init)
            xn, hn, aff = _outproj_layer(o, xs, mods, b_w_out[j].astype(BF), g2, wrh, wrl, seq)
        else:
            assert not update_ctx
            fwd_np, inv_np = _dft_half_tables(seq)
            fwd, inv = jnp.asarray(fwd_np).astype(BF), jnp.asarray(inv_np).astype(BF)
            w = c_f2_w.shape[-1]
            f1w = jnp.zeros((V7X_LANES, w), F32).at[:HY_EMB_DIM].set(c_f1_w[j])
            kspec = _hyena_filter(jnp.asarray(_hyena_features(seq)), f1w, c_f1_b[j][None, :], c_f1_freq[j][None, :],
                                  c_f2_w[j], c_f2_b[j][None, :], c_f2_freq[j][None, :], c_f3_w[j],
                                  jnp.asarray(_hyena_deltas(d)), fwd, jnp.asarray(_dft_half_shift(seq)), d)
            u3 = _hyena_in(xs, mods, g1, c_w_in, j, c_conv[j], seq)
            z = _hyena_conv(u3, fwd, inv, kspec, c_skip[j], nb, seq, d)
            xn, hn, aff = _outproj_layer(z, xs, mods, c_w_out[j].astype(BF), g2, wrh, wrl, seq)

        outs = _moe_block([(xn, hn, aff, seq, None)] + ctx_stream, mods, i, fg, *moe_w, nb, final=final)
        xs = outs[0]
        if ctx_stream:
            cs_tok = outs[1]

    return xs.reshape(nb, seq, d)
```

```python
import functools
import math

import jax
import jax.numpy as jnp
import numpy as np
from jax import lax
from jax.experimental import pallas as pl
from jax.experimental.pallas import tpu as pltpu

BF = jnp.bfloat16
F32 = jnp.float32

GRID_W = 64
DA_HEAD_DIM = 64
ROPE_AXIS_DIM = DA_HEAD_DIM // 2
ROPE_THETA = 10000.0
SUBLN_EPS = 1e-5
NORM_EPS = 1e-6
N_MIXERS = 3
EC_CAPACITY = 2
HY_EMB_DIM = 33
HY_BANDS = (HY_EMB_DIM - 1) // 2
HY_FAST_DECAY_PCT = 0.3
HY_SLOW_DECAY_PCT = 1.5
HY_DECAY_TARGET = 1e-2

V7X_LANES = 128
V7X_BF16_SUBLANES = 16
V7X_VMEM_BYTES = 64 * 2**20
VMEM_COMPILER_RESERVE = 8 * 2**20
VMEM_LIMIT = V7X_VMEM_BYTES - VMEM_COMPILER_RESERVE
F32_MAGNITUDE_BITS = 31
BF16_BITS = 16
HIGH_HALF = 0xFFFF0000
HALO = V7X_BF16_SUBLANES
MOD_ROWS = 16
ROW_TILE = 512
COMBINE_ROW_TILE = 1024
EXPERT_ROW_STEPS = 2
PROJ_ROW_TILE = 1024
ATTN_Q_TILE = 256
CHAN_TILE = 256
ADA_COL_TILE = 1536
FREQ_TILE = 1024
SC_GATHER_ROWS = 64


def _params(*sem):
    return pltpu.CompilerParams(dimension_semantics=sem, vmem_limit_bytes=VMEM_LIMIT)


def _dot(a, b):
    return jnp.dot(a, b, preferred_element_type=F32)


def _dot_nt(a, b):
    return lax.dot_general(a, b, (((1,), (1,)), ((), ())), preferred_element_type=F32)


def _split(a):
    hi = a.astype(BF)
    lo = (a - hi.astype(F32)).astype(BF)
    return hi, lo


def _dot3(a, b):
    ah, al = _split(a)
    bh, bl = _split(b)
    return _dot(ah, bh) + (_dot(ah, bl) + _dot(al, bh))


def _sigmoid(a):
    return 1.0 / (1.0 + jnp.exp(-a))


def _norm_mod(x, g, shift, scale):
    ms = jnp.mean(x * x, axis=-1, keepdims=True)
    return (x * lax.rsqrt(ms + NORM_EPS) * g) * (1.0 + scale) + shift


def _tile(n, pref):
    t = min(n, pref)
    assert n % t == 0, (n, pref)
    return t


def _full(shape):
    nd = len(shape)
    return pl.BlockSpec(shape, lambda *_: (0,) * nd)


def _ada_kernel(c_ref, w_ref, b_ref, o_ref):
    c = c_ref[...]
    o_ref[0] = _dot3(c * _sigmoid(c), w_ref[0]) + b_ref[0]


def _ada(cvec, w_ada, b_ada):
    depth, d, n6 = w_ada.shape
    tn = _tile(n6, ADA_COL_TILE)
    return pl.pallas_call(
        _ada_kernel,
        grid=(depth, n6 // tn),
        in_specs=[
            _full((MOD_ROWS, d)),
            pl.BlockSpec((1, d, tn), lambda i, j: (i, 0, j)),
            pl.BlockSpec((1, 1, tn), lambda i, j: (i, 0, j)),
        ],
        out_specs=pl.BlockSpec((1, MOD_ROWS, tn), lambda i, j: (i, 0, j)),
        out_shape=jax.ShapeDtypeStruct((depth, MOD_ROWS, n6), F32),
        compiler_params=_params("parallel", "parallel"),
        name="ada",
    )(cvec, w_ada, b_ada.reshape(depth, 1, n6))


def _pack_pairs(hh):
    half = hh.shape[1] // 2
    lo = lax.shift_right_logical(pltpu.bitcast(hh[:, :half].astype(F32), jnp.uint32), jnp.uint32(BF16_BITS))
    hi = pltpu.bitcast(hh[:, half:].astype(F32), jnp.uint32) & jnp.uint32(HIGH_HALF)
    return pltpu.bitcast(lo | hi, jnp.int32)


def _unpack_pairs(words):
    u = pltpu.bitcast(words, jnp.uint32)
    lo = pltpu.bitcast(lax.shift_left(u, jnp.uint32(BF16_BITS)), F32).astype(BF)
    hi = pltpu.bitcast(u & jnp.uint32(HIGH_HALF), F32).astype(BF)
    return jnp.concatenate([lo, hi], axis=1)


def _residual_router(x, y, mod, g2_ref, wrh_ref, wrl_ref, xo_ref, hn_ref, aff_ref):
    xn = x + mod[2:3] * y
    xo_ref[...] = xn
    hn = _norm_mod(xn, g2_ref[...], mod[3:4], mod[4:5])
    hh, hl = _split(hn)
    hn_ref[...] = _pack_pairs(hh) if hn_ref.dtype == jnp.int32 else hh
    e = wrh_ref.shape[0]
    both = _dot_nt(jnp.concatenate([wrh_ref[...], wrl_ref[...]], axis=0), hh)
    logits = both[:e] + (_dot_nt(wrh_ref[...], hl) + both[e:])
    p = jnp.exp(logits - jnp.max(logits, axis=0, keepdims=True))
    aff_ref[...] = p / jnp.sum(p, axis=0, keepdims=True)


def _router_specs(n, d, e, tm, packed):
    in_specs = [_full((1, d)), _full((e, d)), _full((e, d))]
    hn_cols, hn_dtype = (d // 2, jnp.int32) if packed else (d, BF)
    out_specs = [
        pl.BlockSpec((tm, d), lambda i: (i, 0)),
        pl.BlockSpec((tm, hn_cols), lambda i: (i, 0)),
        pl.BlockSpec((e, tm), lambda i: (0, i)),
    ]
    out_shape = [
        jax.ShapeDtypeStruct((n, d), F32),
        jax.ShapeDtypeStruct((n, hn_cols), hn_dtype),
        jax.ShapeDtypeStruct((e, n), F32),
    ]
    return in_specs, out_specs, out_shape


def _halo_specs(n, d, tm):
    per = tm // HALO
    last = n // HALO - 1
    return [
        pl.BlockSpec((HALO, d), lambda i: (jnp.maximum(i * per - 1, 0), 0)),
        pl.BlockSpec((tm, d), lambda i: (i, 0)),
        pl.BlockSpec((HALO, d), lambda i: (jnp.minimum((i + 1) * per, last), 0)),
    ]


def _mod_spec(d, tiles_per_seq, ctx_row):
    if ctx_row is None:
        return pl.BlockSpec((1, 6, d), lambda i: (i // tiles_per_seq, 0, 0))
    return pl.BlockSpec((1, 6, d), lambda i: (ctx_row, 0, 0))


def _halo_rows(xp_ref, x_ref, xn_ref, g_ref, mod, tiles_per_seq):
    t = pl.program_id(0) % tiles_per_seq
    g = g_ref[...]
    hp, hx, hn = [_norm_mod(r[...], g, mod[0:1], mod[1:2]) for r in (xp_ref, x_ref, xn_ref)]
    hp = jnp.where(t == 0, 0.0, hp)
    hn = jnp.where(t == tiles_per_seq - 1, 0.0, hn)
    return jnp.concatenate([hp, hx, hn], axis=0).astype(BF)


def _conv3(s, w_ref, tm, seq_rows=None):
    n = s.shape[0]
    prev = pltpu.roll(s, 1, 0)[HALO:HALO + tm]
    nxt = pltpu.roll(s, n - 1, 0)[HALO:HALO + tm]
    if seq_rows is not None and seq_rows < tm:
        within = lax.broadcasted_iota(jnp.int32, (tm, 1), 0) % seq_rows
        prev = jnp.where(within == 0, 0.0, prev)
        nxt = jnp.where(within == seq_rows - 1, 0.0, nxt)
    return prev * w_ref[0:1, :] + s[HALO:HALO + tm] * w_ref[1:2, :] + nxt * w_ref[2:3, :]


def _stacked_weight_spec(w_all, j):
    return pl.BlockSpec((1,) + w_all.shape[1:], lambda i: (j, 0, 0), pipeline_mode=pl.Buffered(1))


def _cast_weight_once(w_ref, w_scr):
    @pl.when(pl.program_id(0) == 0)
    def _():
        w_scr[...] = w_ref[0].astype(BF)


def _shortconv_kernel(xp_ref, x_ref, xn_ref, mod_ref, g1_ref, win32_ref, wconv_ref, wout32_ref,
                      g2_ref, wrh_ref, wrl_ref, xo_ref, hn_ref, aff_ref, win_ref, wout_ref, *, tiles_per_seq, seq_rows):
    tm, d = x_ref.shape
    _cast_weight_once(win32_ref, win_ref)
    _cast_weight_once(wout32_ref, wout_ref)
    mod = mod_ref[0]
    h = _halo_rows(xp_ref, x_ref, xn_ref, g1_ref, mod, tiles_per_seq)
    gate = _dot(h[HALO:HALO + tm], win_ref[:, 0:d])
    s = _dot(h, win_ref[:, d:2 * d]) * _dot(h, win_ref[:, 2 * d:3 * d])
    z = (gate * _conv3(s, wconv_ref, tm, seq_rows)).astype(BF)
    y = _dot(z, wout_ref[...])
    _residual_router(x_ref[...], y, mod, g2_ref, wrh_ref, wrl_ref, xo_ref, hn_ref, aff_ref)


def _shortconv_layer(xs, mods, g1, w_in_all, j, w_conv, w_out_all, g2, wrh, wrl, seq, ctx_row=None):
    n, d = xs.shape
    e = wrh.shape[0]
    if ctx_row is not None and seq < PROJ_ROW_TILE:
        tm = _tile(n, PROJ_ROW_TILE // seq * seq)
        tps = 1
    else:
        tm = _tile(seq, PROJ_ROW_TILE)
        tps = seq // tm
    r_in, r_out, r_shape = _router_specs(n, d, e, tm, packed=ctx_row is None)
    return pl.pallas_call(
        functools.partial(_shortconv_kernel, tiles_per_seq=tps, seq_rows=seq),
        grid=(n // tm,),
        in_specs=_halo_specs(n, d, tm) + [
            _mod_spec(d, tps, ctx_row), _full((1, d)), _stacked_weight_spec(w_in_all, j), _full((3, d)),
            _stacked_weight_spec(w_out_all, j),
        ] + r_in,
        out_specs=r_out,
        out_shape=r_shape,
        scratch_shapes=[pltpu.VMEM((d, 3 * d), BF), pltpu.VMEM((d, d), BF)],
        compiler_params=_params("arbitrary"),
        name="shortconv_layer",
    )(xs, xs, xs, mods, g1, w_in_all, w_conv, w_out_all, g2, wrh, wrl)


def _outproj_kernel(o_ref, x_ref, mod_ref, wout32_ref, g2_ref, wrh_ref, wrl_ref, xo_ref, hn_ref, aff_ref, wout_ref):
    _cast_weight_once(wout32_ref, wout_ref)
    y = _dot(o_ref[...], wout_ref[...])
    _residual_router(x_ref[...], y, mod_ref[0], g2_ref, wrh_ref, wrl_ref, xo_ref, hn_ref, aff_ref)


def _outproj_layer(o, xs, mods, w_out_all, j, g2, wrh, wrl, seq):
    n, d = xs.shape
    e = wrh.shape[0]
    tm = _tile(seq, PROJ_ROW_TILE)
    tps = seq // tm
    r_in, r_out, r_shape = _router_specs(n, d, e, tm, packed=True)
    return pl.pallas_call(
        _outproj_kernel,
        grid=(n // tm,),
        in_specs=[pl.BlockSpec((tm, d), lambda i: (i, 0)), pl.BlockSpec((tm, d), lambda i: (i, 0)),
                  _mod_spec(d, tps, None), _stacked_weight_spec(w_out_all, j)] + r_in,
        out_specs=r_out,
        out_shape=r_shape,
        scratch_shapes=[pltpu.VMEM((d, d), BF)],
        compiler_params=_params("arbitrary"),
        name="outproj_layer",
    )(o, xs, mods, w_out_all, g2, wrh, wrl)


def _excl_cumsum_lanes(m):
    rows, t = m.shape
    a = lax.broadcasted_iota(jnp.int32, (V7X_LANES, V7X_LANES), 0)
    b = lax.broadcasted_iota(jnp.int32, (V7X_LANES, V7X_LANES), 1)
    tri = jnp.where(a < b, 1.0, 0.0).astype(BF)
    carry = jnp.zeros((rows, 1), F32)
    out = []
    for c in range(t // V7X_LANES):
        blk = m[:, c * V7X_LANES:(c + 1) * V7X_LANES]
        out.append(_dot(blk.astype(BF), tri) + carry)
        carry = carry + jnp.sum(blk, axis=1, keepdims=True)
    return jnp.concatenate(out, axis=1)


def _route_kernel(aff_ref, posm_ref, gate_ref, *, cap, nb):
    e = aff_ref.shape[0]
    t = aff_ref.shape[1] // nb
    aff = jnp.concatenate([aff_ref[:, b * t:(b + 1) * t] for b in range(nb)], axis=0)
    bits = pltpu.bitcast(aff, jnp.int32)

    def step(i, thr):
        cand = thr | jnp.left_shift(jnp.int32(1), F32_MAGNITUDE_BITS - 1 - i)
        cnt = jnp.sum(jnp.where(bits >= cand, 1.0, 0.0), axis=1, keepdims=True)
        return jnp.where(cnt >= cap, cand, thr)

    thr = lax.fori_loop(0, F32_MAGNITUDE_BITS, step, jnp.zeros((nb * e, 1), jnp.int32))
    gt = jnp.where(bits > thr, 1.0, 0.0)
    eq = jnp.where(bits == thr, 1.0, 0.0)
    need = cap - jnp.sum(gt, axis=1, keepdims=True)
    sel = gt + eq * jnp.where(_excl_cumsum_lanes(eq) < need, 1.0, 0.0)
    pos = jnp.where(sel > 0.0, _excl_cumsum_lanes(sel), -1.0)
    posm_ref[...] = pos.astype(jnp.int32)
    gate_ref[...] = sel * aff


def _route_by_token_kernel(posm_ref, gate_ref, bt_ref, *, nb):
    e = posm_ref.shape[0] // nb
    t = posm_ref.shape[1]
    pad = jnp.zeros((V7X_LANES - 2 * e, t), F32)
    for b in range(nb):
        rows = slice(b * e, (b + 1) * e)
        bt_ref[b * t:(b + 1) * t, :] = jnp.concatenate(
            [posm_ref[rows, :].astype(F32), gate_ref[rows, :], pad], axis=0).T


def _route(aff, nb, t, cap):
    e = aff.shape[0]
    return pl.pallas_call(
        functools.partial(_route_kernel, cap=cap, nb=nb),
        grid=(1,),
        in_specs=[_full((e, nb * t))],
        out_specs=[_full((nb * e, t)), _full((nb * e, t))],
        out_shape=[jax.ShapeDtypeStruct((nb * e, t), jnp.int32), jax.ShapeDtypeStruct((nb * e, t), F32)],
        compiler_params=_params("arbitrary"),
        name="route",
    )(aff)


def _route_by_token(posm, gate, nb):
    rows, t = posm.shape
    assert 2 * rows // nb <= V7X_LANES
    return pl.pallas_call(
        functools.partial(_route_by_token_kernel, nb=nb),
        grid=(1,),
        in_specs=[_full((rows, t)), _full((rows, t))],
        out_specs=_full((nb * t, V7X_LANES)),
        out_shape=jax.ShapeDtypeStruct((nb * t, V7X_LANES), F32),
        compiler_params=_params("arbitrary"),
        name="route_by_token",
    )(posm, gate)


def _sc_gather(table, posm, e0, nb, n_experts, t, cap):
    from jax.experimental.pallas import tpu_sc as plsc

    info = plsc.get_sparse_core_info()
    cores, lanes = info.num_cores, info.num_lanes
    workers = cores * info.num_subcores
    group = workers // nb
    words = table.shape[1]
    assert workers % nb == 0 and n_experts % group == 0 and cap % SC_GATHER_ROWS == 0 and t % lanes == 0

    @functools.partial(
        pl.kernel,
        mesh=plsc.VectorSubcoreMesh(core_axis_name="core", subcore_axis_name="subcore"),
        compiler_params=pltpu.CompilerParams(needs_layout_passes=False),
        out_type=jax.ShapeDtypeStruct((group * nb * cap, words), jnp.int32),
        scratch_types=[pltpu.VMEM((t,), jnp.int32), pltpu.VMEM((cap,), jnp.int32),
                       pltpu.VMEM((SC_GATHER_ROWS, words), jnp.int32), pltpu.SemaphoreType.DMA],
    )
    def gather(table_hbm, posm_hbm, out_hbm, pos_v, idx_v, rows_v, sem):
        w = lax.axis_index("subcore") * cores + lax.axis_index("core")
        e_local = w // nb
        b = w % nb
        pltpu.sync_copy(posm_hbm.at[b * n_experts + e0 + e_local], pos_v)

        @pl.loop(0, t // lanes)
        def _(i):
            p = pos_v[pl.ds(i * lanes, lanes)]
            token = lax.iota(jnp.int32, lanes) + (i * lanes + b * t)
            plsc.store_scatter(idx_v, [p], token, mask=p >= 0)

        out_base = (e_local * nb + b) * cap

        @pl.loop(0, cap // SC_GATHER_ROWS)
        def _(j):
            pltpu.async_copy(table_hbm.at[idx_v.at[pl.ds(j * SC_GATHER_ROWS, SC_GATHER_ROWS)]], rows_v, sem).wait()
            pltpu.sync_copy(rows_v, out_hbm.at[pl.ds(out_base + j * SC_GATHER_ROWS, SC_GATHER_ROWS)])

    return gather(table, posm).reshape(group, nb * cap, words)


def _sc_group(nb):
    from jax.experimental.pallas import tpu_sc as plsc

    info = plsc.get_sparse_core_info()
    return info.num_cores * info.num_subcores // nb


def _gather_kernel(posm_ref, hn_ref, o_ref, *, cap):
    e = posm_ref.shape[0]
    t, d = hn_ref.shape
    slot = lax.broadcasted_iota(jnp.int32, (cap, t), 0)
    onehot = jnp.concatenate(
        [jnp.where(posm_ref[k:k + 1, :] == slot, 1.0, 0.0).astype(BF) for k in range(e)], axis=0)
    o_ref[...] = _dot(onehot, hn_ref[...]).astype(BF).reshape(e, cap, d)


def _gather(posm, hn, nb, t, cap):
    e = posm.shape[0] // nb
    d = hn.shape[1]
    return pl.pallas_call(
        functools.partial(_gather_kernel, cap=cap),
        grid=(nb,),
        in_specs=[pl.BlockSpec((e, t), lambda b: (b, 0)), pl.BlockSpec((t, d), lambda b: (b, 0))],
        out_specs=pl.BlockSpec((e, cap, d), lambda b: (0, b, 0)),
        out_shape=jax.ShapeDtypeStruct((e, nb * cap, d), BF),
        compiler_params=_params("parallel"),
        name="moe_gather",
    )(posm, hn)


def _expert_kernel(*refs, pieces, per):
    n_in = sum(pieces)
    wg_ref, wu_ref, wd_ref = refs[n_in:n_in + 3]
    y_refs = refs[n_in + 3:n_in + 3 + len(pieces)]
    w_scr = refs[-1]

    @pl.when(pl.program_id(1) == 0)
    def _():
        w_scr[0] = wg_ref[0, 0].astype(BF)
        w_scr[1] = wu_ref[0, 0].astype(BF)
        w_scr[2] = wd_ref[0, 0].astype(BF)

    which = pl.program_id(0) // per

    start = 0
    for n_pieces, y_ref in zip(pieces, y_refs):
        x_refs = refs[start:start + n_pieces]
        start += n_pieces
        rows = x_refs[0].shape[1]
        tr = _tile(rows, ROW_TILE)

        def body(j, carry, x_refs=x_refs, y_ref=y_ref, tr=tr):
            r0 = pl.multiple_of(j * tr, tr)
            xs = x_refs[0][0, pl.ds(r0, tr), :]
            for p in range(1, len(x_refs)):
                xs = jnp.where(which == p, x_refs[p][0, pl.ds(r0, tr), :], xs)
            if xs.dtype == jnp.int32:
                xs = _unpack_pairs(xs)
            a = _dot(xs, w_scr[0])
            b = _dot(xs, w_scr[1])
            hm = (a * _sigmoid(a) * b).astype(BF)
            y_ref[0, pl.ds(r0, tr), :] = _dot(hm, w_scr[2]).astype(BF)
            return carry

        lax.fori_loop(0, rows // tr, body, 0)


def _experts(streams, w_gate, w_up, w_down, layer, e0, count, per):
    d = w_gate.shape[-2]
    f = w_gate.shape[-1]
    assert f == d
    wspec = pl.BlockSpec((1, 1, d, f), lambda k, j: (layer, e0 + k, 0, 0))
    xspecs, xargs = [], []
    for arrays, first in streams:
        for p, x in enumerate(arrays):
            if len(arrays) == 1:
                index = lambda k, j, first=first: (first + k, j, 0)
            else:
                index = lambda k, j, p=p: (
                    jnp.clip(k - p * per, 0, per - 1),
                    jnp.where(k < p * per, 0, jnp.where(k >= (p + 1) * per, EXPERT_ROW_STEPS - 1, j)), 0)
            xspecs.append(pl.BlockSpec((1, x.shape[1] // EXPERT_ROW_STEPS, x.shape[2]), index))
            xargs.append(x)
    rows = [arrays[0].shape[1] for arrays, _ in streams]
    return pl.pallas_call(
        functools.partial(_expert_kernel, pieces=tuple(len(arrays) for arrays, _ in streams), per=per),
        grid=(count, EXPERT_ROW_STEPS),
        in_specs=xspecs + [wspec, wspec, wspec],
        out_specs=[pl.BlockSpec((1, r // EXPERT_ROW_STEPS, d), lambda k, j: (k, j, 0)) for r in rows],
        out_shape=[jax.ShapeDtypeStruct((count, r, d), BF) for r in rows],
        scratch_shapes=[pltpu.VMEM((3, d, f), BF)],
        compiler_params=_params("parallel", "arbitrary"),
        name="moe_experts",
    )(*xargs, w_gate, w_up, w_down)


def _combine_kernel(bt_ref, *refs, cap, final):
    y_refs = refs[:-4]
    x_ref, mod_ref, fg_ref, o_ref = refs[-4:]
    d = y_refs[0].shape[2]
    e = sum(r.shape[0] for r in y_refs)
    tq = x_ref.shape[0]
    pt = bt_ref[:, 0:e]
    gt = bt_ref[:, e:2 * e]
    if cap % V7X_LANES == 0:
        slot = lax.broadcasted_iota(jnp.int32, (tq, cap), 1).astype(F32)
        pieces = [jnp.where(pt[:, k:k + 1] == slot, gt[:, k:k + 1], 0.0).astype(BF) for k in range(e)]
        scat = jnp.concatenate(pieces, axis=1)
    else:
        slot = lax.broadcasted_iota(jnp.int32, (tq, e * cap), 1).astype(F32)
        scat = jnp.zeros((tq, e * cap), F32)
        for k in range(e):
            pk = pt[:, k:k + 1]
            scat = jnp.where((pk >= 0.0) & (pk + float(k * cap) == slot), gt[:, k:k + 1], scat)
        scat = scat.astype(BF)
    out, col = None, 0
    for y_ref in y_refs:
        width = y_ref.shape[0] * cap
        part = _dot(scat[:, col:col + width], y_ref[...].reshape(width, d))
        out = part if out is None else out + part
        col += width
    xn = x_ref[...] + mod_ref[0][5:6] * out
    if final:
        ms = jnp.mean(xn * xn, axis=-1, keepdims=True)
        xn = xn * lax.rsqrt(ms + NORM_EPS) * fg_ref[...]
    o_ref[...] = xn


def _combine(by_token, ys, xs, mods, fg, nb, t, cap, ctx_row=None, final=False):
    n, d = xs.shape
    tq = _tile(t, COMBINE_ROW_TILE)
    tpb = t // tq
    if ctx_row is None:
        mspec = pl.BlockSpec((1, 6, d), lambda b, i: (b, 0, 0))
    else:
        mspec = pl.BlockSpec((1, 6, d), lambda b, i: (ctx_row, 0, 0))
    return pl.pallas_call(
        functools.partial(_combine_kernel, cap=cap, final=final),
        grid=(nb, tpb),
        in_specs=[
            pl.BlockSpec((tq, V7X_LANES), lambda b, i: (b * tpb + i, 0)),
        ] + [pl.BlockSpec((y.shape[0], cap, d), lambda b, i: (0, b, 0)) for y in ys] + [
            pl.BlockSpec((tq, d), lambda b, i: (b * tpb + i, 0)),
            mspec,
            _full((1, d)),
        ],
        out_specs=pl.BlockSpec((tq, d), lambda b, i: (b * tpb + i, 0)),
        out_shape=jax.ShapeDtypeStruct((n, d), F32),
        compiler_params=_params("parallel", "parallel"),
        name="moe_combine",
    )(by_token, *ys, xs, mods, fg)


def _rope_tables(seq):
    rows = seq // GRID_W
    row = np.repeat(np.arange(rows, dtype=np.float32), GRID_W)
    col = np.tile(np.arange(GRID_W, dtype=np.float32), rows)
    inv_freq = (ROPE_THETA ** (-np.arange(0, ROPE_AXIS_DIM, 2, dtype=np.float32) / ROPE_AXIS_DIM)).astype(np.float32)
    lane = np.arange(2 * DA_HEAD_DIM)
    within = lane % DA_HEAD_DIM
    axis = within // ROPE_AXIS_DIM
    half = (within % ROPE_AXIS_DIM) // (ROPE_AXIS_DIM // 2)
    idx = within % (ROPE_AXIS_DIM // 2)
    pos = np.where(axis[None, :] == 0, row[:, None], col[:, None])
    ang = (pos * inv_freq[idx][None, :]).astype(np.float32)
    cos = np.cos(ang).astype(np.float32)
    sin = np.sin(ang).astype(np.float32)
    sin_lo = np.where(half[None, :] == 1, sin, 0.0).astype(np.float32)
    sin_hi = np.where(half[None, :] == 0, -sin, 0.0).astype(np.float32)
    return cos, sin_lo, sin_hi


def _qkv_kernel(x_ref, mod_ref, g_ref, w32_ref, cos_ref, sa_ref, sb_ref, *refs, rope, first):
    o_refs, w_ref = refs[:-1], refs[-1]
    d = x_ref.shape[1]
    _cast_weight_once(w32_ref, w_ref)
    mod = mod_ref[0]
    h = _norm_mod(x_ref[...], g_ref[...], mod[0:1], mod[1:2]).astype(BF)
    slab = 2 * DA_HEAD_DIM
    shift = ROPE_AXIS_DIM // 2
    for j, o_ref in enumerate(o_refs):
        u = _dot(h, w_ref[:, (first + j) * d:(first + j + 1) * d])
        if rope and j < 2:
            scale = DA_HEAD_DIM ** -0.5 * math.log2(math.e) if j == 0 else 1.0
            cos, sa, sb = cos_ref[...] * scale, sa_ref[...] * scale, sb_ref[...] * scale
            for hd in range(d // slab):
                xs = u[:, hd * slab:(hd + 1) * slab]
                r = xs * cos + pltpu.roll(xs, shift, 1) * sa + pltpu.roll(xs, slab - shift, 1) * sb
                o_ref[:, hd * slab:(hd + 1) * slab] = r.astype(BF)
        else:
            o_ref[...] = u.astype(BF)


def _qkv(xs, mods, g1, w_all, j, tables, seq, nout, rope, ctx_row=None, first=0):
    n, d = xs.shape
    tm = _tile(seq, PROJ_ROW_TILE)
    tps = seq // tm
    slab = 2 * DA_HEAD_DIM
    tspec = pl.BlockSpec((tm, slab), lambda i: (i % tps, 0))
    return pl.pallas_call(
        functools.partial(_qkv_kernel, rope=rope, first=first),
        grid=(n // tm,),
        in_specs=[pl.BlockSpec((tm, d), lambda i: (i, 0)), _mod_spec(d, tps, ctx_row), _full((1, d)),
                  _stacked_weight_spec(w_all, j), tspec, tspec, tspec],
        out_specs=[pl.BlockSpec((tm, d), lambda i: (i, 0))] * nout,
        out_shape=[jax.ShapeDtypeStruct((n, d), BF)] * nout,
        scratch_shapes=[pltpu.VMEM(w_all.shape[1:], BF)],
        compiler_params=_params("arbitrary"),
        name="attn_qkv",
    )(xs, mods, g1, w_all, *tables)


def _attn_kernel(q_ref, k_ref, v_ref, kc_ref, vc_ref, lq1_ref, lk1_ref, lq2_ref, lk2_ref, sg_ref, o_ref,
                 sl_a, sc_a, sl_b, sc_b, va_scr, vca_scr, *, lam_init, tq):
    seq = q_ref.shape[0]
    lam = (jnp.exp(jnp.sum(lq1_ref[...] * lk1_ref[...], axis=1, keepdims=True))
           - jnp.exp(jnp.sum(lq2_ref[...] * lk2_ref[...], axis=1, keepdims=True)) + lam_init)
    slots = ((sl_a, sc_a), (sl_b, sc_b))

    def scores(i, slot):
        sl_ref, sc_ref = slot
        q = q_ref[i * tq:(i + 1) * tq, :]
        lane = lax.broadcasted_iota(jnp.int32, q.shape, 1)
        zero = jnp.zeros_like(q)
        for mp, qm in enumerate((jnp.where(lane < DA_HEAD_DIM, q, zero), jnp.where(lane >= DA_HEAD_DIM, q, zero))):
            sl_ref[mp] = _dot_nt(qm, k_ref[...])
            sc_ref[mp] = _dot_nt(qm, kc_ref[...])

    slab = v_ref.shape[1]
    va_scr[:, 0:slab] = v_ref[...]
    va_scr[:, slab:2 * slab] = jnp.ones_like(v_ref)
    vca_scr[:, 0:slab] = vc_ref[...]
    vca_scr[:, slab:2 * slab] = jnp.ones_like(vc_ref)

    def unnormalised(sl_ref, sc_ref, mp):
        s_l = sl_ref[mp]
        s_c = sc_ref[mp]
        m = jnp.maximum(jnp.max(s_l, axis=1, keepdims=True), jnp.max(s_c, axis=1, keepdims=True))
        p_l = jnp.exp2(s_l - m).astype(BF)
        p_c = jnp.exp2(s_c - m).astype(BF)
        both = _dot(p_l, va_scr[...]) + _dot(p_c, vca_scr[...])
        return both[:, 0:slab], both[:, slab:slab + 1]

    def attend(i, slot):
        o1, t1 = unnormalised(*slot, 0)
        o2, t2 = unnormalised(*slot, 1)
        o = o1 * (1.0 / t1) - o2 * (lam / t2)
        ms = jnp.mean(o * o, axis=-1, keepdims=True)
        o_ref[i * tq:(i + 1) * tq, :] = (o * lax.rsqrt(ms + SUBLN_EPS) * sg_ref[...] * (1.0 - lam_init)).astype(BF)

    n = seq // tq
    scores(0, slots[0])
    for i in range(n):
        if i + 1 < n:
            scores(i + 1, slots[(i + 1) % 2])
        attend(i, slots[i % 2])


def _attention(q, k, v, kc, vc, lq1, lk1, lq2, lk2, sg, nb, seq, ctx_len, lam_init):
    n, d = q.shape
    slab = 2 * DA_HEAD_DIM
    heads = d // slab
    tq = _tile(seq, ATTN_Q_TILE)
    small = _full((1, DA_HEAD_DIM))
    lat = pl.BlockSpec((seq, slab), lambda b, h: (b, h))
    ctx = pl.BlockSpec((ctx_len, slab), lambda b, h: (b, h))
    score_scratch = [pltpu.VMEM((2, tq, seq), F32), pltpu.VMEM((2, tq, ctx_len), F32)]
    return pl.pallas_call(
        functools.partial(_attn_kernel, lam_init=lam_init, tq=tq),
        grid=(nb, heads),
        in_specs=[lat, lat, lat, ctx, ctx, small, small, small, small, _full((1, slab))],
        out_specs=lat,
        out_shape=jax.ShapeDtypeStruct((n, d), BF),
        scratch_shapes=score_scratch + score_scratch + [pltpu.VMEM((seq, 2 * slab), BF),
                                                        pltpu.VMEM((ctx_len, 2 * slab), BF)],
        compiler_params=_params("parallel", "parallel"),
        name="diff_attention",
    )(q, k, v, kc, vc, lq1, lk1, lq2, lk2, sg)


def _hyena_in_kernel(xp_ref, x_ref, xn_ref, mod_ref, g1_ref, win32_ref, wconv_ref, o_ref, win_ref, *, tiles_per_seq):
    tm, d = x_ref.shape
    _cast_weight_once(win32_ref, win_ref)
    h = _halo_rows(xp_ref, x_ref, xn_ref, g1_ref, mod_ref[0], tiles_per_seq)
    for j in range(3):
        u = _dot(h, win_ref[:, j * d:(j + 1) * d])
        o_ref[:, j * d:(j + 1) * d] = _conv3(u, wconv_ref.at[:, j * d:(j + 1) * d], tm)


def _hyena_in(xs, mods, g1, w_in_all, j, w_conv, seq):
    n, d = xs.shape
    tm = _tile(seq, PROJ_ROW_TILE)
    tps = seq // tm
    return pl.pallas_call(
        functools.partial(_hyena_in_kernel, tiles_per_seq=tps),
        grid=(n // tm,),
        in_specs=_halo_specs(n, d, tm) + [_mod_spec(d, tps, None), _full((1, d)), _stacked_weight_spec(w_in_all, j),
                                          _full((3, 3 * d))],
        out_specs=pl.BlockSpec((tm, 3 * d), lambda i: (i, 0)),
        out_shape=jax.ShapeDtypeStruct((n, 3 * d), F32),
        scratch_shapes=[pltpu.VMEM((d, 3 * d), BF)],
        compiler_params=_params("arbitrary"),
        name="hyena_in",
    )(xs, xs, xs, mods, g1, w_in_all, w_conv)


def _dft_half_shift(seq):
    order = np.concatenate([np.arange(seq // 2), seq - 1 - np.arange(seq // 2)])
    half = np.pi * (order + 0.5) / (2 * seq)
    return np.stack([np.cos(half), np.sin(half)], axis=1).astype(np.float32)


def _dft_half_tables(seq):
    h = seq // 2
    th = 2.0 * np.pi * (np.arange(h, dtype=np.float64) + 0.5) / seq
    s = np.arange(h, dtype=np.float64)
    even, odd = np.outer(th, s + 0.25), np.outer(th, s + 0.75)
    fwd = np.stack([np.cos(even), np.sin(even), np.cos(odd), np.sin(odd)]).astype(np.float32)
    inv = np.ascontiguousarray(np.transpose(fwd, (0, 2, 1)))
    return fwd, inv


def _hyena_features(seq):
    t = np.linspace(0.0, 1.0, seq, dtype=np.float32)[:, None]
    w = (2.0 * math.pi * np.arange(seq, dtype=np.float32)[:, None] / seq).astype(np.float32)
    f = np.linspace(1e-4, HY_BANDS - 1, HY_BANDS, dtype=np.float32)[None, :]
    z = np.concatenate([t, np.cos(f * w), -np.sin(f * w)], axis=-1).astype(np.float32)
    zp = np.zeros((seq, V7X_LANES), np.float32)
    zp[:, :HY_EMB_DIM] = z
    return zp


def _hyena_deltas(d):
    max_decay = math.log(HY_DECAY_TARGET) / HY_FAST_DECAY_PCT
    min_decay = math.log(HY_DECAY_TARGET) / HY_SLOW_DECAY_PCT
    return np.abs(np.linspace(min_decay, max_decay, d, dtype=np.float32))[None, :].astype(np.float32)


def _hyena_filter_kernel(z_ref, f1w_ref, f1b_ref, f1f_ref, f2w_ref, f2b_ref, f2f_ref,
                         f3a_ref, f3b_ref, f3c_ref, f3d_ref, delta_ref, fwd_ref, rot_ref, k_ref, hid_scr, *pm_scrs):
    seq = z_ref.shape[0]
    half = seq // 2
    tc = delta_ref.shape[1]
    even = pl.ds(0, half, stride=2)
    odd = pl.ds(1, half, stride=2)

    @pl.when(pl.program_id(0) == 0)
    def _():
        h1 = jnp.sin(f1f_ref[...] * (_dot3(z_ref[...], f1w_ref[...]) + f1b_ref[...]))
        hid_scr[...] = jnp.sin(f2f_ref[...] * (_dot3(h1, f2w_ref[...]) + f2b_ref[...]))

    hid = hid_scr[...]
    decay = jnp.exp(-z_ref[:, 0:1] * delta_ref[...])
    row = lax.broadcasted_iota(jnp.int32, (seq, 1), 0)
    cr = rot_ref[:, 0:1]
    sr = rot_ref[:, 1:2]
    for order, (f3_fwd_ref, f3_bwd_ref) in enumerate(((f3a_ref, f3b_ref), (f3c_ref, f3d_ref))):
        h_fwd = _dot3(hid, f3_fwd_ref[...]) * decay
        h_bwd = jnp.where(row == 0, 0.0, _dot3(hid, f3_bwd_ref[...]) * decay)
        pm = jnp.concatenate([h_fwd + h_bwd, h_bwd - h_fwd], axis=1)
        for k, scr in enumerate(pm_scrs):
            scr[...] = pm[:, k * V7X_LANES:(k + 1) * V7X_LANES]
        x_even = jnp.concatenate([scr[even, :] for scr in pm_scrs], axis=1).astype(BF)
        x_odd = jnp.concatenate([scr[odd, :] for scr in pm_scrs], axis=1).astype(BF)
        a = _dot(fwd_ref[0], x_even)
        b = _dot(fwd_ref[1], x_even)
        c = _dot(fwd_ref[2], x_odd)
        s = _dot(fwd_ref[3], x_odd)
        cos_sum = jnp.concatenate([a + c, b - s], axis=0)
        sin_sum = jnp.concatenate([b + s, a - c], axis=0)
        k_ref[2 * order] = cr * cos_sum[:, :tc] + sr * sin_sum[:, :tc]
        k_ref[2 * order + 1] = cr * sin_sum[:, tc:] - sr * cos_sum[:, tc:]


def _hyena_filter(z, f1w, f1b, f1f, f2w, f2b, f2f, f3w, deltas, fwd, rot, d):
    seq = z.shape[0]
    half = seq // 2
    w = f2w.shape[0]
    tc = _tile(d, CHAN_TILE)
    nc = d // tc
    f3spec = [pl.BlockSpec((w, tc), lambda j, o=o: (0, o * nc + j)) for o in range(4)]
    return pl.pallas_call(
        _hyena_filter_kernel,
        grid=(nc,),
        in_specs=[_full((seq, V7X_LANES)), _full((V7X_LANES, w)), _full((1, w)), _full((1, w)),
                  _full((w, w)), _full((1, w)), _full((1, w))] + f3spec + [
            pl.BlockSpec((1, tc), lambda j: (0, j)),
            pl.BlockSpec((4, half, half), lambda j: (0, 0, 0), pipeline_mode=pl.Buffered(1)),
            _full((seq, 2)),
        ],
        out_specs=pl.BlockSpec((4, seq, tc), lambda j: (0, 0, j)),
        out_shape=jax.ShapeDtypeStruct((4, seq, d), F32),
        scratch_shapes=[pltpu.VMEM((seq, w), F32)] + [pltpu.VMEM((seq, V7X_LANES), F32)] * (2 * tc // V7X_LANES),
        compiler_params=_params("arbitrary"),
        name="hyena_filter",
    )(z, f1w, f1b, f1f, f2w, f2b, f2f, f3w, f3w, f3w, f3w, deltas, fwd, rot)


def _hyena_conv_kernel(*refs, gt, pieces):
    x1_refs, x2_refs, v_refs = refs[:pieces], refs[pieces:2 * pieces], refs[2 * pieces:3 * pieces]
    fwd_ref, inv_ref, k_ref, skip_ref, o_ref = refs[3 * pieces:3 * pieces + 5]
    ue_scr, uo_scr, ze_scr, zo_scr, ye_scr, yo_scr = refs[3 * pieces + 5:3 * pieces + 11]
    out_scrs = refs[3 * pieces + 11:]
    seq = o_ref.shape[0]
    half = seq // 2
    even = pl.ds(0, half, stride=2)
    odd = pl.ds(1, half, stride=2)

    def samples(piece_refs, rows):
        return jnp.concatenate([r[rows, :] for r in piece_refs], axis=1)

    def longconv(order):
        ye_scr[...] = jnp.zeros_like(ye_scr)
        yo_scr[...] = jnp.zeros_like(yo_scr)

        def body(c, carry):
            g0 = pl.multiple_of(c * gt, gt)
            rows = pl.ds(g0, gt)
            ue = ue_scr[...]
            uo = uo_scr[...]
            a = _dot(fwd_ref[0, rows, :], ue)
            b = _dot(fwd_ref[1, rows, :], ue)
            cc = _dot(fwd_ref[2, rows, :], uo)
            d = _dot(fwd_ref[3, rows, :], uo)

            def times_filter(first, ur, ui):
                kr = k_ref[2 * order, pl.ds(first + g0, gt), :]
                ki = k_ref[2 * order + 1, pl.ds(first + g0, gt), :]
                return kr * ur + ki * ui, kr * ui - ki * ur

            yra, yia = times_filter(0, a + cc, b + d)
            yrb, yib = times_filter(half, b - d, a - cc)
            ye_scr[...] += (_dot(inv_ref[0, :, rows], (yra + yib).astype(BF))
                            + _dot(inv_ref[1, :, rows], (yia + yrb).astype(BF)))
            yo_scr[...] += (_dot(inv_ref[2, :, rows], (yra - yib).astype(BF))
                            + _dot(inv_ref[3, :, rows], (yia - yrb).astype(BF)))
            return carry

        lax.fori_loop(0, half // gt, body, 0)

    scale = 1.0 / seq
    ve = samples(v_refs, even)
    vo = samples(v_refs, odd)
    ue_scr[...] = ve.astype(BF)
    uo_scr[...] = vo.astype(BF)
    longconv(0)
    ze = samples(x1_refs, even) * (ye_scr[...] * scale + ve * skip_ref[0:1, :])
    zo = samples(x1_refs, odd) * (yo_scr[...] * scale + vo * skip_ref[0:1, :])
    ze_scr[...] = ze
    zo_scr[...] = zo
    ue_scr[...] = ze.astype(BF)
    uo_scr[...] = zo.astype(BF)
    longconv(1)
    oe = samples(x2_refs, even) * (ye_scr[...] * scale + ze_scr[...] * skip_ref[1:2, :])
    oo = samples(x2_refs, odd) * (yo_scr[...] * scale + zo_scr[...] * skip_ref[1:2, :])
    for k, out_scr in enumerate(out_scrs):
        lanes = slice(k * V7X_LANES, (k + 1) * V7X_LANES)
        out_scr[even, :] = oe[:, lanes]
        out_scr[odd, :] = oo[:, lanes]
        o_ref[:, lanes] = out_scr[...].astype(BF)


def _hyena_conv(u3, fwd, inv, kspec, skip, nb, seq, d):
    tc = _tile(d, CHAN_TILE)
    nc = d // tc
    half = seq // 2
    gt = _tile(half, FREQ_TILE)
    pieces = tc // V7X_LANES
    once = dict(pipeline_mode=pl.Buffered(1))
    piece_specs = [pl.BlockSpec((seq, V7X_LANES), lambda j, b, o=o, k=k: (b, (o * nc + j) * pieces + k))
                   for o in range(3) for k in range(pieces)]
    return pl.pallas_call(
        functools.partial(_hyena_conv_kernel, gt=gt, pieces=pieces),
        scratch_shapes=[pltpu.VMEM((half, tc), BF), pltpu.VMEM((half, tc), BF)]
        + [pltpu.VMEM((half, tc), F32)] * 4 + [pltpu.VMEM((seq, V7X_LANES), F32)] * pieces,
        grid=(nc, nb),
        in_specs=piece_specs + [
            pl.BlockSpec((4, half, half), lambda j, b: (0, 0, 0), **once),
            pl.BlockSpec((4, half, half), lambda j, b: (0, 0, 0), **once),
            pl.BlockSpec((4, seq, tc), lambda j, b: (0, 0, j), **once),
            pl.BlockSpec((2, tc), lambda j, b: (0, j)),
        ],
        out_specs=pl.BlockSpec((seq, tc), lambda j, b: (b, j)),
        out_shape=jax.ShapeDtypeStruct((nb * seq, d), BF),
        compiler_params=_params("parallel", "parallel"),
        name="hyena_conv",
    )(*([u3] * (3 * pieces)), fwd, inv, kspec, skip)


def _moe_block(streams, mods, layer, fg, w_gate, w_up, w_down, nb, final=False):
    n_experts = streams[0][2].shape[0]
    group = _sc_group(nb)
    routed = []
    for xs, hn, aff, t, ctx_row in streams:
        cap = max(1, EC_CAPACITY * t // n_experts)
        posm, gate = _route(aff, nb, t, cap)
        by_token = _route_by_token(posm, gate, nb)
        xin = None if hn.dtype == jnp.int32 else _gather(posm, hn, nb, t, cap)
        routed.append((posm, by_token, cap, xin))
    ys = [[] for _ in streams]
    for e0, count in ((0, group), (group, n_experts - group)):
        xins = []
        for (xs, hn, aff, t, ctx_row), (posm, by_token, cap, xin) in zip(streams, routed):
            if xin is None:
                xins.append(([_sc_gather(hn, posm, e, nb, n_experts, t, cap) for e in range(e0, e0 + count, group)], 0))
            else:
                xins.append(([xin], e0))
        for acc, y in zip(ys, _experts(xins, w_gate, w_up, w_down, layer, e0, count, group)):
            acc.append(y)
    return [_combine(by_token, y, xs, mods, fg, nb, t, cap, ctx_row=ctx_row, final=final and ctx_row is None)
            for (xs, _, _, t, ctx_row), (_, by_token, cap, _), y in zip(streams, routed, ys)]


def kernel(x, c, ctx, c_ctx, w_ada, b_ada, norm1_g, norm2_g, final_g, a_w_in, a_conv, a_w_out, b_w_qkv, b_lq1, b_lk1, b_lq2, b_lk2, b_subln_g, b_w_out, c_w_in, c_conv, c_f1_w, c_f1_b, c_f1_freq, c_f2_w, c_f2_b, c_f2_freq, c_f3_w, c_skip, c_w_out, moe_router, moe_w_gate, moe_w_up, moe_w_down):
    nb, seq, d = x.shape
    ctx_len = ctx.shape[1]
    depth = w_ada.shape[0]
    assert nb < MOD_ROWS and d % (2 * DA_HEAD_DIM) == 0
    ctx_row = nb

    cvec = jnp.concatenate([c, c_ctx[None, :], jnp.zeros((MOD_ROWS - nb - 1, d), F32)], axis=0)
    mods_all = _ada(cvec, w_ada, b_ada).reshape(depth, MOD_ROWS, 6, d)

    attn_layers = [i for i in range(depth) if i % N_MIXERS == 1]
    last_ctx_read = max(attn_layers) if attn_layers else -1

    xs = x.reshape(nb * seq, d)
    cs_tok = ctx.reshape(nb * ctx_len, d)
    fg = final_g[None, :]
    wrh_all, wrl_all = _split(jnp.swapaxes(moe_router, 1, 2))

    for i in range(depth):
        kind, j = i % N_MIXERS, i // N_MIXERS
        update_ctx = i < last_ctx_read
        final = i == depth - 1
        mods = mods_all[i]
        g1 = norm1_g[i][None, :]
        g2 = norm2_g[i][None, :]
        wrh, wrl = wrh_all[i], wrl_all[i]
        moe_w = (moe_w_gate, moe_w_up, moe_w_down)

        ctx_stream = []
        if kind == 0:
            if update_ctx:
                cn, chn, caff = _shortconv_layer(cs_tok, mods, g1, a_w_in, j, a_conv[j], a_w_out, g2, wrh, wrl,
                                                 ctx_len, ctx_row=ctx_row)
                ctx_stream = [(cn, chn, caff, ctx_len, ctx_row)]
            xn, hn, aff = _shortconv_layer(xs, mods, g1, a_w_in, j, a_conv[j], a_w_out, g2, wrh, wrl, seq)
        elif kind == 1:
            assert not update_ctx
            lam_init = 0.8 - 0.6 * math.exp(-0.3 * i)
            tables = [jnp.asarray(t) for t in _rope_tables(seq)]
            q, k, v = _qkv(xs, mods, g1, b_w_qkv, j, tables, seq, 3, True)
            kc, vc = _qkv(cs_tok, mods, g1, b_w_qkv, j, tables, min(seq, nb * ctx_len), 2, False, ctx_row=ctx_row,
                          first=1)
            o = _attention(q, k, v, kc, vc, b_lq1[j][None, :], b_lk1[j][None, :], b_lq2[j][None, :],
                           b_lk2[j][None, :], b_subln_g[j][None, :], nb, seq, ctx_len, lam_init)
            xn, hn, aff = _outproj_layer(o, xs, mods, b_w_out, j, g2, wrh, wrl, seq)
        else:
            assert not update_ctx
            fwd_np, inv_np = _dft_half_tables(seq)
            fwd, inv = jnp.asarray(fwd_np).astype(BF), jnp.asarray(inv_np).astype(BF)
            w = c_f2_w.shape[-1]
            f1w = jnp.zeros((V7X_LANES, w), F32).at[:HY_EMB_DIM].set(c_f1_w[j])
            kspec = _hyena_filter(jnp.asarray(_hyena_features(seq)), f1w, c_f1_b[j][None, :], c_f1_freq[j][None, :],
                                  c_f2_w[j], c_f2_b[j][None, :], c_f2_freq[j][None, :], c_f3_w[j],
                                  jnp.asarray(_hyena_deltas(d)), fwd, jnp.asarray(_dft_half_shift(seq)), d)
            u3 = _hyena_in(xs, mods, g1, c_w_in, j, c_conv[j], seq)
            z = _hyena_conv(u3, fwd, inv, kspec, c_skip[j], nb, seq, d)
            xn, hn, aff = _outproj_layer(z, xs, mods, c_w_out, j, g2, wrh, wrl, seq)

        outs = _moe_block([(xn, hn, aff, seq, None)] + ctx_stream, mods, i, fg, *moe_w, nb, final=final)
        xs = outs[0]
        if ctx_stream:
            cs_tok = outs[1]

    return xs.reshape(nb, seq, d)
```

```python
import functools
import math

import jax
import jax.numpy as jnp
import numpy as np
from jax import lax
from jax.experimental import pallas as pl
from jax.experimental.pallas import tpu as pltpu

BF = jnp.bfloat16
F32 = jnp.float32

GRID_W = 64
DA_HEAD_DIM = 64
ROPE_AXIS_DIM = DA_HEAD_DIM // 2
ROPE_THETA = 10000.0
SUBLN_EPS = 1e-5
NORM_EPS = 1e-6
N_MIXERS = 3
EC_CAPACITY = 2
HY_EMB_DIM = 33
HY_BANDS = (HY_EMB_DIM - 1) // 2
HY_FAST_DECAY_PCT = 0.3
HY_SLOW_DECAY_PCT = 1.5
HY_DECAY_TARGET = 1e-2

V7X_LANES = 128
V7X_BF16_SUBLANES = 16
V7X_VMEM_BYTES = 64 * 2**20
VMEM_COMPILER_RESERVE = 8 * 2**20
VMEM_LIMIT = V7X_VMEM_BYTES - VMEM_COMPILER_RESERVE
F32_MAGNITUDE_BITS = 31
BF16_BITS = 16
HIGH_HALF = 0xFFFF0000
HALO = V7X_BF16_SUBLANES
MOD_ROWS = 16
ROW_TILE = 512
COMBINE_ROW_TILE = 1024
EXPERT_ROW_STEPS = 2
PROJ_ROW_TILE = 1024
ATTN_Q_TILE = 256
CHAN_TILE = 256
ADA_COL_TILE = 1536
FREQ_TILE = 1024
SC_GATHER_ROWS = 64


def _params(*sem):
    return pltpu.CompilerParams(dimension_semantics=sem, vmem_limit_bytes=VMEM_LIMIT)


def _dot(a, b):
    return jnp.dot(a, b, preferred_element_type=F32)


def _dot_nt(a, b):
    return lax.dot_general(a, b, (((1,), (1,)), ((), ())), preferred_element_type=F32)


def _split(a):
    hi = a.astype(BF)
    lo = (a - hi.astype(F32)).astype(BF)
    return hi, lo


def _dot3(a, b):
    ah, al = _split(a)
    bh, bl = _split(b)
    return _dot(ah, bh) + (_dot(ah, bl) + _dot(al, bh))


def _sigmoid(a):
    return 1.0 / (1.0 + jnp.exp(-a))


def _norm_mod(x, g, shift, scale):
    ms = jnp.mean(x * x, axis=-1, keepdims=True)
    return (x * lax.rsqrt(ms + NORM_EPS) * g) * (1.0 + scale) + shift


def _tile(n, pref):
    t = min(n, pref)
    assert n % t == 0, (n, pref)
    return t


def _full(shape):
    nd = len(shape)
    return pl.BlockSpec(shape, lambda *_: (0,) * nd)


def _ada_kernel(c_ref, w_ref, b_ref, o_ref):
    c = c_ref[...]
    o_ref[0] = _dot3(c * _sigmoid(c), w_ref[0]) + b_ref[0]


def _ada(cvec, w_ada, b_ada):
    depth, d, n6 = w_ada.shape
    tn = _tile(n6, ADA_COL_TILE)
    return pl.pallas_call(
        _ada_kernel,
        grid=(depth, n6 // tn),
        in_specs=[
            _full((MOD_ROWS, d)),
            pl.BlockSpec((1, d, tn), lambda i, j: (i, 0, j)),
            pl.BlockSpec((1, 1, tn), lambda i, j: (i, 0, j)),
        ],
        out_specs=pl.BlockSpec((1, MOD_ROWS, tn), lambda i, j: (i, 0, j)),
        out_shape=jax.ShapeDtypeStruct((depth, MOD_ROWS, n6), F32),
        compiler_params=_params("parallel", "parallel"),
        name="ada",
    )(cvec, w_ada, b_ada.reshape(depth, 1, n6))


def _pack_pairs(hh):
    half = hh.shape[1] // 2
    lo = lax.shift_right_logical(pltpu.bitcast(hh[:, :half].astype(F32), jnp.uint32), jnp.uint32(BF16_BITS))
    hi = pltpu.bitcast(hh[:, half:].astype(F32), jnp.uint32) & jnp.uint32(HIGH_HALF)
    return pltpu.bitcast(lo | hi, jnp.int32)


def _unpack_pairs(words):
    u = pltpu.bitcast(words, jnp.uint32)
    lo = pltpu.bitcast(lax.shift_left(u, jnp.uint32(BF16_BITS)), F32).astype(BF)
    hi = pltpu.bitcast(u & jnp.uint32(HIGH_HALF), F32).astype(BF)
    return jnp.concatenate([lo, hi], axis=1)


def _residual_router(x, y, mod, g2_ref, wrh_ref, wrl_ref, xo_ref, hn_ref, aff_ref):
    xn = x + mod[2:3] * y
    xo_ref[...] = xn
    hn = _norm_mod(xn, g2_ref[...], mod[3:4], mod[4:5])
    hh, hl = _split(hn)
    hn_ref[...] = _pack_pairs(hh) if hn_ref.dtype == jnp.int32 else hh
    e = wrh_ref.shape[0]
    both = _dot_nt(jnp.concatenate([wrh_ref[...], wrl_ref[...]], axis=0), hh)
    logits = both[:e] + (_dot_nt(wrh_ref[...], hl) + both[e:])
    p = jnp.exp(logits - jnp.max(logits, axis=0, keepdims=True))
    aff_ref[...] = p / jnp.sum(p, axis=0, keepdims=True)


def _router_specs(n, d, e, tm, packed):
    in_specs = [_full((1, d)), _full((e, d)), _full((e, d))]
    hn_cols, hn_dtype = (d // 2, jnp.int32) if packed else (d, BF)
    out_specs = [
        pl.BlockSpec((tm, d), lambda i: (i, 0)),
        pl.BlockSpec((tm, hn_cols), lambda i: (i, 0)),
        pl.BlockSpec((e, tm), lambda i: (0, i)),
    ]
    out_shape = [
        jax.ShapeDtypeStruct((n, d), F32),
        jax.ShapeDtypeStruct((n, hn_cols), hn_dtype),
        jax.ShapeDtypeStruct((e, n), F32),
    ]
    return in_specs, out_specs, out_shape


def _halo_specs(n, d, tm):
    per = tm // HALO
    last = n // HALO - 1
    return [
        pl.BlockSpec((HALO, d), lambda i: (jnp.maximum(i * per - 1, 0), 0)),
        pl.BlockSpec((tm, d), lambda i: (i, 0)),
        pl.BlockSpec((HALO, d), lambda i: (jnp.minimum((i + 1) * per, last), 0)),
    ]


def _mod_spec(d, tiles_per_seq, ctx_row):
    if ctx_row is None:
        return pl.BlockSpec((1, 6, d), lambda i: (i // tiles_per_seq, 0, 0))
    return pl.BlockSpec((1, 6, d), lambda i: (ctx_row, 0, 0))


def _halo_rows(xp_ref, x_ref, xn_ref, g_ref, mod, tiles_per_seq):
    t = pl.program_id(0) % tiles_per_seq
    g = g_ref[...]
    hp, hx, hn = [_norm_mod(r[...], g, mod[0:1], mod[1:2]) for r in (xp_ref, x_ref, xn_ref)]
    hp = jnp.where(t == 0, 0.0, hp)
    hn = jnp.where(t == tiles_per_seq - 1, 0.0, hn)
    return jnp.concatenate([hp, hx, hn], axis=0).astype(BF)


def _conv3(s, w_ref, tm, seq_rows=None):
    n = s.shape[0]
    prev = pltpu.roll(s, 1, 0)[HALO:HALO + tm]
    nxt = pltpu.roll(s, n - 1, 0)[HALO:HALO + tm]
    if seq_rows is not None and seq_rows < tm:
        within = lax.broadcasted_iota(jnp.int32, (tm, 1), 0) % seq_rows
        prev = jnp.where(within == 0, 0.0, prev)
        nxt = jnp.where(within == seq_rows - 1, 0.0, nxt)
    return prev * w_ref[0:1, :] + s[HALO:HALO + tm] * w_ref[1:2, :] + nxt * w_ref[2:3, :]


def _stacked_weight_spec(w_all, j):
    return pl.BlockSpec((1,) + w_all.shape[1:], lambda i: (j, 0, 0), pipeline_mode=pl.Buffered(1))


def _cast_weight_once(w_ref, w_scr):
    @pl.when(pl.program_id(0) == 0)
    def _():
        w_scr[...] = w_ref[0].astype(BF)


def _shortconv_kernel(xp_ref, x_ref, xn_ref, mod_ref, g1_ref, win32_ref, wconv_ref, wout32_ref,
                      g2_ref, wrh_ref, wrl_ref, xo_ref, hn_ref, aff_ref, win_ref, wout_ref, *, tiles_per_seq, seq_rows):
    tm, d = x_ref.shape
    _cast_weight_once(win32_ref, win_ref)
    _cast_weight_once(wout32_ref, wout_ref)
    mod = mod_ref[0]
    h = _halo_rows(xp_ref, x_ref, xn_ref, g1_ref, mod, tiles_per_seq)
    gate = _dot(h[HALO:HALO + tm], win_ref[:, 0:d])
    s = _dot(h, win_ref[:, d:2 * d]) * _dot(h, win_ref[:, 2 * d:3 * d])
    z = (gate * _conv3(s, wconv_ref, tm, seq_rows)).astype(BF)
    y = _dot(z, wout_ref[...])
    _residual_router(x_ref[...], y, mod, g2_ref, wrh_ref, wrl_ref, xo_ref, hn_ref, aff_ref)


def _shortconv_layer(xs, mods, g1, w_in_all, j, w_conv, w_out_all, g2, wrh, wrl, seq, ctx_row=None):
    n, d = xs.shape
    e = wrh.shape[0]
    if ctx_row is not None and seq < PROJ_ROW_TILE:
        tm = _tile(n, PROJ_ROW_TILE // seq * seq)
        tps = 1
    else:
        tm = _tile(seq, PROJ_ROW_TILE)
        tps = seq // tm
    r_in, r_out, r_shape = _router_specs(n, d, e, tm, packed=ctx_row is None)
    return pl.pallas_call(
        functools.partial(_shortconv_kernel, tiles_per_seq=tps, seq_rows=seq),
        grid=(n // tm,),
        in_specs=_halo_specs(n, d, tm) + [
            _mod_spec(d, tps, ctx_row), _full((1, d)), _stacked_weight_spec(w_in_all, j), _full((3, d)),
            _stacked_weight_spec(w_out_all, j),
        ] + r_in,
        out_specs=r_out,
        out_shape=r_shape,
        scratch_shapes=[pltpu.VMEM((d, 3 * d), BF), pltpu.VMEM((d, d), BF)],
        compiler_params=_params("arbitrary"),
        name="shortconv_layer",
    )(xs, xs, xs, mods, g1, w_in_all, w_conv, w_out_all, g2, wrh, wrl)


def _outproj_kernel(o_ref, x_ref, mod_ref, wout32_ref, g2_ref, wrh_ref, wrl_ref, xo_ref, hn_ref, aff_ref, wout_ref):
    _cast_weight_once(wout32_ref, wout_ref)
    y = _dot(o_ref[...], wout_ref[...])
    _residual_router(x_ref[...], y, mod_ref[0], g2_ref, wrh_ref, wrl_ref, xo_ref, hn_ref, aff_ref)


def _outproj_layer(o, xs, mods, w_out_all, j, g2, wrh, wrl, seq):
    n, d = xs.shape
    e = wrh.shape[0]
    tm = _tile(seq, PROJ_ROW_TILE)
    tps = seq // tm
    r_in, r_out, r_shape = _router_specs(n, d, e, tm, packed=True)
    return pl.pallas_call(
        _outproj_kernel,
        grid=(n // tm,),
        in_specs=[pl.BlockSpec((tm, d), lambda i: (i, 0)), pl.BlockSpec((tm, d), lambda i: (i, 0)),
                  _mod_spec(d, tps, None), _stacked_weight_spec(w_out_all, j)] + r_in,
        out_specs=r_out,
        out_shape=r_shape,
        scratch_shapes=[pltpu.VMEM((d, d), BF)],
        compiler_params=_params("arbitrary"),
        name="outproj_layer",
    )(o, xs, mods, w_out_all, g2, wrh, wrl)


def _excl_cumsum_lanes(m):
    rows, t = m.shape
    a = lax.broadcasted_iota(jnp.int32, (V7X_LANES, V7X_LANES), 0)
    b = lax.broadcasted_iota(jnp.int32, (V7X_LANES, V7X_LANES), 1)
    tri = jnp.where(a < b, 1.0, 0.0).astype(BF)
    carry = jnp.zeros((rows, 1), F32)
    out = []
    for c in range(t // V7X_LANES):
        blk = m[:, c * V7X_LANES:(c + 1) * V7X_LANES]
        out.append(_dot(blk.astype(BF), tri) + carry)
        carry = carry + jnp.sum(blk, axis=1, keepdims=True)
    return jnp.concatenate(out, axis=1)


def _route_kernel(aff_ref, posm_ref, gate_ref, *, cap, nb):
    e = aff_ref.shape[0]
    t = aff_ref.shape[1] // nb
    aff = jnp.concatenate([aff_ref[:, b * t:(b + 1) * t] for b in range(nb)], axis=0)
    bits = pltpu.bitcast(aff, jnp.int32)

    def step(i, thr):
        cand = thr | jnp.left_shift(jnp.int32(1), F32_MAGNITUDE_BITS - 1 - i)
        cnt = jnp.sum(jnp.where(bits >= cand, 1.0, 0.0), axis=1, keepdims=True)
        return jnp.where(cnt >= cap, cand, thr)

    thr = lax.fori_loop(0, F32_MAGNITUDE_BITS, step, jnp.zeros((nb * e, 1), jnp.int32))
    gt = jnp.where(bits > thr, 1.0, 0.0)
    eq = jnp.where(bits == thr, 1.0, 0.0)
    need = cap - jnp.sum(gt, axis=1, keepdims=True)
    sel = gt + eq * jnp.where(_excl_cumsum_lanes(eq) < need, 1.0, 0.0)
    pos = jnp.where(sel > 0.0, _excl_cumsum_lanes(sel), -1.0)
    posm_ref[...] = pos.astype(jnp.int32)
    gate_ref[...] = sel * aff


def _route_by_token_kernel(posm_ref, gate_ref, bt_ref, *, nb):
    e = posm_ref.shape[0] // nb
    t = posm_ref.shape[1]
    pad = jnp.zeros((V7X_LANES - 2 * e, t), F32)
    for b in range(nb):
        rows = slice(b * e, (b + 1) * e)
        bt_ref[b * t:(b + 1) * t, :] = jnp.concatenate(
            [posm_ref[rows, :].astype(F32), gate_ref[rows, :], pad], axis=0).T


def _route(aff, nb, t, cap):
    e = aff.shape[0]
    return pl.pallas_call(
        functools.partial(_route_kernel, cap=cap, nb=nb),
        grid=(1,),
        in_specs=[_full((e, nb * t))],
        out_specs=[_full((nb * e, t)), _full((nb * e, t))],
        out_shape=[jax.ShapeDtypeStruct((nb * e, t), jnp.int32), jax.ShapeDtypeStruct((nb * e, t), F32)],
        compiler_params=_params("arbitrary"),
        name="route",
    )(aff)


def _route_by_token(posm, gate, nb):
    rows, t = posm.shape
    assert 2 * rows // nb <= V7X_LANES
    return pl.pallas_call(
        functools.partial(_route_by_token_kernel, nb=nb),
        grid=(1,),
        in_specs=[_full((rows, t)), _full((rows, t))],
        out_specs=_full((nb * t, V7X_LANES)),
        out_shape=jax.ShapeDtypeStruct((nb * t, V7X_LANES), F32),
        compiler_params=_params("arbitrary"),
        name="route_by_token",
    )(posm, gate)


def _sc_gather(table, posm, e0, nb, n_experts, t, cap):
    from jax.experimental.pallas import tpu_sc as plsc

    info = plsc.get_sparse_core_info()
    cores, lanes = info.num_cores, info.num_lanes
    workers = cores * info.num_subcores
    group = workers // nb
    words = table.shape[1]
    assert workers % nb == 0 and n_experts % group == 0 and cap % SC_GATHER_ROWS == 0 and t % lanes == 0

    @functools.partial(
        pl.kernel,
        mesh=plsc.VectorSubcoreMesh(core_axis_name="core", subcore_axis_name="subcore"),
        compiler_params=pltpu.CompilerParams(needs_layout_passes=False),
        out_type=jax.ShapeDtypeStruct((group * nb * cap, words), jnp.int32),
        scratch_types=[pltpu.VMEM((t,), jnp.int32), pltpu.VMEM((cap,), jnp.int32),
                       pltpu.VMEM((SC_GATHER_ROWS, words), jnp.int32), pltpu.SemaphoreType.DMA],
    )
    def gather(table_hbm, posm_hbm, out_hbm, pos_v, idx_v, rows_v, sem):
        w = lax.axis_index("subcore") * cores + lax.axis_index("core")
        e_local = w // nb
        b = w % nb
        pltpu.sync_copy(posm_hbm.at[b * n_experts + e0 + e_local], pos_v)

        @pl.loop(0, t // lanes)
        def _(i):
            p = pos_v[pl.ds(i * lanes, lanes)]
            token = lax.iota(jnp.int32, lanes) + (i * lanes + b * t)
            plsc.store_scatter(idx_v, [p], token, mask=p >= 0)

        out_base = (e_local * nb + b) * cap

        @pl.loop(0, cap // SC_GATHER_ROWS)
        def _(j):
            pltpu.async_copy(table_hbm.at[idx_v.at[pl.ds(j * SC_GATHER_ROWS, SC_GATHER_ROWS)]], rows_v, sem).wait()
            pltpu.sync_copy(rows_v, out_hbm.at[pl.ds(out_base + j * SC_GATHER_ROWS, SC_GATHER_ROWS)])

    return gather(table, posm).reshape(group, nb * cap, words)


def _sc_group(nb):
    from jax.experimental.pallas import tpu_sc as plsc

    info = plsc.get_sparse_core_info()
    return info.num_cores * info.num_subcores // nb


def _gather_kernel(posm_ref, hn_ref, o_ref, *, cap):
    e = posm_ref.shape[0]
    t, d = hn_ref.shape
    slot = lax.broadcasted_iota(jnp.int32, (cap, t), 0)
    onehot = jnp.concatenate(
        [jnp.where(posm_ref[k:k + 1, :] == slot, 1.0, 0.0).astype(BF) for k in range(e)], axis=0)
    o_ref[...] = _dot(onehot, hn_ref[...]).astype(BF).reshape(e, cap, d)


def _gather(posm, hn, nb, t, cap):
    e = posm.shape[0] // nb
    d = hn.shape[1]
    return pl.pallas_call(
        functools.partial(_gather_kernel, cap=cap),
        grid=(nb,),
        in_specs=[pl.BlockSpec((e, t), lambda b: (b, 0)), pl.BlockSpec((t, d), lambda b: (b, 0))],
        out_specs=pl.BlockSpec((e, cap, d), lambda b: (0, b, 0)),
        out_shape=jax.ShapeDtypeStruct((e, nb * cap, d), BF),
        compiler_params=_params("parallel"),
        name="moe_gather",
    )(posm, hn)


def _expert_kernel(*refs, pieces, per):
    n_in = sum(pieces)
    wg_ref, wu_ref, wd_ref = refs[n_in:n_in + 3]
    y_refs = refs[n_in + 3:n_in + 3 + len(pieces)]
    w_scr = refs[-1]

    @pl.when(pl.program_id(1) == 0)
    def _():
        w_scr[0] = wg_ref[0, 0].astype(BF)
        w_scr[1] = wu_ref[0, 0].astype(BF)
        w_scr[2] = wd_ref[0, 0].astype(BF)

    which = pl.program_id(0) // per

    start = 0
    for n_pieces, y_ref in zip(pieces, y_refs):
        x_refs = refs[start:start + n_pieces]
        start += n_pieces
        rows = x_refs[0].shape[1]
        tr = _tile(rows, ROW_TILE)

        def body(j, carry, x_refs=x_refs, y_ref=y_ref, tr=tr):
            r0 = pl.multiple_of(j * tr, tr)
            xs = x_refs[0][0, pl.ds(r0, tr), :]
            for p in range(1, len(x_refs)):
                xs = jnp.where(which == p, x_refs[p][0, pl.ds(r0, tr), :], xs)
            if xs.dtype == jnp.int32:
                xs = _unpack_pairs(xs)
            a = _dot(xs, w_scr[0])
            b = _dot(xs, w_scr[1])
            hm = (a * _sigmoid(a) * b).astype(BF)
            y_ref[0, pl.ds(r0, tr), :] = _dot(hm, w_scr[2]).astype(BF)
            return carry

        lax.fori_loop(0, rows // tr, body, 0)


def _experts(streams, w_gate, w_up, w_down, layer, e0, count, per):
    d = w_gate.shape[-2]
    f = w_gate.shape[-1]
    assert f == d
    wspec = pl.BlockSpec((1, 1, d, f), lambda k, j: (layer, e0 + k, 0, 0))
    xspecs, xargs = [], []
    for arrays, first in streams:
        for p, x in enumerate(arrays):
            if len(arrays) == 1:
                index = lambda k, j, first=first: (first + k, j, 0)
            else:
                index = lambda k, j, p=p: (
                    jnp.clip(k - p * per, 0, per - 1),
                    jnp.where(k < p * per, 0, jnp.where(k >= (p + 1) * per, EXPERT_ROW_STEPS - 1, j)), 0)
            xspecs.append(pl.BlockSpec((1, x.shape[1] // EXPERT_ROW_STEPS, x.shape[2]), index))
            xargs.append(x)
    rows = [arrays[0].shape[1] for arrays, _ in streams]
    return pl.pallas_call(
        functools.partial(_expert_kernel, pieces=tuple(len(arrays) for arrays, _ in streams), per=per),
        grid=(count, EXPERT_ROW_STEPS),
        in_specs=xspecs + [wspec, wspec, wspec],
        out_specs=[pl.BlockSpec((1, r // EXPERT_ROW_STEPS, d), lambda k, j: (k, j, 0)) for r in rows],
        out_shape=[jax.ShapeDtypeStruct((count, r, d), BF) for r in rows],
        scratch_shapes=[pltpu.VMEM((3, d, f), BF)],
        compiler_params=_params("parallel", "arbitrary"),
        name="moe_experts",
    )(*xargs, w_gate, w_up, w_down)


def _combine_kernel(bt_ref, *refs, cap, final):
    y_refs = refs[:-4]
    x_ref, mod_ref, fg_ref, o_ref = refs[-4:]
    d = y_refs[0].shape[2]
    e = sum(r.shape[0] for r in y_refs)
    tq = x_ref.shape[0]
    pt = bt_ref[:, 0:e]
    gt = bt_ref[:, e:2 * e]
    if cap % V7X_LANES == 0:
        slot = lax.broadcasted_iota(jnp.int32, (tq, cap), 1).astype(F32)
        pieces = [jnp.where(pt[:, k:k + 1] == slot, gt[:, k:k + 1], 0.0).astype(BF) for k in range(e)]
        scat = jnp.concatenate(pieces, axis=1)
    else:
        slot = lax.broadcasted_iota(jnp.int32, (tq, e * cap), 1).astype(F32)
        scat = jnp.zeros((tq, e * cap), F32)
        for k in range(e):
            pk = pt[:, k:k + 1]
            scat = jnp.where((pk >= 0.0) & (pk + float(k * cap) == slot), gt[:, k:k + 1], scat)
        scat = scat.astype(BF)
    out, col = None, 0
    for y_ref in y_refs:
        width = y_ref.shape[0] * cap
        part = _dot(scat[:, col:col + width], y_ref[...].reshape(width, d))
        out = part if out is None else out + part
        col += width
    xn = x_ref[...] + mod_ref[0][5:6] * out
    if final:
        ms = jnp.mean(xn * xn, axis=-1, keepdims=True)
        xn = xn * lax.rsqrt(ms + NORM_EPS) * fg_ref[...]
    o_ref[...] = xn


def _combine(by_token, ys, xs, mods, fg, nb, t, cap, ctx_row=None, final=False):
    n, d = xs.shape
    tq = _tile(t, COMBINE_ROW_TILE)
    tpb = t // tq
    if ctx_row is None:
        mspec = pl.BlockSpec((1, 6, d), lambda b, i: (b, 0, 0))
    else:
        mspec = pl.BlockSpec((1, 6, d), lambda b, i: (ctx_row, 0, 0))
    return pl.pallas_call(
        functools.partial(_combine_kernel, cap=cap, final=final),
        grid=(nb, tpb),
        in_specs=[
            pl.BlockSpec((tq, V7X_LANES), lambda b, i: (b * tpb + i, 0)),
        ] + [pl.BlockSpec((y.shape[0], cap, d), lambda b, i: (0, b, 0)) for y in ys] + [
            pl.BlockSpec((tq, d), lambda b, i: (b * tpb + i, 0)),
            mspec,
            _full((1, d)),
        ],
        out_specs=pl.BlockSpec((tq, d), lambda b, i: (b * tpb + i, 0)),
        out_shape=jax.ShapeDtypeStruct((n, d), F32),
        compiler_params=_params("parallel", "parallel"),
        name="moe_combine",
    )(by_token, *ys, xs, mods, fg)


def _rope_tables(seq):
    rows = seq // GRID_W
    row = np.repeat(np.arange(rows, dtype=np.float32), GRID_W)
    col = np.tile(np.arange(GRID_W, dtype=np.float32), rows)
    inv_freq = (ROPE_THETA ** (-np.arange(0, ROPE_AXIS_DIM, 2, dtype=np.float32) / ROPE_AXIS_DIM)).astype(np.float32)
    lane = np.arange(2 * DA_HEAD_DIM)
    within = lane % DA_HEAD_DIM
    axis = within // ROPE_AXIS_DIM
    half = (within % ROPE_AXIS_DIM) // (ROPE_AXIS_DIM // 2)
    idx = within % (ROPE_AXIS_DIM // 2)
    pos = np.where(axis[None, :] == 0, row[:, None], col[:, None])
    ang = (pos * inv_freq[idx][None, :]).astype(np.float32)
    cos = np.cos(ang).astype(np.float32)
    sin = np.sin(ang).astype(np.float32)
    sin_lo = np.where(half[None, :] == 1, sin, 0.0).astype(np.float32)
    sin_hi = np.where(half[None, :] == 0, -sin, 0.0).astype(np.float32)
    return cos, sin_lo, sin_hi


def _qkv_kernel(x_ref, mod_ref, g_ref, w32_ref, cos_ref, sa_ref, sb_ref, *refs, rope, first):
    o_refs, w_ref = refs[:-1], refs[-1]
    d = x_ref.shape[1]
    _cast_weight_once(w32_ref, w_ref)
    mod = mod_ref[0]
    h = _norm_mod(x_ref[...], g_ref[...], mod[0:1], mod[1:2]).astype(BF)
    slab = 2 * DA_HEAD_DIM
    shift = ROPE_AXIS_DIM // 2
    for j, o_ref in enumerate(o_refs):
        u = _dot(h, w_ref[:, (first + j) * d:(first + j + 1) * d])
        if rope and j < 2:
            scale = DA_HEAD_DIM ** -0.5 * math.log2(math.e) if j == 0 else 1.0
            cos, sa, sb = cos_ref[...] * scale, sa_ref[...] * scale, sb_ref[...] * scale
            for hd in range(d // slab):
                xs = u[:, hd * slab:(hd + 1) * slab]
                r = xs * cos + pltpu.roll(xs, shift, 1) * sa + pltpu.roll(xs, slab - shift, 1) * sb
                o_ref[:, hd * slab:(hd + 1) * slab] = r.astype(BF)
        else:
            o_ref[...] = u.astype(BF)


def _qkv(xs, mods, g1, w_all, j, tables, seq, nout, rope, ctx_row=None, first=0):
    n, d = xs.shape
    tm = _tile(seq, PROJ_ROW_TILE)
    tps = seq // tm
    slab = 2 * DA_HEAD_DIM
    tspec = pl.BlockSpec((tm, slab), lambda i: (i % tps, 0))
    return pl.pallas_call(
        functools.partial(_qkv_kernel, rope=rope, first=first),
        grid=(n // tm,),
        in_specs=[pl.BlockSpec((tm, d), lambda i: (i, 0)), _mod_spec(d, tps, ctx_row), _full((1, d)),
                  _stacked_weight_spec(w_all, j), tspec, tspec, tspec],
        out_specs=[pl.BlockSpec((tm, d), lambda i: (i, 0))] * nout,
        out_shape=[jax.ShapeDtypeStruct((n, d), BF)] * nout,
        scratch_shapes=[pltpu.VMEM(w_all.shape[1:], BF)],
        compiler_params=_params("arbitrary"),
        name="attn_qkv",
    )(xs, mods, g1, w_all, *tables)


def _attn_kernel(q_ref, k_ref, v_ref, kc_ref, vc_ref, lq1_ref, lk1_ref, lq2_ref, lk2_ref, sg_ref, o_ref,
                 sl_a, sc_a, sl_b, sc_b, va_scr, vca_scr, *, lam_init, tq):
    seq = q_ref.shape[0]
    lam = (jnp.exp(jnp.sum(lq1_ref[...] * lk1_ref[...], axis=1, keepdims=True))
           - jnp.exp(jnp.sum(lq2_ref[...] * lk2_ref[...], axis=1, keepdims=True)) + lam_init)
    slots = ((sl_a, sc_a), (sl_b, sc_b))

    def scores(i, slot):
        sl_ref, sc_ref = slot
        q = q_ref[i * tq:(i + 1) * tq, :]
        lane = lax.broadcasted_iota(jnp.int32, q.shape, 1)
        zero = jnp.zeros_like(q)
        for mp, qm in enumerate((jnp.where(lane < DA_HEAD_DIM, q, zero), jnp.where(lane >= DA_HEAD_DIM, q, zero))):
            sl_ref[mp] = _dot_nt(qm, k_ref[...])
            sc_ref[mp] = _dot_nt(qm, kc_ref[...])

    slab = v_ref.shape[1]
    va_scr[:, 0:slab] = v_ref[...]
    va_scr[:, slab:2 * slab] = jnp.ones_like(v_ref)
    vca_scr[:, 0:slab] = vc_ref[...]
    vca_scr[:, slab:2 * slab] = jnp.ones_like(vc_ref)

    def unnormalised(sl_ref, sc_ref, mp):
        s_l = sl_ref[mp]
        s_c = sc_ref[mp]
        m = jnp.maximum(jnp.max(s_l, axis=1, keepdims=True), jnp.max(s_c, axis=1, keepdims=True))
        p_l = jnp.exp2(s_l - m).astype(BF)
        p_c = jnp.exp2(s_c - m).astype(BF)
        both = _dot(p_l, va_scr[...]) + _dot(p_c, vca_scr[...])
        return both[:, 0:slab], both[:, slab:slab + 1]

    def attend(i, slot):
        o1, t1 = unnormalised(*slot, 0)
        o2, t2 = unnormalised(*slot, 1)
        o = o1 * (1.0 / t1) - o2 * (lam / t2)
        ms = jnp.mean(o * o, axis=-1, keepdims=True)
        o_ref[i * tq:(i + 1) * tq, :] = (o * lax.rsqrt(ms + SUBLN_EPS) * sg_ref[...] * (1.0 - lam_init)).astype(BF)

    n = seq // tq
    scores(0, slots[0])
    for i in range(n):
        if i + 1 < n:
            scores(i + 1, slots[(i + 1) % 2])
        attend(i, slots[i % 2])


def _attention(q, k, v, kc, vc, lq1, lk1, lq2, lk2, sg, nb, seq, ctx_len, lam_init):
    n, d = q.shape
    slab = 2 * DA_HEAD_DIM
    heads = d // slab
    tq = _tile(seq, ATTN_Q_TILE)
    small = _full((1, DA_HEAD_DIM))
    lat = pl.BlockSpec((seq, slab), lambda b, h: (b, h))
    ctx = pl.BlockSpec((ctx_len, slab), lambda b, h: (b, h))
    score_scratch = [pltpu.VMEM((2, tq, seq), F32), pltpu.VMEM((2, tq, ctx_len), F32)]
    return pl.pallas_call(
        functools.partial(_attn_kernel, lam_init=lam_init, tq=tq),
        grid=(nb, heads),
        in_specs=[lat, lat, lat, ctx, ctx, small, small, small, small, _full((1, slab))],
        out_specs=lat,
        out_shape=jax.ShapeDtypeStruct((n, d), BF),
        scratch_shapes=score_scratch + score_scratch + [pltpu.VMEM((seq, 2 * slab), BF),
                                                        pltpu.VMEM((ctx_len, 2 * slab), BF)],
        compiler_params=_params("parallel", "parallel"),
        name="diff_attention",
    )(q, k, v, kc, vc, lq1, lk1, lq2, lk2, sg)


def _hyena_in_kernel(xp_ref, x_ref, xn_ref, mod_ref, g1_ref, win32_ref, wconv_ref, o_ref, win_ref, *, tiles_per_seq):
    tm, d = x_ref.shape
    _cast_weight_once(win32_ref, win_ref)
    h = _halo_rows(xp_ref, x_ref, xn_ref, g1_ref, mod_ref[0], tiles_per_seq)
    for j in range(3):
        u = _dot(h, win_ref[:, j * d:(j + 1) * d])
        o_ref[:, j * d:(j + 1) * d] = _conv3(u, wconv_ref.at[:, j * d:(j + 1) * d], tm)


def _hyena_in(xs, mods, g1, w_in_all, j, w_conv, seq):
    n, d = xs.shape
    tm = _tile(seq, PROJ_ROW_TILE)
    tps = seq // tm
    return pl.pallas_call(
        functools.partial(_hyena_in_kernel, tiles_per_seq=tps),
        grid=(n // tm,),
        in_specs=_halo_specs(n, d, tm) + [_mod_spec(d, tps, None), _full((1, d)), _stacked_weight_spec(w_in_all, j),
                                          _full((3, 3 * d))],
        out_specs=pl.BlockSpec((tm, 3 * d), lambda i: (i, 0)),
        out_shape=jax.ShapeDtypeStruct((n, 3 * d), F32),
        scratch_shapes=[pltpu.VMEM((d, 3 * d), BF)],
        compiler_params=_params("arbitrary"),
        name="hyena_in",
    )(xs, xs, xs, mods, g1, w_in_all, w_conv)


def _dft_half_shift(seq):
    order = np.concatenate([np.arange(seq // 2), seq - 1 - np.arange(seq // 2)])
    half = np.pi * (order + 0.5) / (2 * seq)
    return np.stack([np.cos(half), np.sin(half)], axis=1).astype(np.float32)


def _dft_half_tables(seq):
    h = seq // 2
    th = 2.0 * np.pi * (np.arange(h, dtype=np.float64) + 0.5) / seq
    s = np.arange(h, dtype=np.float64)
    even, odd = np.outer(th, s + 0.25), np.outer(th, s + 0.75)
    fwd = np.stack([np.cos(even), np.sin(even), np.cos(odd), np.sin(odd)]).astype(np.float32)
    inv = np.ascontiguousarray(np.transpose(fwd, (0, 2, 1)))
    return fwd, inv


def _hyena_features(seq):
    t = np.linspace(0.0, 1.0, seq, dtype=np.float32)[:, None]
    w = (2.0 * math.pi * np.arange(seq, dtype=np.float32)[:, None] / seq).astype(np.float32)
    f = np.linspace(1e-4, HY_BANDS - 1, HY_BANDS, dtype=np.float32)[None, :]
    z = np.concatenate([t, np.cos(f * w), -np.sin(f * w)], axis=-1).astype(np.float32)
    zp = np.zeros((seq, V7X_LANES), np.float32)
    zp[:, :HY_EMB_DIM] = z
    return zp


def _hyena_deltas(d):
    max_decay = math.log(HY_DECAY_TARGET) / HY_FAST_DECAY_PCT
    min_decay = math.log(HY_DECAY_TARGET) / HY_SLOW_DECAY_PCT
    return np.abs(np.linspace(min_decay, max_decay, d, dtype=np.float32))[None, :].astype(np.float32)


def _hyena_filter_kernel(z_ref, f1w_ref, f1b_ref, f1f_ref, f2w_ref, f2b_ref, f2f_ref,
                         f3a_ref, f3b_ref, f3c_ref, f3d_ref, delta_ref, fwd_ref, rot_ref, k_ref, hid_scr, *pm_scrs):
    seq = z_ref.shape[0]
    half = seq // 2
    tc = delta_ref.shape[1]
    even = pl.ds(0, half, stride=2)
    odd = pl.ds(1, half, stride=2)

    @pl.when(pl.program_id(0) == 0)
    def _():
        h1 = jnp.sin(f1f_ref[...] * (_dot3(z_ref[...], f1w_ref[...]) + f1b_ref[...]))
        hid_scr[...] = jnp.sin(f2f_ref[...] * (_dot3(h1, f2w_ref[...]) + f2b_ref[...]))

    hid = hid_scr[...]
    decay = jnp.exp(-z_ref[:, 0:1] * delta_ref[...])
    row = lax.broadcasted_iota(jnp.int32, (seq, 1), 0)
    cr = rot_ref[:, 0:1]
    sr = rot_ref[:, 1:2]
    for order, (f3_fwd_ref, f3_bwd_ref) in enumerate(((f3a_ref, f3b_ref), (f3c_ref, f3d_ref))):
        h_fwd = _dot3(hid, f3_fwd_ref[...]) * decay
        h_bwd = jnp.where(row == 0, 0.0, _dot3(hid, f3_bwd_ref[...]) * decay)
        pm = jnp.concatenate([h_fwd + h_bwd, h_bwd - h_fwd], axis=1)
        for k, scr in enumerate(pm_scrs):
            scr[...] = pm[:, k * V7X_LANES:(k + 1) * V7X_LANES]
        x_even = jnp.concatenate([scr[even, :] for scr in pm_scrs], axis=1).astype(BF)
        x_odd = jnp.concatenate([scr[odd, :] for scr in pm_scrs], axis=1).astype(BF)
        a = _dot(fwd_ref[0], x_even)
        b = _dot(fwd_ref[1], x_even)
        c = _dot(fwd_ref[2], x_odd)
        s = _dot(fwd_ref[3], x_odd)
        cos_sum = jnp.concatenate([a + c, b - s], axis=0)
        sin_sum = jnp.concatenate([b + s, a - c], axis=0)
        k_ref[2 * order] = cr * cos_sum[:, :tc] + sr * sin_sum[:, :tc]
        k_ref[2 * order + 1] = cr * sin_sum[:, tc:] - sr * cos_sum[:, tc:]


def _hyena_filter(z, f1w, f1b, f1f, f2w, f2b, f2f, f3w, deltas, fwd, rot, d):
    seq = z.shape[0]
    half = seq // 2
    w = f2w.shape[0]
    tc = _tile(d, CHAN_TILE)
    nc = d // tc
    f3spec = [pl.BlockSpec((w, tc), lambda j, o=o: (0, o * nc + j)) for o in range(4)]
    return pl.pallas_call(
        _hyena_filter_kernel,
        grid=(nc,),
        in_specs=[_full((seq, V7X_LANES)), _full((V7X_LANES, w)), _full((1, w)), _full((1, w)),
                  _full((w, w)), _full((1, w)), _full((1, w))] + f3spec + [
            pl.BlockSpec((1, tc), lambda j: (0, j)),
            pl.BlockSpec((4, half, half), lambda j: (0, 0, 0), pipeline_mode=pl.Buffered(1)),
            _full((seq, 2)),
        ],
        out_specs=pl.BlockSpec((4, seq, tc), lambda j: (0, 0, j)),
        out_shape=jax.ShapeDtypeStruct((4, seq, d), F32),
        scratch_shapes=[pltpu.VMEM((seq, w), F32)] + [pltpu.VMEM((seq, V7X_LANES), F32)] * (2 * tc // V7X_LANES),
        compiler_params=_params("arbitrary"),
        name="hyena_filter",
    )(z, f1w, f1b, f1f, f2w, f2b, f2f, f3w, f3w, f3w, f3w, deltas, fwd, rot)


def _hyena_conv_kernel(*refs, gt, pieces):
    x1_refs, x2_refs, v_refs = refs[:pieces], refs[pieces:2 * pieces], refs[2 * pieces:3 * pieces]
    fwd_ref, inv_ref, k_ref, skip_ref, o_ref = refs[3 * pieces:3 * pieces + 5]
    ue_scr, uo_scr, ze_scr, zo_scr, ye_scr, yo_scr = refs[3 * pieces + 5:3 * pieces + 11]
    out_scrs = refs[3 * pieces + 11:]
    seq = o_ref.shape[0]
    half = seq // 2
    even = pl.ds(0, half, stride=2)
    odd = pl.ds(1, half, stride=2)

    def samples(piece_refs, rows):
        return jnp.concatenate([r[rows, :] for r in piece_refs], axis=1)

    def longconv(order):
        ye_scr[...] = jnp.zeros_like(ye_scr)
        yo_scr[...] = jnp.zeros_like(yo_scr)

        def body(c, carry):
            g0 = pl.multiple_of(c * gt, gt)
            rows = pl.ds(g0, gt)
            ue = ue_scr[...]
            uo = uo_scr[...]
            a = _dot(fwd_ref[0, rows, :], ue)
            b = _dot(fwd_ref[1, rows, :], ue)
            cc = _dot(fwd_ref[2, rows, :], uo)
            d = _dot(fwd_ref[3, rows, :], uo)

            def times_filter(first, ur, ui):
                kr = k_ref[2 * order, pl.ds(first + g0, gt), :]
                ki = k_ref[2 * order + 1, pl.ds(first + g0, gt), :]
                return kr * ur + ki * ui, kr * ui - ki * ur

            yra, yia = times_filter(0, a + cc, b + d)
            yrb, yib = times_filter(half, b - d, a - cc)
            ye_scr[...] += (_dot(inv_ref[0, :, rows], (yra + yib).astype(BF))
                            + _dot(inv_ref[1, :, rows], (yia + yrb).astype(BF)))
            yo_scr[...] += (_dot(inv_ref[2, :, rows], (yra - yib).astype(BF))
                            + _dot(inv_ref[3, :, rows], (yia - yrb).astype(BF)))
            return carry

        lax.fori_loop(0, half // gt, body, 0)

    scale = 1.0 / seq
    ve = samples(v_refs, even)
    vo = samples(v_refs, odd)
    ue_scr[...] = ve.astype(BF)
    uo_scr[...] = vo.astype(BF)
    longconv(0)
    ze = samples(x1_refs, even) * (ye_scr[...] * scale + ve * skip_ref[0:1, :])
    zo = samples(x1_refs, odd) * (yo_scr[...] * scale + vo * skip_ref[0:1, :])
    ze_scr[...] = ze
    zo_scr[...] = zo
    ue_scr[...] = ze.astype(BF)
    uo_scr[...] = zo.astype(BF)
    longconv(1)
    oe = samples(x2_refs, even) * (ye_scr[...] * scale + ze_scr[...] * skip_ref[1:2, :])
    oo = samples(x2_refs, odd) * (yo_scr[...] * scale + zo_scr[...] * skip_ref[1:2, :])
    for k, out_scr in enumerate(out_scrs):
        lanes = slice(k * V7X_LANES, (k + 1) * V7X_LANES)
        out_scr[even, :] = oe[:, lanes]
        out_scr[odd, :] = oo[:, lanes]
        o_ref[:, lanes] = out_scr[...].astype(BF)


def _hyena_conv(u3, fwd, inv, kspec, skip, nb, seq, d):
    tc = _tile(d, CHAN_TILE)
    nc = d // tc
    half = seq // 2
    gt = _tile(half, FREQ_TILE)
    pieces = tc // V7X_LANES
    once = dict(pipeline_mode=pl.Buffered(1))
    piece_specs = [pl.BlockSpec((seq, V7X_LANES), lambda j, b, o=o, k=k: (b, (o * nc + j) * pieces + k))
                   for o in range(3) for k in range(pieces)]
    return pl.pallas_call(
        functools.partial(_hyena_conv_kernel, gt=gt, pieces=pieces),
        scratch_shapes=[pltpu.VMEM((half, tc), BF), pltpu.VMEM((half, tc), BF)]
        + [pltpu.VMEM((half, tc), F32)] * 4 + [pltpu.VMEM((seq, V7X_LANES), F32)] * pieces,
        grid=(nc, nb),
        in_specs=piece_specs + [
            pl.BlockSpec((4, half, half), lambda j, b: (0, 0, 0), **once),
            pl.BlockSpec((4, half, half), lambda j, b: (0, 0, 0), **once),
            pl.BlockSpec((4, seq, tc), lambda j, b: (0, 0, j), **once),
            pl.BlockSpec((2, tc), lambda j, b: (0, j)),
        ],
        out_specs=pl.BlockSpec((seq, tc), lambda j, b: (b, j)),
        out_shape=jax.ShapeDtypeStruct((nb * seq, d), BF),
        compiler_params=_params("parallel", "parallel"),
        name="hyena_conv",
    )(*([u3] * (3 * pieces)), fwd, inv, kspec, skip)


def _moe_block(streams, mods, layer, fg, w_gate, w_up, w_down, nb, final=False):
    n_experts = streams[0][2].shape[0]
    group = _sc_group(nb)
    routed = []
    for xs, hn, aff, t, ctx_row in streams:
        cap = max(1, EC_CAPACITY * t // n_experts)
        posm, gate = _route(aff, nb, t, cap)
        by_token = _route_by_token(posm, gate, nb)
        xin = None if hn.dtype == jnp.int32 else _gather(posm, hn, nb, t, cap)
        routed.append((posm, by_token, cap, xin))
    ys = [[] for _ in streams]
    for e0, count in ((0, group), (group, n_experts - group)):
        xins = []
        for (xs, hn, aff, t, ctx_row), (posm, by_token, cap, xin) in zip(streams, routed):
            if xin is None:
                xins.append(([_sc_gather(hn, posm, e, nb, n_experts, t, cap) for e in range(e0, e0 + count, group)], 0))
            else:
                xins.append(([xin], e0))
        for acc, y in zip(ys, _experts(xins, w_gate, w_up, w_down, layer, e0, count, group)):
            acc.append(y)
    return [_combine(by_token, y, xs, mods, fg, nb, t, cap, ctx_row=ctx_row, final=final and ctx_row is None)
            for (xs, _, _, t, ctx_row), (_, by_token, cap, _), y in zip(streams, routed, ys)]


def kernel(x, c, ctx, c_ctx, w_ada, b_ada, norm1_g, norm2_g, final_g, a_w_in, a_conv, a_w_out, b_w_qkv, b_lq1, b_lk1, b_lq2, b_lk2, b_subln_g, b_w_out, c_w_in, c_conv, c_f1_w, c_f1_b, c_f1_freq, c_f2_w, c_f2_b, c_f2_freq, c_f3_w, c_skip, c_w_out, moe_router, moe_w_gate, moe_w_up, moe_w_down):
    nb, seq, d = x.shape
    ctx_len = ctx.shape[1]
    depth = w_ada.shape[0]
    assert nb < MOD_ROWS and d % (2 * DA_HEAD_DIM) == 0
    ctx_row = nb

    cvec = jnp.concatenate([c, c_ctx[None, :], jnp.zeros((MOD_ROWS - nb - 1, d), F32)], axis=0)
    mods_all = _ada(cvec, w_ada, b_ada).reshape(depth, MOD_ROWS, 6, d)

    attn_layers = [i for i in range(depth) if i % N_MIXERS == 1]
    last_ctx_read = max(attn_layers) if attn_layers else -1

    xs = x.reshape(nb * seq, d)
    cs_tok = ctx.reshape(nb * ctx_len, d)
    fg = final_g[None, :]
    wrh_all, wrl_all = _split(jnp.swapaxes(moe_router, 1, 2))

    for i in range(depth):
        kind, j = i % N_MIXERS, i // N_MIXERS
        update_ctx = i < last_ctx_read
        final = i == depth - 1
        mods = mods_all[i]
        g1 = norm1_g[i][None, :]
        g2 = norm2_g[i][None, :]
        wrh, wrl = wrh_all[i], wrl_all[i]
        moe_w = (moe_w_gate, moe_w_up, moe_w_down)

        ctx_stream = []
        if kind == 0:
            if update_ctx:
                cn, chn, caff = _shortconv_layer(cs_tok, mods, g1, a_w_in, j, a_conv[j], a_w_out, g2, wrh, wrl,
                                                 ctx_len, ctx_row=ctx_row)
                ctx_stream = [(cn, chn, caff, ctx_len, ctx_row)]
            xn, hn, aff = _shortconv_layer(xs, mods, g1, a_w_in, j, a_conv[j], a_w_out, g2, wrh, wrl, seq)
        elif kind == 1:
            assert not update_ctx
            lam_init = 0.8 - 0.6 * math.exp(-0.3 * i)
            tables = [jnp.asarray(t) for t in _rope_tables(seq)]
            q, k, v = _qkv(xs, mods, g1, b_w_qkv, j, tables, seq, 3, True)
            kc, vc = _qkv(cs_tok, mods, g1, b_w_qkv, j, tables, min(seq, nb * ctx_len), 2, False, ctx_row=ctx_row,
                          first=1)
            o = _attention(q, k, v, kc, vc, b_lq1[j][None, :], b_lk1[j][None, :], b_lq2[j][None, :],
                           b_lk2[j][None, :], b_subln_g[j][None, :], nb, seq, ctx_len, lam_init)
            xn, hn, aff = _outproj_layer(o, xs, mods, b_w_out, j, g2, wrh, wrl, seq)
        else:
            assert not update_ctx
            fwd_np, inv_np = _dft_half_tables(seq)
            fwd, inv = jnp.asarray(fwd_np.astype(BF)), jnp.asarray(inv_np.astype(BF))
            w = c_f2_w.shape[-1]
            f1w = jnp.zeros((V7X_LANES, w), F32).at[:HY_EMB_DIM].set(c_f1_w[j])
            kspec = _hyena_filter(jnp.asarray(_hyena_features(seq)), f1w, c_f1_b[j][None, :], c_f1_freq[j][None, :],
                                  c_f2_w[j], c_f2_b[j][None, :], c_f2_freq[j][None, :], c_f3_w[j],
                                  jnp.asarray(_hyena_deltas(d)), fwd, jnp.asarray(_dft_half_shift(seq)), d)
            u3 = _hyena_in(xs, mods, g1, c_w_in, j, c_conv[j], seq)
            z = _hyena_conv(u3, fwd, inv, kspec, c_skip[j], nb, seq, d)
            xn, hn, aff = _outproj_layer(z, xs, mods, c_w_out, j, g2, wrh, wrl, seq)

        outs = _moe_block([(xn, hn, aff, seq, None)] + ctx_stream, mods, i, fg, *moe_w, nb, final=final)
        xs = outs[0]
        if ctx_stream:
            cs_tok = outs[1]

    return xs.reshape(nb, seq, d)
```
